```python
import jax, jax.numpy as jnp
from jax import lax
import numpy as np

D_MODEL = 1024
BATCH = 8
SEQ = 4096
DEPTH = 2

SSD_EXPAND = 2
SSD_D_INNER = SSD_EXPAND * D_MODEL
SSD_HEAD_DIM = 64
SSD_HEADS = SSD_D_INNER // SSD_HEAD_DIM
SSD_GROUPS = 4
SSD_HEADS_PER_GROUP = SSD_HEADS // SSD_GROUPS
SSD_STATE = 128
SSD_CONV = 4
SSD_CHUNK = 128
SSD_CONV_CH = SSD_D_INNER + 2 * SSD_GROUPS * SSD_STATE
ATTN_HEADS = 16
ATTN_KV_HEADS = 2
ATTN_HEAD_DIM = 64
ATTN_WIDTH = ATTN_HEADS * ATTN_HEAD_DIM
Q_BLOCK = 128
TOPK_MAX = 256
IDX_HEADS = 8
IDX_DIM = 64
ROPE_THETA = 10000.0
MEM_LEN = 256
XATTN_HEADS = 4
XATTN_HEAD_DIM = D_MODEL // XATTN_HEADS
MOE_GROUPS = 4
MOE_EXPERTS_PER_GROUP = 8
MOE_EXPERTS = MOE_GROUPS * MOE_EXPERTS_PER_GROUP
MOE_D_FF = 256
MOE_TOPK_IN_GROUP = 2
RMS_EPS = 1e-6

IN_WIDTHS = (SSD_D_INNER, SSD_CONV_CH, SSD_HEADS, ATTN_WIDTH, ATTN_KV_HEADS * ATTN_HEAD_DIM,
             ATTN_KV_HEADS * ATTN_HEAD_DIM, IDX_HEADS * IDX_DIM, IDX_DIM, IDX_HEADS, D_MODEL, D_MODEL)
D_IN_PROJ = sum(IN_WIDTHS)

kernel_name = 'hybrid_ssd_dsa_hmoe_block'

F32 = jnp.float32


def rms_norm(x, g):
    xf = x.astype(F32)
    y = xf * lax.rsqrt(jnp.mean(xf * xf, axis=-1, keepdims=True) + RMS_EPS)
    return (y * g.astype(F32)).astype(x.dtype)


def rope_tables(seq, dim):
    inv = 1.0 / (ROPE_THETA ** (jnp.arange(0, dim, 2, dtype=F32) / dim))
    ang = jnp.arange(seq, dtype=F32)[:, None] * inv[None, :]
    return jnp.cos(ang), jnp.sin(ang)


def apply_rope(x, cos, sin):
    x1, x2 = jnp.split(x.astype(F32), 2, axis=-1)
    c = cos[None, :, None, :]
    s = sin[None, :, None, :]
    return jnp.concatenate([x1 * c - x2 * s, x1 * s + x2 * c], axis=-1).astype(x.dtype)


def ssd_branch(z, xbc, dt_raw, conv_w, conv_b, dt_bias, a_log, d_skip, norm_g):
    Bsz, S, _ = xbc.shape
    G, HG, P, N, L = SSD_GROUPS, SSD_HEADS_PER_GROUP, SSD_HEAD_DIM, SSD_STATE, SSD_CHUNK
    xbc = lax.conv_general_dilated(
        xbc, conv_w[:, None, :].astype(xbc.dtype), window_strides=(1,),
        padding=[(SSD_CONV - 1, 0)], dimension_numbers=('NWC', 'WIO', 'NWC'),
        feature_group_count=SSD_CONV_CH) + conv_b.astype(xbc.dtype)
    xbc = jax.nn.silu(xbc)
    xs, bm, cm = jnp.split(xbc, [SSD_D_INNER, SSD_D_INNER + G * N], axis=-1)
    xs = xs.reshape(Bsz, S, G, HG, P)
    bm = bm.reshape(Bsz, S, G, N)
    cm = cm.reshape(Bsz, S, G, N)
    dt = jax.nn.softplus(dt_raw.astype(F32) + dt_bias.astype(F32)).reshape(Bsz, S, G, HG)
    a_neg = -jnp.exp(a_log.astype(F32)).reshape(G, HG)
    nc = S // L

    def to_chunks(t):
        return jnp.moveaxis(t.reshape((Bsz, nc, L) + t.shape[2:]), 1, 0)

    causal = jnp.tril(jnp.ones((L, L), dtype=bool))[None, :, :, None, None]

    def step(state, inp):
        xc, bc, cc, dtc = inp
        xc, bc, cc = xc.astype(F32), bc.astype(F32), cc.astype(F32)
        acum = jnp.cumsum(dtc * a_neg, axis=1)
        seg = acum[:, :, None] - acum[:, None, :]
        decay = jnp.exp(jnp.where(causal, seg, -jnp.inf))
        xdt = xc * dtc[..., None]
        cb = jnp.einsum('btgn,bsgn->btsg', cc, bc)
        y_diag = jnp.einsum('btsg,btsgh,bsghp->btghp', cb, decay, xdt)
        y_off = jnp.einsum('btgn,bghpn->btghp', cc, state) * jnp.exp(acum)[..., None]
        tail = jnp.exp(acum[:, -1:] - acum)
        new_state = (state * jnp.exp(acum[:, -1])[..., None, None]
                     + jnp.einsum('bsgh,bsghp,bsgn->bghpn', tail, xdt, bc))
        return new_state, y_diag + y_off

    state0 = jnp.zeros((Bsz, G, HG, P, N), F32)
    _, ys = lax.scan(step, state0, (to_chunks(xs), to_chunks(bm), to_chunks(cm), to_chunks(dt)))
    y = jnp.moveaxis(ys, 0, 1).reshape(Bsz, S, G, HG, P)
    y = y + d_skip.astype(F32).reshape(G, HG)[None, None, :, :, None] * xs.astype(F32)
    y = y.reshape(Bsz, S, SSD_D_INNER) * jax.nn.silu(z.astype(F32))
    y = y.reshape(Bsz, S, SSD_GROUPS, SSD_D_INNER // SSD_GROUPS)
    y = y * lax.rsqrt(jnp.mean(y * y, axis=-1, keepdims=True) + RMS_EPS)
    y = y.reshape(Bsz, S, SSD_D_INNER) * norm_g.astype(F32)
    return y.astype(z.dtype)


def dsa_branch(q, k, v, qi, ki, wi):
    Bsz, S = q.shape[:2]
    topk = min(TOPK_MAX, S // 4)
    nb = S // Q_BLOCK
    rep = ATTN_HEADS // ATTN_KV_HEADS
    idx_scale = (IDX_DIM ** -0.5) * (IDX_HEADS ** -0.5)
    attn_scale = ATTN_HEAD_DIM ** -0.5
    key_pos = jnp.arange(S)
    ki32 = ki.astype(F32)

    def block(bi):
        q0 = bi * Q_BLOCK
        qb = lax.dynamic_slice_in_dim(q, q0, Q_BLOCK, axis=1)
        qib = lax.dynamic_slice_in_dim(qi, q0, Q_BLOCK, axis=1)
        wib = lax.dynamic_slice_in_dim(wi, q0, Q_BLOCK, axis=1)
        qpos = q0 + jnp.arange(Q_BLOCK)
        admissible = key_pos[None, :] <= qpos[:, None]
        rel = jax.nn.relu(jnp.einsum('bqhd,bsd->bqhs', qib.astype(F32), ki32))
        iscore = jnp.einsum('bqh,bqhs->bqs', wib.astype(F32) * idx_scale, rel)
        iscore = jnp.where(admissible[None], iscore, -jnp.inf)
        _, sel = lax.top_k(iscore, topk)
        valid = sel <= qpos[None, :, None]
        kg = jax.vmap(lambda kb, ib: kb[ib])(k, sel)
        vg = jax.vmap(lambda vb, ib: vb[ib])(v, sel)
        qg = qb.reshape(Bsz, Q_BLOCK, ATTN_KV_HEADS, rep, ATTN_HEAD_DIM).astype(F32)
        s = jnp.einsum('bqgrd,bqkgd->bqgrk', qg, kg.astype(F32)) * attn_scale
        s = jnp.where(valid[:, :, None, None, :], s, -jnp.inf)
        p = jax.nn.softmax(s, axis=-1)
        o = jnp.einsum('bqgrk,bqkgd->bqgrd', p, vg.astype(F32))
        return o.reshape(Bsz, Q_BLOCK, ATTN_WIDTH).astype(q.dtype)

    out = lax.map(block, jnp.arange(nb))
    return jnp.moveaxis(out, 0, 1).reshape(Bsz, S, ATTN_WIDTH)


def hybrid_mixer(hn, w_in, conv_w, conv_b, dt_bias, a_log, d_skip, ssd_norm, w_ssd_o, w_attn_o, w_out):
    Bsz, S, _ = hn.shape
    proj = hn @ w_in
    splits = [int(p) for p in np.cumsum(IN_WIDTHS)[:-1]]
    z, xbc, dt_raw, q, k, v, qi, ki, wi, g_ssd, g_attn = jnp.split(proj, splits, axis=-1)
    y_ssd = ssd_branch(z, xbc, dt_raw, conv_w, conv_b, dt_bias, a_log, d_skip, ssd_norm)
    cos_a, sin_a = rope_tables(S, ATTN_HEAD_DIM)
    cos_i, sin_i = rope_tables(S, IDX_DIM)
    q = apply_rope(q.reshape(Bsz, S, ATTN_HEADS, ATTN_HEAD_DIM), cos_a, sin_a)
    k = apply_rope(k.reshape(Bsz, S, ATTN_KV_HEADS, ATTN_HEAD_DIM), cos_a, sin_a)
    v = v.reshape(Bsz, S, ATTN_KV_HEADS, ATTN_HEAD_DIM)
    qi = apply_rope(qi.reshape(Bsz, S, IDX_HEADS, IDX_DIM), cos_i, sin_i)
    ki = apply_rope(ki[:, :, None, :], cos_i, sin_i)[:, :, 0, :]
    y_attn = dsa_branch(q, k, v, qi, ki, wi)
    merged = (jax.nn.sigmoid(g_ssd) * (y_ssd @ w_ssd_o)
              + jax.nn.sigmoid(g_attn) * (y_attn @ w_attn_o))
    return merged @ w_out


def cross_attention(hn, mn, w_cq, w_ckv, w_co):
    Bsz, S, _ = hn.shape
    M = mn.shape[1]
    q = (hn @ w_cq).reshape(Bsz, S, XATTN_HEADS, XATTN_HEAD_DIM)
    k, v = jnp.split(mn @ w_ckv, 2, axis=-1)
    k = k.reshape(Bsz, M, XATTN_HEADS, XATTN_HEAD_DIM)
    v = v.reshape(Bsz, M, XATTN_HEADS, XATTN_HEAD_DIM)
    s = jnp.einsum('bshd,bmhd->bhsm', q.astype(F32), k.astype(F32)) * (XATTN_HEAD_DIM ** -0.5)
    p = jax.nn.softmax(s, axis=-1)
    o = jnp.einsum('bhsm,bmhd->bshd', p, v.astype(F32)).reshape(Bsz, S, D_MODEL)
    return o.astype(hn.dtype) @ w_co


def hier_moe(hn, w_group, b_group, w_router, b_router, w_gate_e, w_up_e, w_down_e):
    Bsz, S, D = hn.shape
    t = hn.reshape(-1, D)
    n_tok = t.shape[0]
    g_logits = (t @ w_group).astype(F32) + b_group.astype(F32)
    g_prob = jax.nn.softmax(g_logits, axis=-1)
    g_sel = jnp.argmax(g_logits, axis=-1)
    g_w = jnp.take_along_axis(g_prob, g_sel[:, None], axis=-1)
    e_logits = ((t @ w_router).astype(F32) + b_router.astype(F32)).reshape(
        n_tok, MOE_GROUPS, MOE_EXPERTS_PER_GROUP)
    e_in_group = jnp.take_along_axis(e_logits, g_sel[:, None, None], axis=1)[:, 0]
    top_v, top_i = lax.top_k(e_in_group, MOE_TOPK_IN_GROUP)
    top_w = jax.nn.softmax(top_v, axis=-1) * g_w
    within = jnp.einsum('nk,nke->ne', top_w,
                        jax.nn.one_hot(top_i, MOE_EXPERTS_PER_GROUP, dtype=F32))
    combine = jax.nn.one_hot(g_sel, MOE_GROUPS, dtype=F32)[:, :, None] * within[:, None, :]
    out = jnp.zeros((n_tok, D), F32)
    for g in range(MOE_GROUPS):
        hid = (jax.nn.silu(jnp.einsum('nd,edf->nef', t, w_gate_e[g]))
               * jnp.einsum('nd,edf->nef', t, w_up_e[g]))
        hid = hid * combine[:, g, :, None].astype(hid.dtype)
        out = out + jnp.einsum('nef,efd->nd', hid, w_down_e[g]).astype(F32)
    return out.astype(hn.dtype).reshape(Bsz, S, D)


def setup_inputs(seed: int = 0) -> dict:
    key = jax.random.key(seed)
    ks = iter(jax.random.split(key, 40))

    def nrm(shape, scale):
        return jax.random.normal(next(ks), shape, F32) * scale

    def gain(shape):
        return 1.0 + nrm(shape, 0.02)

    D, L_ = D_MODEL, DEPTH
    dt0 = jnp.exp(jax.random.uniform(next(ks), (L_, SSD_HEADS), F32)
                  * (np.log(0.1) - np.log(1e-3)).astype(np.float32) + np.float32(np.log(1e-3)))
    dt_bias = dt0 + jnp.log(-jnp.expm1(-dt0))
    a_log = jnp.log(jax.random.uniform(next(ks), (L_, SSD_HEADS), F32, 1.0, 16.0))
    return {
        'x': nrm((BATCH, SEQ, D), 1.0),
        'mem': nrm((BATCH, MEM_LEN, D), 1.0),
        'norm_mix': gain((L_, D)),
        'w_in': nrm((L_, D, D_IN_PROJ), D ** -0.5),
        'conv_w': nrm((L_, SSD_CONV, SSD_CONV_CH), 0.5),
        'conv_b': nrm((L_, SSD_CONV_CH), 0.02),
        'dt_bias': dt_bias,
        'a_log': a_log,
        'd_skip': gain((L_, SSD_HEADS)),
        'ssd_norm': gain((L_, SSD_D_INNER)),
        'w_ssd_o': nrm((L_, SSD_D_INNER, D), SSD_D_INNER ** -0.5),
        'w_attn_o': nrm((L_, ATTN_WIDTH, D), ATTN_WIDTH ** -0.5),
        'w_out': nrm((L_, D, D), D ** -0.5),
        'norm_xattn': gain((L_, D)),
        'norm_mem': gain((L_, D)),
        'w_cq': nrm((L_, D, D), D ** -0.5),
        'w_ckv': nrm((L_, D, 2 * D), D ** -0.5),
        'w_co': nrm((L_, D, D), D ** -0.5),
        'norm_ffn': gain((L_, D)),
        'w_group': nrm((L_, D, MOE_GROUPS), D ** -0.5),
        'b_group': nrm((L_, MOE_GROUPS), 0.01),
        'w_router': nrm((L_, D, MOE_EXPERTS), D ** -0.5),
        'b_router': nrm((L_, MOE_EXPERTS), 0.01),
        'w_gate_e': nrm((L_, MOE_GROUPS, MOE_EXPERTS_PER_GROUP, D, MOE_D_FF), D ** -0.5),
        'w_up_e': nrm((L_, MOE_GROUPS, MOE_EXPERTS_PER_GROUP, D, MOE_D_FF), D ** -0.5),
        'w_down_e': nrm((L_, MOE_GROUPS, MOE_EXPERTS_PER_GROUP, MOE_D_FF, D), MOE_D_FF ** -0.5),
        'norm_final': gain((D,)),
    }


def reference(x, mem, norm_mix, w_in, conv_w, conv_b, dt_bias, a_log, d_skip, ssd_norm,
              w_ssd_o, w_attn_o, w_out, norm_xattn, norm_mem, w_cq, w_ckv, w_co, norm_ffn,
              w_group, b_group, w_router, b_router, w_gate_e, w_up_e, w_down_e, norm_final):
    h = x
    for l in range(DEPTH):
        hn = rms_norm(h, norm_mix[l])
        h = h + hybrid_mixer(hn, w_in[l], conv_w[l], conv_b[l], dt_bias[l], a_log[l], d_skip[l],
                             ssd_norm[l], w_ssd_o[l], w_attn_o[l], w_out[l])
        hn = rms_norm(h, norm_xattn[l])
        mn = rms_norm(mem, norm_mem[l])
        h = h + cross_attention(hn, mn, w_cq[l], w_ckv[l], w_co[l])
        hn = rms_norm(h, norm_ffn[l])
        h = h + hier_moe(hn, w_group[l], b_group[l], w_router[l], b_router[l],
                         w_gate_e[l], w_up_e[l], w_down_e[l])
    return rms_norm(h, norm_final)
```

```python
import functools
import math

import jax
import jax.numpy as jnp
import numpy as np
from jax import lax
from jax.experimental import pallas as pl
from jax.experimental.pallas import tpu as pltpu

F32 = jnp.float32
BF16 = jnp.bfloat16
I32 = jnp.int32

SSD_HEAD_DIM = 64
SSD_GROUPS = 4
SSD_STATE = 128
SSD_CONV = 4
SSD_CHUNK = 128
ATTN_HEADS = 16
ATTN_KV_HEADS = 2
ATTN_HEAD_DIM = 64
Q_BLOCK = 128
TOPK_MAX = 256
IDX_HEADS = 8
IDX_DIM = 64
ROPE_THETA = 10000.0
XATTN_HEADS = 4
MOE_GROUPS = 4
MOE_EXPERTS_PER_GROUP = 8
MOE_TOPK_IN_GROUP = 2
RMS_EPS = 1e-6

LANES = 128
KEY_CHUNK = 512
INT_MIN = -2 ** 31
NEG_BIG = -1e30
VMEM_LIMIT = 56 * 1024 * 1024


def _cparams(sem):
    return pltpu.CompilerParams(dimension_semantics=sem, vmem_limit_bytes=VMEM_LIMIT)


def _split3(v):
    hi = v.astype(BF16)
    r1 = v - hi.astype(F32)
    mid = r1.astype(BF16)
    lo = (r1 - mid.astype(F32)).astype(BF16)
    return hi, mid, lo


def _dot(a, b):
    return jnp.dot(a, b, preferred_element_type=F32)


def _dot_nt(a, b):
    return lax.dot_general(a, b, (((1,), (1,)), ((), ())), preferred_element_type=F32)


def _dot3_rhs01(v, m01):
    hi, mid, lo = _split3(v)
    return _dot(hi, m01) + _dot(mid, m01) + _dot(lo, m01)


def _dot3_lhs01(m01, v):
    hi, mid, lo = _split3(v)
    return _dot(m01, hi) + _dot(m01, mid) + _dot(m01, lo)


def _rms(x, g):
    ms = jnp.mean(x * x, axis=-1, keepdims=True)
    return x * lax.rsqrt(ms + RMS_EPS) * g


def _silu(x):
    return x * jax.nn.sigmoid(x)


Q_TILE = 5
MISC_TILE = 8
N_PROJ_TILES = 9
MISC_ROPE_CHUNKS = (0, 1, 2, 3, 4, 6)
COL_QI = 16
COL_K, COL_V, COL_KI, COL_DTWI = 68, 69, 70, 71
DT_LANES = 32


def _rope_chunk(a, c, s1, s2):
    return a * c + pltpu.roll(a, 32, 1) * s1 + pltpu.roll(a, 96, 1) * s2


def _inproj_kernel(x_ref, g_ref, w_ref, c_ref, s1_ref, s2_ref, o_ref, xn_ref):
    j = pl.program_id(1)

    @pl.when(j == 0)
    def _():
        xn_ref[...] = _rms(x_ref[...], g_ref[...]).astype(BF16)

    acc = _dot(xn_ref[...], w_ref[...])
    n_chunks = acc.shape[1] // LANES

    def roped(chunks):
        c, s1, s2 = c_ref[...], s1_ref[...], s2_ref[...]
        for k in range(n_chunks):
            a = acc[:, k * LANES:(k + 1) * LANES]
            if k in chunks:
                a = _rope_chunk(a, c, s1, s2)
            o_ref[:, k * LANES:(k + 1) * LANES] = a

    @pl.when(j == Q_TILE)
    def _():
        roped(tuple(range(n_chunks)))

    @pl.when(j == MISC_TILE)
    def _():
        roped(MISC_ROPE_CHUNKS)

    @pl.when(jnp.logical_and(j != Q_TILE, j != MISC_TILE))
    def _():
        o_ref[...] = acc


def _in_projection(x2d, gain, w_bf16, rope_c, rope_s1, rope_s2, seq):
    n, d = x2d.shape
    tm = min(1024, seq)
    tn = 1024
    pos_tiles = seq // tm
    return pl.pallas_call(
        _inproj_kernel,
        grid=(n // tm, w_bf16.shape[1] // tn),
        in_specs=[
            pl.BlockSpec((tm, d), lambda i, j: (i, 0)),
            pl.BlockSpec((1, d), lambda i, j: (0, 0)),
            pl.BlockSpec((d, tn), lambda i, j: (0, j)),
            pl.BlockSpec((tm, LANES), lambda i, j: (i % pos_tiles, 0)),
            pl.BlockSpec((tm, LANES), lambda i, j: (i % pos_tiles, 0)),
            pl.BlockSpec((tm, LANES), lambda i, j: (i % pos_tiles, 0)),
        ],
        out_specs=pl.BlockSpec((tm, tn), lambda i, j: (i, j)),
        out_shape=jax.ShapeDtypeStruct((n, w_bf16.shape[1]), F32),
        scratch_shapes=[pltpu.VMEM((tm, d), BF16)],
        compiler_params=_cparams(("parallel", "arbitrary")),
        name="in_projection",
    )(x2d, gain.reshape(1, d), w_bf16, rope_c, rope_s1, rope_s2)


def _prep_w_in(w, d_inner, conv_ch, ssd_heads):
    widths = (d_inner, conv_ch, ssd_heads, ATTN_HEADS * ATTN_HEAD_DIM, ATTN_KV_HEADS * ATTN_HEAD_DIM,
              ATTN_KV_HEADS * ATTN_HEAD_DIM, IDX_HEADS * IDX_DIM, IDX_DIM, IDX_HEADS,
              w.shape[0], w.shape[0])
    splits = [int(p) for p in np.cumsum(widths)[:-1]]
    z, xbc, dt, q, k, v, qi, ki, wi, gs, ga = jnp.split(w, splits, axis=1)

    def pad(a, n):
        return jnp.pad(a, ((0, 0), (0, n - a.shape[1])))

    misc = jnp.concatenate([qi, k, v, pad(ki, LANES), pad(jnp.concatenate([dt, wi], axis=1), LANES)], axis=1)
    return jnp.concatenate([z, xbc, q, gs, ga, misc], axis=1).astype(BF16)


def _rope_tables(seq):
    half = ATTN_HEAD_DIM // 2
    inv = 1.0 / (ROPE_THETA ** (jnp.arange(0, ATTN_HEAD_DIM, 2, dtype=F32) / ATTN_HEAD_DIM))
    ang = jnp.arange(seq, dtype=F32)[:, None] * inv[None, :]
    cos, sin = jnp.cos(ang), jnp.sin(ang)
    zero = jnp.zeros_like(sin)
    reps = LANES // ATTN_HEAD_DIM
    c = jnp.tile(jnp.concatenate([cos, cos], axis=1), (1, reps))
    s1 = jnp.tile(jnp.concatenate([zero, sin], axis=1), (1, reps))
    s2 = jnp.tile(jnp.concatenate([-sin, zero], axis=1), (1, reps))
    del half
    return c, s1, s2


def _ssd_kernel(z_ref, xs_ref, bc_ref, dt_ref, cw_ref, cb_ref, dtb_ref, alog_ref, dsk_ref, ng_ref,
                r64_ref, r128_ref, tri_ref, y_ref, buf_ref, st_ref, act_ref, *, heads, groups):
    c = pl.program_id(1)
    L = SSD_CHUNK
    P = SSD_HEAD_DIM
    hpg = heads // groups
    d_inner = heads * P
    gw = hpg * P
    nst = SSD_STATE
    halo = 8

    @pl.when(c == 0)
    def _():
        buf_ref[0:halo, :] = jnp.zeros((halo, buf_ref.shape[1]), F32)
        st_ref[...] = jnp.zeros(st_ref.shape, F32)

    buf_ref[halo:halo + L, 0:d_inner] = xs_ref[...]
    buf_ref[halo:halo + L, d_inner:] = bc_ref[...]
    acc = jnp.broadcast_to(cb_ref[...], (L, buf_ref.shape[1]))
    for j in range(SSD_CONV):
        off = halo - (SSD_CONV - 1) + j
        acc = acc + cw_ref[j:j + 1, :] * buf_ref[off:off + L, :]
    buf_ref[0:halo, :] = buf_ref[L:L + halo, :]
    xbc = _silu(acc)
    xs = xbc[:, :d_inner]
    bmat = xbc[:, d_inner:d_inner + groups * nst]
    cmat = xbc[:, d_inner + groups * nst:]

    dt = jax.nn.softplus(dt_ref[...] + dtb_ref[...])
    da = dt * (-jnp.exp(alog_ref[...]))
    tri = tri_ref[...]
    acum = _dot3_lhs01(tri, da)
    act_ref[...] = acum.T
    e_acum = jnp.exp(acum)
    e_tail = jnp.exp(acum[L - 1:L, :] - acum)
    r64 = r64_ref[...]
    a128 = _dot3_rhs01(acum, r128_ref[...])
    e64 = _dot3_rhs01(e_acum, r64)
    t64 = _dot3_rhs01(e_tail, r64)
    d64 = _dot3_rhs01(dt, r64)
    xdt = xs * d64
    xdt_b = xdt.astype(BF16)
    xtl_b = (xdt * t64).astype(BF16)

    row = lax.broadcasted_iota(I32, (L, L), 0)
    col = lax.broadcasted_iota(I32, (L, L), 1)
    causal = row >= col

    for g in range(groups):
        bg = bmat[:, g * nst:(g + 1) * nst]
        cg_b = cmat[:, g * nst:(g + 1) * nst].astype(BF16)
        cb = _dot_nt(cg_b, bg.astype(BF16))
        sg = st_ref[g]
        y_off = _dot(cg_b, sg.astype(BF16))
        y_heads = []
        for hh in range(hpg):
            h = g * hpg + hh
            seg = a128[:, h * LANES:(h + 1) * LANES] - act_ref[h:h + 1, :]
            m = (cb * jnp.where(causal, jnp.exp(seg), 0.0)).astype(BF16)
            y_heads.append(_dot(m, xdt_b[:, h * P:(h + 1) * P]))
        lo, hi = g * gw, (g + 1) * gw
        y_g = jnp.concatenate(y_heads, axis=1) + y_off * e64[:, lo:hi]
        st_ref[g] = sg * e64[L - 1:L, lo:hi] + _dot(bg.T.astype(BF16), xtl_b[:, lo:hi])
        y_g = (y_g + dsk_ref[:, lo:hi] * xs[:, lo:hi]) * _silu(z_ref[:, lo:hi])
        ms = jnp.mean(y_g * y_g, axis=-1, keepdims=True)
        y_ref[:, lo:hi] = (y_g * lax.rsqrt(ms + RMS_EPS) * ng_ref[:, lo:hi]).astype(y_ref.dtype)


def _ssd_branch(proj3d, conv_w, conv_b, dt_bias, a_log, d_skip, ssd_norm):
    bsz, seq, _ = proj3d.shape
    heads = dt_bias.shape[0]
    d_inner = heads * SSD_HEAD_DIM
    groups = SSD_GROUPS
    conv_ch = conv_w.shape[1]
    bc_w = conv_ch - d_inner
    L = SSD_CHUNK

    def pad_lanes(v):
        return jnp.pad(v.astype(F32), (0, LANES - v.shape[0])).reshape(1, LANES)

    hidx = jnp.arange(LANES)[:, None]
    r64 = (hidx == (jnp.arange(d_inner) // SSD_HEAD_DIM)[None, :]).astype(BF16)
    r128 = (hidx == (jnp.arange(heads * LANES) // LANES)[None, :]).astype(BF16)
    tri = (jnp.arange(L)[:, None] >= jnp.arange(L)[None, :]).astype(BF16)
    dsk = jnp.repeat(d_skip.astype(F32), SSD_HEAD_DIM).reshape(1, d_inner)
    const = lambda shape: pl.BlockSpec(shape, lambda b, c: (0,) * len(shape))
    kern = functools.partial(_ssd_kernel, heads=heads, groups=groups)
    return pl.pallas_call(
        kern,
        grid=(bsz, seq // L),
        in_specs=[
            pl.BlockSpec((None, L, d_inner), lambda b, c: (b, c, 0)),
            pl.BlockSpec((None, L, d_inner), lambda b, c: (b, c, 1)),
            pl.BlockSpec((None, L, bc_w), lambda b, c: (b, c, 2 * d_inner // bc_w)),
            pl.BlockSpec((None, L, LANES), lambda b, c: (b, c, COL_DTWI)),
            const((SSD_CONV, conv_ch)), const((1, conv_ch)), const((1, LANES)), const((1, LANES)),
            const((1, d_inner)), const((1, d_inner)),
            const((LANES, d_inner)), const((LANES, heads * LANES)), const((L, L)),
        ],
        out_specs=pl.BlockSpec((None, L, d_inner), lambda b, c: (b, c, 0)),
        out_shape=jax.ShapeDtypeStruct((bsz, seq, d_inner), BF16),
        scratch_shapes=[
            pltpu.VMEM((L + 8, conv_ch), F32),
            pltpu.VMEM((groups, SSD_STATE, d_inner // groups), F32),
            pltpu.VMEM((LANES, L), F32),
        ],
        compiler_params=_cparams(("parallel", "arbitrary")),
        name="ssd_scan",
    )(proj3d, proj3d, proj3d, proj3d, conv_w.astype(F32), conv_b.reshape(1, conv_ch).astype(F32),
      pad_lanes(dt_bias), pad_lanes(a_log), dsk, ssd_norm.reshape(1, d_inner).astype(F32), r64, r128, tri)


def _sortable_key(x):
    x = jnp.where(x == 0.0, 0.0, x)
    b = pltpu.bitcast(x, I32)
    return b ^ (lax.shift_right_arithmetic(b, 31) & 0x7FFFFFFF)


def _dsa_kernel(q_ref, qi_ref, k_ref, v_ref, ki_ref, wi_ref, o_ref,
                kl_ref, kr_ref, vl_ref, vr_ref, kil_ref, kir_ref, qb_ref,
                keys_ref, bias_ref, thr_ref, jb_ref, *, topk):
    i = pl.program_id(1)
    QB, KC = Q_BLOCK, KEY_CHUNK
    hd = ATTN_HEAD_DIM
    n_chunks = (i * QB + QB + KC - 1) // KC
    lane = lax.broadcasted_iota(I32, (1, LANES), 1)
    left = lane < hd

    @pl.when(i == 0)
    def _():
        def fill(c, _):
            rows = pl.ds(pl.multiple_of(c * KC, KC), KC)
            kk = k_ref[rows, :]
            vv = v_ref[rows, :]
            kik = ki_ref[rows, :]
            for g in range(ATTN_KV_HEADS):
                own = left if g == 0 else jnp.logical_not(left)
                kg = jnp.where(own, kk, 0.0)
                vg = jnp.where(own, vv, 0.0)
                kg_sw = pltpu.roll(kg, hd, 1)
                vg_sw = pltpu.roll(vg, hd, 1)
                kl_ref[g, rows, :] = (kg if g == 0 else kg_sw).astype(BF16)
                kr_ref[g, rows, :] = (kg_sw if g == 0 else kg).astype(BF16)
                vl_ref[g, rows, :] = (vg if g == 0 else vg_sw).astype(BF16)
                vr_ref[g, rows, :] = (vg_sw if g == 0 else vg).astype(BF16)
            kil = jnp.where(left, kik, 0.0)
            kil_ref[rows, :] = kil.astype(BF16)
            kir_ref[rows, :] = pltpu.roll(kil, hd, 1).astype(BF16)
            return 0
        lax.fori_loop(0, k_ref.shape[0] // KC, fill, 0)

    qpos = i * QB + lax.broadcasted_iota(I32, (QB, 1), 0)
    idx_scale = (IDX_DIM ** -0.5) * (IDX_HEADS ** -0.5)
    wsc = wi_ref[...] * idx_scale
    w_cols = [jnp.broadcast_to(wsc[:, DT_LANES + h:DT_LANES + h + 1], (QB, LANES)) for h in range(IDX_HEADS)]
    qi_b = qi_ref[...].astype(BF16)

    def score_chunk(c, _):
        rows = pl.ds(pl.multiple_of(c * KC, KC), KC)
        kil = kil_ref[rows, :]
        kir = kir_ref[rows, :]
        acc = jnp.zeros((QB, KC), F32)
        for p in range(IDX_HEADS // 2):
            slab = qi_b[:, p * LANES:(p + 1) * LANES]
            for side, kside in ((0, kil), (1, kir)):
                rel = jnp.maximum(_dot_nt(slab, kside), 0.0)
                w = w_cols[2 * p + side]
                acc = acc + jnp.concatenate([w] * (KC // LANES), axis=1) * rel
        kpos = c * KC + lax.broadcasted_iota(I32, (QB, KC), 1)
        keys_ref[c] = jnp.where(kpos <= qpos, _sortable_key(acc), INT_MIN)
        return 0
    lax.fori_loop(0, n_chunks, score_chunk, 0)

    def count(pred):
        def body(c, acc):
            m = jnp.where(pred(keys_ref[c], c), 1, 0)
            for k in range(KC // LANES):
                acc = acc + m[:, k * LANES:(k + 1) * LANES]
            return acc
        acc = lax.fori_loop(0, n_chunks, body, jnp.zeros((QB, LANES), I32))
        return jnp.sum(acc, axis=1, keepdims=True)

    def bit_step(it, thr):
        cand = thr ^ lax.shift_left(jnp.int32(1), 31 - it)
        cnt = count(lambda kc, c: kc >= cand)
        return jnp.where(cnt >= topk, cand, thr)
    thr = lax.fori_loop(0, 32, bit_step, jnp.full((QB, 1), INT_MIN, I32))

    cnt_gt = count(lambda kc, c: kc > thr)
    cnt_ge = count(lambda kc, c: kc >= thr)
    need = topk - cnt_gt
    select_all = qpos < topk
    thr_ref[...] = jnp.where(select_all, INT_MIN, thr)
    jb_ref[...] = jnp.where(select_all, 0, n_chunks * KC)
    surplus = jnp.where(select_all, 0, cnt_ge - topk)

    @pl.when(jnp.max(surplus) > 0)
    def _():
        def tie_step(it, jb):
            cand = jb | lax.shift_left(jnp.int32(1), 12 - it)
            def pred(kc, c):
                kpos = c * KC + lax.broadcasted_iota(I32, (QB, KC), 1)
                return jnp.logical_and(kc == thr, kpos < cand)
            cnt = count(pred)
            return jnp.where(cnt <= need, cand, jb)
        jb = lax.fori_loop(0, 13, tie_step, jnp.zeros((QB, 1), I32))
        jb_ref[...] = jnp.where(select_all, 0, jb)

    thr_f = thr_ref[...]
    jb_f = jb_ref[...]
    def bias_chunk(c, _):
        kc = keys_ref[c]
        kpos = c * KC + lax.broadcasted_iota(I32, (QB, KC), 1)
        sel = jnp.logical_or(kc > thr_f, jnp.logical_and(kc == thr_f, kpos < jb_f))
        bias_ref[c] = jnp.where(sel, 0.0, NEG_BIG)
        return 0
    lax.fori_loop(0, n_chunks, bias_chunk, 0)

    qb_ref[...] = (q_ref[...] * (hd ** -0.5)).astype(BF16)
    heads_per_group = ATTN_HEADS // ATTN_KV_HEADS
    for p in range(ATTN_HEADS // 2):
        g = (2 * p) // heads_per_group
        slab = qb_ref[:, p * LANES:(p + 1) * LANES]

        def attend(c, carry, g=g, slab=slab):
            m_e, l_e, m_o, l_o, acc = carry
            rows = pl.ds(pl.multiple_of(c * KC, KC), KC)
            bias = bias_ref[c]
            s_e = _dot_nt(slab, kl_ref[g, rows, :]) + bias
            s_o = _dot_nt(slab, kr_ref[g, rows, :]) + bias
            mn_e = jnp.maximum(m_e, jnp.max(s_e, axis=1, keepdims=True))
            mn_o = jnp.maximum(m_o, jnp.max(s_o, axis=1, keepdims=True))
            p_e = jnp.exp(s_e - mn_e)
            p_o = jnp.exp(s_o - mn_o)
            a_e = jnp.exp(m_e - mn_e)
            a_o = jnp.exp(m_o - mn_o)
            l_e = a_e * l_e + jnp.sum(p_e, axis=1, keepdims=True)
            l_o = a_o * l_o + jnp.sum(p_o, axis=1, keepdims=True)
            pv = _dot(p_e.astype(BF16), vl_ref[g, rows, :]) + _dot(p_o.astype(BF16), vr_ref[g, rows, :])
            acc = jnp.where(left, a_e, a_o) * acc + pv
            return mn_e, l_e, mn_o, l_o, acc

        neg = jnp.full((QB, 1), NEG_BIG, F32)
        zero = jnp.zeros((QB, 1), F32)
        _, l_e, _, l_o, acc = lax.fori_loop(0, n_chunks, attend, (neg, zero, neg, zero, jnp.zeros((QB, LANES), F32)))
        o_ref[:, p * LANES:(p + 1) * LANES] = (acc / jnp.where(left, l_e, l_o)).astype(o_ref.dtype)


def _dsa_branch(proj3d):
    bsz, seq, _ = proj3d.shape
    topk = min(TOPK_MAX, seq // 4)
    width = ATTN_HEADS * ATTN_HEAD_DIM
    nkc = seq // KEY_CHUNK
    kern = functools.partial(_dsa_kernel, topk=topk)
    return pl.pallas_call(
        kern,
        grid=(bsz, seq // Q_BLOCK),
        in_specs=[
            pl.BlockSpec((None, Q_BLOCK, width), lambda b, i: (b, i, Q_TILE)),
            pl.BlockSpec((None, Q_BLOCK, IDX_HEADS * IDX_DIM), lambda b, i: (b, i, COL_QI)),
            pl.BlockSpec((None, seq, LANES), lambda b, i: (b, 0, COL_K)),
            pl.BlockSpec((None, seq, LANES), lambda b, i: (b, 0, COL_V)),
            pl.BlockSpec((None, seq, LANES), lambda b, i: (b, 0, COL_KI)),
            pl.BlockSpec((None, Q_BLOCK, LANES), lambda b, i: (b, i, COL_DTWI)),
        ],
        out_specs=pl.BlockSpec((None, Q_BLOCK, width), lambda b, i: (b, i, 0)),
        out_shape=jax.ShapeDtypeStruct((bsz, seq, width), BF16),
        scratch_shapes=[
            pltpu.VMEM((ATTN_KV_HEADS, seq, LANES), BF16),
            pltpu.VMEM((ATTN_KV_HEADS, seq, LANES), BF16),
            pltpu.VMEM((ATTN_KV_HEADS, seq, LANES), BF16),
            pltpu.VMEM((ATTN_KV_HEADS, seq, LANES), BF16),
            pltpu.VMEM((seq, LANES), BF16),
            pltpu.VMEM((seq, LANES), BF16),
            pltpu.VMEM((Q_BLOCK, width), BF16),
            pltpu.VMEM((nkc, Q_BLOCK, KEY_CHUNK), I32),
            pltpu.VMEM((nkc, Q_BLOCK, KEY_CHUNK), F32),
            pltpu.VMEM((Q_BLOCK, 1), I32),
            pltpu.VMEM((Q_BLOCK, 1), I32),
        ],
        compiler_params=_cparams(("parallel", "arbitrary")),
        name="dsa_attention",
    )(proj3d, proj3d, proj3d, proj3d, proj3d, proj3d)


def _merge_kernel(h_ref, ys_ref, ya_ref, gs_ref, ga_ref, wso_ref, wao_ref, wo_ref, o_ref):
    a = _dot(ys_ref[...], wso_ref[...])
    b = _dot(ya_ref[...], wao_ref[...])
    merged = jax.nn.sigmoid(gs_ref[...]) * a + jax.nn.sigmoid(ga_ref[...]) * b
    o_ref[...] = h_ref[...] + _dot(merged.astype(BF16), wo_ref[...])


def _merge(h2d, y_ssd, y_attn, proj2d, wso, wao, wo):
    n, d = h2d.shape
    tm = 512
    full = lambda a: pl.BlockSpec(a.shape, lambda i: (0, 0))
    return pl.pallas_call(
        _merge_kernel,
        grid=(n // tm,),
        in_specs=[
            pl.BlockSpec((tm, d), lambda i: (i, 0)),
            pl.BlockSpec((tm, y_ssd.shape[1]), lambda i: (i, 0)),
            pl.BlockSpec((tm, y_attn.shape[1]), lambda i: (i, 0)),
            pl.BlockSpec((tm, d), lambda i: (i, 6)),
            pl.BlockSpec((tm, d), lambda i: (i, 7)),
            full(wso), full(wao), full(wo),
        ],
        out_specs=pl.BlockSpec((tm, d), lambda i: (i, 0)),
        out_shape=jax.ShapeDtypeStruct((n, d), F32),
        compiler_params=_cparams(("parallel",)),
        name="merge_out",
    )(h2d, y_ssd, y_attn, proj2d, proj2d, wso, wao, wo)


def _normproj_kernel(x_ref, g_ref, w_ref, o_ref):
    o_ref[...] = _dot(_rms(x_ref[...], g_ref[...]).astype(BF16), w_ref[...]).astype(o_ref.dtype)


def _norm_project(x2d, gain, w_bf16, out_dtype):
    n, d = x2d.shape
    tm = min(512, n)
    return pl.pallas_call(
        _normproj_kernel,
        grid=(n // tm,),
        in_specs=[pl.BlockSpec((tm, d), lambda i: (i, 0)), pl.BlockSpec((1, d), lambda i: (0, 0)),
                  pl.BlockSpec(w_bf16.shape, lambda i: (0, 0))],
        out_specs=pl.BlockSpec((tm, w_bf16.shape[1]), lambda i: (i, 0)),
        out_shape=jax.ShapeDtypeStruct((n, w_bf16.shape[1]), out_dtype),
        compiler_params=_cparams(("parallel",)),
        name="mem_kv_projection",
    )(x2d, gain.reshape(1, d), w_bf16)


def _xattn_kernel(h_ref, g_ref, kv_ref, wq_ref, wo_ref, o_ref):
    d = h_ref.shape[1]
    hd = d // XATTN_HEADS
    h = h_ref[...]
    q = _dot(_rms(h, g_ref[...]).astype(BF16), wq_ref[...]) * (hd ** -0.5)
    qb = q.astype(BF16)
    outs = []
    for a in range(XATTN_HEADS):
        k = kv_ref[:, a * hd:(a + 1) * hd]
        v = kv_ref[:, d + a * hd:d + (a + 1) * hd]
        s = _dot_nt(qb[:, a * hd:(a + 1) * hd], k)
        p = jnp.exp(s - jnp.max(s, axis=1, keepdims=True))
        o = _dot(p.astype(BF16), v) / jnp.sum(p, axis=1, keepdims=True)
        outs.append(o.astype(BF16))
    o_ref[...] = h + _dot(jnp.concatenate(outs, axis=1), wo_ref[...])


def _cross_attention(h3d, gain, kv3d, wq, wo):
    bsz, seq, d = h3d.shape
    tm = min(512, seq)
    mlen = kv3d.shape[1]
    return pl.pallas_call(
        _xattn_kernel,
        grid=(bsz, seq // tm),
        in_specs=[
            pl.BlockSpec((None, tm, d), lambda b, i: (b, i, 0)),
            pl.BlockSpec((1, d), lambda b, i: (0, 0)),
            pl.BlockSpec((None, mlen, 2 * d), lambda b, i: (b, 0, 0)),
            pl.BlockSpec(wq.shape, lambda b, i: (0, 0)),
            pl.BlockSpec(wo.shape, lambda b, i: (0, 0)),
        ],
        out_specs=pl.BlockSpec((None, tm, d), lambda b, i: (b, i, 0)),
        out_shape=jax.ShapeDtypeStruct((bsz, seq, d), F32),
        compiler_params=_cparams(("parallel", "parallel")),
        name="cross_attention",
    )(h3d, gain.reshape(1, d), kv3d, wq, wo)


def _router_kernel(h_ref, g_ref, wr_ref, br_ref, hn_ref, cw_ref):
    hn = _rms(h_ref[...], g_ref[...])
    hn_ref[...] = hn.astype(BF16)
    h_hi, h_mid, h_lo = _split3(hn)
    w_hi, w_mid, w_lo = wr_ref[0], wr_ref[1], wr_ref[2]
    logits = (_dot(h_hi, w_hi) + (_dot(h_hi, w_mid) + _dot(h_mid, w_hi))
              + (_dot(h_hi, w_lo) + _dot(h_mid, w_mid) + _dot(h_lo, w_hi))) + br_ref[...]
    tm = logits.shape[0]
    lane = lax.broadcasted_iota(I32, (tm, LANES), 1)
    neg_inf = -jnp.inf
    n_e = MOE_GROUPS * MOE_EXPERTS_PER_GROUP
    is_g = jnp.logical_and(lane >= n_e, lane < n_e + MOE_GROUPS)
    gl = jnp.where(is_g, logits, neg_inf)
    g_max = jnp.max(gl, axis=1, keepdims=True)
    g_sel = jnp.min(jnp.where(gl == g_max, lane, LANES), axis=1, keepdims=True) - n_e
    g_w = 1.0 / jnp.sum(jnp.where(is_g, jnp.exp(gl - g_max), 0.0), axis=1, keepdims=True)
    in_grp = jnp.logical_and(lane >= g_sel * MOE_EXPERTS_PER_GROUP,
                             lane < (g_sel + 1) * MOE_EXPERTS_PER_GROUP)
    el = jnp.where(in_grp, logits, neg_inf)
    v1 = jnp.max(el, axis=1, keepdims=True)
    i1 = jnp.min(jnp.where(el == v1, lane, LANES), axis=1, keepdims=True)
    el2 = jnp.where(lane == i1, neg_inf, el)
    v2 = jnp.max(el2, axis=1, keepdims=True)
    i2 = jnp.min(jnp.where(el2 == v2, lane, LANES), axis=1, keepdims=True)
    e2 = jnp.exp(v2 - v1)
    w1 = g_w / (1.0 + e2)
    w2 = g_w * e2 / (1.0 + e2)
    cw_ref[...] = jnp.where(lane == i1, w1, 0.0) + jnp.where(lane == i2, w2, 0.0)


def _moe_route(h2d, gain, wr3, br):
    n, d = h2d.shape
    tm = 512
    return pl.pallas_call(
        _router_kernel,
        grid=(n // tm,),
        in_specs=[pl.BlockSpec((tm, d), lambda i: (i, 0)), pl.BlockSpec((1, d), lambda i: (0, 0)),
                  pl.BlockSpec(wr3.shape, lambda i: (0, 0, 0)), pl.BlockSpec((1, LANES), lambda i: (0, 0))],
        out_specs=[pl.BlockSpec((tm, d), lambda i: (i, 0)), pl.BlockSpec((tm, LANES), lambda i: (i, 0))],
        out_shape=[jax.ShapeDtypeStruct((n, d), BF16), jax.ShapeDtypeStruct((n, LANES), F32)],
        compiler_params=_cparams(("parallel",)),
        name="moe_router",
    )(h2d, gain.reshape(1, d), wr3, br)


def _experts_kernel(h_ref, hn_ref, cw_ref, wg_ref, wu_ref, wd_ref, fg_ref, o_ref, acc_ref, *, final_norm):
    e = pl.program_id(1)

    @pl.when(e == 0)
    def _():
        acc_ref[...] = jnp.zeros(acc_ref.shape, F32)

    hn = hn_ref[...]
    lane = lax.broadcasted_iota(I32, cw_ref.shape, 1)
    w_e = jnp.sum(jnp.where(lane == e, cw_ref[...], 0.0), axis=1, keepdims=True)
    hid = _silu(_dot(hn, wg_ref[...])) * _dot(hn, wu_ref[...]) * w_e
    acc_ref[...] += _dot(hid.astype(BF16), wd_ref[...])

    @pl.when(e == pl.num_programs(1) - 1)
    def _():
        out = h_ref[...] + acc_ref[...]
        if final_norm:
            out = _rms(out, fg_ref[...])
        o_ref[...] = out


def _moe_experts(h2d, hn, cw, wg, wu, wd, final_gain, final_norm):
    n, d = h2d.shape
    n_e, _, dff = wg.shape
    tm = 1024
    kern = functools.partial(_experts_kernel, final_norm=final_norm)
    return pl.pallas_call(
        kern,
        grid=(n // tm, n_e),
        in_specs=[
            pl.BlockSpec((tm, d), lambda i, e: (i, 0)),
            pl.BlockSpec((tm, d), lambda i, e: (i, 0)),
            pl.BlockSpec((tm, LANES), lambda i, e: (i, 0)),
            pl.BlockSpec((None, d, dff), lambda i, e: (e, 0, 0)),
            pl.BlockSpec((None, d, dff), lambda i, e: (e, 0, 0)),
            pl.BlockSpec((None, dff, d), lambda i, e: (e, 0, 0)),
            pl.BlockSpec((1, d), lambda i, e: (0, 0)),
        ],
        out_specs=pl.BlockSpec((tm, d), lambda i, e: (i, 0)),
        out_shape=jax.ShapeDtypeStruct((n, d), F32),
        scratch_shapes=[pltpu.VMEM((tm, d), F32)],
        compiler_params=_cparams(("parallel", "arbitrary")),
        name="moe_experts",
    )(h2d, hn, cw, wg, wu, wd, final_gain.reshape(1, d))


def _prep_router(w_group, b_group, w_router, b_router):
    d = w_group.shape[0]
    w = jnp.concatenate([w_router, w_group], axis=1).astype(F32)
    w = jnp.pad(w, ((0, 0), (0, LANES - w.shape[1])))
    hi = w.astype(BF16)
    r1 = w - hi.astype(F32)
    mid = r1.astype(BF16)
    lo = (r1 - mid.astype(F32)).astype(BF16)
    b = jnp.concatenate([b_router, b_group]).astype(F32)
    b = jnp.pad(b, (0, LANES - b.shape[0])).reshape(1, LANES)
    del d
    return jnp.stack([hi, mid, lo]), b


def kernel(x, mem, norm_mix, w_in, conv_w, conv_b, dt_bias, a_log, d_skip, ssd_norm, w_ssd_o, w_attn_o, w_out,
           norm_xattn, norm_mem, w_cq, w_ckv, w_co, norm_ffn, w_group, b_group, w_router, b_router,
           w_gate_e, w_up_e, w_down_e, norm_final):
    bsz, seq, d = x.shape
    depth = w_in.shape[0]
    mlen = mem.shape[1]
    heads = dt_bias.shape[1]
    d_inner = heads * SSD_HEAD_DIM
    conv_ch = conv_w.shape[2]
    n = bsz * seq
    rope_c, rope_s1, rope_s2 = _rope_tables(seq)
    h = x.reshape(n, d)
    mem2d = mem.reshape(bsz * mlen, d)
    for l in range(depth):
        w_in_l = _prep_w_in(w_in[l], d_inner, conv_ch, heads)
        proj = _in_projection(h, norm_mix[l], w_in_l, rope_c, rope_s1, rope_s2, seq)
        proj3d = proj.reshape(bsz, seq, proj.shape[1])
        y_ssd = _ssd_branch(proj3d, conv_w[l], conv_b[l], dt_bias[l], a_log[l], d_skip[l], ssd_norm[l])
        y_attn = _dsa_branch(proj3d)
        h = _merge(h, y_ssd.reshape(n, d_inner), y_attn.reshape(n, -1), proj,
                   w_ssd_o[l].astype(BF16), w_attn_o[l].astype(BF16), w_out[l].astype(BF16))
        kv = _norm_project(mem2d, norm_mem[l], w_ckv[l].astype(BF16), BF16)
        h = _cross_attention(h.reshape(bsz, seq, d), norm_xattn[l], kv.reshape(bsz, mlen, 2 * d),
                             w_cq[l].astype(BF16), w_co[l].astype(BF16)).reshape(n, d)
        wr3, br = _prep_router(w_group[l], b_group[l], w_router[l], b_router[l])
        hn, cw = _moe_route(h, norm_ffn[l], wr3, br)
        n_e = MOE_GROUPS * MOE_EXPERTS_PER_GROUP
        dff = w_gate_e.shape[-1]
        h = _moe_experts(h, hn, cw,
                         w_gate_e[l].reshape(n_e, d, dff).astype(BF16),
                         w_up_e[l].reshape(n_e, d, dff).astype(BF16),
                         w_down_e[l].reshape(n_e, dff, d).astype(BF16),
                         norm_final, final_norm=(l == depth - 1))
    return h.reshape(bsz, seq, d)
```

```python
import functools
import math

import jax
import jax.numpy as jnp
import numpy as np
from jax import lax
from jax.experimental import pallas as pl
from jax.experimental.pallas import tpu as pltpu

F32 = jnp.float32
BF16 = jnp.bfloat16
I32 = jnp.int32

SSD_HEAD_DIM = 64
SSD_GROUPS = 4
SSD_STATE = 128
SSD_CONV = 4
SSD_CHUNK = 128
ATTN_HEADS = 16
ATTN_KV_HEADS = 2
ATTN_HEAD_DIM = 64
Q_BLOCK = 128
TOPK_MAX = 256
IDX_HEADS = 8
IDX_DIM = 64
ROPE_THETA = 10000.0
XATTN_HEADS = 4
MOE_GROUPS = 4
MOE_EXPERTS_PER_GROUP = 8
MOE_TOPK_IN_GROUP = 2
RMS_EPS = 1e-6

LANES = 128
KEY_CHUNK = 512
KEY_SUB = 128
KEY_TILE = 256
INT_MIN = -2 ** 31
NEG_BIG = -1e30
VMEM_LIMIT = 56 * 1024 * 1024


def _cparams(sem):
    return pltpu.CompilerParams(dimension_semantics=sem, vmem_limit_bytes=VMEM_LIMIT)


def _split3(v):
    hi = v.astype(BF16)
    r1 = v - hi.astype(F32)
    mid = r1.astype(BF16)
    lo = (r1 - mid.astype(F32)).astype(BF16)
    return hi, mid, lo


def _dot(a, b):
    return jnp.dot(a, b, preferred_element_type=F32)


def _dot_nt(a, b):
    return lax.dot_general(a, b, (((1,), (1,)), ((), ())), preferred_element_type=F32)


def _dot3_rhs01(v, m01):
    hi, mid, lo = _split3(v)
    return _dot(hi, m01) + _dot(mid, m01) + _dot(lo, m01)


def _dot3_lhs01(m01, v):
    hi, mid, lo = _split3(v)
    return _dot(m01, hi) + _dot(m01, mid) + _dot(m01, lo)


def _rms(x, g):
    ms = jnp.mean(x * x, axis=-1, keepdims=True)
    return x * lax.rsqrt(ms + RMS_EPS) * g


def _silu(x):
    return x * jax.nn.sigmoid(x)


Q_TILE = 5
MISC_TILE = 8
N_PROJ_TILES = 9
MISC_ROPE_CHUNKS = (0, 1, 2, 3, 4, 6)
COL_QI = 16
COL_K, COL_V, COL_KI, COL_DTWI = 68, 69, 70, 71
DT_LANES = 32


def _rope_chunk(a, c, s1, s2):
    return a * c + pltpu.roll(a, 32, 1) * s1 + pltpu.roll(a, 96, 1) * s2


def _inproj_kernel(x_ref, g_ref, w_ref, c_ref, s1_ref, s2_ref, o_ref, xn_ref):
    j = pl.program_id(1)

    @pl.when(j == 0)
    def _():
        xn_ref[...] = _rms(x_ref[...], g_ref[...]).astype(BF16)

    acc = _dot(xn_ref[...], w_ref[...])
    n_chunks = acc.shape[1] // LANES

    def roped(chunks):
        c, s1, s2 = c_ref[...], s1_ref[...], s2_ref[...]
        for k in range(n_chunks):
            a = acc[:, k * LANES:(k + 1) * LANES]
            if k in chunks:
                a = _rope_chunk(a, c, s1, s2)
            o_ref[:, k * LANES:(k + 1) * LANES] = a

    @pl.when(j == Q_TILE)
    def _():
        roped(tuple(range(n_chunks)))

    @pl.when(j == MISC_TILE)
    def _():
        roped(MISC_ROPE_CHUNKS)

    @pl.when(jnp.logical_and(j != Q_TILE, j != MISC_TILE))
    def _():
        o_ref[...] = acc


def _in_projection(x2d, gain, w_bf16, rope_c, rope_s1, rope_s2, seq):
    n, d = x2d.shape
    tm = min(1024, seq)
    tn = 1024
    pos_tiles = seq // tm
    return pl.pallas_call(
        _inproj_kernel,
        grid=(n // tm, w_bf16.shape[1] // tn),
        in_specs=[
            pl.BlockSpec((tm, d), lambda i, j: (i, 0)),
            pl.BlockSpec((1, d), lambda i, j: (0, 0)),
            pl.BlockSpec((d, tn), lambda i, j: (0, j)),
            pl.BlockSpec((tm, LANES), lambda i, j: (i % pos_tiles, 0)),
            pl.BlockSpec((tm, LANES), lambda i, j: (i % pos_tiles, 0)),
            pl.BlockSpec((tm, LANES), lambda i, j: (i % pos_tiles, 0)),
        ],
        out_specs=pl.BlockSpec((tm, tn), lambda i, j: (i, j)),
        out_shape=jax.ShapeDtypeStruct((n, w_bf16.shape[1]), F32),
        scratch_shapes=[pltpu.VMEM((tm, d), BF16)],
        compiler_params=_cparams(("parallel", "arbitrary")),
        name="in_projection",
    )(x2d, gain.reshape(1, d), w_bf16, rope_c, rope_s1, rope_s2)


def _prep_w_in(w, d_inner, conv_ch, ssd_heads):
    widths = (d_inner, conv_ch, ssd_heads, ATTN_HEADS * ATTN_HEAD_DIM, ATTN_KV_HEADS * ATTN_HEAD_DIM,
              ATTN_KV_HEADS * ATTN_HEAD_DIM, IDX_HEADS * IDX_DIM, IDX_DIM, IDX_HEADS,
              w.shape[0], w.shape[0])
    splits = [int(p) for p in np.cumsum(widths)[:-1]]
    z, xbc, dt, q, k, v, qi, ki, wi, gs, ga = jnp.split(w, splits, axis=1)

    def pad(a, n):
        return jnp.pad(a, ((0, 0), (0, n - a.shape[1])))

    misc = jnp.concatenate([qi, k, v, pad(ki, LANES), pad(jnp.concatenate([dt, wi], axis=1), LANES)], axis=1)
    return jnp.concatenate([z, xbc, q, gs, ga, misc], axis=1).astype(BF16)


def _rope_tables(seq):
    half = ATTN_HEAD_DIM // 2
    inv = 1.0 / (ROPE_THETA ** (jnp.arange(0, ATTN_HEAD_DIM, 2, dtype=F32) / ATTN_HEAD_DIM))
    ang = jnp.arange(seq, dtype=F32)[:, None] * inv[None, :]
    cos, sin = jnp.cos(ang), jnp.sin(ang)
    zero = jnp.zeros_like(sin)
    reps = LANES // ATTN_HEAD_DIM
    c = jnp.tile(jnp.concatenate([cos, cos], axis=1), (1, reps))
    s1 = jnp.tile(jnp.concatenate([zero, sin], axis=1), (1, reps))
    s2 = jnp.tile(jnp.concatenate([-sin, zero], axis=1), (1, reps))
    del half
    return c, s1, s2


def _ssd_kernel(z_ref, xs_ref, bc_ref, dt_ref, cw_ref, cb_ref, dtb_ref, alog_ref, dsk_ref, ng_ref,
                r64_ref, r128_ref, tri_ref, y_ref, buf_ref, st_ref, act_ref, *, heads, groups):
    c = pl.program_id(1)
    L = SSD_CHUNK
    P = SSD_HEAD_DIM
    hpg = heads // groups
    d_inner = heads * P
    gw = hpg * P
    nst = SSD_STATE
    halo = 8

    @pl.when(c == 0)
    def _():
        buf_ref[0:halo, :] = jnp.zeros((halo, buf_ref.shape[1]), F32)
        st_ref[...] = jnp.zeros(st_ref.shape, F32)

    buf_ref[halo:halo + L, 0:d_inner] = xs_ref[...]
    buf_ref[halo:halo + L, d_inner:] = bc_ref[...]
    acc = jnp.broadcast_to(cb_ref[...], (L, buf_ref.shape[1]))
    for j in range(SSD_CONV):
        off = halo - (SSD_CONV - 1) + j
        acc = acc + cw_ref[j:j + 1, :] * buf_ref[off:off + L, :]
    buf_ref[0:halo, :] = buf_ref[L:L + halo, :]
    xbc = _silu(acc)
    xs = xbc[:, :d_inner]
    bmat = xbc[:, d_inner:d_inner + groups * nst]
    cmat = xbc[:, d_inner + groups * nst:]

    dt = jax.nn.softplus(dt_ref[...] + dtb_ref[...])
    da = dt * (-jnp.exp(alog_ref[...]))
    tri = tri_ref[...]
    acum = _dot3_lhs01(tri, da)
    act_ref[...] = acum.T
    e_acum = jnp.exp(acum)
    e_tail = jnp.exp(acum[L - 1:L, :] - acum)
    r64 = r64_ref[...]
    a128 = _dot3_rhs01(acum, r128_ref[...])
    e64 = _dot3_rhs01(e_acum, r64)
    t64 = _dot3_rhs01(e_tail, r64)
    d64 = _dot3_rhs01(dt, r64)
    xdt = xs * d64
    xdt_b = xdt.astype(BF16)
    xtl_b = (xdt * t64).astype(BF16)

    row = lax.broadcasted_iota(I32, (L, L), 0)
    col = lax.broadcasted_iota(I32, (L, L), 1)
    causal = row >= col

    for g in range(groups):
        bg = bmat[:, g * nst:(g + 1) * nst]
        cg_b = cmat[:, g * nst:(g + 1) * nst].astype(BF16)
        cb = _dot_nt(cg_b, bg.astype(BF16))
        sg = st_ref[g]
        y_off = _dot(cg_b, sg.astype(BF16))
        y_heads = []
        for hh in range(hpg):
            h = g * hpg + hh
            seg = a128[:, h * LANES:(h + 1) * LANES] - act_ref[h:h + 1, :]
            m = (cb * jnp.where(causal, jnp.exp(seg), 0.0)).astype(BF16)
            y_heads.append(_dot(m, xdt_b[:, h * P:(h + 1) * P]))
        lo, hi = g * gw, (g + 1) * gw
        y_g = jnp.concatenate(y_heads, axis=1) + y_off * e64[:, lo:hi]
        st_ref[g] = sg * e64[L - 1:L, lo:hi] + _dot(bg.T.astype(BF16), xtl_b[:, lo:hi])
        y_g = (y_g + dsk_ref[:, lo:hi] * xs[:, lo:hi]) * _silu(z_ref[:, lo:hi])
        ms = jnp.mean(y_g * y_g, axis=-1, keepdims=True)
        y_ref[:, lo:hi] = (y_g * lax.rsqrt(ms + RMS_EPS) * ng_ref[:, lo:hi]).astype(y_ref.dtype)


def _ssd_branch(proj3d, conv_w, conv_b, dt_bias, a_log, d_skip, ssd_norm):
    bsz, seq, _ = proj3d.shape
    heads = dt_bias.shape[0]
    d_inner = heads * SSD_HEAD_DIM
    groups = SSD_GROUPS
    conv_ch = conv_w.shape[1]
    bc_w = conv_ch - d_inner
    L = SSD_CHUNK

    def pad_lanes(v):
        return jnp.pad(v.astype(F32), (0, LANES - v.shape[0])).reshape(1, LANES)

    hidx = jnp.arange(LANES)[:, None]
    r64 = (hidx == (jnp.arange(d_inner) // SSD_HEAD_DIM)[None, :]).astype(BF16)
    r128 = (hidx == (jnp.arange(heads * LANES) // LANES)[None, :]).astype(BF16)
    tri = (jnp.arange(L)[:, None] >= jnp.arange(L)[None, :]).astype(BF16)
    dsk = jnp.repeat(d_skip.astype(F32), SSD_HEAD_DIM).reshape(1, d_inner)
    const = lambda shape: pl.BlockSpec(shape, lambda b, c: (0,) * len(shape))
    kern = functools.partial(_ssd_kernel, heads=heads, groups=groups)
    return pl.pallas_call(
        kern,
        grid=(bsz, seq // L),
        in_specs=[
            pl.BlockSpec((None, L, d_inner), lambda b, c: (b, c, 0)),
            pl.BlockSpec((None, L, d_inner), lambda b, c: (b, c, 1)),
            pl.BlockSpec((None, L, bc_w), lambda b, c: (b, c, 2 * d_inner // bc_w)),
            pl.BlockSpec((None, L, LANES), lambda b, c: (b, c, COL_DTWI)),
            const((SSD_CONV, conv_ch)), const((1, conv_ch)), const((1, LANES)), const((1, LANES)),
            const((1, d_inner)), const((1, d_inner)),
            const((LANES, d_inner)), const((LANES, heads * LANES)), const((L, L)),
        ],
        out_specs=pl.BlockSpec((None, L, d_inner), lambda b, c: (b, c, 0)),
        out_shape=jax.ShapeDtypeStruct((bsz, seq, d_inner), BF16),
        scratch_shapes=[
            pltpu.VMEM((L + 8, conv_ch), F32),
            pltpu.VMEM((groups, SSD_STATE, d_inner // groups), F32),
            pltpu.VMEM((LANES, L), F32),
        ],
        compiler_params=_cparams(("parallel", "arbitrary")),
        name="ssd_scan",
    )(proj3d, proj3d, proj3d, proj3d, conv_w.astype(F32), conv_b.reshape(1, conv_ch).astype(F32),
      pad_lanes(dt_bias), pad_lanes(a_log), dsk, ssd_norm.reshape(1, d_inner).astype(F32), r64, r128, tri)


def _sortable_key(x):
    x = jnp.where(x == 0.0, 0.0, x)
    b = pltpu.bitcast(x, I32)
    return b ^ (lax.shift_right_arithmetic(b, 31) & 0x7FFFFFFF)


def _dsa_kernel(q_ref, qi_ref, k_ref, v_ref, ki_ref, wi_ref, o_ref,
                kb_ref, kib_ref, vt_ref, qs_ref, qis_ref, keys_ref, s0_ref, s1_ref, m_ref, l_ref, acc_ref,
                thr_ref, jb_ref, *, topk):
    i = pl.program_id(1)
    QB, KC, KT = Q_BLOCK, KEY_CHUNK, KEY_TILE
    TPC = KC // KT
    hd = ATTN_HEAD_DIM
    seq = k_ref.shape[0]
    n_chunks = (i * QB + QB + KC - 1) // KC
    lane = lax.broadcasted_iota(I32, (1, LANES), 1)
    left = lane < hd

    @pl.when(i == 0)
    def _():
        def fill(t, _):
            rows = pl.ds(pl.multiple_of(t * KT, KT), KT)
            kb_ref[rows, :] = k_ref[rows, :].astype(BF16)
            kib_ref[rows, :] = ki_ref[rows, :].astype(BF16)
            vt_ref[t] = v_ref[rows, :].T.astype(BF16)
            return 0
        lax.fori_loop(0, seq // KT, fill, 0)

    qpos = i * QB + lax.broadcasted_iota(I32, (1, QB), 1)
    idx_scale = (IDX_DIM ** -0.5) * (IDX_HEADS ** -0.5)
    wt = (wi_ref[...] * idx_scale).T
    w_rows = [wt[DT_LANES + h:DT_LANES + h + 1, :] for h in range(IDX_HEADS)]

    heads_per_group = ATTN_HEADS // ATTN_KV_HEADS
    for p in range(ATTN_HEADS // 2):
        slab = q_ref[:, p * LANES:(p + 1) * LANES] * (hd ** -0.5)
        swapped = pltpu.roll(slab, hd, 1)
        if (2 * p) // heads_per_group == 0:
            even, odd = jnp.where(left, slab, 0.0), jnp.where(left, swapped, 0.0)
        else:
            even, odd = jnp.where(left, 0.0, swapped), jnp.where(left, 0.0, slab)
        qs_ref[2 * p] = even.astype(BF16)
        qs_ref[2 * p + 1] = odd.astype(BF16)
    for p in range(IDX_HEADS // 2):
        slab = qi_ref[:, p * LANES:(p + 1) * LANES]
        qis_ref[2 * p] = slab.astype(BF16)
        qis_ref[2 * p + 1] = pltpu.roll(slab, hd, 1).astype(BF16)

    def score_chunk(c, _):
        for sub in range(KC // KEY_SUB):
            r0 = sub * KEY_SUB
            kic = kib_ref[pl.ds(pl.multiple_of(c * KC + r0, KEY_SUB), KEY_SUB), :]
            acc = jnp.zeros((KEY_SUB, QB), F32)
            for p in range(IDX_HEADS // 2):
                rel = _dot_nt(kic, qis_ref[2 * p:2 * p + 2].reshape(2 * QB, LANES))
                acc = acc + w_rows[2 * p] * jnp.maximum(rel[:, :QB], 0.0)
                acc = acc + w_rows[2 * p + 1] * jnp.maximum(rel[:, QB:], 0.0)
            kpos = c * KC + r0 + lax.broadcasted_iota(I32, (KEY_SUB, QB), 0)
            keys_ref[c * TPC + r0 // KT, r0 % KT:r0 % KT + KEY_SUB, :] = jnp.where(
                kpos <= qpos, _sortable_key(acc), INT_MIN)
        return 0
    lax.fori_loop(0, n_chunks, score_chunk, 0)

    def count(pred):
        def body(c, acc):
            kc = keys_ref[pl.ds(c * TPC, TPC)].reshape(KC, QB)
            return acc + jnp.sum(jnp.where(pred(kc, c), 1, 0), axis=0, keepdims=True)
        return lax.fori_loop(0, n_chunks, body, jnp.zeros((1, QB), I32))

    def bit_step(it, thr):
        cand = thr ^ lax.shift_left(jnp.int32(1), 31 - it)
        cnt = count(lambda kc, c: kc >= cand)
        return jnp.where(cnt >= topk, cand, thr)
    thr = lax.fori_loop(0, 32, bit_step, jnp.full((1, QB), INT_MIN, I32))

    cnt_gt = count(lambda kc, c: kc > thr)
    cnt_ge = count(lambda kc, c: kc >= thr)
    need = topk - cnt_gt
    select_all = qpos < topk
    thr_ref[...] = jnp.where(select_all, INT_MIN, thr)
    jb_ref[...] = jnp.where(select_all, 0, n_chunks * KC)
    surplus = jnp.where(select_all, 0, cnt_ge - topk)

    @pl.when(jnp.max(surplus) > 0)
    def _():
        n_bits = seq.bit_length()
        def tie_step(it, jb):
            cand = jb | lax.shift_left(jnp.int32(1), n_bits - 1 - it)
            def pred(kc, c):
                kpos = c * KC + lax.broadcasted_iota(I32, (KC, QB), 0)
                return jnp.logical_and(kc == thr, kpos < cand)
            cnt = count(pred)
            return jnp.where(cnt <= need, cand, jb)
        jb = lax.fori_loop(0, n_bits, tie_step, jnp.zeros((1, QB), I32))
        jb_ref[...] = jnp.where(select_all, 0, jb)

    thr_f = thr_ref[...]
    jb_f = jb_ref[...]
    m_ref[...] = jnp.full(m_ref.shape, NEG_BIG, F32)
    l_ref[...] = jnp.zeros(l_ref.shape, F32)
    acc_ref[...] = jnp.zeros(acc_ref.shape, F32)

    n_tiles = n_chunks * TPC

    def raw_scores(t, dst_ref):
        kc = kb_ref[pl.ds(pl.multiple_of(t * KT, KT), KT), :]
        for p in range(ATTN_HEADS // 2):
            dst_ref[p] = _dot_nt(kc, qs_ref[2 * p:2 * p + 2].reshape(2 * QB, LANES))

    def softmax_pv(t, src_ref):
        kcv = keys_ref[t]
        kpos = t * KT + lax.broadcasted_iota(I32, (KT, QB), 0)
        sel = jnp.logical_or(kcv > thr_f, jnp.logical_and(kcv == thr_f, kpos < jb_f))
        bias = jnp.where(sel, 0.0, NEG_BIG)
        bias2 = jnp.concatenate([bias, bias], axis=1)
        for p in range(ATTN_HEADS // 2):
            g = (2 * p) // heads_per_group
            s = src_ref[p] + bias2
            m_old = m_ref[p]
            m_new = jnp.maximum(m_old, jnp.max(s, axis=0, keepdims=True))
            alpha = jnp.exp(m_old - m_new)
            pt = jnp.exp(s - m_new)
            l_ref[p] = alpha * l_ref[p] + jnp.sum(pt, axis=0, keepdims=True)
            m_ref[p] = m_new
            pv = _dot(vt_ref[t, g * hd:(g + 1) * hd, :], pt.astype(BF16))
            acc_ref[p] = alpha * acc_ref[p] + pv

    raw_scores(0, s0_ref)

    def attend(u, _):
        t = 2 * u
        raw_scores(t + 1, s1_ref)
        softmax_pv(t, s0_ref)
        raw_scores(jnp.minimum(t + 2, n_tiles - 1), s0_ref)
        softmax_pv(t + 1, s1_ref)
        return 0
    lax.fori_loop(0, n_tiles // 2, attend, 0)

    for p in range(ATTN_HEADS // 2):
        o = acc_ref[p] / l_ref[p]
        both = jnp.concatenate([o[:, :QB], o[:, QB:]], axis=0)
        o_ref[:, p * LANES:(p + 1) * LANES] = both.T.astype(o_ref.dtype)


def _dsa_branch(proj3d):
    bsz, seq, _ = proj3d.shape
    topk = min(TOPK_MAX, seq // 4)
    width = ATTN_HEADS * ATTN_HEAD_DIM
    nkt = seq // KEY_TILE
    n_pairs = ATTN_HEADS // 2
    kern = functools.partial(_dsa_kernel, topk=topk)
    return pl.pallas_call(
        kern,
        grid=(bsz, seq // Q_BLOCK),
        in_specs=[
            pl.BlockSpec((None, Q_BLOCK, width), lambda b, i: (b, i, Q_TILE)),
            pl.BlockSpec((None, Q_BLOCK, IDX_HEADS * IDX_DIM), lambda b, i: (b, i, COL_QI)),
            pl.BlockSpec((None, seq, LANES), lambda b, i: (b, 0, COL_K)),
            pl.BlockSpec((None, seq, LANES), lambda b, i: (b, 0, COL_V)),
            pl.BlockSpec((None, seq, LANES), lambda b, i: (b, 0, COL_KI)),
            pl.BlockSpec((None, Q_BLOCK, LANES), lambda b, i: (b, i, COL_DTWI)),
        ],
        out_specs=pl.BlockSpec((None, Q_BLOCK, width), lambda b, i: (b, i, 0)),
        out_shape=jax.ShapeDtypeStruct((bsz, seq, width), BF16),
        scratch_shapes=[
            pltpu.VMEM((seq, LANES), BF16),
            pltpu.VMEM((seq, LANES), BF16),
            pltpu.VMEM((nkt, LANES, KEY_TILE), BF16),
            pltpu.VMEM((ATTN_HEADS, Q_BLOCK, LANES), BF16),
            pltpu.VMEM((IDX_HEADS, Q_BLOCK, LANES), BF16),
            pltpu.VMEM((nkt, KEY_TILE, Q_BLOCK), I32),
            pltpu.VMEM((n_pairs, KEY_TILE, 2 * Q_BLOCK), F32),
            pltpu.VMEM((n_pairs, KEY_TILE, 2 * Q_BLOCK), F32),
            pltpu.VMEM((n_pairs, 1, 2 * Q_BLOCK), F32),
            pltpu.VMEM((n_pairs, 1, 2 * Q_BLOCK), F32),
            pltpu.VMEM((n_pairs, ATTN_HEAD_DIM, 2 * Q_BLOCK), F32),
            pltpu.VMEM((1, Q_BLOCK), I32),
            pltpu.VMEM((1, Q_BLOCK), I32),
        ],
        compiler_params=_cparams(("parallel", "arbitrary")),
        name="dsa_attention",
    )(proj3d, proj3d, proj3d, proj3d, proj3d, proj3d)


def _merge_kernel(h_ref, ys_ref, ya_ref, gs_ref, ga_ref, wso_ref, wao_ref, wo_ref, o_ref):
    a = _dot(ys_ref[...], wso_ref[...])
    b = _dot(ya_ref[...], wao_ref[...])
    merged = jax.nn.sigmoid(gs_ref[...]) * a + jax.nn.sigmoid(ga_ref[...]) * b
    o_ref[...] = h_ref[...] + _dot(merged.astype(BF16), wo_ref[...])


def _merge(h2d, y_ssd, y_attn, proj2d, wso, wao, wo):
    n, d = h2d.shape
    tm = 512
    full = lambda a: pl.BlockSpec(a.shape, lambda i: (0, 0))
    return pl.pallas_call(
        _merge_kernel,
        grid=(n // tm,),
        in_specs=[
            pl.BlockSpec((tm, d), lambda i: (i, 0)),
            pl.BlockSpec((tm, y_ssd.shape[1]), lambda i: (i, 0)),
            pl.BlockSpec((tm, y_attn.shape[1]), lambda i: (i, 0)),
            pl.BlockSpec((tm, d), lambda i: (i, 6)),
            pl.BlockSpec((tm, d), lambda i: (i, 7)),
            full(wso), full(wao), full(wo),
        ],
        out_specs=pl.BlockSpec((tm, d), lambda i: (i, 0)),
        out_shape=jax.ShapeDtypeStruct((n, d), F32),
        compiler_params=_cparams(("parallel",)),
        name="merge_out",
    )(h2d, y_ssd, y_attn, proj2d, proj2d, wso, wao, wo)


def _normproj_kernel(x_ref, g_ref, w_ref, o_ref):
    o_ref[...] = _dot(_rms(x_ref[...], g_ref[...]).astype(BF16), w_ref[...]).astype(o_ref.dtype)


def _norm_project(x2d, gain, w_bf16, out_dtype):
    n, d = x2d.shape
    tm = min(512, n)
    return pl.pallas_call(
        _normproj_kernel,
        grid=(n // tm,),
        in_specs=[pl.BlockSpec((tm, d), lambda i: (i, 0)), pl.BlockSpec((1, d), lambda i: (0, 0)),
                  pl.BlockSpec(w_bf16.shape, lambda i: (0, 0))],
        out_specs=pl.BlockSpec((tm, w_bf16.shape[1]), lambda i: (i, 0)),
        out_shape=jax.ShapeDtypeStruct((n, w_bf16.shape[1]), out_dtype),
        compiler_params=_cparams(("parallel",)),
        name="mem_kv_projection",
    )(x2d, gain.reshape(1, d), w_bf16)


def _xattn_kernel(h_ref, g_ref, kv_ref, wq_ref, wo_ref, o_ref):
    d = h_ref.shape[1]
    hd = d // XATTN_HEADS
    h = h_ref[...]
    q = _dot(_rms(h, g_ref[...]).astype(BF16), wq_ref[...]) * (hd ** -0.5)
    qb = q.astype(BF16)
    outs = []
    for a in range(XATTN_HEADS):
        k = kv_ref[:, a * hd:(a + 1) * hd]
        v = kv_ref[:, d + a * hd:d + (a + 1) * hd]
        s = _dot_nt(qb[:, a * hd:(a + 1) * hd], k)
        p = jnp.exp(s - jnp.max(s, axis=1, keepdims=True))
        o = _dot(p.astype(BF16), v) / jnp.sum(p, axis=1, keepdims=True)
        outs.append(o.astype(BF16))
    o_ref[...] = h + _dot(jnp.concatenate(outs, axis=1), wo_ref[...])


def _cross_attention(h3d, gain, kv3d, wq, wo):
    bsz, seq, d = h3d.shape
    tm = min(512, seq)
    mlen = kv3d.shape[1]
    return pl.pallas_call(
        _xattn_kernel,
        grid=(bsz, seq // tm),
        in_specs=[
            pl.BlockSpec((None, tm, d), lambda b, i: (b, i, 0)),
            pl.BlockSpec((1, d), lambda b, i: (0, 0)),
            pl.BlockSpec((None, mlen, 2 * d), lambda b, i: (b, 0, 0)),
            pl.BlockSpec(wq.shape, lambda b, i: (0, 0)),
            pl.BlockSpec(wo.shape, lambda b, i: (0, 0)),
        ],
        out_specs=pl.BlockSpec((None, tm, d), lambda b, i: (b, i, 0)),
        out_shape=jax.ShapeDtypeStruct((bsz, seq, d), F32),
        compiler_params=_cparams(("parallel", "parallel")),
        name="cross_attention",
    )(h3d, gain.reshape(1, d), kv3d, wq, wo)


def _router_kernel(h_ref, g_ref, wr_ref, br_ref, hn_ref, cw_ref):
    hn = _rms(h_ref[...], g_ref[...])
    hn_ref[...] = hn.astype(BF16)
    h_hi, h_mid, h_lo = _split3(hn)
    w_hi, w_mid, w_lo = wr_ref[0], wr_ref[1], wr_ref[2]
    logits = (_dot(h_hi, w_hi) + (_dot(h_hi, w_mid) + _dot(h_mid, w_hi))
              + (_dot(h_hi, w_lo) + _dot(h_mid, w_mid) + _dot(h_lo, w_hi))) + br_ref[...]
    tm = logits.shape[0]
    lane = lax.broadcasted_iota(I32, (tm, LANES), 1)
    neg_inf = -jnp.inf
    n_e = MOE_GROUPS * MOE_EXPERTS_PER_GROUP
    is_g = jnp.logical_and(lane >= n_e, lane < n_e + MOE_GROUPS)
    gl = jnp.where(is_g, logits, neg_inf)
    g_max = jnp.max(gl, axis=1, keepdims=True)
    g_sel = jnp.min(jnp.where(gl == g_max, lane, LANES), axis=1, keepdims=True) - n_e
    g_w = 1.0 / jnp.sum(jnp.where(is_g, jnp.exp(gl - g_max), 0.0), axis=1, keepdims=True)
    in_grp = jnp.logical_and(lane >= g_sel * MOE_EXPERTS_PER_GROUP,
                             lane < (g_sel + 1) * MOE_EXPERTS_PER_GROUP)
    el = jnp.where(in_grp, logits, neg_inf)
    v1 = jnp.max(el, axis=1, keepdims=True)
    i1 = jnp.min(jnp.where(el == v1, lane, LANES), axis=1, keepdims=True)
    el2 = jnp.where(lane == i1, neg_inf, el)
    v2 = jnp.max(el2, axis=1, keepdims=True)
    i2 = jnp.min(jnp.where(el2 == v2, lane, LANES), axis=1, keepdims=True)
    e2 = jnp.exp(v2 - v1)
    w1 = g_w / (1.0 + e2)
    w2 = g_w * e2 / (1.0 + e2)
    cw_ref[...] = jnp.where(lane == i1, w1, 0.0) + jnp.where(lane == i2, w2, 0.0)


def _moe_route(h2d, gain, wr3, br):
    n, d = h2d.shape
    tm = 512
    return pl.pallas_call(
        _router_kernel,
        grid=(n // tm,),
        in_specs=[pl.BlockSpec((tm, d), lambda i: (i, 0)), pl.BlockSpec((1, d), lambda i: (0, 0)),
                  pl.BlockSpec(wr3.shape, lambda i: (0, 0, 0)), pl.BlockSpec((1, LANES), lambda i: (0, 0))],
        out_specs=[pl.BlockSpec((tm, d), lambda i: (i, 0)), pl.BlockSpec((tm, LANES), lambda i: (i, 0))],
        out_shape=[jax.ShapeDtypeStruct((n, d), BF16), jax.ShapeDtypeStruct((n, LANES), F32)],
        compiler_params=_cparams(("parallel",)),
        name="moe_router",
    )(h2d, gain.reshape(1, d), wr3, br)


def _experts_kernel(h_ref, hn_ref, cw_ref, wg_ref, wu_ref, wd_ref, fg_ref, o_ref, acc_ref, *, final_norm):
    e = pl.program_id(1)

    @pl.when(e == 0)
    def _():
        acc_ref[...] = jnp.zeros(acc_ref.shape, F32)

    hn = hn_ref[...]
    lane = lax.broadcasted_iota(I32, cw_ref.shape, 1)
    w_e = jnp.sum(jnp.where(lane == e, cw_ref[...], 0.0), axis=1, keepdims=True)
    hid = _silu(_dot(hn, wg_ref[...])) * _dot(hn, wu_ref[...]) * w_e
    acc_ref[...] += _dot(hid.astype(BF16), wd_ref[...])

    @pl.when(e == pl.num_programs(1) - 1)
    def _():
        out = h_ref[...] + acc_ref[...]
        if final_norm:
            out = _rms(out, fg_ref[...])
        o_ref[...] = out


def _moe_experts(h2d, hn, cw, wg, wu, wd, final_gain, final_norm):
    n, d = h2d.shape
    n_e, _, dff = wg.shape
    tm = 1024
    kern = functools.partial(_experts_kernel, final_norm=final_norm)
    return pl.pallas_call(
        kern,
        grid=(n // tm, n_e),
        in_specs=[
            pl.BlockSpec((tm, d), lambda i, e: (i, 0)),
            pl.BlockSpec((tm, d), lambda i, e: (i, 0)),
            pl.BlockSpec((tm, LANES), lambda i, e: (i, 0)),
            pl.BlockSpec((None, d, dff), lambda i, e: (e, 0, 0)),
            pl.BlockSpec((None, d, dff), lambda i, e: (e, 0, 0)),
            pl.BlockSpec((None, dff, d), lambda i, e: (e, 0, 0)),
            pl.BlockSpec((1, d), lambda i, e: (0, 0)),
        ],
        out_specs=pl.BlockSpec((tm, d), lambda i, e: (i, 0)),
        out_shape=jax.ShapeDtypeStruct((n, d), F32),
        scratch_shapes=[pltpu.VMEM((tm, d), F32)],
        compiler_params=_cparams(("parallel", "arbitrary")),
        name="moe_experts",
    )(h2d, hn, cw, wg, wu, wd, final_gain.reshape(1, d))


def _prep_router(w_group, b_group, w_router, b_router):
    d = w_group.shape[0]
    w = jnp.concatenate([w_router, w_group], axis=1).astype(F32)
    w = jnp.pad(w, ((0, 0), (0, LANES - w.shape[1])))
    hi = w.astype(BF16)
    r1 = w - hi.astype(F32)
    mid = r1.astype(BF16)
    lo = (r1 - mid.astype(F32)).astype(BF16)
    b = jnp.concatenate([b_router, b_group]).astype(F32)
    b = jnp.pad(b, (0, LANES - b.shape[0])).reshape(1, LANES)
    del d
    return jnp.stack([hi, mid, lo]), b


def kernel(x, mem, norm_mix, w_in, conv_w, conv_b, dt_bias, a_log, d_skip, ssd_norm, w_ssd_o, w_attn_o, w_out,
           norm_xattn, norm_mem, w_cq, w_ckv, w_co, norm_ffn, w_group, b_group, w_router, b_router,
           w_gate_e, w_up_e, w_down_e, norm_final):
    bsz, seq, d = x.shape
    depth = w_in.shape[0]
    mlen = mem.shape[1]
    heads = dt_bias.shape[1]
    d_inner = heads * SSD_HEAD_DIM
    conv_ch = conv_w.shape[2]
    n = bsz * seq
    rope_c, rope_s1, rope_s2 = _rope_tables(seq)
    h = x.reshape(n, d)
    mem2d = mem.reshape(bsz * mlen, d)
    for l in range(depth):
        w_in_l = _prep_w_in(w_in[l], d_inner, conv_ch, heads)
        proj = _in_projection(h, norm_mix[l], w_in_l, rope_c, rope_s1, rope_s2, seq)
        proj3d = proj.reshape(bsz, seq, proj.shape[1])
        y_ssd = _ssd_branch(proj3d, conv_w[l], conv_b[l], dt_bias[l], a_log[l], d_skip[l], ssd_norm[l])
        y_attn = _dsa_branch(proj3d)
        h = _merge(h, y_ssd.reshape(n, d_inner), y_attn.reshape(n, -1), proj,
                   w_ssd_o[l].astype(BF16), w_attn_o[l].astype(BF16), w_out[l].astype(BF16))
        kv = _norm_project(mem2d, norm_mem[l], w_ckv[l].astype(BF16), BF16)
        h = _cross_attention(h.reshape(bsz, seq, d), norm_xattn[l], kv.reshape(bsz, mlen, 2 * d),
                             w_cq[l].astype(BF16), w_co[l].astype(BF16)).reshape(n, d)
        wr3, br = _prep_router(w_group[l], b_group[l], w_router[l], b_router[l])
        hn, cw = _moe_route(h, norm_ffn[l], wr3, br)
        n_e = MOE_GROUPS * MOE_EXPERTS_PER_GROUP
        dff = w_gate_e.shape[-1]
        h = _moe_experts(h, hn, cw,
                         w_gate_e[l].reshape(n_e, d, dff).astype(BF16),
                         w_up_e[l].reshape(n_e, d, dff).astype(BF16),
                         w_down_e[l].reshape(n_e, dff, d).astype(BF16),
                         norm_final, final_norm=(l == depth - 1))
    return h.reshape(bsz, seq, d)
```

```python
import functools
import math

import jax
import jax.numpy as jnp
import numpy as np
from jax import lax
from jax.experimental import pallas as pl
from jax.experimental.pallas import tpu as pltpu

F32 = jnp.float32
BF16 = jnp.bfloat16
I32 = jnp.int32

SSD_HEAD_DIM = 64
SSD_GROUPS = 4
SSD_STATE = 128
SSD_CONV = 4
SSD_CHUNK = 128
ATTN_HEADS = 16
ATTN_KV_HEADS = 2
ATTN_HEAD_DIM = 64
Q_BLOCK = 128
TOPK_MAX = 256
IDX_HEADS = 8
IDX_DIM = 64
ROPE_THETA = 10000.0
XATTN_HEADS = 4
MOE_GROUPS = 4
MOE_EXPERTS_PER_GROUP = 8
MOE_TOPK_IN_GROUP = 2
RMS_EPS = 1e-6

LANES = 128
KEY_CHUNK = 512
KEY_SUB = 128
KEY_TILE = 256
INT_MIN = -2 ** 31
NEG_BIG = -1e30
LOG2E = math.log2(math.e)
ONES_ROWS = 16
VMEM_LIMIT = 56 * 1024 * 1024


def _cparams(sem):
    return pltpu.CompilerParams(dimension_semantics=sem, vmem_limit_bytes=VMEM_LIMIT)


def _split3(v):
    hi = v.astype(BF16)
    r1 = v - hi.astype(F32)
    mid = r1.astype(BF16)
    lo = (r1 - mid.astype(F32)).astype(BF16)
    return hi, mid, lo


def _dot(a, b):
    return jnp.dot(a, b, preferred_element_type=F32)


def _dot_nt(a, b):
    return lax.dot_general(a, b, (((1,), (1,)), ((), ())), preferred_element_type=F32)


def _spread(v, m01, terms):
    parts = _split3(v)[:terms]
    out = _dot(parts[0], m01)
    for part in parts[1:]:
        out = out + _dot(part, m01)
    return out


def _dot3_lhs01(m01, v):
    hi, mid, lo = _split3(v)
    return _dot(m01, hi) + _dot(m01, mid) + _dot(m01, lo)


def _rms(x, g):
    ms = jnp.mean(x * x, axis=-1, keepdims=True)
    return x * lax.rsqrt(ms + RMS_EPS) * g


def _silu(x):
    return x * jax.nn.sigmoid(x)


Q_TILE = 5
MISC_TILE = 8
N_PROJ_TILES = 9
MISC_ROPE_CHUNKS = (0, 1, 2, 3, 4, 6)
COL_QI = 16
COL_K, COL_V, COL_KI, COL_DTWI = 68, 69, 70, 71
DT_LANES = 32


def _rope_chunk(a, c, s1, s2):
    return a * c + pltpu.roll(a, 32, 1) * s1 + pltpu.roll(a, 96, 1) * s2


def _inproj_kernel(x_ref, g_ref, w_ref, c_ref, s1_ref, s2_ref, o_ref, xn_ref):
    j = pl.program_id(1)

    @pl.when(j == 0)
    def _():
        xn_ref[...] = _rms(x_ref[...], g_ref[...]).astype(BF16)

    acc = _dot(xn_ref[...], w_ref[...])
    n_chunks = acc.shape[1] // LANES

    def roped(chunks):
        c, s1, s2 = c_ref[...], s1_ref[...], s2_ref[...]
        for k in range(n_chunks):
            a = acc[:, k * LANES:(k + 1) * LANES]
            if k in chunks:
                a = _rope_chunk(a, c, s1, s2)
            o_ref[:, k * LANES:(k + 1) * LANES] = a

    @pl.when(j == Q_TILE)
    def _():
        roped(tuple(range(n_chunks)))

    @pl.when(j == MISC_TILE)
    def _():
        roped(MISC_ROPE_CHUNKS)

    @pl.when(jnp.logical_and(j != Q_TILE, j != MISC_TILE))
    def _():
        o_ref[...] = acc


def _in_projection(x2d, gain, w_bf16, rope_c, rope_s1, rope_s2, seq):
    n, d = x2d.shape
    tm = min(1024, seq)
    tn = 1024
    pos_tiles = seq // tm
    return pl.pallas_call(
        _inproj_kernel,
        grid=(n // tm, w_bf16.shape[1] // tn),
        in_specs=[
            pl.BlockSpec((tm, d), lambda i, j: (i, 0)),
            pl.BlockSpec((1, d), lambda i, j: (0, 0)),
            pl.BlockSpec((d, tn), lambda i, j: (0, j)),
            pl.BlockSpec((tm, LANES), lambda i, j: (i % pos_tiles, 0)),
            pl.BlockSpec((tm, LANES), lambda i, j: (i % pos_tiles, 0)),
            pl.BlockSpec((tm, LANES), lambda i, j: (i % pos_tiles, 0)),
        ],
        out_specs=pl.BlockSpec((tm, tn), lambda i, j: (i, j)),
        out_shape=jax.ShapeDtypeStruct((n, w_bf16.shape[1]), F32),
        scratch_shapes=[pltpu.VMEM((tm, d), BF16)],
        compiler_params=_cparams(("parallel", "arbitrary")),
        name="in_projection",
    )(x2d, gain.reshape(1, d), w_bf16, rope_c, rope_s1, rope_s2)


def _prep_w_in(w, d_inner, conv_ch, ssd_heads):
    widths = (d_inner, conv_ch, ssd_heads, ATTN_HEADS * ATTN_HEAD_DIM, ATTN_KV_HEADS * ATTN_HEAD_DIM,
              ATTN_KV_HEADS * ATTN_HEAD_DIM, IDX_HEADS * IDX_DIM, IDX_DIM, IDX_HEADS,
              w.shape[0], w.shape[0])
    splits = [int(p) for p in np.cumsum(widths)[:-1]]
    z, xbc, dt, q, k, v, qi, ki, wi, gs, ga = jnp.split(w, splits, axis=1)

    def pad(a, n):
        return jnp.pad(a, ((0, 0), (0, n - a.shape[1])))

    misc = jnp.concatenate([qi, k, v, pad(ki, LANES), pad(jnp.concatenate([dt, wi], axis=1), LANES)], axis=1)
    return jnp.concatenate([z, xbc, q, gs, ga, misc], axis=1).astype(BF16)


def _rope_tables(seq):
    half = ATTN_HEAD_DIM // 2
    inv = 1.0 / (ROPE_THETA ** (jnp.arange(0, ATTN_HEAD_DIM, 2, dtype=F32) / ATTN_HEAD_DIM))
    ang = jnp.arange(seq, dtype=F32)[:, None] * inv[None, :]
    cos, sin = jnp.cos(ang), jnp.sin(ang)
    zero = jnp.zeros_like(sin)
    reps = LANES // ATTN_HEAD_DIM
    c = jnp.tile(jnp.concatenate([cos, cos], axis=1), (1, reps))
    s1 = jnp.tile(jnp.concatenate([zero, sin], axis=1), (1, reps))
    s2 = jnp.tile(jnp.concatenate([-sin, zero], axis=1), (1, reps))
    del half
    return c, s1, s2


def _ssd_kernel(z_ref, xs_ref, bc_ref, dt_ref, cw_ref, cb_ref, dtb_ref, alog_ref, dsk_ref, ng_ref,
                r64_ref, r128_ref, tri_ref, y_ref, buf_ref, st_ref, act_ref, *, heads, groups):
    c = pl.program_id(1)
    L = SSD_CHUNK
    P = SSD_HEAD_DIM
    hpg = heads // groups
    d_inner = heads * P
    gw = hpg * P
    nst = SSD_STATE
    halo = 8

    @pl.when(c == 0)
    def _():
        buf_ref[...] = jnp.zeros(buf_ref.shape, F32)
        st_ref[...] = jnp.zeros(st_ref.shape, F32)

    def conv(cur, lo, hi):
        tail = buf_ref[:, lo:hi]
        row = lax.broadcasted_iota(I32, (halo, 1), 0)
        acc = cb_ref[:, lo:hi] + cw_ref[SSD_CONV - 1:SSD_CONV, lo:hi] * cur
        for k in range(1, SSD_CONV):
            rolled = pltpu.roll(cur, k, 0)
            head = jnp.where(row < k, pltpu.roll(tail, k, 0), rolled[0:halo, :])
            shifted = jnp.concatenate([head, rolled[halo:, :]], axis=0)
            acc = acc + cw_ref[SSD_CONV - 1 - k:SSD_CONV - k, lo:hi] * shifted
        buf_ref[:, lo:hi] = cur[L - halo:, :]
        return _silu(acc)

    xs = conv(xs_ref[...], 0, d_inner)
    bc = conv(bc_ref[...], d_inner, buf_ref.shape[1])
    bmat = bc[:, :groups * nst]
    cmat = bc[:, groups * nst:]

    dt = jax.nn.softplus(dt_ref[...] + dtb_ref[...])
    da = dt * (-jnp.exp(alog_ref[...]))
    tri = tri_ref[...]
    acum = _dot3_lhs01(tri, da)
    act_ref[...] = acum.T
    e_acum = jnp.exp(acum)
    e_tail = jnp.exp(acum[L - 1:L, :] - acum)
    r64 = r64_ref[...]
    a128 = _spread(acum, r128_ref[...], 2)
    e64 = _spread(e_acum, r64, 2)
    t64 = _spread(e_tail, r64, 1)
    d64 = _spread(dt, r64, 1)
    xdt = xs * d64
    xdt_b = xdt.astype(BF16)
    xtl_b = (xdt * t64).astype(BF16)

    row = lax.broadcasted_iota(I32, (L, L), 0)
    col = lax.broadcasted_iota(I32, (L, L), 1)
    causal = row >= col

    for g in range(groups):
        bg = bmat[:, g * nst:(g + 1) * nst]
        cg_b = cmat[:, g * nst:(g + 1) * nst].astype(BF16)
        cb = _dot_nt(cg_b, bg.astype(BF16))
        sg = st_ref[g]
        y_off = _dot(cg_b, sg.astype(BF16))
        y_heads = []
        for hh in range(hpg):
            h = g * hpg + hh
            seg = a128[:, h * LANES:(h + 1) * LANES] - act_ref[h:h + 1, :]
            m = (cb * jnp.where(causal, jnp.exp(seg), 0.0)).astype(BF16)
            y_heads.append(_dot(m, xdt_b[:, h * P:(h + 1) * P]))
        lo, hi = g * gw, (g + 1) * gw
        y_g = jnp.concatenate(y_heads, axis=1) + y_off * e64[:, lo:hi]
        st_ref[g] = sg * e64[L - 1:L, lo:hi] + _dot(bg.T.astype(BF16), xtl_b[:, lo:hi])
        y_g = (y_g + dsk_ref[:, lo:hi] * xs[:, lo:hi]) * _silu(z_ref[:, lo:hi])
        ms = jnp.mean(y_g * y_g, axis=-1, keepdims=True)
        y_ref[:, lo:hi] = (y_g * lax.rsqrt(ms + RMS_EPS) * ng_ref[:, lo:hi]).astype(y_ref.dtype)


def _ssd_branch(proj3d, conv_w, conv_b, dt_bias, a_log, d_skip, ssd_norm):
    bsz, seq, _ = proj3d.shape
    heads = dt_bias.shape[0]
    d_inner = heads * SSD_HEAD_DIM
    groups = SSD_GROUPS
    conv_ch = conv_w.shape[1]
    bc_w = conv_ch - d_inner
    L = SSD_CHUNK

    def pad_lanes(v):
        return jnp.pad(v.astype(F32), (0, LANES - v.shape[0])).reshape(1, LANES)

    hidx = jnp.arange(LANES)[:, None]
    r64 = (hidx == (jnp.arange(d_inner) // SSD_HEAD_DIM)[None, :]).astype(BF16)
    r128 = (hidx == (jnp.arange(heads * LANES) // LANES)[None, :]).astype(BF16)
    tri = (jnp.arange(L)[:, None] >= jnp.arange(L)[None, :]).astype(BF16)
    dsk = jnp.repeat(d_skip.astype(F32), SSD_HEAD_DIM).reshape(1, d_inner)
    const = lambda shape: pl.BlockSpec(shape, lambda b, c: (0,) * len(shape))
    kern = functools.partial(_ssd_kernel, heads=heads, groups=groups)
    return pl.pallas_call(
        kern,
        grid=(bsz, seq // L),
        in_specs=[
            pl.BlockSpec((None, L, d_inner), lambda b, c: (b, c, 0)),
            pl.BlockSpec((None, L, d_inner), lambda b, c: (b, c, 1)),
            pl.BlockSpec((None, L, bc_w), lambda b, c: (b, c, 2 * d_inner // bc_w)),
            pl.BlockSpec((None, L, LANES), lambda b, c: (b, c, COL_DTWI)),
            const((SSD_CONV, conv_ch)), const((1, conv_ch)), const((1, LANES)), const((1, LANES)),
            const((1, d_inner)), const((1, d_inner)),
            const((LANES, d_inner)), const((LANES, heads * LANES)), const((L, L)),
        ],
        out_specs=pl.BlockSpec((None, L, d_inner), lambda b, c: (b, c, 0)),
        out_shape=jax.ShapeDtypeStruct((bsz, seq, d_inner), BF16),
        scratch_shapes=[
            pltpu.VMEM((8, conv_ch), F32),
            pltpu.VMEM((groups, SSD_STATE, d_inner // groups), F32),
            pltpu.VMEM((LANES, L), F32),
        ],
        compiler_params=_cparams(("parallel", "arbitrary")),
        name="ssd_scan",
    )(proj3d, proj3d, proj3d, proj3d, conv_w.astype(F32), conv_b.reshape(1, conv_ch).astype(F32),
      pad_lanes(dt_bias), pad_lanes(a_log), dsk, ssd_norm.reshape(1, d_inner).astype(F32), r64, r128, tri)


def _sortable_key(x):
    x = jnp.where(x == 0.0, 0.0, x)
    b = pltpu.bitcast(x, I32)
    return b ^ (lax.shift_right_arithmetic(b, 31) & 0x7FFFFFFF)


def _dsa_kernel(q_ref, qi_ref, k_ref, v_ref, ki_ref, wi_ref, o_ref,
                kb_ref, kib_ref, vt_ref, qs_ref, qis_ref, keys_ref, s0_ref, s1_ref, m_ref, l_ref, acc_ref,
                thr_ref, jb_ref, *, topk):
    i = pl.program_id(1)
    QB, KC, KT = Q_BLOCK, KEY_CHUNK, KEY_TILE
    TPC = KC // KT
    hd = ATTN_HEAD_DIM
    seq = k_ref.shape[0]
    n_chunks = (i * QB + QB + KC - 1) // KC
    lane = lax.broadcasted_iota(I32, (1, LANES), 1)
    left = lane < hd

    @pl.when(i == 0)
    def _():
        def fill(t, _):
            rows = pl.ds(pl.multiple_of(t * KT, KT), KT)
            kb_ref[rows, :] = k_ref[rows, :].astype(BF16)
            kib_ref[rows, :] = ki_ref[rows, :].astype(BF16)
            vt = v_ref[rows, :].T.astype(BF16)
            for g in range(ATTN_KV_HEADS):
                vt_ref[t, g, 0:hd, :] = vt[g * hd:(g + 1) * hd, :]
                vt_ref[t, g, hd:hd + ONES_ROWS, :] = jnp.ones((ONES_ROWS, KT), BF16)
            return 0
        lax.fori_loop(0, seq // KT, fill, 0)

    qpos = i * QB + lax.broadcasted_iota(I32, (1, QB), 1)
    idx_scale = (IDX_DIM ** -0.5) * (IDX_HEADS ** -0.5)
    wt = (wi_ref[...] * idx_scale).T
    w_rows = [wt[DT_LANES + h:DT_LANES + h + 1, :] for h in range(IDX_HEADS)]

    heads_per_group = ATTN_HEADS // ATTN_KV_HEADS
    for p in range(ATTN_HEADS // 2):
        slab = q_ref[:, p * LANES:(p + 1) * LANES] * ((hd ** -0.5) * LOG2E)
        swapped = pltpu.roll(slab, hd, 1)
        if (2 * p) // heads_per_group == 0:
            even, odd = jnp.where(left, slab, 0.0), jnp.where(left, swapped, 0.0)
        else:
            even, odd = jnp.where(left, 0.0, swapped), jnp.where(left, 0.0, slab)
        qs_ref[2 * p] = even.astype(BF16)
        qs_ref[2 * p + 1] = odd.astype(BF16)
    for p in range(IDX_HEADS // 2):
        slab = qi_ref[:, p * LANES:(p + 1) * LANES]
        qis_ref[2 * p] = slab.astype(BF16)
        qis_ref[2 * p + 1] = pltpu.roll(slab, hd, 1).astype(BF16)

    def score_chunk(c, _):
        for sub in range(KC // KEY_SUB):
            r0 = sub * KEY_SUB
            kic = kib_ref[pl.ds(pl.multiple_of(c * KC + r0, KEY_SUB), KEY_SUB), :]
            acc = jnp.zeros((KEY_SUB, QB), F32)
            for p in range(IDX_HEADS // 2):
                rel = _dot_nt(kic, qis_ref[2 * p:2 * p + 2].reshape(2 * QB, LANES))
                acc = acc + w_rows[2 * p] * jnp.maximum(rel[:, :QB], 0.0)
                acc = acc + w_rows[2 * p + 1] * jnp.maximum(rel[:, QB:], 0.0)
            kpos = c * KC + r0 + lax.broadcasted_iota(I32, (KEY_SUB, QB), 0)
            keys_ref[c * TPC + r0 // KT, r0 % KT:r0 % KT + KEY_SUB, :] = jnp.where(
                kpos <= qpos, _sortable_key(acc), INT_MIN)
        return 0
    lax.fori_loop(0, n_chunks, score_chunk, 0)

    def count(pred):
        def body(c, acc):
            kc = keys_ref[pl.ds(c * TPC, TPC)].reshape(KC, QB)
            hit = jnp.where(pred(kc, c), 1, 0)
            return acc + jnp.sum(hit.reshape(KC // 8, 8, QB), axis=0)
        acc = lax.fori_loop(0, n_chunks, body, jnp.zeros((8, QB), I32))
        return jnp.sum(acc, axis=0, keepdims=True)

    def bit_step(it, thr):
        cand = thr ^ lax.shift_left(jnp.int32(1), 31 - it)
        cnt = count(lambda kc, c: kc >= cand)
        return jnp.where(cnt >= topk, cand, thr)
    thr = lax.fori_loop(0, 32, bit_step, jnp.full((1, QB), INT_MIN, I32))

    cnt_gt = count(lambda kc, c: kc > thr)
    cnt_ge = count(lambda kc, c: kc >= thr)
    need = topk - cnt_gt
    select_all = qpos < topk
    thr_ref[...] = jnp.where(select_all, INT_MIN, thr)
    jb_ref[...] = jnp.where(select_all, 0, n_chunks * KC)
    surplus = jnp.where(select_all, 0, cnt_ge - topk)

    @pl.when(jnp.max(surplus) > 0)
    def _():
        n_bits = seq.bit_length()
        def tie_step(it, jb):
            cand = jb | lax.shift_left(jnp.int32(1), n_bits - 1 - it)
            def pred(kc, c):
                kpos = c * KC + lax.broadcasted_iota(I32, (KC, QB), 0)
                return jnp.logical_and(kc == thr, kpos < cand)
            cnt = count(pred)
            return jnp.where(cnt <= need, cand, jb)
        jb = lax.fori_loop(0, n_bits, tie_step, jnp.zeros((1, QB), I32))
        jb_ref[...] = jnp.where(select_all, 0, jb)

    thr_f = thr_ref[...]
    jb_f = jb_ref[...]
    m_ref[...] = jnp.full(m_ref.shape, NEG_BIG, F32)
    l_ref[...] = jnp.zeros(l_ref.shape, F32)
    acc_ref[...] = jnp.zeros(acc_ref.shape, F32)

    n_tiles = n_chunks * TPC

    def raw_scores(t, dst_ref):
        kc = kb_ref[pl.ds(pl.multiple_of(t * KT, KT), KT), :]
        for p in range(ATTN_HEADS // 2):
            dst_ref[p] = _dot_nt(kc, qs_ref[2 * p:2 * p + 2].reshape(2 * QB, LANES))

    def softmax_pv(t, src_ref):
        kcv = keys_ref[t]
        kpos = t * KT + lax.broadcasted_iota(I32, (KT, QB), 0)
        sel = jnp.logical_or(kcv > thr_f, jnp.logical_and(kcv == thr_f, kpos < jb_f))
        bias = jnp.where(sel, 0.0, NEG_BIG)
        bias2 = jnp.concatenate([bias, bias], axis=1)
        for p in range(ATTN_HEADS // 2):
            g = (2 * p) // heads_per_group
            s = src_ref[p] + bias2
            m_old = m_ref[p]
            m_new = jnp.maximum(m_old, jnp.max(s, axis=0, keepdims=True))
            alpha = jnp.exp2(m_old - m_new)
            pt = jnp.exp2(s - m_new).astype(BF16)
            m_ref[p] = m_new
            pv = _dot(vt_ref[t, g], pt)
            l_ref[p] = alpha * l_ref[p] + pv[hd:hd + 1, :]
            acc_ref[p] = alpha * acc_ref[p] + pv[0:hd, :]

    raw_scores(0, s0_ref)

    def attend(u, _):
        t = 2 * u
        raw_scores(t + 1, s1_ref)
        softmax_pv(t, s0_ref)
        raw_scores(jnp.minimum(t + 2, n_tiles - 1), s0_ref)
        softmax_pv(t + 1, s1_ref)
        return 0
    lax.fori_loop(0, n_tiles // 2, attend, 0)

    for p in range(ATTN_HEADS // 2):
        o = acc_ref[p] / l_ref[p]
        both = jnp.concatenate([o[:, :QB], o[:, QB:]], axis=0)
        o_ref[:, p * LANES:(p + 1) * LANES] = both.T.astype(o_ref.dtype)


def _dsa_branch(proj3d):
    bsz, seq, _ = proj3d.shape
    topk = min(TOPK_MAX, seq // 4)
    width = ATTN_HEADS * ATTN_HEAD_DIM
    nkt = seq // KEY_TILE
    n_pairs = ATTN_HEADS // 2
    kern = functools.partial(_dsa_kernel, topk=topk)
    return pl.pallas_call(
        kern,
        grid=(bsz, seq // Q_BLOCK),
        in_specs=[
            pl.BlockSpec((None, Q_BLOCK, width), lambda b, i: (b, i, Q_TILE)),
            pl.BlockSpec((None, Q_BLOCK, IDX_HEADS * IDX_DIM), lambda b, i: (b, i, COL_QI)),
            pl.BlockSpec((None, seq, LANES), lambda b, i: (b, 0, COL_K)),
            pl.BlockSpec((None, seq, LANES), lambda b, i: (b, 0, COL_V)),
            pl.BlockSpec((None, seq, LANES), lambda b, i: (b, 0, COL_KI)),
            pl.BlockSpec((None, Q_BLOCK, LANES), lambda b, i: (b, i, COL_DTWI)),
        ],
        out_specs=pl.BlockSpec((None, Q_BLOCK, width), lambda b, i: (b, i, 0)),
        out_shape=jax.ShapeDtypeStruct((bsz, seq, width), BF16),
        scratch_shapes=[
            pltpu.VMEM((seq, LANES), BF16),
            pltpu.VMEM((seq, LANES), BF16),
            pltpu.VMEM((nkt, ATTN_KV_HEADS, ATTN_HEAD_DIM + ONES_ROWS, KEY_TILE), BF16),
            pltpu.VMEM((ATTN_HEADS, Q_BLOCK, LANES), BF16),
            pltpu.VMEM((IDX_HEADS, Q_BLOCK, LANES), BF16),
            pltpu.VMEM((nkt, KEY_TILE, Q_BLOCK), I32),
            pltpu.VMEM((n_pairs, KEY_TILE, 2 * Q_BLOCK), F32),
            pltpu.VMEM((n_pairs, KEY_TILE, 2 * Q_BLOCK), F32),
            pltpu.VMEM((n_pairs, 1, 2 * Q_BLOCK), F32),
            pltpu.VMEM((n_pairs, 1, 2 * Q_BLOCK), F32),
            pltpu.VMEM((n_pairs, ATTN_HEAD_DIM, 2 * Q_BLOCK), F32),
            pltpu.VMEM((1, Q_BLOCK), I32),
            pltpu.VMEM((1, Q_BLOCK), I32),
        ],
        compiler_params=_cparams(("parallel", "arbitrary")),
        name="dsa_attention",
    )(proj3d, proj3d, proj3d, proj3d, proj3d, proj3d)


def _merge_kernel(h_ref, ys_ref, ya_ref, gs_ref, ga_ref, wso_ref, wao_ref, wo_ref, o_ref):
    a = _dot(ys_ref[...], wso_ref[...])
    b = _dot(ya_ref[...], wao_ref[...])
    merged = jax.nn.sigmoid(gs_ref[...]) * a + jax.nn.sigmoid(ga_ref[...]) * b
    o_ref[...] = h_ref[...] + _dot(merged.astype(BF16), wo_ref[...])


def _merge(h2d, y_ssd, y_attn, proj2d, wso, wao, wo):
    n, d = h2d.shape
    tm = 512
    full = lambda a: pl.BlockSpec(a.shape, lambda i: (0, 0))
    return pl.pallas_call(
        _merge_kernel,
        grid=(n // tm,),
        in_specs=[
            pl.BlockSpec((tm, d), lambda i: (i, 0)),
            pl.BlockSpec((tm, y_ssd.shape[1]), lambda i: (i, 0)),
            pl.BlockSpec((tm, y_attn.shape[1]), lambda i: (i, 0)),
            pl.BlockSpec((tm, d), lambda i: (i, 6)),
            pl.BlockSpec((tm, d), lambda i: (i, 7)),
            full(wso), full(wao), full(wo),
        ],
        out_specs=pl.BlockSpec((tm, d), lambda i: (i, 0)),
        out_shape=jax.ShapeDtypeStruct((n, d), F32),
        compiler_params=_cparams(("parallel",)),
        name="merge_out",
    )(h2d, y_ssd, y_attn, proj2d, proj2d, wso, wao, wo)


def _normproj_kernel(x_ref, g_ref, w_ref, o_ref):
    o_ref[...] = _dot(_rms(x_ref[...], g_ref[...]).astype(BF16), w_ref[...]).astype(o_ref.dtype)


def _norm_project(x2d, gain, w_bf16, out_dtype):
    n, d = x2d.shape
    tm = min(512, n)
    return pl.pallas_call(
        _normproj_kernel,
        grid=(n // tm,),
        in_specs=[pl.BlockSpec((tm, d), lambda i: (i, 0)), pl.BlockSpec((1, d), lambda i: (0, 0)),
                  pl.BlockSpec(w_bf16.shape, lambda i: (0, 0))],
        out_specs=pl.BlockSpec((tm, w_bf16.shape[1]), lambda i: (i, 0)),
        out_shape=jax.ShapeDtypeStruct((n, w_bf16.shape[1]), out_dtype),
        compiler_params=_cparams(("parallel",)),
        name="mem_kv_projection",
    )(x2d, gain.reshape(1, d), w_bf16)


def _xattn_kernel(h_ref, g_ref, kv_ref, wq_ref, wo_ref, o_ref):
    d = h_ref.shape[1]
    hd = d // XATTN_HEADS
    h = h_ref[...]
    q = _dot(_rms(h, g_ref[...]).astype(BF16), wq_ref[...]) * (hd ** -0.5)
    qb = q.astype(BF16)
    outs = []
    for a in range(XATTN_HEADS):
        k = kv_ref[:, a * hd:(a + 1) * hd]
        v = kv_ref[:, d + a * hd:d + (a + 1) * hd]
        s = _dot_nt(qb[:, a * hd:(a + 1) * hd], k)
        p = jnp.exp(s - jnp.max(s, axis=1, keepdims=True))
        o = _dot(p.astype(BF16), v) / jnp.sum(p, axis=1, keepdims=True)
        outs.append(o.astype(BF16))
    o_ref[...] = h + _dot(jnp.concatenate(outs, axis=1), wo_ref[...])


def _cross_attention(h3d, gain, kv3d, wq, wo):
    bsz, seq, d = h3d.shape
    tm = min(512, seq)
    mlen = kv3d.shape[1]
    return pl.pallas_call(
        _xattn_kernel,
        grid=(bsz, seq // tm),
        in_specs=[
            pl.BlockSpec((None, tm, d), lambda b, i: (b, i, 0)),
            pl.BlockSpec((1, d), lambda b, i: (0, 0)),
            pl.BlockSpec((None, mlen, 2 * d), lambda b, i: (b, 0, 0)),
            pl.BlockSpec(wq.shape, lambda b, i: (0, 0)),
            pl.BlockSpec(wo.shape, lambda b, i: (0, 0)),
        ],
        out_specs=pl.BlockSpec((None, tm, d), lambda b, i: (b, i, 0)),
        out_shape=jax.ShapeDtypeStruct((bsz, seq, d), F32),
        compiler_params=_cparams(("parallel", "parallel")),
        name="cross_attention",
    )(h3d, gain.reshape(1, d), kv3d, wq, wo)


def _router_kernel(h_ref, g_ref, wr_ref, br_ref, hn_ref, cw_ref):
    hn = _rms(h_ref[...], g_ref[...])
    hn_ref[...] = hn.astype(BF16)
    h_hi, h_mid, h_lo = _split3(hn)
    w_hi, w_mid, w_lo = wr_ref[0], wr_ref[1], wr_ref[2]
    logits = (_dot(h_hi, w_hi) + (_dot(h_hi, w_mid) + _dot(h_mid, w_hi))
              + (_dot(h_hi, w_lo) + _dot(h_mid, w_mid) + _dot(h_lo, w_hi))) + br_ref[...]
    tm = logits.shape[0]
    lane = lax.broadcasted_iota(I32, (tm, LANES), 1)
    neg_inf = -jnp.inf
    n_e = MOE_GROUPS * MOE_EXPERTS_PER_GROUP
    is_g = jnp.logical_and(lane >= n_e, lane < n_e + MOE_GROUPS)
    gl = jnp.where(is_g, logits, neg_inf)
    g_max = jnp.max(gl, axis=1, keepdims=True)
    g_sel = jnp.min(jnp.where(gl == g_max, lane, LANES), axis=1, keepdims=True) - n_e
    g_w = 1.0 / jnp.sum(jnp.where(is_g, jnp.exp(gl - g_max), 0.0), axis=1, keepdims=True)
    in_grp = jnp.logical_and(lane >= g_sel * MOE_EXPERTS_PER_GROUP,
                             lane < (g_sel + 1) * MOE_EXPERTS_PER_GROUP)
    el = jnp.where(in_grp, logits, neg_inf)
    v1 = jnp.max(el, axis=1, keepdims=True)
    i1 = jnp.min(jnp.where(el == v1, lane, LANES), axis=1, keepdims=True)
    el2 = jnp.where(lane == i1, neg_inf, el)
    v2 = jnp.max(el2, axis=1, keepdims=True)
    i2 = jnp.min(jnp.where(el2 == v2, lane, LANES), axis=1, keepdims=True)
    e2 = jnp.exp(v2 - v1)
    w1 = g_w / (1.0 + e2)
    w2 = g_w * e2 / (1.0 + e2)
    cw_ref[...] = jnp.where(lane == i1, w1, 0.0) + jnp.where(lane == i2, w2, 0.0)


def _moe_route(h2d, gain, wr3, br):
    n, d = h2d.shape
    tm = 512
    return pl.pallas_call(
        _router_kernel,
        grid=(n // tm,),
        in_specs=[pl.BlockSpec((tm, d), lambda i: (i, 0)), pl.BlockSpec((1, d), lambda i: (0, 0)),
                  pl.BlockSpec(wr3.shape, lambda i: (0, 0, 0)), pl.BlockSpec((1, LANES), lambda i: (0, 0))],
        out_specs=[pl.BlockSpec((tm, d), lambda i: (i, 0)), pl.BlockSpec((tm, LANES), lambda i: (i, 0))],
        out_shape=[jax.ShapeDtypeStruct((n, d), BF16), jax.ShapeDtypeStruct((n, LANES), F32)],
        compiler_params=_cparams(("parallel",)),
        name="moe_router",
    )(h2d, gain.reshape(1, d), wr3, br)


def _experts_kernel(h_ref, hn_ref, cw_ref, wg_ref, wu_ref, wd_ref, fg_ref, o_ref, acc_ref, *, final_norm):
    e = pl.program_id(1)

    @pl.when(e == 0)
    def _():
        acc_ref[...] = jnp.zeros(acc_ref.shape, F32)

    hn = hn_ref[...]
    lane = lax.broadcasted_iota(I32, cw_ref.shape, 1)
    w_e = jnp.sum(jnp.where(lane == e, cw_ref[...], 0.0), axis=1, keepdims=True)
    hid = _silu(_dot(hn, wg_ref[...])) * _dot(hn, wu_ref[...]) * w_e
    acc_ref[...] += _dot(hid.astype(BF16), wd_ref[...])

    @pl.when(e == pl.num_programs(1) - 1)
    def _():
        out = h_ref[...] + acc_ref[...]
        if final_norm:
            out = _rms(out, fg_ref[...])
        o_ref[...] = out


def _moe_experts(h2d, hn, cw, wg, wu, wd, final_gain, final_norm):
    n, d = h2d.shape
    n_e, _, dff = wg.shape
    tm = 1024
    kern = functools.partial(_experts_kernel, final_norm=final_norm)
    return pl.pallas_call(
        kern,
        grid=(n // tm, n_e),
        in_specs=[
            pl.BlockSpec((tm, d), lambda i, e: (i, 0)),
            pl.BlockSpec((tm, d), lambda i, e: (i, 0)),
            pl.BlockSpec((tm, LANES), lambda i, e: (i, 0)),
            pl.BlockSpec((None, d, dff), lambda i, e: (e, 0, 0)),
            pl.BlockSpec((None, d, dff), lambda i, e: (e, 0, 0)),
            pl.BlockSpec((None, dff, d), lambda i, e: (e, 0, 0)),
            pl.BlockSpec((1, d), lambda i, e: (0, 0)),
        ],
        out_specs=pl.BlockSpec((tm, d), lambda i, e: (i, 0)),
        out_shape=jax.ShapeDtypeStruct((n, d), F32),
        scratch_shapes=[pltpu.VMEM((tm, d), F32)],
        compiler_params=_cparams(("parallel", "arbitrary")),
        name="moe_experts",
    )(h2d, hn, cw, wg, wu, wd, final_gain.reshape(1, d))


def _prep_router(w_group, b_group, w_router, b_router):
    d = w_group.shape[0]
    w = jnp.concatenate([w_router, w_group], axis=1).astype(F32)
    w = jnp.pad(w, ((0, 0), (0, LANES - w.shape[1])))
    hi = w.astype(BF16)
    r1 = w - hi.astype(F32)
    mid = r1.astype(BF16)
    lo = (r1 - mid.astype(F32)).astype(BF16)
    b = jnp.concatenate([b_router, b_group]).astype(F32)
    b = jnp.pad(b, (0, LANES - b.shape[0])).reshape(1, LANES)
    del d
    return jnp.stack([hi, mid, lo]), b


def kernel(x, mem, norm_mix, w_in, conv_w, conv_b, dt_bias, a_log, d_skip, ssd_norm, w_ssd_o, w_attn_o, w_out,
           norm_xattn, norm_mem, w_cq, w_ckv, w_co, norm_ffn, w_group, b_group, w_router, b_router,
           w_gate_e, w_up_e, w_down_e, norm_final):
    bsz, seq, d = x.shape
    depth = w_in.shape[0]
    mlen = mem.shape[1]
    heads = dt_bias.shape[1]
    d_inner = heads * SSD_HEAD_DIM
    conv_ch = conv_w.shape[2]
    n = bsz * seq
    rope_c, rope_s1, rope_s2 = _rope_tables(seq)
    h = x.reshape(n, d)
    mem2d = mem.reshape(bsz * mlen, d)
    for l in range(depth):
        w_in_l = _prep_w_in(w_in[l], d_inner, conv_ch, heads)
        proj = _in_projection(h, norm_mix[l], w_in_l, rope_c, rope_s1, rope_s2, seq)
        proj3d = proj.reshape(bsz, seq, proj.shape[1])
        y_ssd = _ssd_branch(proj3d, conv_w[l], conv_b[l], dt_bias[l], a_log[l], d_skip[l], ssd_norm[l])
        y_attn = _dsa_branch(proj3d)
        h = _merge(h, y_ssd.reshape(n, d_inner), y_attn.reshape(n, -1), proj,
                   w_ssd_o[l].astype(BF16), w_attn_o[l].astype(BF16), w_out[l].astype(BF16))
        kv = _norm_project(mem2d, norm_mem[l], w_ckv[l].astype(BF16), BF16)
        h = _cross_attention(h.reshape(bsz, seq, d), norm_xattn[l], kv.reshape(bsz, mlen, 2 * d),
                             w_cq[l].astype(BF16), w_co[l].astype(BF16)).reshape(n, d)
        wr3, br = _prep_router(w_group[l], b_group[l], w_router[l], b_router[l])
        hn, cw = _moe_route(h, norm_ffn[l], wr3, br)
        n_e = MOE_GROUPS * MOE_EXPERTS_PER_GROUP
        dff = w_gate_e.shape[-1]
        h = _moe_experts(h, hn, cw,
                         w_gate_e[l].reshape(n_e, d, dff).astype(BF16),
                         w_up_e[l].reshape(n_e, d, dff).astype(BF16),
                         w_down_e[l].reshape(n_e, dff, d).astype(BF16),
                         norm_final, final_norm=(l == depth - 1))
    return h.reshape(bsz, seq, d)
```

```python
import functools
import math

import jax
import jax.numpy as jnp
import numpy as np
from jax import lax
from jax.experimental import pallas as pl
from jax.experimental.pallas import tpu as pltpu

F32 = jnp.float32
BF16 = jnp.bfloat16
I32 = jnp.int32

SSD_HEAD_DIM = 64
SSD_GROUPS = 4
SSD_STATE = 128
SSD_CONV = 4
SSD_CHUNK = 128
ATTN_HEADS = 16
ATTN_KV_HEADS = 2
ATTN_HEAD_DIM = 64
Q_BLOCK = 128
TOPK_MAX = 256
IDX_HEADS = 8
IDX_DIM = 64
ROPE_THETA = 10000.0
XATTN_HEADS = 4
MOE_GROUPS = 4
MOE_EXPERTS_PER_GROUP = 8
MOE_TOPK_IN_GROUP = 2
RMS_EPS = 1e-6

LANES = 128
KEY_CHUNK = 512
KEY_SUB = 128
KEY_TILE = 256
INT_MIN = -2 ** 31
NEG_BIG = -1e30
LOG2E = math.log2(math.e)
ONES_ROWS = 16
VMEM_LIMIT = 56 * 1024 * 1024


def _cparams(sem):
    return pltpu.CompilerParams(dimension_semantics=sem, vmem_limit_bytes=VMEM_LIMIT)


def _split3(v):
    hi = v.astype(BF16)
    r1 = v - hi.astype(F32)
    mid = r1.astype(BF16)
    lo = (r1 - mid.astype(F32)).astype(BF16)
    return hi, mid, lo


def _dot(a, b):
    return jnp.dot(a, b, preferred_element_type=F32)


def _dot_nt(a, b):
    return lax.dot_general(a, b, (((1,), (1,)), ((), ())), preferred_element_type=F32)


def _spread(v, m01, terms):
    parts = _split3(v)[:terms]
    out = _dot(parts[0], m01)
    for part in parts[1:]:
        out = out + _dot(part, m01)
    return out


def _dot3_lhs01(m01, v):
    hi, mid, lo = _split3(v)
    return _dot(m01, hi) + _dot(m01, mid) + _dot(m01, lo)


def _rms(x, g):
    ms = jnp.mean(x * x, axis=-1, keepdims=True)
    return x * lax.rsqrt(ms + RMS_EPS) * g


def _silu(x):
    return x * jax.nn.sigmoid(x)


Q_TILE = 5
MISC_TILE = 8
N_PROJ_TILES = 9
MISC_ROPE_CHUNKS = (0, 1, 2, 3, 4, 6)
COL_QI = 16
COL_K, COL_V, COL_KI, COL_DTWI = 68, 69, 70, 71
DT_LANES = 32


def _rope_chunk(a, c, s1, s2):
    return a * c + pltpu.roll(a, 32, 1) * s1 + pltpu.roll(a, 96, 1) * s2


def _inproj_kernel(x_ref, g_ref, w_ref, c_ref, s1_ref, s2_ref, o_ref, xn_ref):
    j = pl.program_id(1)

    @pl.when(j == 0)
    def _():
        xn_ref[...] = _rms(x_ref[...], g_ref[...]).astype(BF16)

    acc = _dot(xn_ref[...], w_ref[...])
    n_chunks = acc.shape[1] // LANES

    def roped(chunks):
        c, s1, s2 = c_ref[...], s1_ref[...], s2_ref[...]
        for k in range(n_chunks):
            a = acc[:, k * LANES:(k + 1) * LANES]
            if k in chunks:
                a = _rope_chunk(a, c, s1, s2)
            o_ref[:, k * LANES:(k + 1) * LANES] = a

    @pl.when(j == Q_TILE)
    def _():
        roped(tuple(range(n_chunks)))

    @pl.when(j == MISC_TILE)
    def _():
        roped(MISC_ROPE_CHUNKS)

    @pl.when(jnp.logical_and(j != Q_TILE, j != MISC_TILE))
    def _():
        o_ref[...] = acc


def _in_projection(x2d, gain, w_bf16, rope_c, rope_s1, rope_s2, seq):
    n, d = x2d.shape
    tm = min(1024, seq)
    tn = 1024
    pos_tiles = seq // tm
    return pl.pallas_call(
        _inproj_kernel,
        grid=(n // tm, w_bf16.shape[1] // tn),
        in_specs=[
            pl.BlockSpec((tm, d), lambda i, j: (i, 0)),
            pl.BlockSpec((1, d), lambda i, j: (0, 0)),
            pl.BlockSpec((d, tn), lambda i, j: (0, j)),
            pl.BlockSpec((tm, LANES), lambda i, j: (i % pos_tiles, 0)),
            pl.BlockSpec((tm, LANES), lambda i, j: (i % pos_tiles, 0)),
            pl.BlockSpec((tm, LANES), lambda i, j: (i % pos_tiles, 0)),
        ],
        out_specs=pl.BlockSpec((tm, tn), lambda i, j: (i, j)),
        out_shape=jax.ShapeDtypeStruct((n, w_bf16.shape[1]), F32),
        scratch_shapes=[pltpu.VMEM((tm, d), BF16)],
        compiler_params=_cparams(("parallel", "arbitrary")),
        name="in_projection",
    )(x2d, gain.reshape(1, d), w_bf16, rope_c, rope_s1, rope_s2)


def _prep_w_in(w, d_inner, conv_ch, ssd_heads):
    widths = (d_inner, conv_ch, ssd_heads, ATTN_HEADS * ATTN_HEAD_DIM, ATTN_KV_HEADS * ATTN_HEAD_DIM,
              ATTN_KV_HEADS * ATTN_HEAD_DIM, IDX_HEADS * IDX_DIM, IDX_DIM, IDX_HEADS,
              w.shape[0], w.shape[0])
    splits = [int(p) for p in np.cumsum(widths)[:-1]]
    z, xbc, dt, q, k, v, qi, ki, wi, gs, ga = jnp.split(w, splits, axis=1)

    def pad(a, n):
        return jnp.pad(a, ((0, 0), (0, n - a.shape[1])))

    misc = jnp.concatenate([qi, k, v, pad(ki, LANES), pad(jnp.concatenate([dt, wi], axis=1), LANES)], axis=1)
    return jnp.concatenate([z, xbc, q, gs, ga, misc], axis=1).astype(BF16)


def _rope_tables(seq):
    half = ATTN_HEAD_DIM // 2
    inv = 1.0 / (ROPE_THETA ** (jnp.arange(0, ATTN_HEAD_DIM, 2, dtype=F32) / ATTN_HEAD_DIM))
    ang = jnp.arange(seq, dtype=F32)[:, None] * inv[None, :]
    cos, sin = jnp.cos(ang), jnp.sin(ang)
    zero = jnp.zeros_like(sin)
    reps = LANES // ATTN_HEAD_DIM
    c = jnp.tile(jnp.concatenate([cos, cos], axis=1), (1, reps))
    s1 = jnp.tile(jnp.concatenate([zero, sin], axis=1), (1, reps))
    s2 = jnp.tile(jnp.concatenate([-sin, zero], axis=1), (1, reps))
    del half
    return c, s1, s2


def _ssd_kernel(z_ref, xs_ref, bc_ref, dt_ref, cw_ref, cb_ref, dtb_ref, alog_ref, dsk_ref, ng_ref,
                r64_ref, r128_ref, tri_ref, y_ref, buf_ref, st_ref, act_ref, *, heads, groups):
    c = pl.program_id(1)
    L = SSD_CHUNK
    P = SSD_HEAD_DIM
    hpg = heads // groups
    d_inner = heads * P
    gw = hpg * P
    nst = SSD_STATE
    halo = 8

    @pl.when(c == 0)
    def _():
        buf_ref[...] = jnp.zeros(buf_ref.shape, F32)
        st_ref[...] = jnp.zeros(st_ref.shape, F32)

    def conv(cur, lo, hi):
        tail = buf_ref[:, lo:hi]
        row = lax.broadcasted_iota(I32, (halo, 1), 0)
        acc = cb_ref[:, lo:hi] + cw_ref[SSD_CONV - 1:SSD_CONV, lo:hi] * cur
        for k in range(1, SSD_CONV):
            rolled = pltpu.roll(cur, k, 0)
            head = jnp.where(row < k, pltpu.roll(tail, k, 0), rolled[0:halo, :])
            shifted = jnp.concatenate([head, rolled[halo:, :]], axis=0)
            acc = acc + cw_ref[SSD_CONV - 1 - k:SSD_CONV - k, lo:hi] * shifted
        buf_ref[:, lo:hi] = cur[L - halo:, :]
        return _silu(acc)

    xs = conv(xs_ref[...], 0, d_inner)
    bc = conv(bc_ref[...], d_inner, buf_ref.shape[1])
    bmat = bc[:, :groups * nst]
    cmat = bc[:, groups * nst:]

    dt = jax.nn.softplus(dt_ref[...] + dtb_ref[...])
    da = dt * (-jnp.exp(alog_ref[...]))
    tri = tri_ref[...]
    acum = _dot3_lhs01(tri, da)
    act_ref[...] = acum.T
    e_acum = jnp.exp(acum)
    e_tail = jnp.exp(acum[L - 1:L, :] - acum)
    r64 = r64_ref[...]
    a128 = _spread(acum, r128_ref[...], 2)
    e64 = _spread(e_acum, r64, 2)
    t64 = _spread(e_tail, r64, 1)
    d64 = _spread(dt, r64, 1)
    xdt = xs * d64
    xdt_b = xdt.astype(BF16)
    xtl_b = (xdt * t64).astype(BF16)

    row = lax.broadcasted_iota(I32, (L, L), 0)
    col = lax.broadcasted_iota(I32, (L, L), 1)
    causal = row >= col

    for g in range(groups):
        bg = bmat[:, g * nst:(g + 1) * nst]
        cg_b = cmat[:, g * nst:(g + 1) * nst].astype(BF16)
        cb = _dot_nt(cg_b, bg.astype(BF16))
        sg = st_ref[g]
        y_off = _dot(cg_b, sg.astype(BF16))
        y_heads = []
        for hh in range(hpg):
            h = g * hpg + hh
            seg = a128[:, h * LANES:(h + 1) * LANES] - act_ref[h:h + 1, :]
            m = (cb * jnp.where(causal, jnp.exp(seg), 0.0)).astype(BF16)
            y_heads.append(_dot(m, xdt_b[:, h * P:(h + 1) * P]))
        lo, hi = g * gw, (g + 1) * gw
        y_g = jnp.concatenate(y_heads, axis=1) + y_off * e64[:, lo:hi]
        st_ref[g] = sg * e64[L - 1:L, lo:hi] + _dot(bg.T.astype(BF16), xtl_b[:, lo:hi])
        y_g = (y_g + dsk_ref[:, lo:hi] * xs[:, lo:hi]) * _silu(z_ref[:, lo:hi])
        ms = jnp.mean(y_g * y_g, axis=-1, keepdims=True)
        y_ref[:, lo:hi] = (y_g * lax.rsqrt(ms + RMS_EPS) * ng_ref[:, lo:hi]).astype(y_ref.dtype)


def _ssd_branch(proj3d, conv_w, conv_b, dt_bias, a_log, d_skip, ssd_norm):
    bsz, seq, _ = proj3d.shape
    heads = dt_bias.shape[0]
    d_inner = heads * SSD_HEAD_DIM
    groups = SSD_GROUPS
    conv_ch = conv_w.shape[1]
    bc_w = conv_ch - d_inner
    L = SSD_CHUNK

    def pad_lanes(v):
        return jnp.pad(v.astype(F32), (0, LANES - v.shape[0])).reshape(1, LANES)

    hidx = jnp.arange(LANES)[:, None]
    r64 = (hidx == (jnp.arange(d_inner) // SSD_HEAD_DIM)[None, :]).astype(BF16)
    r128 = (hidx == (jnp.arange(heads * LANES) // LANES)[None, :]).astype(BF16)
    tri = (jnp.arange(L)[:, None] >= jnp.arange(L)[None, :]).astype(BF16)
    dsk = jnp.repeat(d_skip.astype(F32), SSD_HEAD_DIM).reshape(1, d_inner)
    const = lambda shape: pl.BlockSpec(shape, lambda b, c: (0,) * len(shape))
    kern = functools.partial(_ssd_kernel, heads=heads, groups=groups)
    return pl.pallas_call(
        kern,
        grid=(bsz, seq // L),
        in_specs=[
            pl.BlockSpec((None, L, d_inner), lambda b, c: (b, c, 0)),
            pl.BlockSpec((None, L, d_inner), lambda b, c: (b, c, 1)),
            pl.BlockSpec((None, L, bc_w), lambda b, c: (b, c, 2 * d_inner // bc_w)),
            pl.BlockSpec((None, L, LANES), lambda b, c: (b, c, COL_DTWI)),
            const((SSD_CONV, conv_ch)), const((1, conv_ch)), const((1, LANES)), const((1, LANES)),
            const((1, d_inner)), const((1, d_inner)),
            const((LANES, d_inner)), const((LANES, heads * LANES)), const((L, L)),
        ],
        out_specs=pl.BlockSpec((None, L, d_inner), lambda b, c: (b, c, 0)),
        out_shape=jax.ShapeDtypeStruct((bsz, seq, d_inner), BF16),
        scratch_shapes=[
            pltpu.VMEM((8, conv_ch), F32),
            pltpu.VMEM((groups, SSD_STATE, d_inner // groups), F32),
            pltpu.VMEM((LANES, L), F32),
        ],
        compiler_params=_cparams(("parallel", "arbitrary")),
        name="ssd_scan",
    )(proj3d, proj3d, proj3d, proj3d, conv_w.astype(F32), conv_b.reshape(1, conv_ch).astype(F32),
      pad_lanes(dt_bias), pad_lanes(a_log), dsk, ssd_norm.reshape(1, d_inner).astype(F32), r64, r128, tri)


def _sortable_key(x):
    x = jnp.where(x == 0.0, 0.0, x)
    b = pltpu.bitcast(x, I32)
    return b ^ (lax.shift_right_arithmetic(b, 31) & 0x7FFFFFFF)


def _dsa_kernel(q_ref, qi_ref, k_ref, v_ref, ki_ref, wi_ref, o_ref,
                kb_ref, kib_ref, vt_ref, qs_ref, qis_ref, keys_ref, s0_ref, s1_ref, m_ref, l_ref, acc_ref,
                thr_ref, jb_ref, *, topk):
    i = pl.program_id(1)
    QB, KC, KT = Q_BLOCK, KEY_CHUNK, KEY_TILE
    TPC = KC // KT
    hd = ATTN_HEAD_DIM
    seq = k_ref.shape[0]
    n_chunks = (i * QB + QB + KC - 1) // KC
    lane = lax.broadcasted_iota(I32, (1, LANES), 1)
    left = lane < hd

    @pl.when(i == 0)
    def _():
        def fill(t, _):
            rows = pl.ds(pl.multiple_of(t * KT, KT), KT)
            kb_ref[rows, :] = k_ref[rows, :].astype(BF16)
            kib_ref[rows, :] = ki_ref[rows, :].astype(BF16)
            vt = v_ref[rows, :].T.astype(BF16)
            for g in range(ATTN_KV_HEADS):
                vt_ref[t, g, 0:hd, :] = vt[g * hd:(g + 1) * hd, :]
                vt_ref[t, g, hd:hd + ONES_ROWS, :] = jnp.ones((ONES_ROWS, KT), BF16)
            return 0
        lax.fori_loop(0, seq // KT, fill, 0)

    qpos = i * QB + lax.broadcasted_iota(I32, (1, QB), 1)
    idx_scale = (IDX_DIM ** -0.5) * (IDX_HEADS ** -0.5)
    wt = (wi_ref[...] * idx_scale).T
    w_rows = [wt[DT_LANES + h:DT_LANES + h + 1, :] for h in range(IDX_HEADS)]

    heads_per_group = ATTN_HEADS // ATTN_KV_HEADS
    for p in range(ATTN_HEADS // 2):
        slab = q_ref[:, p * LANES:(p + 1) * LANES] * ((hd ** -0.5) * LOG2E)
        swapped = pltpu.roll(slab, hd, 1)
        if (2 * p) // heads_per_group == 0:
            even, odd = jnp.where(left, slab, 0.0), jnp.where(left, swapped, 0.0)
        else:
            even, odd = jnp.where(left, 0.0, swapped), jnp.where(left, 0.0, slab)
        qs_ref[2 * p] = even.astype(BF16)
        qs_ref[2 * p + 1] = odd.astype(BF16)
    for p in range(IDX_HEADS // 2):
        slab = qi_ref[:, p * LANES:(p + 1) * LANES]
        qis_ref[2 * p] = slab.astype(BF16)
        qis_ref[2 * p + 1] = pltpu.roll(slab, hd, 1).astype(BF16)

    def score_chunk(c, _):
        for sub in range(KC // KEY_SUB):
            r0 = sub * KEY_SUB
            kic = kib_ref[pl.ds(pl.multiple_of(c * KC + r0, KEY_SUB), KEY_SUB), :]
            acc = jnp.zeros((KEY_SUB, QB), F32)
            for p in range(IDX_HEADS // 2):
                rel = _dot_nt(kic, qis_ref[2 * p:2 * p + 2].reshape(2 * QB, LANES))
                acc = acc + w_rows[2 * p] * jnp.maximum(rel[:, :QB], 0.0)
                acc = acc + w_rows[2 * p + 1] * jnp.maximum(rel[:, QB:], 0.0)
            kpos = c * KC + r0 + lax.broadcasted_iota(I32, (KEY_SUB, QB), 0)
            keys_ref[c * TPC + r0 // KT, r0 % KT:r0 % KT + KEY_SUB, :] = jnp.where(
                kpos <= qpos, _sortable_key(acc), INT_MIN)
        return 0
    lax.fori_loop(0, n_chunks, score_chunk, 0)

    def count(pred):
        def body(c, acc):
            kc = keys_ref[pl.ds(c * TPC, TPC)].reshape(KC, QB)
            hit = jnp.where(pred(kc, c), 1, 0)
            return acc + jnp.sum(hit.reshape(KC // 8, 8, QB), axis=0)
        acc = lax.fori_loop(0, n_chunks, body, jnp.zeros((8, QB), I32))
        return jnp.sum(acc, axis=0, keepdims=True)

    def bit_step(it, thr):
        cand = thr ^ lax.shift_left(jnp.int32(1), 31 - it)
        cnt = count(lambda kc, c: kc >= cand)
        return jnp.where(cnt >= topk, cand, thr)
    thr = lax.fori_loop(0, 32, bit_step, jnp.full((1, QB), INT_MIN, I32))

    cnt_gt = count(lambda kc, c: kc > thr)
    cnt_ge = count(lambda kc, c: kc >= thr)
    need = topk - cnt_gt
    select_all = qpos < topk
    thr_ref[...] = jnp.where(select_all, INT_MIN, thr)
    jb_ref[...] = jnp.where(select_all, 0, n_chunks * KC)
    surplus = jnp.where(select_all, 0, cnt_ge - topk)

    @pl.when(jnp.max(surplus) > 0)
    def _():
        n_bits = seq.bit_length()
        def tie_step(it, jb):
            cand = jb | lax.shift_left(jnp.int32(1), n_bits - 1 - it)
            def pred(kc, c):
                kpos = c * KC + lax.broadcasted_iota(I32, (KC, QB), 0)
                return jnp.logical_and(kc == thr, kpos < cand)
            cnt = count(pred)
            return jnp.where(cnt <= need, cand, jb)
        jb = lax.fori_loop(0, n_bits, tie_step, jnp.zeros((1, QB), I32))
        jb_ref[...] = jnp.where(select_all, 0, jb)

    thr_f = thr_ref[...]
    jb_f = jb_ref[...]
    m_ref[...] = jnp.full(m_ref.shape, NEG_BIG, F32)
    l_ref[...] = jnp.zeros(l_ref.shape, F32)
    acc_ref[...] = jnp.zeros(acc_ref.shape, F32)

    n_tiles = n_chunks * TPC

    def raw_scores(t, dst_ref):
        kc = kb_ref[pl.ds(pl.multiple_of(t * KT, KT), KT), :]
        for p in range(ATTN_HEADS // 2):
            dst_ref[p] = _dot_nt(kc, qs_ref[2 * p:2 * p + 2].reshape(2 * QB, LANES))

    def softmax_pv(t, src_ref):
        kcv = keys_ref[t]
        kpos = t * KT + lax.broadcasted_iota(I32, (KT, QB), 0)
        sel = jnp.logical_or(kcv > thr_f, jnp.logical_and(kcv == thr_f, kpos < jb_f))
        bias = jnp.where(sel, 0.0, NEG_BIG)
        bias2 = jnp.concatenate([bias, bias], axis=1)
        for p in range(ATTN_HEADS // 2):
            g = (2 * p) // heads_per_group
            s = src_ref[p] + bias2
            m_old = m_ref[p]
            m_new = jnp.maximum(m_old, jnp.max(s, axis=0, keepdims=True))
            alpha = jnp.exp2(m_old - m_new)
            pt = jnp.exp2(s - m_new).astype(BF16)
            m_ref[p] = m_new
            pv = _dot(vt_ref[t, g], pt)
            l_ref[p] = alpha * l_ref[p] + pv[hd:hd + 1, :]
            acc_ref[p] = alpha * acc_ref[p] + pv[0:hd, :]

    raw_scores(0, s0_ref)

    def attend(u, _):
        t = 2 * u
        raw_scores(t + 1, s1_ref)
        softmax_pv(t, s0_ref)
        raw_scores(jnp.minimum(t + 2, n_tiles - 1), s0_ref)
        softmax_pv(t + 1, s1_ref)
        return 0
    lax.fori_loop(0, n_tiles // 2, attend, 0)

    for p in range(ATTN_HEADS // 2):
        o = acc_ref[p] / l_ref[p]
        both = jnp.concatenate([o[:, :QB], o[:, QB:]], axis=0)
        o_ref[:, p * LANES:(p + 1) * LANES] = both.T.astype(o_ref.dtype)


def _dsa_branch(proj3d):
    bsz, seq, _ = proj3d.shape
    topk = min(TOPK_MAX, seq // 4)
    width = ATTN_HEADS * ATTN_HEAD_DIM
    nkt = seq // KEY_TILE
    n_pairs = ATTN_HEADS // 2
    kern = functools.partial(_dsa_kernel, topk=topk)
    return pl.pallas_call(
        kern,
        grid=(bsz, seq // Q_BLOCK),
        in_specs=[
            pl.BlockSpec((None, Q_BLOCK, width), lambda b, i: (b, i, Q_TILE)),
            pl.BlockSpec((None, Q_BLOCK, IDX_HEADS * IDX_DIM), lambda b, i: (b, i, COL_QI)),
            pl.BlockSpec((None, seq, LANES), lambda b, i: (b, 0, COL_K)),
            pl.BlockSpec((None, seq, LANES), lambda b, i: (b, 0, COL_V)),
            pl.BlockSpec((None, seq, LANES), lambda b, i: (b, 0, COL_KI)),
            pl.BlockSpec((None, Q_BLOCK, LANES), lambda b, i: (b, i, COL_DTWI)),
        ],
        out_specs=pl.BlockSpec((None, Q_BLOCK, width), lambda b, i: (b, i, 0)),
        out_shape=jax.ShapeDtypeStruct((bsz, seq, width), BF16),
        scratch_shapes=[
            pltpu.VMEM((seq, LANES), BF16),
            pltpu.VMEM((seq, LANES), BF16),
            pltpu.VMEM((nkt, ATTN_KV_HEADS, ATTN_HEAD_DIM + ONES_ROWS, KEY_TILE), BF16),
            pltpu.VMEM((ATTN_HEADS, Q_BLOCK, LANES), BF16),
            pltpu.VMEM((IDX_HEADS, Q_BLOCK, LANES), BF16),
            pltpu.VMEM((nkt, KEY_TILE, Q_BLOCK), I32),
            pltpu.VMEM((n_pairs, KEY_TILE, 2 * Q_BLOCK), F32),
            pltpu.VMEM((n_pairs, KEY_TILE, 2 * Q_BLOCK), F32),
            pltpu.VMEM((n_pairs, 1, 2 * Q_BLOCK), F32),
            pltpu.VMEM((n_pairs, 1, 2 * Q_BLOCK), F32),
            pltpu.VMEM((n_pairs, ATTN_HEAD_DIM, 2 * Q_BLOCK), F32),
            pltpu.VMEM((1, Q_BLOCK), I32),
            pltpu.VMEM((1, Q_BLOCK), I32),
        ],
        compiler_params=_cparams(("parallel", "arbitrary")),
        name="dsa_attention",
    )(proj3d, proj3d, proj3d, proj3d, proj3d, proj3d)


def _merge_kernel(h_ref, ys_ref, ya_ref, gs_ref, ga_ref, wso_ref, wao_ref, wo_ref, o_ref):
    a = _dot(ys_ref[...], wso_ref[...])
    b = _dot(ya_ref[...], wao_ref[...])
    merged = jax.nn.sigmoid(gs_ref[...]) * a + jax.nn.sigmoid(ga_ref[...]) * b
    o_ref[...] = h_ref[...] + _dot(merged.astype(BF16), wo_ref[...])


def _merge(h2d, y_ssd, y_attn, proj2d, wso, wao, wo):
    n, d = h2d.shape
    tm = 512
    full = lambda a: pl.BlockSpec(a.shape, lambda i: (0, 0))
    return pl.pallas_call(
        _merge_kernel,
        grid=(n // tm,),
        in_specs=[
            pl.BlockSpec((tm, d), lambda i: (i, 0)),
            pl.BlockSpec((tm, y_ssd.shape[1]), lambda i: (i, 0)),
            pl.BlockSpec((tm, y_attn.shape[1]), lambda i: (i, 0)),
            pl.BlockSpec((tm, d), lambda i: (i, 6)),
            pl.BlockSpec((tm, d), lambda i: (i, 7)),
            full(wso), full(wao), full(wo),
        ],
        out_specs=pl.BlockSpec((tm, d), lambda i: (i, 0)),
        out_shape=jax.ShapeDtypeStruct((n, d), F32),
        compiler_params=_cparams(("parallel",)),
        name="merge_out",
    )(h2d, y_ssd, y_attn, proj2d, proj2d, wso, wao, wo)


def _normproj_kernel(x_ref, g_ref, w_ref, o_ref):
    o_ref[...] = _dot(_rms(x_ref[...], g_ref[...]).astype(BF16), w_ref[...]).astype(o_ref.dtype)


def _norm_project(x2d, gain, w_bf16, out_dtype):
    n, d = x2d.shape
    tm = min(512, n)
    return pl.pallas_call(
        _normproj_kernel,
        grid=(n // tm,),
        in_specs=[pl.BlockSpec((tm, d), lambda i: (i, 0)), pl.BlockSpec((1, d), lambda i: (0, 0)),
                  pl.BlockSpec(w_bf16.shape, lambda i: (0, 0))],
        out_specs=pl.BlockSpec((tm, w_bf16.shape[1]), lambda i: (i, 0)),
        out_shape=jax.ShapeDtypeStruct((n, w_bf16.shape[1]), out_dtype),
        compiler_params=_cparams(("parallel",)),
        name="mem_kv_projection",
    )(x2d, gain.reshape(1, d), w_bf16)


def _xattn_kernel(h_ref, g_ref, kv_ref, wq_ref, wo_ref, o_ref):
    d = h_ref.shape[1]
    hd = d // XATTN_HEADS
    h = h_ref[...]
    q = _dot(_rms(h, g_ref[...]).astype(BF16), wq_ref[...]) * (hd ** -0.5)
    qb = q.astype(BF16)
    outs = []
    for a in range(XATTN_HEADS):
        k = kv_ref[:, a * hd:(a + 1) * hd]
        v = kv_ref[:, d + a * hd:d + (a + 1) * hd]
        s = _dot_nt(qb[:, a * hd:(a + 1) * hd], k)
        p = jnp.exp(s - jnp.max(s, axis=1, keepdims=True))
        o = _dot(p.astype(BF16), v) / jnp.sum(p, axis=1, keepdims=True)
        outs.append(o.astype(BF16))
    o_ref[...] = h + _dot(jnp.concatenate(outs, axis=1), wo_ref[...])


def _cross_attention(h3d, gain, kv3d, wq, wo):
    bsz, seq, d = h3d.shape
    tm = min(512, seq)
    mlen = kv3d.shape[1]
    return pl.pallas_call(
        _xattn_kernel,
        grid=(bsz, seq // tm),
        in_specs=[
            pl.BlockSpec((None, tm, d), lambda b, i: (b, i, 0)),
            pl.BlockSpec((1, d), lambda b, i: (0, 0)),
            pl.BlockSpec((None, mlen, 2 * d), lambda b, i: (b, 0, 0)),
            pl.BlockSpec(wq.shape, lambda b, i: (0, 0)),
            pl.BlockSpec(wo.shape, lambda b, i: (0, 0)),
        ],
        out_specs=pl.BlockSpec((None, tm, d), lambda b, i: (b, i, 0)),
        out_shape=jax.ShapeDtypeStruct((bsz, seq, d), F32),
        compiler_params=_cparams(("parallel", "parallel")),
        name="cross_attention",
    )(h3d, gain.reshape(1, d), kv3d, wq, wo)


def _router_kernel(h_ref, g_ref, wr_ref, br_ref, trit_ref, aug_ref, rank_ref, gsel_ref, cnt_ref, base_ref):
    i = pl.program_id(0)
    d = h_ref.shape[1]

    @pl.when(i == 0)
    def _():
        base_ref[...] = jnp.zeros(base_ref.shape, F32)

    hn = _rms(h_ref[...], g_ref[...])
    aug_ref[:, 0:d] = hn
    h_hi, h_mid, h_lo = _split3(hn)
    w_hi, w_mid, w_lo = wr_ref[0], wr_ref[1], wr_ref[2]
    logits = (_dot(h_hi, w_hi) + (_dot(h_hi, w_mid) + _dot(h_mid, w_hi))
              + (_dot(h_hi, w_lo) + _dot(h_mid, w_mid) + _dot(h_lo, w_hi))) + br_ref[...]
    tm = logits.shape[0]
    lane = lax.broadcasted_iota(I32, (tm, LANES), 1)
    neg_inf = -jnp.inf
    n_e = MOE_GROUPS * MOE_EXPERTS_PER_GROUP
    is_g = jnp.logical_and(lane >= n_e, lane < n_e + MOE_GROUPS)
    gl = jnp.where(is_g, logits, neg_inf)
    g_max = jnp.max(gl, axis=1, keepdims=True)
    g_sel = jnp.min(jnp.where(gl == g_max, lane, LANES), axis=1, keepdims=True) - n_e
    g_w = 1.0 / jnp.sum(jnp.where(is_g, jnp.exp(gl - g_max), 0.0), axis=1, keepdims=True)
    in_grp = jnp.logical_and(lane >= g_sel * MOE_EXPERTS_PER_GROUP,
                             lane < (g_sel + 1) * MOE_EXPERTS_PER_GROUP)
    el = jnp.where(in_grp, logits, neg_inf)
    v1 = jnp.max(el, axis=1, keepdims=True)
    i1 = jnp.min(jnp.where(el == v1, lane, LANES), axis=1, keepdims=True)
    el2 = jnp.where(lane == i1, neg_inf, el)
    v2 = jnp.max(el2, axis=1, keepdims=True)
    i2 = jnp.min(jnp.where(el2 == v2, lane, LANES), axis=1, keepdims=True)
    e2 = jnp.exp(v2 - v1)
    w1 = g_w / (1.0 + e2)
    w2 = g_w * e2 / (1.0 + e2)
    aug_ref[:, d:] = jnp.where(lane == i1, w1, 0.0) + jnp.where(lane == i2, w2, 0.0)

    onehot_t = jnp.where(lane == g_sel, 1.0, 0.0).T
    upto = _dot(onehot_t.astype(BF16), trit_ref[...]) + base_ref[...]
    grp = lax.broadcasted_iota(I32, onehot_t.shape, 0).astype(F32)
    rank_ref[...] = (jnp.sum(onehot_t * upto, axis=0, keepdims=True) - 1.0).astype(I32)
    gsel_ref[...] = jnp.sum(onehot_t * grp, axis=0, keepdims=True).astype(I32)
    base_ref[...] = upto[:, tm - 1:tm]
    cnt_ref[...] = jnp.broadcast_to(upto[:, tm - 1:tm], cnt_ref.shape)


def _moe_route(h2d, gain, wr3, br):
    n, d = h2d.shape
    tm = 512
    nt = n // tm
    trit = (jnp.arange(tm)[:, None] <= jnp.arange(tm)[None, :]).astype(BF16)
    return pl.pallas_call(
        _router_kernel,
        grid=(nt,),
        in_specs=[pl.BlockSpec((tm, d), lambda i: (i, 0)), pl.BlockSpec((1, d), lambda i: (0, 0)),
                  pl.BlockSpec(wr3.shape, lambda i: (0, 0, 0)), pl.BlockSpec((1, LANES), lambda i: (0, 0)),
                  pl.BlockSpec((tm, tm), lambda i: (0, 0))],
        out_specs=[pl.BlockSpec((tm, d + LANES), lambda i: (i, 0)),
                   pl.BlockSpec((None, 1, tm), lambda i: (i, 0, 0)),
                   pl.BlockSpec((None, 1, tm), lambda i: (i, 0, 0)),
                   pl.BlockSpec((LANES, LANES), lambda i: (0, 0))],
        out_shape=[jax.ShapeDtypeStruct((n, d + LANES), F32),
                   jax.ShapeDtypeStruct((nt, 1, tm), I32),
                   jax.ShapeDtypeStruct((nt, 1, tm), I32),
                   jax.ShapeDtypeStruct((LANES, LANES), F32)],
        scratch_shapes=[pltpu.VMEM((LANES, 1), F32)],
        compiler_params=_cparams(("arbitrary",)),
        name="moe_router",
    )(h2d, gain.reshape(1, d), wr3, br, trit)


MOE_TILE = 512
ROW_BLOCK = 256
ISSUE_UNROLL = 8


def _scatter_rows_kernel(dest_ref, src_ref, init_ref, out_ref, sem):
    del init_ref
    base = pl.program_id(0) * ROW_BLOCK

    def row_copy(r):
        return pltpu.make_async_copy(src_ref.at[pl.ds(r, 1)], out_ref.at[pl.ds(dest_ref[base + r], 1)], sem)

    def issue(r, _):
        row_copy(r).start()
        return 0
    lax.fori_loop(0, ROW_BLOCK, issue, 0, unroll=ISSUE_UNROLL)
    pltpu.make_async_copy(src_ref, out_ref.at[pl.ds(0, ROW_BLOCK)], sem).wait()


def _scatter_rows(src, dest, n_out):
    n, w = src.shape
    init = jnp.zeros((n_out, w), src.dtype)
    return pl.pallas_call(
        _scatter_rows_kernel,
        grid_spec=pltpu.PrefetchScalarGridSpec(
            num_scalar_prefetch=1,
            grid=(n // ROW_BLOCK,),
            in_specs=[pl.BlockSpec((ROW_BLOCK, w), lambda i, dest: (i, 0)),
                      pl.BlockSpec(memory_space=pl.ANY)],
            out_specs=pl.BlockSpec(memory_space=pl.ANY),
            scratch_shapes=[pltpu.SemaphoreType.DMA(())],
        ),
        out_shape=jax.ShapeDtypeStruct((n_out, w), src.dtype),
        input_output_aliases={2: 0},
        compiler_params=_cparams(("arbitrary",)),
        name="moe_sort_rows",
    )(dest, src, init)


def _sorted_experts_kernel(tile_group_ref, n_used_ref, hn_ref, cw_ref, wg_ref, wu_ref, wd_ref, o_ref):
    j = pl.program_id(0)

    @pl.when(j < n_used_ref[0])
    def _():
        hn = hn_ref[...].astype(BF16)
        first = tile_group_ref[j] * MOE_EXPERTS_PER_GROUP
        lane = lax.broadcasted_iota(I32, cw_ref.shape, 1)
        cw = cw_ref[...]
        acc = jnp.zeros(o_ref.shape, F32)
        for e in range(MOE_EXPERTS_PER_GROUP):
            w_e = jnp.sum(jnp.where(lane == first + e, cw, 0.0), axis=1, keepdims=True)
            hid = _silu(_dot(hn, wg_ref[e])) * _dot(hn, wu_ref[e]) * w_e
            acc = acc + _dot(hid.astype(BF16), wd_ref[e])
        o_ref[...] = acc

    @pl.when(j >= n_used_ref[0])
    def _():
        o_ref[...] = jnp.zeros(o_ref.shape, F32)


def _sorted_experts(aug_sorted, tile_group, n_used, wg, wu, wd):
    n_pad, wa = aug_sorted.shape
    d = wa - LANES
    _, n_e, _, dff = wg.shape
    tm = MOE_TILE
    return pl.pallas_call(
        _sorted_experts_kernel,
        grid_spec=pltpu.PrefetchScalarGridSpec(
            num_scalar_prefetch=2,
            grid=(n_pad // tm,),
            in_specs=[
                pl.BlockSpec((tm, d), lambda j, tg, nu: (j, 0)),
                pl.BlockSpec((tm, LANES), lambda j, tg, nu: (j, d // LANES)),
                pl.BlockSpec((None, n_e, d, dff), lambda j, tg, nu: (tg[j], 0, 0, 0)),
                pl.BlockSpec((None, n_e, d, dff), lambda j, tg, nu: (tg[j], 0, 0, 0)),
                pl.BlockSpec((None, n_e, dff, d), lambda j, tg, nu: (tg[j], 0, 0, 0)),
            ],
            out_specs=pl.BlockSpec((tm, d), lambda j, tg, nu: (j, 0)),
        ),
        out_shape=jax.ShapeDtypeStruct((n_pad, d), F32),
        compiler_params=_cparams(("arbitrary",)),
        name="moe_experts",
    )(tile_group, n_used, aug_sorted, aug_sorted, wg, wu, wd)


def _gather_add_kernel(dest_ref, h_ref, delta_ref, fg_ref, o_ref, buf_ref, sem, *, final_norm):
    i = pl.program_id(0)
    n_steps = pl.num_programs(0)

    def row_copy(step, r):
        slot = lax.rem(step, 2)
        return pltpu.make_async_copy(delta_ref.at[pl.ds(dest_ref[step * ROW_BLOCK + r], 1)],
                                     buf_ref.at[slot, pl.ds(r, 1)], sem.at[slot])

    def issue_all(step):
        def issue(r, _):
            row_copy(step, r).start()
            return 0
        lax.fori_loop(0, ROW_BLOCK, issue, 0, unroll=ISSUE_UNROLL)

    @pl.when(i == 0)
    def _():
        issue_all(0)

    @pl.when(i + 1 < n_steps)
    def _():
        issue_all(i + 1)

    slot = lax.rem(i, 2)
    pltpu.make_async_copy(delta_ref.at[pl.ds(0, ROW_BLOCK)], buf_ref.at[slot], sem.at[slot]).wait()

    out = h_ref[...] + buf_ref[slot]
    if final_norm:
        out = _rms(out, fg_ref[...])
    o_ref[...] = out


def _gather_add(h2d, delta_sorted, dest, final_gain, final_norm):
    n, d = h2d.shape
    kern = functools.partial(_gather_add_kernel, final_norm=final_norm)
    return pl.pallas_call(
        kern,
        grid_spec=pltpu.PrefetchScalarGridSpec(
            num_scalar_prefetch=1,
            grid=(n // ROW_BLOCK,),
            in_specs=[pl.BlockSpec((ROW_BLOCK, d), lambda i, dest: (i, 0)),
                      pl.BlockSpec(memory_space=pl.ANY),
                      pl.BlockSpec((1, d), lambda i, dest: (0, 0))],
            out_specs=pl.BlockSpec((ROW_BLOCK, d), lambda i, dest: (i, 0)),
            scratch_shapes=[pltpu.VMEM((2, ROW_BLOCK, d), F32), pltpu.SemaphoreType.DMA((2,))],
        ),
        out_shape=jax.ShapeDtypeStruct((n, d), F32),
        compiler_params=_cparams(("arbitrary",)),
        name="moe_unsort_add",
    )(dest, h2d, delta_sorted, final_gain.reshape(1, d))


def _moe_layout(rank, gsel, counts, n):
    tm = MOE_TILE
    cnt = counts[:MOE_GROUPS, 0].astype(I32)
    seg = ((cnt + tm - 1) // tm) * tm
    ends = jnp.cumsum(seg)
    dest = (ends - seg)[gsel.reshape(n)] + rank.reshape(n)
    n_tiles = n // tm + MOE_GROUPS
    starts = jnp.arange(n_tiles, dtype=I32) * tm
    tile_group = jnp.minimum(jnp.sum(starts[:, None] >= ends[None, :], axis=1), MOE_GROUPS - 1).astype(I32)
    n_used = (ends[-1:] // tm).astype(I32)
    return dest.astype(I32), tile_group, n_used, n_tiles * tm


def _prep_router(w_group, b_group, w_router, b_router):
    d = w_group.shape[0]
    w = jnp.concatenate([w_router, w_group], axis=1).astype(F32)
    w = jnp.pad(w, ((0, 0), (0, LANES - w.shape[1])))
    hi = w.astype(BF16)
    r1 = w - hi.astype(F32)
    mid = r1.astype(BF16)
    lo = (r1 - mid.astype(F32)).astype(BF16)
    b = jnp.concatenate([b_router, b_group]).astype(F32)
    b = jnp.pad(b, (0, LANES - b.shape[0])).reshape(1, LANES)
    del d
    return jnp.stack([hi, mid, lo]), b


def kernel(x, mem, norm_mix, w_in, conv_w, conv_b, dt_bias, a_log, d_skip, ssd_norm, w_ssd_o, w_attn_o, w_out,
           norm_xattn, norm_mem, w_cq, w_ckv, w_co, norm_ffn, w_group, b_group, w_router, b_router,
           w_gate_e, w_up_e, w_down_e, norm_final):
    bsz, seq, d = x.shape
    depth = w_in.shape[0]
    mlen = mem.shape[1]
    heads = dt_bias.shape[1]
    d_inner = heads * SSD_HEAD_DIM
    conv_ch = conv_w.shape[2]
    n = bsz * seq
    rope_c, rope_s1, rope_s2 = _rope_tables(seq)
    h = x.reshape(n, d)
    mem2d = mem.reshape(bsz * mlen, d)
    for l in range(depth):
        w_in_l = _prep_w_in(w_in[l], d_inner, conv_ch, heads)
        proj = _in_projection(h, norm_mix[l], w_in_l, rope_c, rope_s1, rope_s2, seq)
        proj3d = proj.reshape(bsz, seq, proj.shape[1])
        y_ssd = _ssd_branch(proj3d, conv_w[l], conv_b[l], dt_bias[l], a_log[l], d_skip[l], ssd_norm[l])
        y_attn = _dsa_branch(proj3d)
        h = _merge(h, y_ssd.reshape(n, d_inner), y_attn.reshape(n, -1), proj,
                   w_ssd_o[l].astype(BF16), w_attn_o[l].astype(BF16), w_out[l].astype(BF16))
        kv = _norm_project(mem2d, norm_mem[l], w_ckv[l].astype(BF16), BF16)
        h = _cross_attention(h.reshape(bsz, seq, d), norm_xattn[l], kv.reshape(bsz, mlen, 2 * d),
                             w_cq[l].astype(BF16), w_co[l].astype(BF16)).reshape(n, d)
        wr3, br = _prep_router(w_group[l], b_group[l], w_router[l], b_router[l])
        aug, rank, gsel, counts = _moe_route(h, norm_ffn[l], wr3, br)
        dest, tile_group, n_used, n_pad = _moe_layout(rank, gsel, counts, n)
        aug_sorted = _scatter_rows(aug, dest, n_pad)
        delta = _sorted_experts(aug_sorted, tile_group, n_used, w_gate_e[l].astype(BF16),
                                w_up_e[l].astype(BF16), w_down_e[l].astype(BF16))
        h = _gather_add(h, delta, dest, norm_final, final_norm=(l == depth - 1))
    return h.reshape(bsz, seq, d)
```

```python
import functools
import math

import jax
import jax.numpy as jnp
import numpy as np
from jax import lax
from jax.experimental import pallas as pl
from jax.experimental.pallas import tpu as pltpu

F32 = jnp.float32
BF16 = jnp.bfloat16
I32 = jnp.int32

SSD_HEAD_DIM = 64
SSD_GROUPS = 4
SSD_STATE = 128
SSD_CONV = 4
SSD_CHUNK = 128
ATTN_HEADS = 16
ATTN_KV_HEADS = 2
ATTN_HEAD_DIM = 64
Q_BLOCK = 128
TOPK_MAX = 256
IDX_HEADS = 8
IDX_DIM = 64
ROPE_THETA = 10000.0
XATTN_HEADS = 4
MOE_GROUPS = 4
MOE_EXPERTS_PER_GROUP = 8
MOE_TOPK_IN_GROUP = 2
RMS_EPS = 1e-6

LANES = 128
KEY_CHUNK = 512
KEY_SUB = 128
KEY_TILE = 256
INT_MIN = -2 ** 31
NEG_BIG = -1e30
LOG2E = math.log2(math.e)
ONES_ROWS = 16
VMEM_LIMIT = 56 * 1024 * 1024


def _cparams(sem):
    return pltpu.CompilerParams(dimension_semantics=sem, vmem_limit_bytes=VMEM_LIMIT)


def _split3(v):
    hi = v.astype(BF16)
    r1 = v - hi.astype(F32)
    mid = r1.astype(BF16)
    lo = (r1 - mid.astype(F32)).astype(BF16)
    return hi, mid, lo


def _dot(a, b):
    return jnp.dot(a, b, preferred_element_type=F32)


def _dot_nt(a, b):
    return lax.dot_general(a, b, (((1,), (1,)), ((), ())), preferred_element_type=F32)


def _spread(v, m01, terms):
    parts = _split3(v)[:terms]
    out = _dot(parts[0], m01)
    for part in parts[1:]:
        out = out + _dot(part, m01)
    return out


def _dot3_lhs01(m01, v):
    hi, mid, lo = _split3(v)
    return _dot(m01, hi) + _dot(m01, mid) + _dot(m01, lo)


def _rms(x, g):
    ms = jnp.mean(x * x, axis=-1, keepdims=True)
    return x * lax.rsqrt(ms + RMS_EPS) * g


def _silu(x):
    return x * jax.nn.sigmoid(x)


Q_TILE = 5
MISC_TILE = 8
MISC_ROPE_CHUNKS = (0, 1, 2, 3, 4, 6)
COL_QI = 0
COL_K, COL_V, COL_KI, COL_DTWI = 4, 5, 6, 7
DT_LANES = 32


def _rope_chunk(a, c, s1, s2):
    return a * c + pltpu.roll(a, 32, 1) * s1 + pltpu.roll(a, 96, 1) * s2


def _inproj_kernel(x_ref, g_ref, w_ref, c_ref, s1_ref, s2_ref, o_ref, misc_ref, xn_ref):
    j = pl.program_id(1)

    @pl.when(j == 0)
    def _():
        xn_ref[...] = _rms(x_ref[...], g_ref[...]).astype(BF16)

    acc = _dot(xn_ref[...], w_ref[...])
    n_chunks = acc.shape[1] // LANES

    def roped(dst_ref, chunks):
        c, s1, s2 = c_ref[...], s1_ref[...], s2_ref[...]
        for k in range(n_chunks):
            a = acc[:, k * LANES:(k + 1) * LANES]
            if k in chunks:
                a = _rope_chunk(a, c, s1, s2)
            dst_ref[:, k * LANES:(k + 1) * LANES] = a.astype(dst_ref.dtype)

    @pl.when(j == Q_TILE)
    def _():
        roped(o_ref, tuple(range(n_chunks)))

    @pl.when(j == MISC_TILE)
    def _():
        roped(misc_ref, MISC_ROPE_CHUNKS)

    @pl.when(jnp.logical_and(j != Q_TILE, j != MISC_TILE))
    def _():
        o_ref[...] = acc.astype(o_ref.dtype)


def _in_projection(x2d, gain, w_bf16, rope_c, rope_s1, rope_s2, seq):
    n, d = x2d.shape
    tm = min(1024, seq)
    tn = 1024
    pos_tiles = seq // tm
    n_tiles = w_bf16.shape[1] // tn
    assert n_tiles == MISC_TILE + 1
    return pl.pallas_call(
        _inproj_kernel,
        grid=(n // tm, n_tiles),
        in_specs=[
            pl.BlockSpec((tm, d), lambda i, j: (i, 0)),
            pl.BlockSpec((1, d), lambda i, j: (0, 0)),
            pl.BlockSpec((d, tn), lambda i, j: (0, j)),
            pl.BlockSpec((tm, LANES), lambda i, j: (i % pos_tiles, 0)),
            pl.BlockSpec((tm, LANES), lambda i, j: (i % pos_tiles, 0)),
            pl.BlockSpec((tm, LANES), lambda i, j: (i % pos_tiles, 0)),
        ],
        out_specs=[pl.BlockSpec((tm, tn), lambda i, j: (i, jnp.minimum(j, MISC_TILE - 1))),
                   pl.BlockSpec((tm, tn), lambda i, j: (i, 0))],
        out_shape=[jax.ShapeDtypeStruct((n, MISC_TILE * tn), BF16), jax.ShapeDtypeStruct((n, tn), F32)],
        scratch_shapes=[pltpu.VMEM((tm, d), BF16)],
        compiler_params=_cparams(("parallel", "arbitrary")),
        name="in_projection",
    )(x2d, gain.reshape(1, d), w_bf16, rope_c, rope_s1, rope_s2)


def _prep_w_in(w, d_inner, conv_ch, ssd_heads):
    widths = (d_inner, conv_ch, ssd_heads, ATTN_HEADS * ATTN_HEAD_DIM, ATTN_KV_HEADS * ATTN_HEAD_DIM,
              ATTN_KV_HEADS * ATTN_HEAD_DIM, IDX_HEADS * IDX_DIM, IDX_DIM, IDX_HEADS,
              w.shape[0], w.shape[0])
    splits = [int(p) for p in np.cumsum(widths)[:-1]]
    z, xbc, dt, q, k, v, qi, ki, wi, gs, ga = jnp.split(w, splits, axis=1)

    def pad(a, n):
        return jnp.pad(a, ((0, 0), (0, n - a.shape[1])))

    misc = jnp.concatenate([qi, k, v, pad(ki, LANES), pad(jnp.concatenate([dt, wi], axis=1), LANES)], axis=1)
    return jnp.concatenate([z, xbc, q, gs, ga, misc], axis=1).astype(BF16)


def _rope_tables(seq):
    half = ATTN_HEAD_DIM // 2
    inv = 1.0 / (ROPE_THETA ** (jnp.arange(0, ATTN_HEAD_DIM, 2, dtype=F32) / ATTN_HEAD_DIM))
    ang = jnp.arange(seq, dtype=F32)[:, None] * inv[None, :]
    cos, sin = jnp.cos(ang), jnp.sin(ang)
    zero = jnp.zeros_like(sin)
    reps = LANES // ATTN_HEAD_DIM
    c = jnp.tile(jnp.concatenate([cos, cos], axis=1), (1, reps))
    s1 = jnp.tile(jnp.concatenate([zero, sin], axis=1), (1, reps))
    s2 = jnp.tile(jnp.concatenate([-sin, zero], axis=1), (1, reps))
    del half
    return c, s1, s2


def _ssd_kernel(z_ref, xs_ref, bc_ref, dt_ref, cw_ref, cb_ref, dtb_ref, alog_ref, dsk_ref, ng_ref,
                r64_ref, tri_ref, shift_ref, y_ref, prev_ref, st_ref, act_ref, *, heads, groups):
    c = pl.program_id(1)
    L = SSD_CHUNK
    P = SSD_HEAD_DIM
    hpg = heads // groups
    d_inner = heads * P
    gw = hpg * P
    nst = SSD_STATE

    @pl.when(c == 0)
    def _():
        prev_ref[...] = jnp.zeros(prev_ref.shape, BF16)
        st_ref[...] = jnp.zeros(st_ref.shape, F32)

    def conv(cur_ref, lo, hi):
        cur = cur_ref[...]
        both = jnp.concatenate([prev_ref[:, lo:hi], cur], axis=0)
        acc = cb_ref[:, lo:hi] + cw_ref[SSD_CONV - 1:SSD_CONV, lo:hi] * cur.astype(F32)
        for k in range(1, SSD_CONV):
            acc = acc + cw_ref[SSD_CONV - 1 - k:SSD_CONV - k, lo:hi] * _dot(shift_ref[k - 1], both)
        prev_ref[:, lo:hi] = cur
        return _silu(acc)

    xs = conv(xs_ref, 0, d_inner)
    bc = conv(bc_ref, d_inner, prev_ref.shape[1])
    bmat = bc[:, :groups * nst]
    cmat = bc[:, groups * nst:]

    dt = jax.nn.softplus(dt_ref[...] + dtb_ref[...])
    da = dt * (-jnp.exp(alog_ref[...]))
    tri = tri_ref[...]
    acum = _dot3_lhs01(tri, da)
    act_ref[...] = acum.T
    e_acum = jnp.exp(acum)
    e_tail = jnp.exp(acum[L - 1:L, :] - acum)
    r64 = r64_ref[...]
    e64 = _spread(e_acum, r64, 2)
    t64 = _spread(e_tail, r64, 1)
    d64 = _spread(dt, r64, 1)
    xdt = xs * d64
    xdt_b = xdt.astype(BF16)
    xtl_b = (xdt * t64).astype(BF16)

    row = lax.broadcasted_iota(I32, (L, L), 0)
    col = lax.broadcasted_iota(I32, (L, L), 1)
    causal = row >= col

    for g in range(groups):
        bg = bmat[:, g * nst:(g + 1) * nst]
        cg_b = cmat[:, g * nst:(g + 1) * nst].astype(BF16)
        cb = _dot_nt(cg_b, bg.astype(BF16))
        sg = st_ref[g]
        y_off = _dot(cg_b, sg.astype(BF16))
        y_heads = []
        for hh in range(hpg):
            h = g * hpg + hh
            seg = jnp.broadcast_to(acum[:, h:h + 1], (L, LANES)) - act_ref[h:h + 1, :]
            m = (cb * jnp.where(causal, jnp.exp(seg), 0.0)).astype(BF16)
            y_heads.append(_dot(m, xdt_b[:, h * P:(h + 1) * P]))
        lo, hi = g * gw, (g + 1) * gw
        y_g = jnp.concatenate(y_heads, axis=1) + y_off * e64[:, lo:hi]
        st_ref[g] = sg * e64[L - 1:L, lo:hi] + _dot(bg.T.astype(BF16), xtl_b[:, lo:hi])
        y_g = (y_g + dsk_ref[:, lo:hi] * xs[:, lo:hi]) * _silu(z_ref[:, lo:hi].astype(F32))
        ms = jnp.mean(y_g * y_g, axis=-1, keepdims=True)
        y_ref[:, lo:hi] = (y_g * lax.rsqrt(ms + RMS_EPS) * ng_ref[:, lo:hi]).astype(y_ref.dtype)


def _ssd_branch(main3d, misc3d, conv_w, conv_b, dt_bias, a_log, d_skip, ssd_norm):
    bsz, seq, _ = main3d.shape
    heads = dt_bias.shape[0]
    d_inner = heads * SSD_HEAD_DIM
    groups = SSD_GROUPS
    conv_ch = conv_w.shape[1]
    bc_w = conv_ch - d_inner
    L = SSD_CHUNK

    def pad_lanes(v):
        return jnp.pad(v.astype(F32), (0, LANES - v.shape[0])).reshape(1, LANES)

    hidx = jnp.arange(LANES)[:, None]
    r64 = (hidx == (jnp.arange(d_inner) // SSD_HEAD_DIM)[None, :]).astype(BF16)
    tri = (jnp.arange(L)[:, None] >= jnp.arange(L)[None, :]).astype(BF16)
    taps = jnp.arange(1, SSD_CONV)[:, None, None]
    shift = (jnp.arange(2 * L)[None, None, :] == (L + jnp.arange(L)[None, :, None] - taps)).astype(BF16)
    dsk = jnp.repeat(d_skip.astype(F32), SSD_HEAD_DIM).reshape(1, d_inner)
    const = lambda shape: pl.BlockSpec(shape, lambda b, c: (0,) * len(shape))
    kern = functools.partial(_ssd_kernel, heads=heads, groups=groups)
    return pl.pallas_call(
        kern,
        grid=(bsz, seq // L),
        in_specs=[
            pl.BlockSpec((None, L, d_inner), lambda b, c: (b, c, 0)),
            pl.BlockSpec((None, L, d_inner), lambda b, c: (b, c, 1)),
            pl.BlockSpec((None, L, bc_w), lambda b, c: (b, c, 2 * d_inner // bc_w)),
            pl.BlockSpec((None, L, LANES), lambda b, c: (b, c, COL_DTWI)),
            const((SSD_CONV, conv_ch)), const((1, conv_ch)), const((1, LANES)), const((1, LANES)),
            const((1, d_inner)), const((1, d_inner)),
            const((LANES, d_inner)), const((L, L)),
            const((SSD_CONV - 1, L, 2 * L)),
        ],
        out_specs=pl.BlockSpec((None, L, d_inner), lambda b, c: (b, c, 0)),
        out_shape=jax.ShapeDtypeStruct((bsz, seq, d_inner), BF16),
        scratch_shapes=[
            pltpu.VMEM((L, conv_ch), BF16),
            pltpu.VMEM((groups, SSD_STATE, d_inner // groups), F32),
            pltpu.VMEM((LANES, L), F32),
        ],
        compiler_params=_cparams(("parallel", "arbitrary")),
        name="ssd_scan",
    )(main3d, main3d, main3d, misc3d, conv_w.astype(F32), conv_b.reshape(1, conv_ch).astype(F32),
      pad_lanes(dt_bias), pad_lanes(a_log), dsk, ssd_norm.reshape(1, d_inner).astype(F32), r64, tri, shift)


def _sortable_key(x):
    x = jnp.where(x == 0.0, 0.0, x)
    b = pltpu.bitcast(x, I32)
    return b ^ (lax.shift_right_arithmetic(b, 31) & 0x7FFFFFFF)


def _dsa_kernel(q_ref, qi_ref, k_ref, v_ref, ki_ref, wi_ref, o_ref,
                kb_ref, kib_ref, vt_ref, qs_ref, qis_ref, keys_ref, s0_ref, s1_ref, x0_ref, x1_ref,
                m_ref, l_ref, acc_ref,
                thr_ref, jb_ref, *, topk):
    i = pl.program_id(1)
    QB, KC, KT = Q_BLOCK, KEY_CHUNK, KEY_TILE
    TPC = KC // KT
    hd = ATTN_HEAD_DIM
    seq = k_ref.shape[0]
    n_chunks = (i * QB + QB + KC - 1) // KC
    lane = lax.broadcasted_iota(I32, (1, LANES), 1)
    left = lane < hd

    @pl.when(i == 0)
    def _():
        def fill(t, _):
            rows = pl.ds(pl.multiple_of(t * KT, KT), KT)
            kb_ref[rows, :] = k_ref[rows, :].astype(BF16)
            kib_ref[rows, :] = ki_ref[rows, :].astype(BF16)
            vt = v_ref[rows, :].T.astype(BF16)
            for g in range(ATTN_KV_HEADS):
                vt_ref[t, g, 0:hd, :] = vt[g * hd:(g + 1) * hd, :]
                vt_ref[t, g, hd:hd + ONES_ROWS, :] = jnp.ones((ONES_ROWS, KT), BF16)
            return 0
        lax.fori_loop(0, seq // KT, fill, 0)

    qpos = i * QB + lax.broadcasted_iota(I32, (1, QB), 1)
    idx_scale = (IDX_DIM ** -0.5) * (IDX_HEADS ** -0.5)
    wt = (wi_ref[...] * idx_scale).T
    w_rows = [wt[DT_LANES + h:DT_LANES + h + 1, :] for h in range(IDX_HEADS)]

    heads_per_group = ATTN_HEADS // ATTN_KV_HEADS
    for p in range(ATTN_HEADS // 2):
        slab = q_ref[:, p * LANES:(p + 1) * LANES].astype(F32) * ((hd ** -0.5) * LOG2E)
        swapped = pltpu.roll(slab, hd, 1)
        if (2 * p) // heads_per_group == 0:
            even, odd = jnp.where(left, slab, 0.0), jnp.where(left, swapped, 0.0)
        else:
            even, odd = jnp.where(left, 0.0, swapped), jnp.where(left, 0.0, slab)
        qs_ref[2 * p] = even.astype(BF16)
        qs_ref[2 * p + 1] = odd.astype(BF16)
    for p in range(IDX_HEADS // 2):
        slab = qi_ref[:, p * LANES:(p + 1) * LANES]
        qis_ref[2 * p] = slab.astype(BF16)
        qis_ref[2 * p + 1] = pltpu.roll(slab, hd, 1).astype(BF16)

    def score_chunk(c, _):
        for sub in range(KC // KEY_SUB):
            r0 = sub * KEY_SUB
            kic = kib_ref[pl.ds(pl.multiple_of(c * KC + r0, KEY_SUB), KEY_SUB), :]
            acc = jnp.zeros((KEY_SUB, QB), F32)
            for p in range(IDX_HEADS // 2):
                rel = _dot_nt(kic, qis_ref[2 * p:2 * p + 2].reshape(2 * QB, LANES))
                acc = acc + w_rows[2 * p] * jnp.maximum(rel[:, :QB], 0.0)
                acc = acc + w_rows[2 * p + 1] * jnp.maximum(rel[:, QB:], 0.0)
            kpos = c * KC + r0 + lax.broadcasted_iota(I32, (KEY_SUB, QB), 0)
            keys_ref[c * TPC + r0 // KT, r0 % KT:r0 % KT + KEY_SUB, :] = jnp.where(
                kpos <= qpos, _sortable_key(acc), INT_MIN)
        return 0
    lax.fori_loop(0, n_chunks, score_chunk, 0)

    def count(pred):
        def body(c, acc):
            kc = keys_ref[pl.ds(c * TPC, TPC)].reshape(KC, QB)
            hit = jnp.where(pred(kc, c), 1, 0)
            return acc + jnp.sum(hit.reshape(KC // 8, 8, QB), axis=0)
        acc = lax.fori_loop(0, n_chunks, body, jnp.zeros((8, QB), I32))
        return jnp.sum(acc, axis=0, keepdims=True)

    def bit_step(it, thr):
        cand = thr ^ lax.shift_left(jnp.int32(1), 31 - it)
        cnt = count(lambda kc, c: kc >= cand)
        return jnp.where(cnt >= topk, cand, thr)
    thr = lax.fori_loop(0, 32, bit_step, jnp.full((1, QB), INT_MIN, I32))

    cnt_gt = count(lambda kc, c: kc > thr)
    cnt_ge = count(lambda kc, c: kc >= thr)
    need = topk - cnt_gt
    select_all = qpos < topk
    thr_ref[...] = jnp.where(select_all, INT_MIN, thr)
    jb_ref[...] = jnp.where(select_all, 0, n_chunks * KC)
    surplus = jnp.where(select_all, 0, cnt_ge - topk)

    @pl.when(jnp.max(surplus) > 0)
    def _():
        n_bits = seq.bit_length()
        def tie_step(it, jb):
            cand = jb | lax.shift_left(jnp.int32(1), n_bits - 1 - it)
            def pred(kc, c):
                kpos = c * KC + lax.broadcasted_iota(I32, (KC, QB), 0)
                return jnp.logical_and(kc == thr, kpos < cand)
            cnt = count(pred)
            return jnp.where(cnt <= need, cand, jb)
        jb = lax.fori_loop(0, n_bits, tie_step, jnp.zeros((1, QB), I32))
        jb_ref[...] = jnp.where(select_all, 0, jb)

    thr_f = thr_ref[...]
    jb_f = jb_ref[...]
    m_ref[...] = jnp.full(m_ref.shape, NEG_BIG, F32)
    l_ref[...] = jnp.zeros(l_ref.shape, F32)
    acc_ref[...] = jnp.zeros(acc_ref.shape, F32)

    n_tiles = n_chunks * TPC

    def masked_scores(t, dst_ref, dmax_ref):
        kcv = keys_ref[t]
        kpos = t * KT + lax.broadcasted_iota(I32, (KT, QB), 0)
        sel = jnp.logical_or(kcv > thr_f, jnp.logical_and(kcv == thr_f, kpos < jb_f))
        bias = jnp.where(sel, 0.0, NEG_BIG)
        bias2 = jnp.concatenate([bias, bias], axis=1)
        kc = kb_ref[pl.ds(pl.multiple_of(t * KT, KT), KT), :]
        for p in range(ATTN_HEADS // 2):
            s = _dot_nt(kc, qs_ref[2 * p:2 * p + 2].reshape(2 * QB, LANES)) + bias2
            dst_ref[p] = s
            dmax_ref[p] = jnp.max(s, axis=0, keepdims=True)

    def softmax_pv(t, src_ref, smax_ref):
        for p in range(ATTN_HEADS // 2):
            g = (2 * p) // heads_per_group
            m_old = m_ref[p]
            m_new = jnp.maximum(m_old, smax_ref[p])
            alpha = jnp.exp2(m_old - m_new)
            pt = jnp.exp2(src_ref[p] - m_new).astype(BF16)
            m_ref[p] = m_new
            pv = _dot(vt_ref[t, g], pt)
            l_ref[p] = alpha * l_ref[p] + pv[hd:hd + 1, :]
            acc_ref[p] = alpha * acc_ref[p] + pv[0:hd, :]

    masked_scores(0, s0_ref, x0_ref)

    def attend(u, _):
        t = 2 * u
        masked_scores(t + 1, s1_ref, x1_ref)
        softmax_pv(t, s0_ref, x0_ref)
        masked_scores(jnp.minimum(t + 2, n_tiles - 1), s0_ref, x0_ref)
        softmax_pv(t + 1, s1_ref, x1_ref)
        return 0
    lax.fori_loop(0, n_tiles // 2, attend, 0)

    for p in range(ATTN_HEADS // 2):
        o = acc_ref[p] / l_ref[p]
        both = jnp.concatenate([o[:, :QB], o[:, QB:]], axis=0)
        o_ref[:, p * LANES:(p + 1) * LANES] = both.T.astype(o_ref.dtype)


def _dsa_branch(main3d, misc3d):
    bsz, seq, _ = main3d.shape
    topk = min(TOPK_MAX, seq // 4)
    width = ATTN_HEADS * ATTN_HEAD_DIM
    nkt = seq // KEY_TILE
    n_pairs = ATTN_HEADS // 2
    kern = functools.partial(_dsa_kernel, topk=topk)
    return pl.pallas_call(
        kern,
        grid=(bsz, seq // Q_BLOCK),
        in_specs=[
            pl.BlockSpec((None, Q_BLOCK, width), lambda b, i: (b, i, Q_TILE)),
            pl.BlockSpec((None, Q_BLOCK, IDX_HEADS * IDX_DIM), lambda b, i: (b, i, COL_QI)),
            pl.BlockSpec((None, seq, LANES), lambda b, i: (b, 0, COL_K)),
            pl.BlockSpec((None, seq, LANES), lambda b, i: (b, 0, COL_V)),
            pl.BlockSpec((None, seq, LANES), lambda b, i: (b, 0, COL_KI)),
            pl.BlockSpec((None, Q_BLOCK, LANES), lambda b, i: (b, i, COL_DTWI)),
        ],
        out_specs=pl.BlockSpec((None, Q_BLOCK, width), lambda b, i: (b, i, 0)),
        out_shape=jax.ShapeDtypeStruct((bsz, seq, width), BF16),
        scratch_shapes=[
            pltpu.VMEM((seq, LANES), BF16),
            pltpu.VMEM((seq, LANES), BF16),
            pltpu.VMEM((nkt, ATTN_KV_HEADS, ATTN_HEAD_DIM + ONES_ROWS, KEY_TILE), BF16),
            pltpu.VMEM((ATTN_HEADS, Q_BLOCK, LANES), BF16),
            pltpu.VMEM((IDX_HEADS, Q_BLOCK, LANES), BF16),
            pltpu.VMEM((nkt, KEY_TILE, Q_BLOCK), I32),
            pltpu.VMEM((n_pairs, KEY_TILE, 2 * Q_BLOCK), F32),
            pltpu.VMEM((n_pairs, KEY_TILE, 2 * Q_BLOCK), F32),
            pltpu.VMEM((n_pairs, 1, 2 * Q_BLOCK), F32),
            pltpu.VMEM((n_pairs, 1, 2 * Q_BLOCK), F32),
            pltpu.VMEM((n_pairs, 1, 2 * Q_BLOCK), F32),
            pltpu.VMEM((n_pairs, 1, 2 * Q_BLOCK), F32),
            pltpu.VMEM((n_pairs, ATTN_HEAD_DIM, 2 * Q_BLOCK), F32),
            pltpu.VMEM((1, Q_BLOCK), I32),
            pltpu.VMEM((1, Q_BLOCK), I32),
        ],
        compiler_params=_cparams(("parallel", "arbitrary")),
        name="dsa_attention",
    )(main3d, misc3d, misc3d, misc3d, misc3d, misc3d)


def _merge_kernel(h_ref, ys_ref, ya_ref, gs_ref, ga_ref, wso_ref, wao_ref, wo_ref, o_ref):
    a = _dot(ys_ref[...], wso_ref[...])
    b = _dot(ya_ref[...], wao_ref[...])
    merged = jax.nn.sigmoid(gs_ref[...].astype(F32)) * a + jax.nn.sigmoid(ga_ref[...].astype(F32)) * b
    o_ref[...] = h_ref[...] + _dot(merged.astype(BF16), wo_ref[...])


def _merge(h2d, y_ssd, y_attn, proj2d, wso, wao, wo):
    n, d = h2d.shape
    tm = 512
    full = lambda a: pl.BlockSpec(a.shape, lambda i: (0, 0))
    return pl.pallas_call(
        _merge_kernel,
        grid=(n // tm,),
        in_specs=[
            pl.BlockSpec((tm, d), lambda i: (i, 0)),
            pl.BlockSpec((tm, y_ssd.shape[1]), lambda i: (i, 0)),
            pl.BlockSpec((tm, y_attn.shape[1]), lambda i: (i, 0)),
            pl.BlockSpec((tm, d), lambda i: (i, 6)),
            pl.BlockSpec((tm, d), lambda i: (i, 7)),
            full(wso), full(wao), full(wo),
        ],
        out_specs=pl.BlockSpec((tm, d), lambda i: (i, 0)),
        out_shape=jax.ShapeDtypeStruct((n, d), F32),
        compiler_params=_cparams(("parallel",)),
        name="merge_out",
    )(h2d, y_ssd, y_attn, proj2d, proj2d, wso, wao, wo)


def _normproj_kernel(x_ref, g_ref, w_ref, o_ref):
    o_ref[...] = _dot(_rms(x_ref[...], g_ref[...]).astype(BF16), w_ref[...]).astype(o_ref.dtype)


def _norm_project(x2d, gain, w_bf16, out_dtype):
    n, d = x2d.shape
    tm = min(512, n)
    return pl.pallas_call(
        _normproj_kernel,
        grid=(n // tm,),
        in_specs=[pl.BlockSpec((tm, d), lambda i: (i, 0)), pl.BlockSpec((1, d), lambda i: (0, 0)),
                  pl.BlockSpec(w_bf16.shape, lambda i: (0, 0))],
        out_specs=pl.BlockSpec((tm, w_bf16.shape[1]), lambda i: (i, 0)),
        out_shape=jax.ShapeDtypeStruct((n, w_bf16.shape[1]), out_dtype),
        compiler_params=_cparams(("parallel",)),
        name="mem_kv_projection",
    )(x2d, gain.reshape(1, d), w_bf16)


def _xattn_kernel(h_ref, g_ref, kv_ref, wq_ref, wo_ref, o_ref):
    d = h_ref.shape[1]
    hd = d // XATTN_HEADS
    h = h_ref[...]
    q = _dot(_rms(h, g_ref[...]).astype(BF16), wq_ref[...]) * (hd ** -0.5)
    qb = q.astype(BF16)
    outs = []
    for a in range(XATTN_HEADS):
        k = kv_ref[:, a * hd:(a + 1) * hd]
        v = kv_ref[:, d + a * hd:d + (a + 1) * hd]
        s = _dot_nt(qb[:, a * hd:(a + 1) * hd], k)
        p = jnp.exp(s - jnp.max(s, axis=1, keepdims=True))
        o = _dot(p.astype(BF16), v) / jnp.sum(p, axis=1, keepdims=True)
        outs.append(o.astype(BF16))
    o_ref[...] = h + _dot(jnp.concatenate(outs, axis=1), wo_ref[...])


def _cross_attention(h3d, gain, kv3d, wq, wo):
    bsz, seq, d = h3d.shape
    tm = min(512, seq)
    mlen = kv3d.shape[1]
    return pl.pallas_call(
        _xattn_kernel,
        grid=(bsz, seq // tm),
        in_specs=[
            pl.BlockSpec((None, tm, d), lambda b, i: (b, i, 0)),
            pl.BlockSpec((1, d), lambda b, i: (0, 0)),
            pl.BlockSpec((None, mlen, 2 * d), lambda b, i: (b, 0, 0)),
            pl.BlockSpec(wq.shape, lambda b, i: (0, 0)),
            pl.BlockSpec(wo.shape, lambda b, i: (0, 0)),
        ],
        out_specs=pl.BlockSpec((None, tm, d), lambda b, i: (b, i, 0)),
        out_shape=jax.ShapeDtypeStruct((bsz, seq, d), F32),
        compiler_params=_cparams(("parallel", "parallel")),
        name="cross_attention",
    )(h3d, gain.reshape(1, d), kv3d, wq, wo)


def _router_kernel(h_ref, g_ref, wr_ref, br_ref, trit_ref, aug_ref, rank_ref, gsel_ref, cnt_ref, base_ref):
    i = pl.program_id(0)
    d = h_ref.shape[1]

    @pl.when(i == 0)
    def _():
        base_ref[...] = jnp.zeros(base_ref.shape, F32)

    hn = _rms(h_ref[...], g_ref[...])
    aug_ref[:, 0:d] = hn
    h_hi, h_mid, h_lo = _split3(hn)
    w_hi, w_mid, w_lo = wr_ref[0], wr_ref[1], wr_ref[2]
    logits = (_dot(h_hi, w_hi) + (_dot(h_hi, w_mid) + _dot(h_mid, w_hi))
              + (_dot(h_hi, w_lo) + _dot(h_mid, w_mid) + _dot(h_lo, w_hi))) + br_ref[...]
    tm = logits.shape[0]
    lane = lax.broadcasted_iota(I32, (tm, LANES), 1)
    neg_inf = -jnp.inf
    n_e = MOE_GROUPS * MOE_EXPERTS_PER_GROUP
    is_g = jnp.logical_and(lane >= n_e, lane < n_e + MOE_GROUPS)
    gl = jnp.where(is_g, logits, neg_inf)
    g_max = jnp.max(gl, axis=1, keepdims=True)
    g_sel = jnp.min(jnp.where(gl == g_max, lane, LANES), axis=1, keepdims=True) - n_e
    g_w = 1.0 / jnp.sum(jnp.where(is_g, jnp.exp(gl - g_max), 0.0), axis=1, keepdims=True)
    in_grp = jnp.logical_and(lane >= g_sel * MOE_EXPERTS_PER_GROUP,
                             lane < (g_sel + 1) * MOE_EXPERTS_PER_GROUP)
    el = jnp.where(in_grp, logits, neg_inf)
    v1 = jnp.max(el, axis=1, keepdims=True)
    i1 = jnp.min(jnp.where(el == v1, lane, LANES), axis=1, keepdims=True)
    el2 = jnp.where(lane == i1, neg_inf, el)
    v2 = jnp.max(el2, axis=1, keepdims=True)
    i2 = jnp.min(jnp.where(el2 == v2, lane, LANES), axis=1, keepdims=True)
    e2 = jnp.exp(v2 - v1)
    w1 = g_w / (1.0 + e2)
    w2 = g_w * e2 / (1.0 + e2)
    aug_ref[:, d:] = jnp.where(lane == i1, w1, 0.0) + jnp.where(lane == i2, w2, 0.0)

    onehot_t = jnp.where(lane == g_sel, 1.0, 0.0).T
    upto = _dot(onehot_t.astype(BF16), trit_ref[...]) + base_ref[...]
    grp = lax.broadcasted_iota(I32, onehot_t.shape, 0).astype(F32)
    rank_ref[...] = (jnp.sum(onehot_t * upto, axis=0, keepdims=True) - 1.0).astype(I32)
    gsel_ref[...] = jnp.sum(onehot_t * grp, axis=0, keepdims=True).astype(I32)
    base_ref[...] = upto[:, tm - 1:tm]
    cnt_ref[...] = jnp.broadcast_to(upto[:, tm - 1:tm], cnt_ref.shape)


def _moe_route(h2d, gain, wr3, br):
    n, d = h2d.shape
    tm = 512
    nt = n // tm
    trit = (jnp.arange(tm)[:, None] <= jnp.arange(tm)[None, :]).astype(BF16)
    return pl.pallas_call(
        _router_kernel,
        grid=(nt,),
        in_specs=[pl.BlockSpec((tm, d), lambda i: (i, 0)), pl.BlockSpec((1, d), lambda i: (0, 0)),
                  pl.BlockSpec(wr3.shape, lambda i: (0, 0, 0)), pl.BlockSpec((1, LANES), lambda i: (0, 0)),
                  pl.BlockSpec((tm, tm), lambda i: (0, 0))],
        out_specs=[pl.BlockSpec((tm, d + LANES), lambda i: (i, 0)),
                   pl.BlockSpec((None, 1, tm), lambda i: (i, 0, 0)),
                   pl.BlockSpec((None, 1, tm), lambda i: (i, 0, 0)),
                   pl.BlockSpec((LANES, LANES), lambda i: (0, 0))],
        out_shape=[jax.ShapeDtypeStruct((n, d + LANES), F32),
                   jax.ShapeDtypeStruct((nt, 1, tm), I32),
                   jax.ShapeDtypeStruct((nt, 1, tm), I32),
                   jax.ShapeDtypeStruct((LANES, LANES), F32)],
        scratch_shapes=[pltpu.VMEM((LANES, 1), F32)],
        compiler_params=_cparams(("arbitrary",)),
        name="moe_router",
    )(h2d, gain.reshape(1, d), wr3, br, trit)


MOE_TILE = 512
ROW_BLOCK = 256
ISSUE_UNROLL = 8


def _scatter_rows_kernel(dest_ref, src_ref, init_ref, out_ref, sem):
    del init_ref
    base = pl.program_id(0) * ROW_BLOCK

    def row_copy(r):
        return pltpu.make_async_copy(src_ref.at[pl.ds(r, 1)], out_ref.at[pl.ds(dest_ref[base + r], 1)], sem)

    def issue(r, _):
        row_copy(r).start()
        return 0
    lax.fori_loop(0, ROW_BLOCK, issue, 0, unroll=ISSUE_UNROLL)
    pltpu.make_async_copy(src_ref, out_ref.at[pl.ds(0, ROW_BLOCK)], sem).wait()


def _scatter_rows(src, dest, n_out):
    n, w = src.shape
    init = jnp.zeros((n_out, w), src.dtype)
    return pl.pallas_call(
        _scatter_rows_kernel,
        grid_spec=pltpu.PrefetchScalarGridSpec(
            num_scalar_prefetch=1,
            grid=(n // ROW_BLOCK,),
            in_specs=[pl.BlockSpec((ROW_BLOCK, w), lambda i, dest: (i, 0)),
                      pl.BlockSpec(memory_space=pl.ANY)],
            out_specs=pl.BlockSpec(memory_space=pl.ANY),
            scratch_shapes=[pltpu.SemaphoreType.DMA(())],
        ),
        out_shape=jax.ShapeDtypeStruct((n_out, w), src.dtype),
        input_output_aliases={2: 0},
        compiler_params=_cparams(("arbitrary",)),
        name="moe_sort_rows",
    )(dest, src, init)


def _sorted_experts_kernel(tile_group_ref, n_used_ref, hn_ref, cw_ref, wg_ref, wu_ref, wd_ref, o_ref):
    j = pl.program_id(0)

    @pl.when(j < n_used_ref[0])
    def _():
        hn = hn_ref[...].astype(BF16)
        first = tile_group_ref[j] * MOE_EXPERTS_PER_GROUP
        lane = lax.broadcasted_iota(I32, cw_ref.shape, 1)
        cw = cw_ref[...]
        acc = jnp.zeros(o_ref.shape, F32)
        for e in range(MOE_EXPERTS_PER_GROUP):
            w_e = jnp.sum(jnp.where(lane == first + e, cw, 0.0), axis=1, keepdims=True)
            hid = _silu(_dot(hn, wg_ref[e])) * _dot(hn, wu_ref[e]) * w_e
            acc = acc + _dot(hid.astype(BF16), wd_ref[e])
        o_ref[...] = acc

    @pl.when(j >= n_used_ref[0])
    def _():
        o_ref[...] = jnp.zeros(o_ref.shape, F32)


def _sorted_experts(aug_sorted, tile_group, n_used, wg, wu, wd):
    n_pad, wa = aug_sorted.shape
    d = wa - LANES
    _, n_e, _, dff = wg.shape
    tm = MOE_TILE
    return pl.pallas_call(
        _sorted_experts_kernel,
        grid_spec=pltpu.PrefetchScalarGridSpec(
            num_scalar_prefetch=2,
            grid=(n_pad // tm,),
            in_specs=[
                pl.BlockSpec((tm, d), lambda j, tg, nu: (j, 0)),
                pl.BlockSpec((tm, LANES), lambda j, tg, nu: (j, d // LANES)),
                pl.BlockSpec((None, n_e, d, dff), lambda j, tg, nu: (tg[j], 0, 0, 0)),
                pl.BlockSpec((None, n_e, d, dff), lambda j, tg, nu: (tg[j], 0, 0, 0)),
                pl.BlockSpec((None, n_e, dff, d), lambda j, tg, nu: (tg[j], 0, 0, 0)),
            ],
            out_specs=pl.BlockSpec((tm, d), lambda j, tg, nu: (j, 0)),
        ),
        out_shape=jax.ShapeDtypeStruct((n_pad, d), F32),
        compiler_params=_cparams(("arbitrary",)),
        name="moe_experts",
    )(tile_group, n_used, aug_sorted, aug_sorted, wg, wu, wd)


def _gather_add_kernel(dest_ref, h_ref, delta_ref, fg_ref, o_ref, buf_ref, sem, *, final_norm):
    i = pl.program_id(0)
    n_steps = pl.num_programs(0)

    def row_copy(step, r):
        slot = lax.rem(step, 2)
        return pltpu.make_async_copy(delta_ref.at[pl.ds(dest_ref[step * ROW_BLOCK + r], 1)],
                                     buf_ref.at[slot, pl.ds(r, 1)], sem.at[slot])

    def issue_all(step):
        def issue(r, _):
            row_copy(step, r).start()
            return 0
        lax.fori_loop(0, ROW_BLOCK, issue, 0, unroll=ISSUE_UNROLL)

    @pl.when(i == 0)
    def _():
        issue_all(0)

    @pl.when(i + 1 < n_steps)
    def _():
        issue_all(i + 1)

    slot = lax.rem(i, 2)
    pltpu.make_async_copy(delta_ref.at[pl.ds(0, ROW_BLOCK)], buf_ref.at[slot], sem.at[slot]).wait()

    out = h_ref[...] + buf_ref[slot]
    if final_norm:
        out = _rms(out, fg_ref[...])
    o_ref[...] = out


def _gather_add(h2d, delta_sorted, dest, final_gain, final_norm):
    n, d = h2d.shape
    kern = functools.partial(_gather_add_kernel, final_norm=final_norm)
    return pl.pallas_call(
        kern,
        grid_spec=pltpu.PrefetchScalarGridSpec(
            num_scalar_prefetch=1,
            grid=(n // ROW_BLOCK,),
            in_specs=[pl.BlockSpec((ROW_BLOCK, d), lambda i, dest: (i, 0)),
                      pl.BlockSpec(memory_space=pl.ANY),
                      pl.BlockSpec((1, d), lambda i, dest: (0, 0))],
            out_specs=pl.BlockSpec((ROW_BLOCK, d), lambda i, dest: (i, 0)),
            scratch_shapes=[pltpu.VMEM((2, ROW_BLOCK, d), F32), pltpu.SemaphoreType.DMA((2,))],
        ),
        out_shape=jax.ShapeDtypeStruct((n, d), F32),
        compiler_params=_cparams(("arbitrary",)),
        name="moe_unsort_add",
    )(dest, h2d, delta_sorted, final_gain.reshape(1, d))


def _moe_layout(rank, gsel, counts, n):
    tm = MOE_TILE
    cnt = counts[:MOE_GROUPS, 0].astype(I32)
    seg = ((cnt + tm - 1) // tm) * tm
    ends = jnp.cumsum(seg)
    dest = (ends - seg)[gsel.reshape(n)] + rank.reshape(n)
    n_tiles = n // tm + MOE_GROUPS
    starts = jnp.arange(n_tiles, dtype=I32) * tm
    tile_group = jnp.minimum(jnp.sum(starts[:, None] >= ends[None, :], axis=1), MOE_GROUPS - 1).astype(I32)
    n_used = (ends[-1:] // tm).astype(I32)
    return dest.astype(I32), tile_group, n_used, n_tiles * tm


def _prep_router(w_group, b_group, w_router, b_router):
    d = w_group.shape[0]
    w = jnp.concatenate([w_router, w_group], axis=1).astype(F32)
    w = jnp.pad(w, ((0, 0), (0, LANES - w.shape[1])))
    hi = w.astype(BF16)
    r1 = w - hi.astype(F32)
    mid = r1.astype(BF16)
    lo = (r1 - mid.astype(F32)).astype(BF16)
    b = jnp.concatenate([b_router, b_group]).astype(F32)
    b = jnp.pad(b, (0, LANES - b.shape[0])).reshape(1, LANES)
    del d
    return jnp.stack([hi, mid, lo]), b


def kernel(x, mem, norm_mix, w_in, conv_w, conv_b, dt_bias, a_log, d_skip, ssd_norm, w_ssd_o, w_attn_o, w_out,
           norm_xattn, norm_mem, w_cq, w_ckv, w_co, norm_ffn, w_group, b_group, w_router, b_router,
           w_gate_e, w_up_e, w_down_e, norm_final):
    bsz, seq, d = x.shape
    depth = w_in.shape[0]
    mlen = mem.shape[1]
    heads = dt_bias.shape[1]
    d_inner = heads * SSD_HEAD_DIM
    conv_ch = conv_w.shape[2]
    n = bsz * seq
    rope_c, rope_s1, rope_s2 = _rope_tables(seq)
    h = x.reshape(n, d)
    mem2d = mem.reshape(bsz * mlen, d)
    for l in range(depth):
        w_in_l = _prep_w_in(w_in[l], d_inner, conv_ch, heads)
        main, misc = _in_projection(h, norm_mix[l], w_in_l, rope_c, rope_s1, rope_s2, seq)
        main3d = main.reshape(bsz, seq, main.shape[1])
        misc3d = misc.reshape(bsz, seq, misc.shape[1])
        y_ssd = _ssd_branch(main3d, misc3d, conv_w[l], conv_b[l], dt_bias[l], a_log[l], d_skip[l], ssd_norm[l])
        y_attn = _dsa_branch(main3d, misc3d)
        h = _merge(h, y_ssd.reshape(n, d_inner), y_attn.reshape(n, -1), main,
                   w_ssd_o[l].astype(BF16), w_attn_o[l].astype(BF16), w_out[l].astype(BF16))
        kv = _norm_project(mem2d, norm_mem[l], w_ckv[l].astype(BF16), BF16)
        h = _cross_attention(h.reshape(bsz, seq, d), norm_xattn[l], kv.reshape(bsz, mlen, 2 * d),
                             w_cq[l].astype(BF16), w_co[l].astype(BF16)).reshape(n, d)
        wr3, br = _prep_router(w_group[l], b_group[l], w_router[l], b_router[l])
        aug, rank, gsel, counts = _moe_route(h, norm_ffn[l], wr3, br)
        dest, tile_group, n_used, n_pad = _moe_layout(rank, gsel, counts, n)
        aug_sorted = _scatter_rows(aug, dest, n_pad)
        delta = _sorted_experts(aug_sorted, tile_group, n_used, w_gate_e[l].astype(BF16),
                                w_up_e[l].astype(BF16), w_down_e[l].astype(BF16))
        h = _gather_add(h, delta, dest, norm_final, final_norm=(l == depth - 1))
    return h.reshape(bsz, seq, d)
```

```python
import functools
import math

import jax
import jax.numpy as jnp
import numpy as np
from jax import lax
from jax.experimental import pallas as pl
from jax.experimental.pallas import tpu as pltpu

F32 = jnp.float32
BF16 = jnp.bfloat16
I32 = jnp.int32

SSD_HEAD_DIM = 64
SSD_GROUPS = 4
SSD_STATE = 128
SSD_CONV = 4
SSD_CHUNK = 128
ATTN_HEADS = 16
ATTN_KV_HEADS = 2
ATTN_HEAD_DIM = 64
Q_BLOCK = 128
TOPK_MAX = 256
IDX_HEADS = 8
IDX_DIM = 64
ROPE_THETA = 10000.0
XATTN_HEADS = 4
MOE_GROUPS = 4
MOE_EXPERTS_PER_GROUP = 8
MOE_TOPK_IN_GROUP = 2
RMS_EPS = 1e-6

LANES = 128
KEY_CHUNK = 512
KEY_TILE = 256
INT_MIN = -2 ** 31
NEG_BIG = -1e30
LOG2E = math.log2(math.e)
ONES_ROWS = 16
VMEM_LIMIT = 56 * 1024 * 1024


def _cparams(sem):
    return pltpu.CompilerParams(dimension_semantics=sem, vmem_limit_bytes=VMEM_LIMIT)


def _split3(v):
    hi = v.astype(BF16)
    r1 = v - hi.astype(F32)
    mid = r1.astype(BF16)
    lo = (r1 - mid.astype(F32)).astype(BF16)
    return hi, mid, lo


def _dot(a, b):
    return jnp.dot(a, b, preferred_element_type=F32)


def _dot_nt(a, b):
    return lax.dot_general(a, b, (((1,), (1,)), ((), ())), preferred_element_type=F32)


def _spread(v, m01, terms):
    parts = _split3(v)[:terms]
    out = _dot(parts[0], m01)
    for part in parts[1:]:
        out = out + _dot(part, m01)
    return out


def _dot3_lhs01(m01, v):
    hi, mid, lo = _split3(v)
    return _dot(m01, hi) + _dot(m01, mid) + _dot(m01, lo)


def _rms(x, g):
    ms = jnp.mean(x * x, axis=-1, keepdims=True)
    return x * lax.rsqrt(ms + RMS_EPS) * g


def _silu(x):
    return x * jax.nn.sigmoid(x)


Q_TILE = 5
MISC_TILE = 8
MISC_ROPE_CHUNKS = (0, 1, 2, 3, 4, 6)
COL_QI = 0
COL_K, COL_V, COL_KI, COL_DTWI = 4, 5, 6, 7
DT_LANES = 32


def _rope_chunk(a, c, s1, s2):
    return a * c + pltpu.roll(a, 32, 1) * s1 + pltpu.roll(a, 96, 1) * s2


def _inproj_kernel(x_ref, g_ref, w_ref, c_ref, s1_ref, s2_ref, o_ref, misc_ref, xn_ref):
    j = pl.program_id(1)

    @pl.when(j == 0)
    def _():
        xn_ref[...] = _rms(x_ref[...], g_ref[...]).astype(BF16)

    acc = _dot(xn_ref[...], w_ref[...])
    n_chunks = acc.shape[1] // LANES

    def roped(dst_ref, chunks):
        c, s1, s2 = c_ref[...], s1_ref[...], s2_ref[...]
        for k in range(n_chunks):
            a = acc[:, k * LANES:(k + 1) * LANES]
            if k in chunks:
                a = _rope_chunk(a, c, s1, s2)
            dst_ref[:, k * LANES:(k + 1) * LANES] = a.astype(dst_ref.dtype)

    @pl.when(j == Q_TILE)
    def _():
        roped(o_ref, tuple(range(n_chunks)))

    @pl.when(j == MISC_TILE)
    def _():
        roped(misc_ref, MISC_ROPE_CHUNKS)

    @pl.when(jnp.logical_and(j != Q_TILE, j != MISC_TILE))
    def _():
        o_ref[...] = acc.astype(o_ref.dtype)


def _in_projection(x2d, gain, w_bf16, rope_c, rope_s1, rope_s2, seq):
    n, d = x2d.shape
    tm = min(1024, seq)
    tn = 1024
    pos_tiles = seq // tm
    n_tiles = w_bf16.shape[1] // tn
    assert n_tiles == MISC_TILE + 1
    return pl.pallas_call(
        _inproj_kernel,
        grid=(n // tm, n_tiles),
        in_specs=[
            pl.BlockSpec((tm, d), lambda i, j: (i, 0)),
            pl.BlockSpec((1, d), lambda i, j: (0, 0)),
            pl.BlockSpec((d, tn), lambda i, j: (0, j)),
            pl.BlockSpec((tm, LANES), lambda i, j: (i % pos_tiles, 0)),
            pl.BlockSpec((tm, LANES), lambda i, j: (i % pos_tiles, 0)),
            pl.BlockSpec((tm, LANES), lambda i, j: (i % pos_tiles, 0)),
        ],
        out_specs=[pl.BlockSpec((tm, tn), lambda i, j: (i, jnp.minimum(j, MISC_TILE - 1))),
                   pl.BlockSpec((tm, tn), lambda i, j: (i, 0))],
        out_shape=[jax.ShapeDtypeStruct((n, MISC_TILE * tn), BF16), jax.ShapeDtypeStruct((n, tn), F32)],
        scratch_shapes=[pltpu.VMEM((tm, d), BF16)],
        compiler_params=_cparams(("parallel", "arbitrary")),
        name="in_projection",
    )(x2d, gain.reshape(1, d), w_bf16, rope_c, rope_s1, rope_s2)


def _prep_w_in(w, d_inner, conv_ch, ssd_heads):
    widths = (d_inner, conv_ch, ssd_heads, ATTN_HEADS * ATTN_HEAD_DIM, ATTN_KV_HEADS * ATTN_HEAD_DIM,
              ATTN_KV_HEADS * ATTN_HEAD_DIM, IDX_HEADS * IDX_DIM, IDX_DIM, IDX_HEADS,
              w.shape[0], w.shape[0])
    splits = [int(p) for p in np.cumsum(widths)[:-1]]
    z, xbc, dt, q, k, v, qi, ki, wi, gs, ga = jnp.split(w, splits, axis=1)

    def pad(a, n):
        return jnp.pad(a, ((0, 0), (0, n - a.shape[1])))

    misc = jnp.concatenate([qi, k, v, pad(ki, LANES), pad(jnp.concatenate([dt, wi], axis=1), LANES)], axis=1)
    return jnp.concatenate([z, xbc, q, gs, ga, misc], axis=1).astype(BF16)


def _rope_tables(seq):
    half = ATTN_HEAD_DIM // 2
    inv = 1.0 / (ROPE_THETA ** (jnp.arange(0, ATTN_HEAD_DIM, 2, dtype=F32) / ATTN_HEAD_DIM))
    ang = jnp.arange(seq, dtype=F32)[:, None] * inv[None, :]
    cos, sin = jnp.cos(ang), jnp.sin(ang)
    zero = jnp.zeros_like(sin)
    reps = LANES // ATTN_HEAD_DIM
    c = jnp.tile(jnp.concatenate([cos, cos], axis=1), (1, reps))
    s1 = jnp.tile(jnp.concatenate([zero, sin], axis=1), (1, reps))
    s2 = jnp.tile(jnp.concatenate([-sin, zero], axis=1), (1, reps))
    del half
    return c, s1, s2


def _ssd_kernel(z_ref, xs_ref, bc_ref, dt_ref, cw_ref, cb_ref, dtb_ref, alog_ref, dsk_ref, ng_ref,
                r64_ref, tri_ref, shift_ref, y_ref, prev_ref, st_ref, act_ref, *, heads, groups):
    c = pl.program_id(1)
    L = SSD_CHUNK
    P = SSD_HEAD_DIM
    hpg = heads // groups
    d_inner = heads * P
    gw = hpg * P
    nst = SSD_STATE

    @pl.when(c == 0)
    def _():
        prev_ref[...] = jnp.zeros(prev_ref.shape, BF16)
        st_ref[...] = jnp.zeros(st_ref.shape, F32)

    def conv(cur_ref, lo, hi):
        cur = cur_ref[...]
        both = jnp.concatenate([prev_ref[:, lo:hi], cur], axis=0)
        acc = cb_ref[:, lo:hi] + cw_ref[SSD_CONV - 1:SSD_CONV, lo:hi] * cur.astype(F32)
        for k in range(1, SSD_CONV):
            acc = acc + cw_ref[SSD_CONV - 1 - k:SSD_CONV - k, lo:hi] * _dot(shift_ref[k - 1], both)
        prev_ref[:, lo:hi] = cur
        return _silu(acc)

    xs = conv(xs_ref, 0, d_inner)
    bc = conv(bc_ref, d_inner, prev_ref.shape[1])
    bmat = bc[:, :groups * nst]
    cmat = bc[:, groups * nst:]

    dt = jax.nn.softplus(dt_ref[...] + dtb_ref[...])
    da = dt * (-jnp.exp(alog_ref[...]))
    tri = tri_ref[...]
    acum = _dot3_lhs01(tri, da)
    act_ref[...] = acum.T
    e_acum = jnp.exp(acum)
    e_tail = jnp.exp(acum[L - 1:L, :] - acum)
    r64 = r64_ref[...]
    e64 = _spread(e_acum, r64, 2)
    t64 = _spread(e_tail, r64, 1)
    d64 = _spread(dt, r64, 1)
    xdt = xs * d64
    xdt_b = xdt.astype(BF16)
    xtl_b = (xdt * t64).astype(BF16)

    row = lax.broadcasted_iota(I32, (L, L), 0)
    col = lax.broadcasted_iota(I32, (L, L), 1)
    causal = row >= col

    for g in range(groups):
        bg = bmat[:, g * nst:(g + 1) * nst]
        cg_b = cmat[:, g * nst:(g + 1) * nst].astype(BF16)
        cb = _dot_nt(cg_b, bg.astype(BF16))
        sg = st_ref[g]
        y_off = _dot(cg_b, sg.astype(BF16))
        y_heads = []
        for hh in range(hpg):
            h = g * hpg + hh
            seg = jnp.broadcast_to(acum[:, h:h + 1], (L, LANES)) - act_ref[h:h + 1, :]
            m = (cb * jnp.where(causal, jnp.exp(seg), 0.0)).astype(BF16)
            y_heads.append(_dot(m, xdt_b[:, h * P:(h + 1) * P]))
        lo, hi = g * gw, (g + 1) * gw
        y_g = jnp.concatenate(y_heads, axis=1) + y_off * e64[:, lo:hi]
        st_ref[g] = sg * e64[L - 1:L, lo:hi] + _dot(bg.T.astype(BF16), xtl_b[:, lo:hi])
        y_g = (y_g + dsk_ref[:, lo:hi] * xs[:, lo:hi]) * _silu(z_ref[:, lo:hi].astype(F32))
        ms = jnp.mean(y_g * y_g, axis=-1, keepdims=True)
        y_ref[:, lo:hi] = (y_g * lax.rsqrt(ms + RMS_EPS) * ng_ref[:, lo:hi]).astype(y_ref.dtype)


def _ssd_branch(main3d, misc3d, conv_w, conv_b, dt_bias, a_log, d_skip, ssd_norm):
    bsz, seq, _ = main3d.shape
    heads = dt_bias.shape[0]
    d_inner = heads * SSD_HEAD_DIM
    groups = SSD_GROUPS
    conv_ch = conv_w.shape[1]
    bc_w = conv_ch - d_inner
    L = SSD_CHUNK

    def pad_lanes(v):
        return jnp.pad(v.astype(F32), (0, LANES - v.shape[0])).reshape(1, LANES)

    hidx = jnp.arange(LANES)[:, None]
    r64 = (hidx == (jnp.arange(d_inner) // SSD_HEAD_DIM)[None, :]).astype(BF16)
    tri = (jnp.arange(L)[:, None] >= jnp.arange(L)[None, :]).astype(BF16)
    taps = jnp.arange(1, SSD_CONV)[:, None, None]
    shift = (jnp.arange(2 * L)[None, None, :] == (L + jnp.arange(L)[None, :, None] - taps)).astype(BF16)
    dsk = jnp.repeat(d_skip.astype(F32), SSD_HEAD_DIM).reshape(1, d_inner)
    const = lambda shape: pl.BlockSpec(shape, lambda b, c: (0,) * len(shape))
    kern = functools.partial(_ssd_kernel, heads=heads, groups=groups)
    return pl.pallas_call(
        kern,
        grid=(bsz, seq // L),
        in_specs=[
            pl.BlockSpec((None, L, d_inner), lambda b, c: (b, c, 0)),
            pl.BlockSpec((None, L, d_inner), lambda b, c: (b, c, 1)),
            pl.BlockSpec((None, L, bc_w), lambda b, c: (b, c, 2 * d_inner // bc_w)),
            pl.BlockSpec((None, L, LANES), lambda b, c: (b, c, COL_DTWI)),
            const((SSD_CONV, conv_ch)), const((1, conv_ch)), const((1, LANES)), const((1, LANES)),
            const((1, d_inner)), const((1, d_inner)),
            const((LANES, d_inner)), const((L, L)),
            const((SSD_CONV - 1, L, 2 * L)),
        ],
        out_specs=pl.BlockSpec((None, L, d_inner), lambda b, c: (b, c, 0)),
        out_shape=jax.ShapeDtypeStruct((bsz, seq, d_inner), BF16),
        scratch_shapes=[
            pltpu.VMEM((L, conv_ch), BF16),
            pltpu.VMEM((groups, SSD_STATE, d_inner // groups), F32),
            pltpu.VMEM((LANES, L), F32),
        ],
        compiler_params=_cparams(("parallel", "arbitrary")),
        name="ssd_scan",
    )(main3d, main3d, main3d, misc3d, conv_w.astype(F32), conv_b.reshape(1, conv_ch).astype(F32),
      pad_lanes(dt_bias), pad_lanes(a_log), dsk, ssd_norm.reshape(1, d_inner).astype(F32), r64, tri, shift)


def _sortable_key(x):
    x = jnp.where(x == 0.0, 0.0, x)
    b = pltpu.bitcast(x, I32)
    return b ^ (lax.shift_right_arithmetic(b, 31) & 0x7FFFFFFF)


_BIT_MASKS = (0x0000FFFF, 0x00FF00FF, 0x0F0F0F0F, 0x33333333, 0x55555555)


def _bit_planes(rows):
    x = list(rows[::-1])
    j = 16
    for m in _BIT_MASKS:
        mask = np.int32(np.uint32(m))
        k = 0
        while k < 32:
            t = (x[k] ^ lax.shift_right_logical(x[k + j], jnp.int32(j))) & mask
            x[k] = x[k] ^ t
            x[k + j] = x[k + j] ^ lax.shift_left(t, jnp.int32(j))
            k = (k + j + 1) & ~j
        j >>= 1
    return x


def _dsa_kernel(q_ref, qi_ref, k_ref, v_ref, ki_ref, wi_ref, o_ref,
                kb_ref, kib_ref, vt_ref, qs_ref, qis_ref, keys_ref, planes_ref, sel_ref, cand_ref, above_ref,
                s0_ref, s1_ref, x0_ref, x1_ref, m_ref, l_ref, acc_ref, *, topk):
    i = pl.program_id(1)
    QB, KC, KT = Q_BLOCK, KEY_CHUNK, KEY_TILE
    TPC = KC // KT
    hd = ATTN_HEAD_DIM
    seq = k_ref.shape[0]
    n_chunks = (i * QB + QB + KC - 1) // KC
    lane = lax.broadcasted_iota(I32, (1, LANES), 1)
    left = lane < hd

    @pl.when(i == 0)
    def _():
        def fill(t, _):
            rows = pl.ds(pl.multiple_of(t * KT, KT), KT)
            kb_ref[rows, :] = k_ref[rows, :].astype(BF16)
            kib_ref[rows, :] = ki_ref[rows, :].astype(BF16)
            vt = v_ref[rows, :].T.astype(BF16)
            for g in range(ATTN_KV_HEADS):
                vt_ref[t, g, 0:hd, :] = vt[g * hd:(g + 1) * hd, :]
                vt_ref[t, g, hd:hd + ONES_ROWS, :] = jnp.ones((ONES_ROWS, KT), BF16)
            return 0
        lax.fori_loop(0, seq // KT, fill, 0)
        planes_ref[...] = jnp.zeros(planes_ref.shape, I32)

    qpos = i * QB + lax.broadcasted_iota(I32, (1, QB), 1)
    idx_scale = (IDX_DIM ** -0.5) * (IDX_HEADS ** -0.5)
    wt = (wi_ref[...] * idx_scale).T
    w_rows = [wt[DT_LANES + h:DT_LANES + h + 1, :] for h in range(IDX_HEADS)]

    heads_per_group = ATTN_HEADS // ATTN_KV_HEADS
    for p in range(ATTN_HEADS // 2):
        slab = q_ref[:, p * LANES:(p + 1) * LANES].astype(F32) * ((hd ** -0.5) * LOG2E)
        swapped = pltpu.roll(slab, hd, 1)
        if (2 * p) // heads_per_group == 0:
            even, odd = jnp.where(left, slab, 0.0), jnp.where(left, swapped, 0.0)
        else:
            even, odd = jnp.where(left, 0.0, swapped), jnp.where(left, 0.0, slab)
        qs_ref[2 * p] = even.astype(BF16)
        qs_ref[2 * p + 1] = odd.astype(BF16)
    for p in range(IDX_HEADS // 2):
        slab = qi_ref[:, p * LANES:(p + 1) * LANES]
        qis_ref[2 * p] = slab.astype(BF16)
        qis_ref[2 * p + 1] = pltpu.roll(slab, hd, 1).astype(BF16)

    n_tiles = n_chunks * TPC
    nkt = seq // KT
    rows_per_word = KT // 32

    def score_tile(t, _):
        kic = kib_ref[pl.ds(pl.multiple_of(t * KT, KT), KT), :]
        acc = jnp.zeros((KT, QB), F32)
        for p in range(IDX_HEADS // 2):
            rel = _dot_nt(kic, qis_ref[2 * p:2 * p + 2].reshape(2 * QB, LANES))
            acc = acc + w_rows[2 * p] * jnp.maximum(rel[:, :QB], 0.0)
            acc = acc + w_rows[2 * p + 1] * jnp.maximum(rel[:, QB:], 0.0)
        kpos = t * KT + lax.broadcasted_iota(I32, (KT, QB), 0)
        keys = jnp.where(kpos <= qpos, _sortable_key(acc), INT_MIN)
        keys_ref[t] = keys
        unsigned = keys ^ INT_MIN
        planes = _bit_planes([unsigned[r * rows_per_word:(r + 1) * rows_per_word, :] for r in range(32)])
        for bit in range(32):
            planes_ref[t, bit] = planes[bit]
        return 0
    lax.fori_loop(0, n_tiles, score_tile, 0)

    select_all = qpos < topk
    for t in range(nkt):
        cand_ref[t] = jnp.where(t < n_tiles, jnp.full((rows_per_word, QB), -1, I32), 0)
        above_ref[t] = jnp.zeros((rows_per_word, QB), I32)

    def radix_step(it, need):
        cnt = jnp.zeros((rows_per_word, QB), I32)
        for t in range(nkt):
            cnt = cnt + lax.population_count(cand_ref[t] & planes_ref[t, it])
        cnt = jnp.sum(cnt, axis=0, keepdims=True)
        take = cnt >= need
        for t in range(nkt):
            cand = cand_ref[t]
            ones = cand & planes_ref[t, it]
            cand_ref[t] = jnp.where(take, ones, cand ^ ones)
            above_ref[t] = jnp.where(take, above_ref[t], above_ref[t] | ones)
        return jnp.where(take, need, need - cnt)
    need = lax.fori_loop(0, 32, radix_step, jnp.full((1, QB), topk, I32))

    ties = jnp.zeros((rows_per_word, QB), I32)
    for t in range(nkt):
        ties = ties + lax.population_count(cand_ref[t])
        sel_ref[t] = jnp.where(select_all, -1, above_ref[t] | cand_ref[t])
    surplus = jnp.where(select_all, 0, jnp.sum(ties, axis=0, keepdims=True) - need)

    @pl.when(jnp.max(surplus) > 0)
    def _():
        def count(pred):
            def body(c, acc):
                kc = keys_ref[pl.ds(c * TPC, TPC)].reshape(KC, QB)
                hit = jnp.where(pred(kc, c), 1, 0)
                return acc + jnp.sum(hit.reshape(KC // 8, 8, QB), axis=0)
            acc = lax.fori_loop(0, n_chunks, body, jnp.zeros((8, QB), I32))
            return jnp.sum(acc, axis=0, keepdims=True)

        def bit_step(it, thr):
            cand = thr ^ lax.shift_left(jnp.int32(1), 31 - it)
            cnt = count(lambda kc, c: kc >= cand)
            return jnp.where(cnt >= topk, cand, thr)
        thr = lax.fori_loop(0, 32, bit_step, jnp.full((1, QB), INT_MIN, I32))
        want = topk - count(lambda kc, c: kc > thr)
        n_bits = seq.bit_length()

        def tie_step(it, jb):
            cand = jb | lax.shift_left(jnp.int32(1), n_bits - 1 - it)
            def pred(kc, c):
                kpos = c * KC + lax.broadcasted_iota(I32, (KC, QB), 0)
                return jnp.logical_and(kc == thr, kpos < cand)
            return jnp.where(count(pred) <= want, cand, jb)
        jb = lax.fori_loop(0, n_bits, tie_step, jnp.zeros((1, QB), I32))

        def pack_tile(t, _):
            kc = keys_ref[t]
            kpos = t * KT + lax.broadcasted_iota(I32, (KT, QB), 0)
            chosen = jnp.logical_or(kc > thr, jnp.logical_and(kc == thr, kpos < jb))
            word = jnp.zeros((rows_per_word, QB), I32)
            for r in range(32):
                bit = np.int32(np.uint32(1 << r))
                word = word | jnp.where(chosen[r * rows_per_word:(r + 1) * rows_per_word, :], bit, 0)
            sel_ref[t] = jnp.where(select_all, -1, word)
            return 0
        lax.fori_loop(0, n_tiles, pack_tile, 0)

    m_ref[...] = jnp.full(m_ref.shape, NEG_BIG, F32)
    l_ref[...] = jnp.zeros(l_ref.shape, F32)
    acc_ref[...] = jnp.zeros(acc_ref.shape, F32)

    def masked_scores(t, dst_ref, dmax_ref):
        word = sel_ref[t]
        picked = jnp.concatenate([lax.shift_left(word, jnp.int32(31 - r)) for r in range(32)], axis=0)
        kpos = t * KT + lax.broadcasted_iota(I32, (KT, QB), 0)
        sel = jnp.logical_and(picked < 0, kpos <= qpos)
        bias = jnp.where(sel, 0.0, NEG_BIG)
        bias2 = jnp.concatenate([bias, bias], axis=1)
        kc = kb_ref[pl.ds(pl.multiple_of(t * KT, KT), KT), :]
        for p in range(ATTN_HEADS // 2):
            s = _dot_nt(kc, qs_ref[2 * p:2 * p + 2].reshape(2 * QB, LANES)) + bias2
            dst_ref[p] = s
            dmax_ref[p] = jnp.max(s, axis=0, keepdims=True)

    def softmax_pv(t, src_ref, smax_ref):
        for p in range(ATTN_HEADS // 2):
            g = (2 * p) // heads_per_group
            m_old = m_ref[p]
            m_new = jnp.maximum(m_old, smax_ref[p])
            alpha = jnp.exp2(m_old - m_new)
            pt = jnp.exp2(src_ref[p] - m_new).astype(BF16)
            m_ref[p] = m_new
            pv = _dot(vt_ref[t, g], pt)
            l_ref[p] = alpha * l_ref[p] + pv[hd:hd + 1, :]
            acc_ref[p] = alpha * acc_ref[p] + pv[0:hd, :]

    masked_scores(0, s0_ref, x0_ref)

    def attend(u, _):
        t = 2 * u
        masked_scores(t + 1, s1_ref, x1_ref)
        softmax_pv(t, s0_ref, x0_ref)
        masked_scores(jnp.minimum(t + 2, n_tiles - 1), s0_ref, x0_ref)
        softmax_pv(t + 1, s1_ref, x1_ref)
        return 0
    lax.fori_loop(0, n_tiles // 2, attend, 0)

    for p in range(ATTN_HEADS // 2):
        o = acc_ref[p] / l_ref[p]
        both = jnp.concatenate([o[:, :QB], o[:, QB:]], axis=0)
        o_ref[:, p * LANES:(p + 1) * LANES] = both.T.astype(o_ref.dtype)


def _dsa_branch(main3d, misc3d):
    bsz, seq, _ = main3d.shape
    topk = min(TOPK_MAX, seq // 4)
    width = ATTN_HEADS * ATTN_HEAD_DIM
    nkt = seq // KEY_TILE
    n_pairs = ATTN_HEADS // 2
    kern = functools.partial(_dsa_kernel, topk=topk)
    return pl.pallas_call(
        kern,
        grid=(bsz, seq // Q_BLOCK),
        in_specs=[
            pl.BlockSpec((None, Q_BLOCK, width), lambda b, i: (b, i, Q_TILE)),
            pl.BlockSpec((None, Q_BLOCK, IDX_HEADS * IDX_DIM), lambda b, i: (b, i, COL_QI)),
            pl.BlockSpec((None, seq, LANES), lambda b, i: (b, 0, COL_K)),
            pl.BlockSpec((None, seq, LANES), lambda b, i: (b, 0, COL_V)),
            pl.BlockSpec((None, seq, LANES), lambda b, i: (b, 0, COL_KI)),
            pl.BlockSpec((None, Q_BLOCK, LANES), lambda b, i: (b, i, COL_DTWI)),
        ],
        out_specs=pl.BlockSpec((None, Q_BLOCK, width), lambda b, i: (b, i, 0)),
        out_shape=jax.ShapeDtypeStruct((bsz, seq, width), BF16),
        scratch_shapes=[
            pltpu.VMEM((seq, LANES), BF16),
            pltpu.VMEM((seq, LANES), BF16),
            pltpu.VMEM((nkt, ATTN_KV_HEADS, ATTN_HEAD_DIM + ONES_ROWS, KEY_TILE), BF16),
            pltpu.VMEM((ATTN_HEADS, Q_BLOCK, LANES), BF16),
            pltpu.VMEM((IDX_HEADS, Q_BLOCK, LANES), BF16),
            pltpu.VMEM((nkt, KEY_TILE, Q_BLOCK), I32),
            pltpu.VMEM((nkt, 32, KEY_TILE // 32, Q_BLOCK), I32),
            pltpu.VMEM((nkt, KEY_TILE // 32, Q_BLOCK), I32),
            pltpu.VMEM((nkt, KEY_TILE // 32, Q_BLOCK), I32),
            pltpu.VMEM((nkt, KEY_TILE // 32, Q_BLOCK), I32),
            pltpu.VMEM((n_pairs, KEY_TILE, 2 * Q_BLOCK), F32),
            pltpu.VMEM((n_pairs, KEY_TILE, 2 * Q_BLOCK), F32),
            pltpu.VMEM((n_pairs, 1, 2 * Q_BLOCK), F32),
            pltpu.VMEM((n_pairs, 1, 2 * Q_BLOCK), F32),
            pltpu.VMEM((n_pairs, 1, 2 * Q_BLOCK), F32),
            pltpu.VMEM((n_pairs, 1, 2 * Q_BLOCK), F32),
            pltpu.VMEM((n_pairs, ATTN_HEAD_DIM, 2 * Q_BLOCK), F32),
        ],
        compiler_params=_cparams(("parallel", "arbitrary")),
        name="dsa_attention",
    )(main3d, misc3d, misc3d, misc3d, misc3d, misc3d)


def _merge_kernel(h_ref, ys_ref, ya_ref, gs_ref, ga_ref, wso_ref, wao_ref, wo_ref, o_ref):
    a = _dot(ys_ref[...], wso_ref[...])
    b = _dot(ya_ref[...], wao_ref[...])
    merged = jax.nn.sigmoid(gs_ref[...].astype(F32)) * a + jax.nn.sigmoid(ga_ref[...].astype(F32)) * b
    o_ref[...] = h_ref[...] + _dot(merged.astype(BF16), wo_ref[...])


def _merge(h2d, y_ssd, y_attn, proj2d, wso, wao, wo):
    n, d = h2d.shape
    tm = 512
    full = lambda a: pl.BlockSpec(a.shape, lambda i: (0, 0))
    return pl.pallas_call(
        _merge_kernel,
        grid=(n // tm,),
        in_specs=[
            pl.BlockSpec((tm, d), lambda i: (i, 0)),
            pl.BlockSpec((tm, y_ssd.shape[1]), lambda i: (i, 0)),
            pl.BlockSpec((tm, y_attn.shape[1]), lambda i: (i, 0)),
            pl.BlockSpec((tm, d), lambda i: (i, 6)),
            pl.BlockSpec((tm, d), lambda i: (i, 7)),
            full(wso), full(wao), full(wo),
        ],
        out_specs=pl.BlockSpec((tm, d), lambda i: (i, 0)),
        out_shape=jax.ShapeDtypeStruct((n, d), F32),
        compiler_params=_cparams(("parallel",)),
        name="merge_out",
    )(h2d, y_ssd, y_attn, proj2d, proj2d, wso, wao, wo)


def _normproj_kernel(x_ref, g_ref, w_ref, o_ref):
    o_ref[...] = _dot(_rms(x_ref[...], g_ref[...]).astype(BF16), w_ref[...]).astype(o_ref.dtype)


def _norm_project(x2d, gain, w_bf16, out_dtype):
    n, d = x2d.shape
    tm = min(512, n)
    return pl.pallas_call(
        _normproj_kernel,
        grid=(n // tm,),
        in_specs=[pl.BlockSpec((tm, d), lambda i: (i, 0)), pl.BlockSpec((1, d), lambda i: (0, 0)),
                  pl.BlockSpec(w_bf16.shape, lambda i: (0, 0))],
        out_specs=pl.BlockSpec((tm, w_bf16.shape[1]), lambda i: (i, 0)),
        out_shape=jax.ShapeDtypeStruct((n, w_bf16.shape[1]), out_dtype),
        compiler_params=_cparams(("parallel",)),
        name="mem_kv_projection",
    )(x2d, gain.reshape(1, d), w_bf16)


def _xattn_kernel(h_ref, g_ref, kv_ref, wq_ref, wo_ref, o_ref):
    d = h_ref.shape[1]
    hd = d // XATTN_HEADS
    h = h_ref[...]
    q = _dot(_rms(h, g_ref[...]).astype(BF16), wq_ref[...]) * (hd ** -0.5)
    qb = q.astype(BF16)
    outs = []
    for a in range(XATTN_HEADS):
        k = kv_ref[:, a * hd:(a + 1) * hd]
        v = kv_ref[:, d + a * hd:d + (a + 1) * hd]
        s = _dot_nt(qb[:, a * hd:(a + 1) * hd], k)
        p = jnp.exp(s - jnp.max(s, axis=1, keepdims=True))
        o = _dot(p.astype(BF16), v) / jnp.sum(p, axis=1, keepdims=True)
        outs.append(o.astype(BF16))
    o_ref[...] = h + _dot(jnp.concatenate(outs, axis=1), wo_ref[...])


def _cross_attention(h3d, gain, kv3d, wq, wo):
    bsz, seq, d = h3d.shape
    tm = min(512, seq)
    mlen = kv3d.shape[1]
    return pl.pallas_call(
        _xattn_kernel,
        grid=(bsz, seq // tm),
        in_specs=[
            pl.BlockSpec((None, tm, d), lambda b, i: (b, i, 0)),
            pl.BlockSpec((1, d), lambda b, i: (0, 0)),
            pl.BlockSpec((None, mlen, 2 * d), lambda b, i: (b, 0, 0)),
            pl.BlockSpec(wq.shape, lambda b, i: (0, 0)),
            pl.BlockSpec(wo.shape, lambda b, i: (0, 0)),
        ],
        out_specs=pl.BlockSpec((None, tm, d), lambda b, i: (b, i, 0)),
        out_shape=jax.ShapeDtypeStruct((bsz, seq, d), F32),
        compiler_params=_cparams(("parallel", "parallel")),
        name="cross_attention",
    )(h3d, gain.reshape(1, d), kv3d, wq, wo)


def _router_kernel(h_ref, g_ref, wr_ref, br_ref, trit_ref, aug_ref, rank_ref, gsel_ref, cnt_ref, base_ref):
    i = pl.program_id(0)
    d = h_ref.shape[1]

    @pl.when(i == 0)
    def _():
        base_ref[...] = jnp.zeros(base_ref.shape, F32)

    hn = _rms(h_ref[...], g_ref[...])
    aug_ref[:, 0:d] = hn
    h_hi, h_mid, h_lo = _split3(hn)
    w_hi, w_mid, w_lo = wr_ref[0], wr_ref[1], wr_ref[2]
    logits = (_dot(h_hi, w_hi) + (_dot(h_hi, w_mid) + _dot(h_mid, w_hi))
              + (_dot(h_hi, w_lo) + _dot(h_mid, w_mid) + _dot(h_lo, w_hi))) + br_ref[...]
    tm = logits.shape[0]
    lane = lax.broadcasted_iota(I32, (tm, LANES), 1)
    neg_inf = -jnp.inf
    n_e = MOE_GROUPS * MOE_EXPERTS_PER_GROUP
    is_g = jnp.logical_and(lane >= n_e, lane < n_e + MOE_GROUPS)
    gl = jnp.where(is_g, logits, neg_inf)
    g_max = jnp.max(gl, axis=1, keepdims=True)
    g_sel = jnp.min(jnp.where(gl == g_max, lane, LANES), axis=1, keepdims=True) - n_e
    g_w = 1.0 / jnp.sum(jnp.where(is_g, jnp.exp(gl - g_max), 0.0), axis=1, keepdims=True)
    in_grp = jnp.logical_and(lane >= g_sel * MOE_EXPERTS_PER_GROUP,
                             lane < (g_sel + 1) * MOE_EXPERTS_PER_GROUP)
    el = jnp.where(in_grp, logits, neg_inf)
    v1 = jnp.max(el, axis=1, keepdims=True)
    i1 = jnp.min(jnp.where(el == v1, lane, LANES), axis=1, keepdims=True)
    el2 = jnp.where(lane == i1, neg_inf, el)
    v2 = jnp.max(el2, axis=1, keepdims=True)
    i2 = jnp.min(jnp.where(el2 == v2, lane, LANES), axis=1, keepdims=True)
    e2 = jnp.exp(v2 - v1)
    w1 = g_w / (1.0 + e2)
    w2 = g_w * e2 / (1.0 + e2)
    aug_ref[:, d:] = jnp.where(lane == i1, w1, 0.0) + jnp.where(lane == i2, w2, 0.0)

    onehot_t = jnp.where(lane == g_sel, 1.0, 0.0).T
    upto = _dot(onehot_t.astype(BF16), trit_ref[...]) + base_ref[...]
    grp = lax.broadcasted_iota(I32, onehot_t.shape, 0).astype(F32)
    rank_ref[...] = (jnp.sum(onehot_t * upto, axis=0, keepdims=True) - 1.0).astype(I32)
    gsel_ref[...] = jnp.sum(onehot_t * grp, axis=0, keepdims=True).astype(I32)
    base_ref[...] = upto[:, tm - 1:tm]
    cnt_ref[...] = jnp.broadcast_to(upto[:, tm - 1:tm], cnt_ref.shape)


def _moe_route(h2d, gain, wr3, br):
    n, d = h2d.shape
    tm = 512
    nt = n // tm
    trit = (jnp.arange(tm)[:, None] <= jnp.arange(tm)[None, :]).astype(BF16)
    return pl.pallas_call(
        _router_kernel,
        grid=(nt,),
        in_specs=[pl.BlockSpec((tm, d), lambda i: (i, 0)), pl.BlockSpec((1, d), lambda i: (0, 0)),
                  pl.BlockSpec(wr3.shape, lambda i: (0, 0, 0)), pl.BlockSpec((1, LANES), lambda i: (0, 0)),
                  pl.BlockSpec((tm, tm), lambda i: (0, 0))],
        out_specs=[pl.BlockSpec((tm, d + LANES), lambda i: (i, 0)),
                   pl.BlockSpec((None, 1, tm), lambda i: (i, 0, 0)),
                   pl.BlockSpec((None, 1, tm), lambda i: (i, 0, 0)),
                   pl.BlockSpec((LANES, LANES), lambda i: (0, 0))],
        out_shape=[jax.ShapeDtypeStruct((n, d + LANES), F32),
                   jax.ShapeDtypeStruct((nt, 1, tm), I32),
                   jax.ShapeDtypeStruct((nt, 1, tm), I32),
                   jax.ShapeDtypeStruct((LANES, LANES), F32)],
        scratch_shapes=[pltpu.VMEM((LANES, 1), F32)],
        compiler_params=_cparams(("arbitrary",)),
        name="moe_router",
    )(h2d, gain.reshape(1, d), wr3, br, trit)


MOE_TILE = 512
ROW_BLOCK = 256
ISSUE_UNROLL = 8


def _scatter_rows_kernel(dest_ref, src_ref, init_ref, out_ref, sem):
    del init_ref
    base = pl.program_id(0) * ROW_BLOCK

    def row_copy(r):
        return pltpu.make_async_copy(src_ref.at[pl.ds(r, 1)], out_ref.at[pl.ds(dest_ref[base + r], 1)], sem)

    def issue(r, _):
        row_copy(r).start()
        return 0
    lax.fori_loop(0, ROW_BLOCK, issue, 0, unroll=ISSUE_UNROLL)
    pltpu.make_async_copy(src_ref, out_ref.at[pl.ds(0, ROW_BLOCK)], sem).wait()


def _scatter_rows(src, dest, n_out):
    n, w = src.shape
    init = jnp.zeros((n_out, w), src.dtype)
    return pl.pallas_call(
        _scatter_rows_kernel,
        grid_spec=pltpu.PrefetchScalarGridSpec(
            num_scalar_prefetch=1,
            grid=(n // ROW_BLOCK,),
            in_specs=[pl.BlockSpec((ROW_BLOCK, w), lambda i, dest: (i, 0)),
                      pl.BlockSpec(memory_space=pl.ANY)],
            out_specs=pl.BlockSpec(memory_space=pl.ANY),
            scratch_shapes=[pltpu.SemaphoreType.DMA(())],
        ),
        out_shape=jax.ShapeDtypeStruct((n_out, w), src.dtype),
        input_output_aliases={2: 0},
        compiler_params=_cparams(("arbitrary",)),
        name="moe_sort_rows",
    )(dest, src, init)


def _sorted_experts_kernel(tile_group_ref, n_used_ref, hn_ref, cw_ref, wg_ref, wu_ref, wd_ref, o_ref):
    j = pl.program_id(0)

    @pl.when(j < n_used_ref[0])
    def _():
        hn = hn_ref[...].astype(BF16)
        first = tile_group_ref[j] * MOE_EXPERTS_PER_GROUP
        lane = lax.broadcasted_iota(I32, cw_ref.shape, 1)
        cw = cw_ref[...]
        acc = jnp.zeros(o_ref.shape, F32)
        for e in range(MOE_EXPERTS_PER_GROUP):
            w_e = jnp.sum(jnp.where(lane == first + e, cw, 0.0), axis=1, keepdims=True)
            hid = _silu(_dot(hn, wg_ref[e])) * _dot(hn, wu_ref[e]) * w_e
            acc = acc + _dot(hid.astype(BF16), wd_ref[e])
        o_ref[...] = acc

    @pl.when(j >= n_used_ref[0])
    def _():
        o_ref[...] = jnp.zeros(o_ref.shape, F32)


def _sorted_experts(aug_sorted, tile_group, n_used, wg, wu, wd):
    n_pad, wa = aug_sorted.shape
    d = wa - LANES
    _, n_e, _, dff = wg.shape
    tm = MOE_TILE
    return pl.pallas_call(
        _sorted_experts_kernel,
        grid_spec=pltpu.PrefetchScalarGridSpec(
            num_scalar_prefetch=2,
            grid=(n_pad // tm,),
            in_specs=[
                pl.BlockSpec((tm, d), lambda j, tg, nu: (j, 0)),
                pl.BlockSpec((tm, LANES), lambda j, tg, nu: (j, d // LANES)),
                pl.BlockSpec((None, n_e, d, dff), lambda j, tg, nu: (tg[j], 0, 0, 0)),
                pl.BlockSpec((None, n_e, d, dff), lambda j, tg, nu: (tg[j], 0, 0, 0)),
                pl.BlockSpec((None, n_e, dff, d), lambda j, tg, nu: (tg[j], 0, 0, 0)),
            ],
            out_specs=pl.BlockSpec((tm, d), lambda j, tg, nu: (j, 0)),
        ),
        out_shape=jax.ShapeDtypeStruct((n_pad, d), F32),
        compiler_params=_cparams(("arbitrary",)),
        name="moe_experts",
    )(tile_group, n_used, aug_sorted, aug_sorted, wg, wu, wd)


def _gather_add_kernel(dest_ref, h_ref, delta_ref, fg_ref, o_ref, buf_ref, sem, *, final_norm):
    i = pl.program_id(0)
    n_steps = pl.num_programs(0)

    def row_copy(step, r):
        slot = lax.rem(step, 2)
        return pltpu.make_async_copy(delta_ref.at[pl.ds(dest_ref[step * ROW_BLOCK + r], 1)],
                                     buf_ref.at[slot, pl.ds(r, 1)], sem.at[slot])

    def issue_all(step):
        def issue(r, _):
            row_copy(step, r).start()
            return 0
        lax.fori_loop(0, ROW_BLOCK, issue, 0, unroll=ISSUE_UNROLL)

    @pl.when(i == 0)
    def _():
        issue_all(0)

    @pl.when(i + 1 < n_steps)
    def _():
        issue_all(i + 1)

    slot = lax.rem(i, 2)
    pltpu.make_async_copy(delta_ref.at[pl.ds(0, ROW_BLOCK)], buf_ref.at[slot], sem.at[slot]).wait()

    out = h_ref[...] + buf_ref[slot]
    if final_norm:
        out = _rms(out, fg_ref[...])
    o_ref[...] = out


def _gather_add(h2d, delta_sorted, dest, final_gain, final_norm):
    n, d = h2d.shape
    kern = functools.partial(_gather_add_kernel, final_norm=final_norm)
    return pl.pallas_call(
        kern,
        grid_spec=pltpu.PrefetchScalarGridSpec(
            num_scalar_prefetch=1,
            grid=(n // ROW_BLOCK,),
            in_specs=[pl.BlockSpec((ROW_BLOCK, d), lambda i, dest: (i, 0)),
                      pl.BlockSpec(memory_space=pl.ANY),
                      pl.BlockSpec((1, d), lambda i, dest: (0, 0))],
            out_specs=pl.BlockSpec((ROW_BLOCK, d), lambda i, dest: (i, 0)),
            scratch_shapes=[pltpu.VMEM((2, ROW_BLOCK, d), F32), pltpu.SemaphoreType.DMA((2,))],
        ),
        out_shape=jax.ShapeDtypeStruct((n, d), F32),
        compiler_params=_cparams(("arbitrary",)),
        name="moe_unsort_add",
    )(dest, h2d, delta_sorted, final_gain.reshape(1, d))


def _moe_layout(rank, gsel, counts, n):
    tm = MOE_TILE
    cnt = counts[:MOE_GROUPS, 0].astype(I32)
    seg = ((cnt + tm - 1) // tm) * tm
    ends = jnp.cumsum(seg)
    dest = (ends - seg)[gsel.reshape(n)] + rank.reshape(n)
    n_tiles = n // tm + MOE_GROUPS
    starts = jnp.arange(n_tiles, dtype=I32) * tm
    tile_group = jnp.minimum(jnp.sum(starts[:, None] >= ends[None, :], axis=1), MOE_GROUPS - 1).astype(I32)
    n_used = (ends[-1:] // tm).astype(I32)
    return dest.astype(I32), tile_group, n_used, n_tiles * tm


def _prep_router(w_group, b_group, w_router, b_router):
    d = w_group.shape[0]
    w = jnp.concatenate([w_router, w_group], axis=1).astype(F32)
    w = jnp.pad(w, ((0, 0), (0, LANES - w.shape[1])))
    hi = w.astype(BF16)
    r1 = w - hi.astype(F32)
    mid = r1.astype(BF16)
    lo = (r1 - mid.astype(F32)).astype(BF16)
    b = jnp.concatenate([b_router, b_group]).astype(F32)
    b = jnp.pad(b, (0, LANES - b.shape[0])).reshape(1, LANES)
    del d
    return jnp.stack([hi, mid, lo]), b


def kernel(x, mem, norm_mix, w_in, conv_w, conv_b, dt_bias, a_log, d_skip, ssd_norm, w_ssd_o, w_attn_o, w_out,
           norm_xattn, norm_mem, w_cq, w_ckv, w_co, norm_ffn, w_group, b_group, w_router, b_router,
           w_gate_e, w_up_e, w_down_e, norm_final):
    bsz, seq, d = x.shape
    depth = w_in.shape[0]
    mlen = mem.shape[1]
    heads = dt_bias.shape[1]
    d_inner = heads * SSD_HEAD_DIM
    conv_ch = conv_w.shape[2]
    n = bsz * seq
    rope_c, rope_s1, rope_s2 = _rope_tables(seq)
    h = x.reshape(n, d)
    mem2d = mem.reshape(bsz * mlen, d)
    for l in range(depth):
        w_in_l = _prep_w_in(w_in[l], d_inner, conv_ch, heads)
        main, misc = _in_projection(h, norm_mix[l], w_in_l, rope_c, rope_s1, rope_s2, seq)
        main3d = main.reshape(bsz, seq, main.shape[1])
        misc3d = misc.reshape(bsz, seq, misc.shape[1])
        y_ssd = _ssd_branch(main3d, misc3d, conv_w[l], conv_b[l], dt_bias[l], a_log[l], d_skip[l], ssd_norm[l])
        y_attn = _dsa_branch(main3d, misc3d)
        h = _merge(h, y_ssd.reshape(n, d_inner), y_attn.reshape(n, -1), main,
                   w_ssd_o[l].astype(BF16), w_attn_o[l].astype(BF16), w_out[l].astype(BF16))
        kv = _norm_project(mem2d, norm_mem[l], w_ckv[l].astype(BF16), BF16)
        h = _cross_attention(h.reshape(bsz, seq, d), norm_xattn[l], kv.reshape(bsz, mlen, 2 * d),
                             w_cq[l].astype(BF16), w_co[l].astype(BF16)).reshape(n, d)
        wr3, br = _prep_router(w_group[l], b_group[l], w_router[l], b_router[l])
        aug, rank, gsel, counts = _moe_route(h, norm_ffn[l], wr3, br)
        dest, tile_group, n_used, n_pad = _moe_layout(rank, gsel, counts, n)
        aug_sorted = _scatter_rows(aug, dest, n_pad)
        delta = _sorted_experts(aug_sorted, tile_group, n_used, w_gate_e[l].astype(BF16),
                                w_up_e[l].astype(BF16), w_down_e[l].astype(BF16))
        h = _gather_add(h, delta, dest, norm_final, final_norm=(l == depth - 1))
    return h.reshape(bsz, seq, d)
```

```python
import functools
import math

import jax
import jax.numpy as jnp
import numpy as np
from jax import lax
from jax.experimental import pallas as pl
from jax.experimental.pallas import tpu as pltpu

F32 = jnp.float32
BF16 = jnp.bfloat16
I32 = jnp.int32

SSD_HEAD_DIM = 64
SSD_GROUPS = 4
SSD_STATE = 128
SSD_CONV = 4
SSD_CHUNK = 128
ATTN_HEADS = 16
ATTN_KV_HEADS = 2
ATTN_HEAD_DIM = 64
Q_BLOCK = 128
TOPK_MAX = 256
IDX_HEADS = 8
IDX_DIM = 64
ROPE_THETA = 10000.0
XATTN_HEADS = 4
MOE_GROUPS = 4
MOE_EXPERTS_PER_GROUP = 8
MOE_TOPK_IN_GROUP = 2
RMS_EPS = 1e-6

LANES = 128
KEY_CHUNK = 512
KEY_TILE = 256
INT_MIN = -2 ** 31
NEG_BIG = -1e30
LOG2E = math.log2(math.e)
ONES_ROWS = 16
VMEM_LIMIT = 56 * 1024 * 1024


def _cparams(sem):
    return pltpu.CompilerParams(dimension_semantics=sem, vmem_limit_bytes=VMEM_LIMIT)


def _split3(v):
    hi = v.astype(BF16)
    r1 = v - hi.astype(F32)
    mid = r1.astype(BF16)
    lo = (r1 - mid.astype(F32)).astype(BF16)
    return hi, mid, lo


def _dot(a, b):
    return jnp.dot(a, b, preferred_element_type=F32)


def _dot_nt(a, b):
    return lax.dot_general(a, b, (((1,), (1,)), ((), ())), preferred_element_type=F32)


def _spread(v, m01, terms):
    parts = _split3(v)[:terms]
    out = _dot(parts[0], m01)
    for part in parts[1:]:
        out = out + _dot(part, m01)
    return out


def _dot3_lhs01(m01, v):
    hi, mid, lo = _split3(v)
    return _dot(m01, hi) + _dot(m01, mid) + _dot(m01, lo)


def _rms(x, g):
    ms = jnp.mean(x * x, axis=-1, keepdims=True)
    return x * lax.rsqrt(ms + RMS_EPS) * g


def _silu(x):
    return x * jax.nn.sigmoid(x)


Q_TILE = 5
MISC_TILE = 8
MISC_ROPE_CHUNKS = (0, 1, 2, 3, 4, 6)
COL_QI = 0
COL_K, COL_V, COL_KI, COL_DTWI = 4, 5, 6, 7
DT_LANES = 32


def _rope_chunk(a, c, s1, s2):
    return a * c + pltpu.roll(a, 32, 1) * s1 + pltpu.roll(a, 96, 1) * s2


def _inproj_kernel(x_ref, g_ref, w_ref, c_ref, s1_ref, s2_ref, o_ref, misc_ref, xn_ref):
    j = pl.program_id(1)

    @pl.when(j == 0)
    def _():
        xn_ref[...] = _rms(x_ref[...], g_ref[...]).astype(BF16)

    acc = _dot(xn_ref[...], w_ref[...])
    n_chunks = acc.shape[1] // LANES

    def roped(dst_ref, chunks):
        c, s1, s2 = c_ref[...], s1_ref[...], s2_ref[...]
        for k in range(n_chunks):
            a = acc[:, k * LANES:(k + 1) * LANES]
            if k in chunks:
                a = _rope_chunk(a, c, s1, s2)
            dst_ref[:, k * LANES:(k + 1) * LANES] = a.astype(dst_ref.dtype)

    @pl.when(j == Q_TILE)
    def _():
        roped(o_ref, tuple(range(n_chunks)))

    @pl.when(j == MISC_TILE)
    def _():
        roped(misc_ref, MISC_ROPE_CHUNKS)

    @pl.when(jnp.logical_and(j != Q_TILE, j != MISC_TILE))
    def _():
        o_ref[...] = acc.astype(o_ref.dtype)


def _in_projection(x2d, gain, w_bf16, rope_c, rope_s1, rope_s2, seq):
    n, d = x2d.shape
    tm = min(1024, seq)
    tn = 1024
    pos_tiles = seq // tm
    n_tiles = w_bf16.shape[1] // tn
    assert n_tiles == MISC_TILE + 1
    return pl.pallas_call(
        _inproj_kernel,
        grid=(n // tm, n_tiles),
        in_specs=[
            pl.BlockSpec((tm, d), lambda i, j: (i, 0)),
            pl.BlockSpec((1, d), lambda i, j: (0, 0)),
            pl.BlockSpec((d, tn), lambda i, j: (0, j)),
            pl.BlockSpec((tm, LANES), lambda i, j: (i % pos_tiles, 0)),
            pl.BlockSpec((tm, LANES), lambda i, j: (i % pos_tiles, 0)),
            pl.BlockSpec((tm, LANES), lambda i, j: (i % pos_tiles, 0)),
        ],
        out_specs=[pl.BlockSpec((tm, tn), lambda i, j: (i, jnp.minimum(j, MISC_TILE - 1))),
                   pl.BlockSpec((tm, tn), lambda i, j: (i, 0))],
        out_shape=[jax.ShapeDtypeStruct((n, MISC_TILE * tn), BF16), jax.ShapeDtypeStruct((n, tn), F32)],
        scratch_shapes=[pltpu.VMEM((tm, d), BF16)],
        compiler_params=_cparams(("parallel", "arbitrary")),
        name="in_projection",
    )(x2d, gain.reshape(1, d), w_bf16, rope_c, rope_s1, rope_s2)


def _prep_w_in(w, d_inner, conv_ch, ssd_heads):
    widths = (d_inner, conv_ch, ssd_heads, ATTN_HEADS * ATTN_HEAD_DIM, ATTN_KV_HEADS * ATTN_HEAD_DIM,
              ATTN_KV_HEADS * ATTN_HEAD_DIM, IDX_HEADS * IDX_DIM, IDX_DIM, IDX_HEADS,
              w.shape[0], w.shape[0])
    splits = [int(p) for p in np.cumsum(widths)[:-1]]
    z, xbc, dt, q, k, v, qi, ki, wi, gs, ga = jnp.split(w, splits, axis=1)

    def pad(a, n):
        return jnp.pad(a, ((0, 0), (0, n - a.shape[1])))

    misc = jnp.concatenate([qi, k, v, pad(ki, LANES), pad(jnp.concatenate([dt, wi], axis=1), LANES)], axis=1)
    return jnp.concatenate([z, xbc, q, gs, ga, misc], axis=1).astype(BF16)


def _rope_tables(seq):
    half = ATTN_HEAD_DIM // 2
    inv = 1.0 / (ROPE_THETA ** (jnp.arange(0, ATTN_HEAD_DIM, 2, dtype=F32) / ATTN_HEAD_DIM))
    ang = jnp.arange(seq, dtype=F32)[:, None] * inv[None, :]
    cos, sin = jnp.cos(ang), jnp.sin(ang)
    zero = jnp.zeros_like(sin)
    reps = LANES // ATTN_HEAD_DIM
    c = jnp.tile(jnp.concatenate([cos, cos], axis=1), (1, reps))
    s1 = jnp.tile(jnp.concatenate([zero, sin], axis=1), (1, reps))
    s2 = jnp.tile(jnp.concatenate([-sin, zero], axis=1), (1, reps))
    del half
    return c, s1, s2


def _ssd_kernel(z_ref, xs_ref, bc_ref, dt_ref, cw_ref, cb_ref, dtb_ref, alog_ref, dsk_ref, ng_ref,
                r64_ref, tri_ref, shift_ref, y_ref, prev_ref, st_ref, act_ref, *, heads, groups):
    c = pl.program_id(1)
    L = SSD_CHUNK
    P = SSD_HEAD_DIM
    hpg = heads // groups
    d_inner = heads * P
    gw = hpg * P
    nst = SSD_STATE

    @pl.when(c == 0)
    def _():
        prev_ref[...] = jnp.zeros(prev_ref.shape, BF16)
        st_ref[...] = jnp.zeros(st_ref.shape, F32)

    def conv(cur_ref, lo, hi):
        cur = cur_ref[...]
        both = jnp.concatenate([prev_ref[:, lo:hi], cur], axis=0)
        acc = cb_ref[:, lo:hi] + cw_ref[SSD_CONV - 1:SSD_CONV, lo:hi] * cur.astype(F32)
        for k in range(1, SSD_CONV):
            acc = acc + cw_ref[SSD_CONV - 1 - k:SSD_CONV - k, lo:hi] * _dot(shift_ref[k - 1], both)
        prev_ref[:, lo:hi] = cur
        return _silu(acc)

    xs = conv(xs_ref, 0, d_inner)
    bc = conv(bc_ref, d_inner, prev_ref.shape[1])
    bmat = bc[:, :groups * nst]
    cmat = bc[:, groups * nst:]

    dt = jax.nn.softplus(dt_ref[...] + dtb_ref[...])
    da = dt * (-jnp.exp(alog_ref[...]))
    tri = tri_ref[...]
    acum = _dot3_lhs01(tri, da)
    act_ref[...] = acum.T
    e_acum = jnp.exp(acum)
    e_tail = jnp.exp(acum[L - 1:L, :] - acum)
    r64 = r64_ref[...]
    e64 = _spread(e_acum, r64, 2)
    t64 = _spread(e_tail, r64, 1)
    d64 = _spread(dt, r64, 1)
    xdt = xs * d64
    xdt_b = xdt.astype(BF16)
    xtl_b = (xdt * t64).astype(BF16)

    row = lax.broadcasted_iota(I32, (L, L), 0)
    col = lax.broadcasted_iota(I32, (L, L), 1)
    causal = row >= col

    for g in range(groups):
        bg = bmat[:, g * nst:(g + 1) * nst]
        cg_b = cmat[:, g * nst:(g + 1) * nst].astype(BF16)
        cb = _dot_nt(cg_b, bg.astype(BF16))
        sg = st_ref[g]
        y_off = _dot(cg_b, sg.astype(BF16))
        y_heads = []
        for hh in range(hpg):
            h = g * hpg + hh
            seg = jnp.broadcast_to(acum[:, h:h + 1], (L, LANES)) - act_ref[h:h + 1, :]
            m = (cb * jnp.where(causal, jnp.exp(seg), 0.0)).astype(BF16)
            y_heads.append(_dot(m, xdt_b[:, h * P:(h + 1) * P]))
        lo, hi = g * gw, (g + 1) * gw
        y_g = jnp.concatenate(y_heads, axis=1) + y_off * e64[:, lo:hi]
        st_ref[g] = sg * e64[L - 1:L, lo:hi] + _dot(bg.T.astype(BF16), xtl_b[:, lo:hi])
        y_g = (y_g + dsk_ref[:, lo:hi] * xs[:, lo:hi]) * _silu(z_ref[:, lo:hi].astype(F32))
        ms = jnp.mean(y_g * y_g, axis=-1, keepdims=True)
        y_ref[:, lo:hi] = (y_g * lax.rsqrt(ms + RMS_EPS) * ng_ref[:, lo:hi]).astype(y_ref.dtype)


def _ssd_branch(main3d, misc3d, conv_w, conv_b, dt_bias, a_log, d_skip, ssd_norm):
    bsz, seq, _ = main3d.shape
    heads = dt_bias.shape[0]
    d_inner = heads * SSD_HEAD_DIM
    groups = SSD_GROUPS
    conv_ch = conv_w.shape[1]
    bc_w = conv_ch - d_inner
    L = SSD_CHUNK

    def pad_lanes(v):
        return jnp.pad(v.astype(F32), (0, LANES - v.shape[0])).reshape(1, LANES)

    hidx = jnp.arange(LANES)[:, None]
    r64 = (hidx == (jnp.arange(d_inner) // SSD_HEAD_DIM)[None, :]).astype(BF16)
    tri = (jnp.arange(L)[:, None] >= jnp.arange(L)[None, :]).astype(BF16)
    taps = jnp.arange(1, SSD_CONV)[:, None, None]
    shift = (jnp.arange(2 * L)[None, None, :] == (L + jnp.arange(L)[None, :, None] - taps)).astype(BF16)
    dsk = jnp.repeat(d_skip.astype(F32), SSD_HEAD_DIM).reshape(1, d_inner)
    const = lambda shape: pl.BlockSpec(shape, lambda b, c: (0,) * len(shape))
    kern = functools.partial(_ssd_kernel, heads=heads, groups=groups)
    return pl.pallas_call(
        kern,
        grid=(bsz, seq // L),
        in_specs=[
            pl.BlockSpec((None, L, d_inner), lambda b, c: (b, c, 0)),
            pl.BlockSpec((None, L, d_inner), lambda b, c: (b, c, 1)),
            pl.BlockSpec((None, L, bc_w), lambda b, c: (b, c, 2 * d_inner // bc_w)),
            pl.BlockSpec((None, L, LANES), lambda b, c: (b, c, COL_DTWI)),
            const((SSD_CONV, conv_ch)), const((1, conv_ch)), const((1, LANES)), const((1, LANES)),
            const((1, d_inner)), const((1, d_inner)),
            const((LANES, d_inner)), const((L, L)),
            const((SSD_CONV - 1, L, 2 * L)),
        ],
        out_specs=pl.BlockSpec((None, L, d_inner), lambda b, c: (b, c, 0)),
        out_shape=jax.ShapeDtypeStruct((bsz, seq, d_inner), BF16),
        scratch_shapes=[
            pltpu.VMEM((L, conv_ch), BF16),
            pltpu.VMEM((groups, SSD_STATE, d_inner // groups), F32),
            pltpu.VMEM((LANES, L), F32),
        ],
        compiler_params=_cparams(("parallel", "arbitrary")),
        name="ssd_scan",
    )(main3d, main3d, main3d, misc3d, conv_w.astype(F32), conv_b.reshape(1, conv_ch).astype(F32),
      pad_lanes(dt_bias), pad_lanes(a_log), dsk, ssd_norm.reshape(1, d_inner).astype(F32), r64, tri, shift)


def _sortable_key(x):
    x = jnp.where(x == 0.0, 0.0, x)
    b = pltpu.bitcast(x, I32)
    return b ^ (lax.shift_right_arithmetic(b, 31) & 0x7FFFFFFF)


_BIT_MASKS = (0x0000FFFF, 0x00FF00FF, 0x0F0F0F0F, 0x33333333, 0x55555555)


def _bit_planes(rows):
    x = list(rows[::-1])
    j = 16
    for m in _BIT_MASKS:
        mask = np.int32(np.uint32(m))
        k = 0
        while k < 32:
            t = (x[k] ^ lax.shift_right_logical(x[k + j], jnp.int32(j))) & mask
            x[k] = x[k] ^ t
            x[k + j] = x[k + j] ^ lax.shift_left(t, jnp.int32(j))
            k = (k + j + 1) & ~j
        j >>= 1
    return x


def _dsa_kernel(q_ref, qi_ref, k_ref, v_ref, ki_ref, wi_ref, o_ref,
                kb_ref, kib_ref, vt_ref, qs_ref, qis_ref, keys_ref, planes_ref, sel_ref, cand_ref, above_ref,
                s0_ref, s1_ref, x0_ref, x1_ref, m_ref, l_ref, acc_ref, *, topk):
    i = pl.program_id(1)
    QB, KC, KT = Q_BLOCK, KEY_CHUNK, KEY_TILE
    TPC = KC // KT
    hd = ATTN_HEAD_DIM
    seq = k_ref.shape[0]
    n_chunks = (i * QB + QB + KC - 1) // KC
    lane = lax.broadcasted_iota(I32, (1, LANES), 1)
    left = lane < hd

    @pl.when(i == 0)
    def _():
        def fill(t, _):
            rows = pl.ds(pl.multiple_of(t * KT, KT), KT)
            kb_ref[rows, :] = k_ref[rows, :].astype(BF16)
            kib_ref[rows, :] = ki_ref[rows, :].astype(BF16)
            vt = v_ref[rows, :].T.astype(BF16)
            for g in range(ATTN_KV_HEADS):
                vt_ref[t, g, 0:hd, :] = vt[g * hd:(g + 1) * hd, :]
                vt_ref[t, g, hd:hd + ONES_ROWS, :] = jnp.ones((ONES_ROWS, KT), BF16)
            return 0
        lax.fori_loop(0, seq // KT, fill, 0)
        planes_ref[...] = jnp.zeros(planes_ref.shape, I32)

    qpos = i * QB + lax.broadcasted_iota(I32, (1, QB), 1)
    idx_scale = (IDX_DIM ** -0.5) * (IDX_HEADS ** -0.5)
    wt = (wi_ref[...] * idx_scale).T
    w_rows = [wt[DT_LANES + h:DT_LANES + h + 1, :] for h in range(IDX_HEADS)]

    heads_per_group = ATTN_HEADS // ATTN_KV_HEADS
    for p in range(ATTN_HEADS // 2):
        slab = q_ref[:, p * LANES:(p + 1) * LANES].astype(F32) * ((hd ** -0.5) * LOG2E)
        swapped = pltpu.roll(slab, hd, 1)
        if (2 * p) // heads_per_group == 0:
            even, odd = jnp.where(left, slab, 0.0), jnp.where(left, swapped, 0.0)
        else:
            even, odd = jnp.where(left, 0.0, swapped), jnp.where(left, 0.0, slab)
        qs_ref[2 * p] = even.astype(BF16)
        qs_ref[2 * p + 1] = odd.astype(BF16)
    for p in range(IDX_HEADS // 2):
        slab = qi_ref[:, p * LANES:(p + 1) * LANES]
        qis_ref[2 * p] = slab.astype(BF16)
        qis_ref[2 * p + 1] = pltpu.roll(slab, hd, 1).astype(BF16)

    n_tiles = n_chunks * TPC
    nkt = seq // KT
    rows_per_word = KT // 32

    def score_tile(t):
        kic = kib_ref[pl.ds(pl.multiple_of(t * KT, KT), KT), :]
        acc = jnp.zeros((KT, QB), F32)
        for p in range(IDX_HEADS // 2):
            rel = _dot_nt(kic, qis_ref[2 * p:2 * p + 2].reshape(2 * QB, LANES))
            acc = acc + w_rows[2 * p] * jnp.maximum(rel[:, :QB], 0.0)
            acc = acc + w_rows[2 * p + 1] * jnp.maximum(rel[:, QB:], 0.0)
        kpos = t * KT + lax.broadcasted_iota(I32, (KT, QB), 0)
        keys = jnp.where(kpos <= qpos, _sortable_key(acc), INT_MIN)
        keys_ref[t] = keys
        unsigned = keys ^ INT_MIN
        planes = _bit_planes([unsigned[r * rows_per_word:(r + 1) * rows_per_word, :] for r in range(32)])
        for bit in range(32):
            planes_ref[t, bit] = planes[bit]

    def score_chunk(c, _):
        for sub in range(TPC):
            score_tile(c * TPC + sub)
        return 0
    lax.fori_loop(0, n_chunks, score_chunk, 0)

    select_all = qpos < topk
    for t in range(nkt):
        cand_ref[t] = jnp.where(t < n_tiles, jnp.full((rows_per_word, QB), -1, I32), 0)
        above_ref[t] = jnp.zeros((rows_per_word, QB), I32)

    def radix_step(it, need):
        cnt = jnp.zeros((rows_per_word, QB), I32)
        for t in range(nkt):
            cnt = cnt + lax.population_count(cand_ref[t] & planes_ref[t, it])
        cnt = jnp.sum(cnt, axis=0, keepdims=True)
        take = cnt >= need
        for t in range(nkt):
            cand = cand_ref[t]
            ones = cand & planes_ref[t, it]
            cand_ref[t] = jnp.where(take, ones, cand ^ ones)
            above_ref[t] = jnp.where(take, above_ref[t], above_ref[t] | ones)
        return jnp.where(take, need, need - cnt)
    need = lax.fori_loop(0, 32, radix_step, jnp.full((1, QB), topk, I32))

    ties = jnp.zeros((rows_per_word, QB), I32)
    for t in range(nkt):
        ties = ties + lax.population_count(cand_ref[t])
        sel_ref[t] = jnp.where(select_all, -1, above_ref[t] | cand_ref[t])
    surplus = jnp.where(select_all, 0, jnp.sum(ties, axis=0, keepdims=True) - need)

    @pl.when(jnp.max(surplus) > 0)
    def _():
        def count(pred):
            def body(c, acc):
                kc = keys_ref[pl.ds(c * TPC, TPC)].reshape(KC, QB)
                hit = jnp.where(pred(kc, c), 1, 0)
                return acc + jnp.sum(hit.reshape(KC // 8, 8, QB), axis=0)
            acc = lax.fori_loop(0, n_chunks, body, jnp.zeros((8, QB), I32))
            return jnp.sum(acc, axis=0, keepdims=True)

        def bit_step(it, thr):
            cand = thr ^ lax.shift_left(jnp.int32(1), 31 - it)
            cnt = count(lambda kc, c: kc >= cand)
            return jnp.where(cnt >= topk, cand, thr)
        thr = lax.fori_loop(0, 32, bit_step, jnp.full((1, QB), INT_MIN, I32))
        want = topk - count(lambda kc, c: kc > thr)
        n_bits = seq.bit_length()

        def tie_step(it, jb):
            cand = jb | lax.shift_left(jnp.int32(1), n_bits - 1 - it)
            def pred(kc, c):
                kpos = c * KC + lax.broadcasted_iota(I32, (KC, QB), 0)
                return jnp.logical_and(kc == thr, kpos < cand)
            return jnp.where(count(pred) <= want, cand, jb)
        jb = lax.fori_loop(0, n_bits, tie_step, jnp.zeros((1, QB), I32))

        def pack_tile(t, _):
            kc = keys_ref[t]
            kpos = t * KT + lax.broadcasted_iota(I32, (KT, QB), 0)
            chosen = jnp.logical_or(kc > thr, jnp.logical_and(kc == thr, kpos < jb))
            word = jnp.zeros((rows_per_word, QB), I32)
            for r in range(32):
                bit = np.int32(np.uint32(1 << r))
                word = word | jnp.where(chosen[r * rows_per_word:(r + 1) * rows_per_word, :], bit, 0)
            sel_ref[t] = jnp.where(select_all, -1, word)
            return 0
        lax.fori_loop(0, n_tiles, pack_tile, 0)

    m_ref[...] = jnp.full(m_ref.shape, NEG_BIG, F32)
    l_ref[...] = jnp.zeros(l_ref.shape, F32)
    acc_ref[...] = jnp.zeros(acc_ref.shape, F32)

    def masked_scores(t, dst_ref, dmax_ref):
        word = sel_ref[t]
        picked = jnp.concatenate([lax.shift_left(word, jnp.int32(31 - r)) for r in range(32)], axis=0)
        kpos = t * KT + lax.broadcasted_iota(I32, (KT, QB), 0)
        sel = jnp.logical_and(picked < 0, kpos <= qpos)
        bias = jnp.where(sel, 0.0, NEG_BIG)
        bias2 = jnp.concatenate([bias, bias], axis=1)
        kc = kb_ref[pl.ds(pl.multiple_of(t * KT, KT), KT), :]
        for p in range(ATTN_HEADS // 2):
            s = _dot_nt(kc, qs_ref[2 * p:2 * p + 2].reshape(2 * QB, LANES)) + bias2
            dst_ref[p] = s
            dmax_ref[p] = jnp.max(s, axis=0, keepdims=True)

    def softmax_pv(t, src_ref, smax_ref):
        for p in range(ATTN_HEADS // 2):
            g = (2 * p) // heads_per_group
            m_old = m_ref[p]
            m_new = jnp.maximum(m_old, smax_ref[p])
            alpha = jnp.exp2(m_old - m_new)
            pt = jnp.exp2(src_ref[p] - m_new).astype(BF16)
            m_ref[p] = m_new
            pv = _dot(vt_ref[t, g], pt)
            l_ref[p] = alpha * l_ref[p] + pv[hd:hd + 1, :]
            acc_ref[p] = alpha * acc_ref[p] + pv[0:hd, :]

    masked_scores(0, s0_ref, x0_ref)

    def attend(u, _):
        t = 2 * u
        masked_scores(t + 1, s1_ref, x1_ref)
        softmax_pv(t, s0_ref, x0_ref)
        masked_scores(jnp.minimum(t + 2, n_tiles - 1), s0_ref, x0_ref)
        softmax_pv(t + 1, s1_ref, x1_ref)
        return 0
    lax.fori_loop(0, n_tiles // 2, attend, 0)

    for p in range(ATTN_HEADS // 2):
        o = acc_ref[p] / l_ref[p]
        both = jnp.concatenate([o[:, :QB], o[:, QB:]], axis=0)
        o_ref[:, p * LANES:(p + 1) * LANES] = both.T.astype(o_ref.dtype)


def _dsa_branch(main3d, misc3d):
    bsz, seq, _ = main3d.shape
    topk = min(TOPK_MAX, seq // 4)
    width = ATTN_HEADS * ATTN_HEAD_DIM
    nkt = seq // KEY_TILE
    n_pairs = ATTN_HEADS // 2
    kern = functools.partial(_dsa_kernel, topk=topk)
    return pl.pallas_call(
        kern,
        grid=(bsz, seq // Q_BLOCK),
        in_specs=[
            pl.BlockSpec((None, Q_BLOCK, width), lambda b, i: (b, i, Q_TILE)),
            pl.BlockSpec((None, Q_BLOCK, IDX_HEADS * IDX_DIM), lambda b, i: (b, i, COL_QI)),
            pl.BlockSpec((None, seq, LANES), lambda b, i: (b, 0, COL_K)),
            pl.BlockSpec((None, seq, LANES), lambda b, i: (b, 0, COL_V)),
            pl.BlockSpec((None, seq, LANES), lambda b, i: (b, 0, COL_KI)),
            pl.BlockSpec((None, Q_BLOCK, LANES), lambda b, i: (b, i, COL_DTWI)),
        ],
        out_specs=pl.BlockSpec((None, Q_BLOCK, width), lambda b, i: (b, i, 0)),
        out_shape=jax.ShapeDtypeStruct((bsz, seq, width), BF16),
        scratch_shapes=[
            pltpu.VMEM((seq, LANES), BF16),
            pltpu.VMEM((seq, LANES), BF16),
            pltpu.VMEM((nkt, ATTN_KV_HEADS, ATTN_HEAD_DIM + ONES_ROWS, KEY_TILE), BF16),
            pltpu.VMEM((ATTN_HEADS, Q_BLOCK, LANES), BF16),
            pltpu.VMEM((IDX_HEADS, Q_BLOCK, LANES), BF16),
            pltpu.VMEM((nkt, KEY_TILE, Q_BLOCK), I32),
            pltpu.VMEM((nkt, 32, KEY_TILE // 32, Q_BLOCK), I32),
            pltpu.VMEM((nkt, KEY_TILE // 32, Q_BLOCK), I32),
            pltpu.VMEM((nkt, KEY_TILE // 32, Q_BLOCK), I32),
            pltpu.VMEM((nkt, KEY_TILE // 32, Q_BLOCK), I32),
            pltpu.VMEM((n_pairs, KEY_TILE, 2 * Q_BLOCK), F32),
            pltpu.VMEM((n_pairs, KEY_TILE, 2 * Q_BLOCK), F32),
            pltpu.VMEM((n_pairs, 1, 2 * Q_BLOCK), F32),
            pltpu.VMEM((n_pairs, 1, 2 * Q_BLOCK), F32),
            pltpu.VMEM((n_pairs, 1, 2 * Q_BLOCK), F32),
            pltpu.VMEM((n_pairs, 1, 2 * Q_BLOCK), F32),
            pltpu.VMEM((n_pairs, ATTN_HEAD_DIM, 2 * Q_BLOCK), F32),
        ],
        compiler_params=_cparams(("parallel", "arbitrary")),
        name="dsa_attention",
    )(main3d, misc3d, misc3d, misc3d, misc3d, misc3d)


def _merge_kernel(h_ref, ys_ref, ya_ref, gs_ref, ga_ref, wso_ref, wao_ref, wo_ref, o_ref):
    a = _dot(ys_ref[...], wso_ref[...])
    b = _dot(ya_ref[...], wao_ref[...])
    merged = jax.nn.sigmoid(gs_ref[...].astype(F32)) * a + jax.nn.sigmoid(ga_ref[...].astype(F32)) * b
    o_ref[...] = h_ref[...] + _dot(merged.astype(BF16), wo_ref[...])


def _merge(h2d, y_ssd, y_attn, proj2d, wso, wao, wo):
    n, d = h2d.shape
    tm = 512
    full = lambda a: pl.BlockSpec(a.shape, lambda i: (0, 0))
    return pl.pallas_call(
        _merge_kernel,
        grid=(n // tm,),
        in_specs=[
            pl.BlockSpec((tm, d), lambda i: (i, 0)),
            pl.BlockSpec((tm, y_ssd.shape[1]), lambda i: (i, 0)),
            pl.BlockSpec((tm, y_attn.shape[1]), lambda i: (i, 0)),
            pl.BlockSpec((tm, d), lambda i: (i, 6)),
            pl.BlockSpec((tm, d), lambda i: (i, 7)),
            full(wso), full(wao), full(wo),
        ],
        out_specs=pl.BlockSpec((tm, d), lambda i: (i, 0)),
        out_shape=jax.ShapeDtypeStruct((n, d), F32),
        compiler_params=_cparams(("parallel",)),
        name="merge_out",
    )(h2d, y_ssd, y_attn, proj2d, proj2d, wso, wao, wo)


def _normproj_kernel(x_ref, g_ref, w_ref, o_ref):
    o_ref[...] = _dot(_rms(x_ref[...], g_ref[...]).astype(BF16), w_ref[...]).astype(o_ref.dtype)


def _norm_project(x2d, gain, w_bf16, out_dtype):
    n, d = x2d.shape
    tm = min(512, n)
    return pl.pallas_call(
        _normproj_kernel,
        grid=(n // tm,),
        in_specs=[pl.BlockSpec((tm, d), lambda i: (i, 0)), pl.BlockSpec((1, d), lambda i: (0, 0)),
                  pl.BlockSpec(w_bf16.shape, lambda i: (0, 0))],
        out_specs=pl.BlockSpec((tm, w_bf16.shape[1]), lambda i: (i, 0)),
        out_shape=jax.ShapeDtypeStruct((n, w_bf16.shape[1]), out_dtype),
        compiler_params=_cparams(("parallel",)),
        name="mem_kv_projection",
    )(x2d, gain.reshape(1, d), w_bf16)


def _xattn_kernel(h_ref, g_ref, kv_ref, wq_ref, wo_ref, o_ref):
    d = h_ref.shape[1]
    hd = d // XATTN_HEADS
    h = h_ref[...]
    q = _dot(_rms(h, g_ref[...]).astype(BF16), wq_ref[...]) * (hd ** -0.5)
    qb = q.astype(BF16)
    outs = []
    for a in range(XATTN_HEADS):
        k = kv_ref[:, a * hd:(a + 1) * hd]
        v = kv_ref[:, d + a * hd:d + (a + 1) * hd]
        s = _dot_nt(qb[:, a * hd:(a + 1) * hd], k)
        p = jnp.exp(s - jnp.max(s, axis=1, keepdims=True))
        o = _dot(p.astype(BF16), v) / jnp.sum(p, axis=1, keepdims=True)
        outs.append(o.astype(BF16))
    o_ref[...] = h + _dot(jnp.concatenate(outs, axis=1), wo_ref[...])


def _cross_attention(h3d, gain, kv3d, wq, wo):
    bsz, seq, d = h3d.shape
    tm = min(512, seq)
    mlen = kv3d.shape[1]
    return pl.pallas_call(
        _xattn_kernel,
        grid=(bsz, seq // tm),
        in_specs=[
            pl.BlockSpec((None, tm, d), lambda b, i: (b, i, 0)),
            pl.BlockSpec((1, d), lambda b, i: (0, 0)),
            pl.BlockSpec((None, mlen, 2 * d), lambda b, i: (b, 0, 0)),
            pl.BlockSpec(wq.shape, lambda b, i: (0, 0)),
            pl.BlockSpec(wo.shape, lambda b, i: (0, 0)),
        ],
        out_specs=pl.BlockSpec((None, tm, d), lambda b, i: (b, i, 0)),
        out_shape=jax.ShapeDtypeStruct((bsz, seq, d), F32),
        compiler_params=_cparams(("parallel", "parallel")),
        name="cross_attention",
    )(h3d, gain.reshape(1, d), kv3d, wq, wo)


def _router_kernel(h_ref, g_ref, wr_ref, br_ref, trit_ref, aug_ref, rank_ref, gsel_ref, cnt_ref, base_ref):
    i = pl.program_id(0)
    d = h_ref.shape[1]

    @pl.when(i == 0)
    def _():
        base_ref[...] = jnp.zeros(base_ref.shape, F32)

    hn = _rms(h_ref[...], g_ref[...])
    aug_ref[:, 0:d] = hn
    h_hi, h_mid, h_lo = _split3(hn)
    w_hi, w_mid, w_lo = wr_ref[0], wr_ref[1], wr_ref[2]
    logits = (_dot(h_hi, w_hi) + (_dot(h_hi, w_mid) + _dot(h_mid, w_hi))
              + (_dot(h_hi, w_lo) + _dot(h_mid, w_mid) + _dot(h_lo, w_hi))) + br_ref[...]
    tm = logits.shape[0]
    lane = lax.broadcasted_iota(I32, (tm, LANES), 1)
    neg_inf = -jnp.inf
    n_e = MOE_GROUPS * MOE_EXPERTS_PER_GROUP
    is_g = jnp.logical_and(lane >= n_e, lane < n_e + MOE_GROUPS)
    gl = jnp.where(is_g, logits, neg_inf)
    g_max = jnp.max(gl, axis=1, keepdims=True)
    g_sel = jnp.min(jnp.where(gl == g_max, lane, LANES), axis=1, keepdims=True) - n_e
    g_w = 1.0 / jnp.sum(jnp.where(is_g, jnp.exp(gl - g_max), 0.0), axis=1, keepdims=True)
    in_grp = jnp.logical_and(lane >= g_sel * MOE_EXPERTS_PER_GROUP,
                             lane < (g_sel + 1) * MOE_EXPERTS_PER_GROUP)
    el = jnp.where(in_grp, logits, neg_inf)
    v1 = jnp.max(el, axis=1, keepdims=True)
    i1 = jnp.min(jnp.where(el == v1, lane, LANES), axis=1, keepdims=True)
    el2 = jnp.where(lane == i1, neg_inf, el)
    v2 = jnp.max(el2, axis=1, keepdims=True)
    i2 = jnp.min(jnp.where(el2 == v2, lane, LANES), axis=1, keepdims=True)
    e2 = jnp.exp(v2 - v1)
    w1 = g_w / (1.0 + e2)
    w2 = g_w * e2 / (1.0 + e2)
    aug_ref[:, d:] = jnp.where(lane == i1, w1, 0.0) + jnp.where(lane == i2, w2, 0.0)

    onehot_t = jnp.where(lane == g_sel, 1.0, 0.0).T
    upto = _dot(onehot_t.astype(BF16), trit_ref[...]) + base_ref[...]
    grp = lax.broadcasted_iota(I32, onehot_t.shape, 0).astype(F32)
    rank_ref[...] = (jnp.sum(onehot_t * upto, axis=0, keepdims=True) - 1.0).astype(I32)
    gsel_ref[...] = jnp.sum(onehot_t * grp, axis=0, keepdims=True).astype(I32)
    base_ref[...] = upto[:, tm - 1:tm]
    cnt_ref[...] = jnp.broadcast_to(upto[:, tm - 1:tm], cnt_ref.shape)


def _moe_route(h2d, gain, wr3, br):
    n, d = h2d.shape
    tm = 512
    nt = n // tm
    trit = (jnp.arange(tm)[:, None] <= jnp.arange(tm)[None, :]).astype(BF16)
    return pl.pallas_call(
        _router_kernel,
        grid=(nt,),
        in_specs=[pl.BlockSpec((tm, d), lambda i: (i, 0)), pl.BlockSpec((1, d), lambda i: (0, 0)),
                  pl.BlockSpec(wr3.shape, lambda i: (0, 0, 0)), pl.BlockSpec((1, LANES), lambda i: (0, 0)),
                  pl.BlockSpec((tm, tm), lambda i: (0, 0))],
        out_specs=[pl.BlockSpec((tm, d + LANES), lambda i: (i, 0)),
                   pl.BlockSpec((None, 1, tm), lambda i: (i, 0, 0)),
                   pl.BlockSpec((None, 1, tm), lambda i: (i, 0, 0)),
                   pl.BlockSpec((LANES, LANES), lambda i: (0, 0))],
        out_shape=[jax.ShapeDtypeStruct((n, d + LANES), F32),
                   jax.ShapeDtypeStruct((nt, 1, tm), I32),
                   jax.ShapeDtypeStruct((nt, 1, tm), I32),
                   jax.ShapeDtypeStruct((LANES, LANES), F32)],
        scratch_shapes=[pltpu.VMEM((LANES, 1), F32)],
        compiler_params=_cparams(("arbitrary",)),
        name="moe_router",
    )(h2d, gain.reshape(1, d), wr3, br, trit)


MOE_TILE = 512
ROW_BLOCK = 256
ISSUE_UNROLL = 8


def _scatter_rows_kernel(dest_ref, src_ref, init_ref, out_ref, sem):
    del init_ref
    base = pl.program_id(0) * ROW_BLOCK

    def row_copy(r):
        return pltpu.make_async_copy(src_ref.at[pl.ds(r, 1)], out_ref.at[pl.ds(dest_ref[base + r], 1)], sem)

    def issue(r, _):
        row_copy(r).start()
        return 0
    lax.fori_loop(0, ROW_BLOCK, issue, 0, unroll=ISSUE_UNROLL)
    pltpu.make_async_copy(src_ref, out_ref.at[pl.ds(0, ROW_BLOCK)], sem).wait()


def _scatter_rows(src, dest, n_out):
    n, w = src.shape
    init = jnp.zeros((n_out, w), src.dtype)
    return pl.pallas_call(
        _scatter_rows_kernel,
        grid_spec=pltpu.PrefetchScalarGridSpec(
            num_scalar_prefetch=1,
            grid=(n // ROW_BLOCK,),
            in_specs=[pl.BlockSpec((ROW_BLOCK, w), lambda i, dest: (i, 0)),
                      pl.BlockSpec(memory_space=pl.ANY)],
            out_specs=pl.BlockSpec(memory_space=pl.ANY),
            scratch_shapes=[pltpu.SemaphoreType.DMA(())],
        ),
        out_shape=jax.ShapeDtypeStruct((n_out, w), src.dtype),
        input_output_aliases={2: 0},
        compiler_params=_cparams(("arbitrary",)),
        name="moe_sort_rows",
    )(dest, src, init)


def _sorted_experts_kernel(tile_group_ref, n_used_ref, hn_ref, cw_ref, wg_ref, wu_ref, wd_ref, o_ref):
    j = pl.program_id(0)

    @pl.when(j < n_used_ref[0])
    def _():
        hn = hn_ref[...].astype(BF16)
        first = tile_group_ref[j] * MOE_EXPERTS_PER_GROUP
        lane = lax.broadcasted_iota(I32, cw_ref.shape, 1)
        cw = cw_ref[...]
        acc = jnp.zeros(o_ref.shape, F32)
        for e in range(MOE_EXPERTS_PER_GROUP):
            w_e = jnp.sum(jnp.where(lane == first + e, cw, 0.0), axis=1, keepdims=True)
            hid = _silu(_dot(hn, wg_ref[e])) * _dot(hn, wu_ref[e]) * w_e
            acc = acc + _dot(hid.astype(BF16), wd_ref[e])
        o_ref[...] = acc

    @pl.when(j >= n_used_ref[0])
    def _():
        o_ref[...] = jnp.zeros(o_ref.shape, F32)


def _sorted_experts(aug_sorted, tile_group, n_used, wg, wu, wd):
    n_pad, wa = aug_sorted.shape
    d = wa - LANES
    _, n_e, _, dff = wg.shape
    tm = MOE_TILE
    return pl.pallas_call(
        _sorted_experts_kernel,
        grid_spec=pltpu.PrefetchScalarGridSpec(
            num_scalar_prefetch=2,
            grid=(n_pad // tm,),
            in_specs=[
                pl.BlockSpec((tm, d), lambda j, tg, nu: (j, 0)),
                pl.BlockSpec((tm, LANES), lambda j, tg, nu: (j, d // LANES)),
                pl.BlockSpec((None, n_e, d, dff), lambda j, tg, nu: (tg[j], 0, 0, 0)),
                pl.BlockSpec((None, n_e, d, dff), lambda j, tg, nu: (tg[j], 0, 0, 0)),
                pl.BlockSpec((None, n_e, dff, d), lambda j, tg, nu: (tg[j], 0, 0, 0)),
            ],
            out_specs=pl.BlockSpec((tm, d), lambda j, tg, nu: (j, 0)),
        ),
        out_shape=jax.ShapeDtypeStruct((n_pad, d), F32),
        compiler_params=_cparams(("arbitrary",)),
        name="moe_experts",
    )(tile_group, n_used, aug_sorted, aug_sorted, wg, wu, wd)


def _gather_add_kernel(dest_ref, h_ref, delta_ref, fg_ref, o_ref, buf_ref, sem, *, final_norm):
    i = pl.program_id(0)
    n_steps = pl.num_programs(0)

    def row_copy(step, r):
        slot = lax.rem(step, 2)
        return pltpu.make_async_copy(delta_ref.at[pl.ds(dest_ref[step * ROW_BLOCK + r], 1)],
                                     buf_ref.at[slot, pl.ds(r, 1)], sem.at[slot])

    def issue_all(step):
        def issue(r, _):
            row_copy(step, r).start()
            return 0
        lax.fori_loop(0, ROW_BLOCK, issue, 0, unroll=ISSUE_UNROLL)

    @pl.when(i == 0)
    def _():
        issue_all(0)

    @pl.when(i + 1 < n_steps)
    def _():
        issue_all(i + 1)

    slot = lax.rem(i, 2)
    pltpu.make_async_copy(delta_ref.at[pl.ds(0, ROW_BLOCK)], buf_ref.at[slot], sem.at[slot]).wait()

    out = h_ref[...] + buf_ref[slot]
    if final_norm:
        out = _rms(out, fg_ref[...])
    o_ref[...] = out


def _gather_add(h2d, delta_sorted, dest, final_gain, final_norm):
    n, d = h2d.shape
    kern = functools.partial(_gather_add_kernel, final_norm=final_norm)
    return pl.pallas_call(
        kern,
        grid_spec=pltpu.PrefetchScalarGridSpec(
            num_scalar_prefetch=1,
            grid=(n // ROW_BLOCK,),
            in_specs=[pl.BlockSpec((ROW_BLOCK, d), lambda i, dest: (i, 0)),
                      pl.BlockSpec(memory_space=pl.ANY),
                      pl.BlockSpec((1, d), lambda i, dest: (0, 0))],
            out_specs=pl.BlockSpec((ROW_BLOCK, d), lambda i, dest: (i, 0)),
            scratch_shapes=[pltpu.VMEM((2, ROW_BLOCK, d), F32), pltpu.SemaphoreType.DMA((2,))],
        ),
        out_shape=jax.ShapeDtypeStruct((n, d), F32),
        compiler_params=_cparams(("arbitrary",)),
        name="moe_unsort_add",
    )(dest, h2d, delta_sorted, final_gain.reshape(1, d))


def _moe_layout(rank, gsel, counts, n):
    tm = MOE_TILE
    cnt = counts[:MOE_GROUPS, 0].astype(I32)
    seg = ((cnt + tm - 1) // tm) * tm
    ends = jnp.cumsum(seg)
    dest = (ends - seg)[gsel.reshape(n)] + rank.reshape(n)
    n_tiles = n // tm + MOE_GROUPS
    starts = jnp.arange(n_tiles, dtype=I32) * tm
    tile_group = jnp.minimum(jnp.sum(starts[:, None] >= ends[None, :], axis=1), MOE_GROUPS - 1).astype(I32)
    n_used = (ends[-1:] // tm).astype(I32)
    return dest.astype(I32), tile_group, n_used, n_tiles * tm


def _prep_router(w_group, b_group, w_router, b_router):
    d = w_group.shape[0]
    w = jnp.concatenate([w_router, w_group], axis=1).astype(F32)
    w = jnp.pad(w, ((0, 0), (0, LANES - w.shape[1])))
    hi = w.astype(BF16)
    r1 = w - hi.astype(F32)
    mid = r1.astype(BF16)
    lo = (r1 - mid.astype(F32)).astype(BF16)
    b = jnp.concatenate([b_router, b_group]).astype(F32)
    b = jnp.pad(b, (0, LANES - b.shape[0])).reshape(1, LANES)
    del d
    return jnp.stack([hi, mid, lo]), b


def kernel(x, mem, norm_mix, w_in, conv_w, conv_b, dt_bias, a_log, d_skip, ssd_norm, w_ssd_o, w_attn_o, w_out,
           norm_xattn, norm_mem, w_cq, w_ckv, w_co, norm_ffn, w_group, b_group, w_router, b_router,
           w_gate_e, w_up_e, w_down_e, norm_final):
    bsz, seq, d = x.shape
    depth = w_in.shape[0]
    mlen = mem.shape[1]
    heads = dt_bias.shape[1]
    d_inner = heads * SSD_HEAD_DIM
    conv_ch = conv_w.shape[2]
    n = bsz * seq
    rope_c, rope_s1, rope_s2 = _rope_tables(seq)
    h = x.reshape(n, d)
    mem2d = mem.reshape(bsz * mlen, d)
    for l in range(depth):
        w_in_l = _prep_w_in(w_in[l], d_inner, conv_ch, heads)
        main, misc = _in_projection(h, norm_mix[l], w_in_l, rope_c, rope_s1, rope_s2, seq)
        main3d = main.reshape(bsz, seq, main.shape[1])
        misc3d = misc.reshape(bsz, seq, misc.shape[1])
        y_ssd = _ssd_branch(main3d, misc3d, conv_w[l], conv_b[l], dt_bias[l], a_log[l], d_skip[l], ssd_norm[l])
        y_attn = _dsa_branch(main3d, misc3d)
        h = _merge(h, y_ssd.reshape(n, d_inner), y_attn.reshape(n, -1), main,
                   w_ssd_o[l].astype(BF16), w_attn_o[l].astype(BF16), w_out[l].astype(BF16))
        kv = _norm_project(mem2d, norm_mem[l], w_ckv[l].astype(BF16), BF16)
        h = _cross_attention(h.reshape(bsz, seq, d), norm_xattn[l], kv.reshape(bsz, mlen, 2 * d),
                             w_cq[l].astype(BF16), w_co[l].astype(BF16)).reshape(n, d)
        wr3, br = _prep_router(w_group[l], b_group[l], w_router[l], b_router[l])
        aug, rank, gsel, counts = _moe_route(h, norm_ffn[l], wr3, br)
        dest, tile_group, n_used, n_pad = _moe_layout(rank, gsel, counts, n)
        aug_sorted = _scatter_rows(aug, dest, n_pad)
        delta = _sorted_experts(aug_sorted, tile_group, n_used, w_gate_e[l].astype(BF16),
                                w_up_e[l].astype(BF16), w_down_e[l].astype(BF16))
        h = _gather_add(h, delta, dest, norm_final, final_norm=(l == depth - 1))
    return h.reshape(bsz, seq, d)
```

```python
import functools
import math

import jax
import jax.numpy as jnp
import numpy as np
from jax import lax
from jax.experimental import pallas as pl
from jax.experimental.pallas import tpu as pltpu

F32 = jnp.float32
BF16 = jnp.bfloat16
I32 = jnp.int32

SSD_HEAD_DIM = 64
SSD_GROUPS = 4
SSD_STATE = 128
SSD_CONV = 4
SSD_CHUNK = 128
SSD_SEQS_PER_STEP = 2
ATTN_HEADS = 16
ATTN_KV_HEADS = 2
ATTN_HEAD_DIM = 64
Q_BLOCK = 128
TOPK_MAX = 256
IDX_HEADS = 8
IDX_DIM = 64
ROPE_THETA = 10000.0
XATTN_HEADS = 4
MOE_GROUPS = 4
MOE_EXPERTS_PER_GROUP = 8
RMS_EPS = 1e-6

LANES = 128
KEY_CHUNK = 512
KEY_TILE = 256
INT_MIN = -2 ** 31
NEG_BIG = -1e30
LOG2E = math.log2(math.e)
ONES_ROWS = 16
VMEM_LIMIT = 56 * 1024 * 1024


def _cparams(sem):
    return pltpu.CompilerParams(dimension_semantics=sem, vmem_limit_bytes=VMEM_LIMIT)


def _split3(v):
    hi = v.astype(BF16)
    r1 = v - hi.astype(F32)
    mid = r1.astype(BF16)
    lo = (r1 - mid.astype(F32)).astype(BF16)
    return hi, mid, lo


def _dot(a, b):
    return jnp.dot(a, b, preferred_element_type=F32)


def _dot_nt(a, b):
    return lax.dot_general(a, b, (((1,), (1,)), ((), ())), preferred_element_type=F32)


def _spread(v, m01, terms):
    parts = _split3(v)[:terms]
    out = _dot(parts[0], m01)
    for part in parts[1:]:
        out = out + _dot(part, m01)
    return out


def _dot3_lhs01(m01, v):
    hi, mid, lo = _split3(v)
    return _dot(m01, hi) + _dot(m01, mid) + _dot(m01, lo)


def _rms(x, g):
    ms = jnp.mean(x * x, axis=-1, keepdims=True)
    return x * lax.rsqrt(ms + RMS_EPS) * g


def _silu(x):
    return x * jax.nn.sigmoid(x)


Q_TILE = 5
MISC_TILE = 8
MISC_ROPE_CHUNKS = (0, 1, 2, 3, 4, 6)
COL_QI = 0
COL_K, COL_V, COL_KI, COL_DTWI = 4, 5, 6, 7
DT_LANES = 32


def _rope_chunk(a, c, s1, s2):
    return a * c + pltpu.roll(a, 32, 1) * s1 + pltpu.roll(a, 96, 1) * s2


def _inproj_kernel(x_ref, g_ref, w_ref, c_ref, s1_ref, s2_ref, o_ref, misc_ref, xn_ref):
    j = pl.program_id(1)

    @pl.when(j == 0)
    def _():
        xn_ref[...] = _rms(x_ref[...], g_ref[...]).astype(BF16)

    acc = _dot(xn_ref[...], w_ref[...])
    n_chunks = acc.shape[1] // LANES

    def roped(dst_ref, chunks):
        c, s1, s2 = c_ref[...], s1_ref[...], s2_ref[...]
        for k in range(n_chunks):
            a = acc[:, k * LANES:(k + 1) * LANES]
            if k in chunks:
                a = _rope_chunk(a, c, s1, s2)
            dst_ref[:, k * LANES:(k + 1) * LANES] = a.astype(dst_ref.dtype)

    @pl.when(j == Q_TILE)
    def _():
        roped(o_ref, tuple(range(n_chunks)))

    @pl.when(j == MISC_TILE)
    def _():
        roped(misc_ref, MISC_ROPE_CHUNKS)

    @pl.when(jnp.logical_and(j != Q_TILE, j != MISC_TILE))
    def _():
        o_ref[...] = acc.astype(o_ref.dtype)


def _in_projection(x2d, gain, w_bf16, rope_c, rope_s1, rope_s2, seq):
    n, d = x2d.shape
    tm = min(1024, seq)
    tn = 1024
    pos_tiles = seq // tm
    n_tiles = w_bf16.shape[1] // tn
    assert n_tiles == MISC_TILE + 1
    return pl.pallas_call(
        _inproj_kernel,
        grid=(n // tm, n_tiles),
        in_specs=[
            pl.BlockSpec((tm, d), lambda i, j: (i, 0)),
            pl.BlockSpec((1, d), lambda i, j: (0, 0)),
            pl.BlockSpec((d, tn), lambda i, j: (0, j)),
            pl.BlockSpec((tm, LANES), lambda i, j: (i % pos_tiles, 0)),
            pl.BlockSpec((tm, LANES), lambda i, j: (i % pos_tiles, 0)),
            pl.BlockSpec((tm, LANES), lambda i, j: (i % pos_tiles, 0)),
        ],
        out_specs=[pl.BlockSpec((tm, tn), lambda i, j: (i, jnp.minimum(j, MISC_TILE - 1))),
                   pl.BlockSpec((tm, tn), lambda i, j: (i, 0))],
        out_shape=[jax.ShapeDtypeStruct((n, MISC_TILE * tn), BF16), jax.ShapeDtypeStruct((n, tn), F32)],
        scratch_shapes=[pltpu.VMEM((tm, d), BF16)],
        compiler_params=_cparams(("parallel", "arbitrary")),
        name="in_projection",
    )(x2d, gain.reshape(1, d), w_bf16, rope_c, rope_s1, rope_s2)


def _prep_w_in(w, d_inner, conv_ch, ssd_heads):
    widths = (d_inner, conv_ch, ssd_heads, ATTN_HEADS * ATTN_HEAD_DIM, ATTN_KV_HEADS * ATTN_HEAD_DIM,
              ATTN_KV_HEADS * ATTN_HEAD_DIM, IDX_HEADS * IDX_DIM, IDX_DIM, IDX_HEADS,
              w.shape[0], w.shape[0])
    splits = [int(p) for p in np.cumsum(widths)[:-1]]
    z, xbc, dt, q, k, v, qi, ki, wi, gs, ga = jnp.split(w, splits, axis=1)

    def pad(a, n):
        return jnp.pad(a, ((0, 0), (0, n - a.shape[1])))

    misc = jnp.concatenate([qi, k, v, pad(ki, LANES), pad(jnp.concatenate([dt, wi], axis=1), LANES)], axis=1)
    return jnp.concatenate([z, xbc, q, gs, ga, misc], axis=1).astype(BF16)


def _rope_tables(seq):
    inv = 1.0 / (ROPE_THETA ** (jnp.arange(0, ATTN_HEAD_DIM, 2, dtype=F32) / ATTN_HEAD_DIM))
    ang = jnp.arange(seq, dtype=F32)[:, None] * inv[None, :]
    cos, sin = jnp.cos(ang), jnp.sin(ang)
    zero = jnp.zeros_like(sin)
    reps = LANES // ATTN_HEAD_DIM
    c = jnp.tile(jnp.concatenate([cos, cos], axis=1), (1, reps))
    s1 = jnp.tile(jnp.concatenate([zero, sin], axis=1), (1, reps))
    s2 = jnp.tile(jnp.concatenate([-sin, zero], axis=1), (1, reps))
    return c, s1, s2


def _ssd_chunk(z_ref, xs_ref, bc_ref, dt_ref, cw_ref, cb_ref, dtb_ref, alog_ref, dsk_ref, ng_ref,
               r64_ref, tri_ref, shift_ref, y_ref, prev_ref, st_ref, act_ref, heads, groups):
    L = SSD_CHUNK
    P = SSD_HEAD_DIM
    hpg = heads // groups
    d_inner = heads * P
    gw = hpg * P
    nst = SSD_STATE

    def conv(cur_ref, lo, hi):
        cur = cur_ref[...]
        both = jnp.concatenate([prev_ref[:, lo:hi], cur], axis=0)
        acc = cb_ref[:, lo:hi] + cw_ref[SSD_CONV - 1:SSD_CONV, lo:hi] * cur.astype(F32)
        for k in range(1, SSD_CONV):
            acc = acc + cw_ref[SSD_CONV - 1 - k:SSD_CONV - k, lo:hi] * _dot(shift_ref[k - 1], both)
        prev_ref[:, lo:hi] = cur
        return _silu(acc)

    xs = conv(xs_ref, 0, d_inner)
    bc = conv(bc_ref, d_inner, prev_ref.shape[1])
    bmat = bc[:, :groups * nst]
    cmat = bc[:, groups * nst:]

    dt = jax.nn.softplus(dt_ref[...] + dtb_ref[...])
    da = dt * (-jnp.exp(alog_ref[...]))
    tri = tri_ref[...]
    acum = _dot3_lhs01(tri, da)
    act_ref[...] = acum.T
    e_acum = jnp.exp(acum)
    e_tail = jnp.exp(acum[L - 1:L, :] - acum)
    r64 = r64_ref[...]
    e64 = _spread(e_acum, r64, 2)
    t64 = _spread(e_tail, r64, 1)
    d64 = _spread(dt, r64, 1)
    xdt = xs * d64
    xdt_b = xdt.astype(BF16)
    xtl_b = (xdt * t64).astype(BF16)

    row = lax.broadcasted_iota(I32, (L, L), 0)
    col = lax.broadcasted_iota(I32, (L, L), 1)
    causal = row >= col
    first_head = lax.broadcasted_iota(I32, (1, 2 * P), 1) < P

    for g in range(groups):
        bg = bmat[:, g * nst:(g + 1) * nst]
        cg_b = cmat[:, g * nst:(g + 1) * nst].astype(BF16)
        cb = _dot_nt(cg_b, bg.astype(BF16))
        sg = st_ref[g]
        y_off = _dot(cg_b, sg.astype(BF16))
        y_heads = []
        for hh in range(0, hpg, 2):
            pair = []
            for h in (g * hpg + hh, g * hpg + hh + 1):
                seg = jnp.broadcast_to(acum[:, h:h + 1], (L, LANES)) - act_ref[h:h + 1, :]
                pair.append((cb * jnp.where(causal, jnp.exp(seg), 0.0)).astype(BF16))
            slab = xdt_b[:, (g * hpg + hh) * P:(g * hpg + hh + 2) * P]
            stacked = jnp.concatenate([jnp.where(first_head, slab, 0), jnp.where(first_head, 0, slab)], axis=0)
            y_heads.append(_dot(jnp.concatenate(pair, axis=1), stacked))
        lo, hi = g * gw, (g + 1) * gw
        y_g = jnp.concatenate(y_heads, axis=1) + y_off * e64[:, lo:hi]
        st_ref[g] = sg * e64[L - 1:L, lo:hi] + _dot(bg.T.astype(BF16), xtl_b[:, lo:hi])
        y_g = (y_g + dsk_ref[:, lo:hi] * xs[:, lo:hi]) * _silu(z_ref[:, lo:hi].astype(F32))
        ms = jnp.mean(y_g * y_g, axis=-1, keepdims=True)
        y_ref[:, lo:hi] = (y_g * lax.rsqrt(ms + RMS_EPS) * ng_ref[:, lo:hi]).astype(y_ref.dtype)


def _ssd_kernel(z_ref, xs_ref, bc_ref, dt_ref, cw_ref, cb_ref, dtb_ref, alog_ref, dsk_ref, ng_ref,
                r64_ref, tri_ref, shift_ref, y_ref, prev_ref, st_ref, act_ref, *, heads, groups):
    @pl.when(pl.program_id(1) == 0)
    def _():
        prev_ref[...] = jnp.zeros(prev_ref.shape, BF16)
        st_ref[...] = jnp.zeros(st_ref.shape, F32)

    for nb in range(z_ref.shape[0]):
        _ssd_chunk(z_ref.at[nb], xs_ref.at[nb], bc_ref.at[nb], dt_ref.at[nb], cw_ref, cb_ref, dtb_ref,
                   alog_ref, dsk_ref, ng_ref, r64_ref, tri_ref, shift_ref, y_ref.at[nb],
                   prev_ref.at[nb], st_ref.at[nb], act_ref.at[nb], heads, groups)


def _ssd_branch(main3d, misc3d, conv_w, conv_b, dt_bias, a_log, d_skip, ssd_norm):
    bsz, seq, _ = main3d.shape
    heads = dt_bias.shape[0]
    d_inner = heads * SSD_HEAD_DIM
    groups = SSD_GROUPS
    conv_ch = conv_w.shape[1]
    bc_w = conv_ch - d_inner
    L = SSD_CHUNK
    nb = SSD_SEQS_PER_STEP if bsz % SSD_SEQS_PER_STEP == 0 else 1

    def pad_lanes(v):
        return jnp.pad(v.astype(F32), (0, LANES - v.shape[0])).reshape(1, LANES)

    hidx = jnp.arange(LANES)[:, None]
    r64 = (hidx == (jnp.arange(d_inner) // SSD_HEAD_DIM)[None, :]).astype(BF16)
    tri = (jnp.arange(L)[:, None] >= jnp.arange(L)[None, :]).astype(BF16)
    taps = jnp.arange(1, SSD_CONV)[:, None, None]
    shift = (jnp.arange(2 * L)[None, None, :] == (L + jnp.arange(L)[None, :, None] - taps)).astype(BF16)
    dsk = jnp.repeat(d_skip.astype(F32), SSD_HEAD_DIM).reshape(1, d_inner)
    const = lambda shape: pl.BlockSpec(shape, lambda b, c: (0,) * len(shape))
    kern = functools.partial(_ssd_kernel, heads=heads, groups=groups)
    return pl.pallas_call(
        kern,
        grid=(bsz // nb, seq // L),
        in_specs=[
            pl.BlockSpec((nb, L, d_inner), lambda b, c: (b, c, 0)),
            pl.BlockSpec((nb, L, d_inner), lambda b, c: (b, c, 1)),
            pl.BlockSpec((nb, L, bc_w), lambda b, c: (b, c, 2 * d_inner // bc_w)),
            pl.BlockSpec((nb, L, LANES), lambda b, c: (b, c, COL_DTWI)),
            const((SSD_CONV, conv_ch)), const((1, conv_ch)), const((1, LANES)), const((1, LANES)),
            const((1, d_inner)), const((1, d_inner)),
            const((LANES, d_inner)), const((L, L)),
            const((SSD_CONV - 1, L, 2 * L)),
        ],
        out_specs=pl.BlockSpec((nb, L, d_inner), lambda b, c: (b, c, 0)),
        out_shape=jax.ShapeDtypeStruct((bsz, seq, d_inner), BF16),
        scratch_shapes=[
            pltpu.VMEM((nb, L, conv_ch), BF16),
            pltpu.VMEM((nb, groups, SSD_STATE, d_inner // groups), F32),
            pltpu.VMEM((nb, LANES, L), F32),
        ],
        compiler_params=_cparams(("parallel", "arbitrary")),
        name="ssd_scan",
    )(main3d, main3d, main3d, misc3d, conv_w.astype(F32), conv_b.reshape(1, conv_ch).astype(F32),
      pad_lanes(dt_bias), pad_lanes(a_log), dsk, ssd_norm.reshape(1, d_inner).astype(F32), r64, tri, shift)


def _sortable_key(x):
    x = jnp.where(x == 0.0, 0.0, x)
    b = pltpu.bitcast(x, I32)
    return b ^ (lax.shift_right_arithmetic(b, 31) & 0x7FFFFFFF)


_BIT_MASKS = (0x0000FFFF, 0x00FF00FF, 0x0F0F0F0F, 0x33333333, 0x55555555)


def _bit_planes(rows):
    x = list(rows[::-1])
    j = 16
    for m in _BIT_MASKS:
        mask = np.int32(np.uint32(m))
        k = 0
        while k < 32:
            t = (x[k] ^ lax.shift_right_logical(x[k + j], jnp.int32(j))) & mask
            x[k] = x[k] ^ t
            x[k + j] = x[k + j] ^ lax.shift_left(t, jnp.int32(j))
            k = (k + j + 1) & ~j
        j >>= 1
    return x


def _dsa_kernel(q_ref, qi_ref, k_ref, v_ref, ki_ref, wi_ref, o_ref,
                kb_ref, kib_ref, vt_ref, qs_ref, qis_ref, keys_ref, planes_ref, sel_ref, cand_ref, above_ref,
                s0_ref, s1_ref, x0_ref, x1_ref, m_ref, l_ref, acc_ref, *, topk):
    i = pl.program_id(1)
    QB, KC, KT = Q_BLOCK, KEY_CHUNK, KEY_TILE
    TPC = KC // KT
    hd = ATTN_HEAD_DIM
    seq = k_ref.shape[0]
    n_chunks = (i * QB + QB + KC - 1) // KC
    lane = lax.broadcasted_iota(I32, (1, LANES), 1)
    left = lane < hd

    @pl.when(i == 0)
    def _():
        def fill(t, _):
            rows = pl.ds(pl.multiple_of(t * KT, KT), KT)
            kb_ref[rows, :] = k_ref[rows, :].astype(BF16)
            kib_ref[rows, :] = ki_ref[rows, :].astype(BF16)
            vt = v_ref[rows, :].T.astype(BF16)
            for g in range(ATTN_KV_HEADS):
                vt_ref[t, g, 0:hd, :] = vt[g * hd:(g + 1) * hd, :]
                vt_ref[t, g, hd:hd + ONES_ROWS, :] = jnp.ones((ONES_ROWS, KT), BF16)
            return 0
        lax.fori_loop(0, seq // KT, fill, 0)
        planes_ref[...] = jnp.zeros(planes_ref.shape, I32)

    qpos = i * QB + lax.broadcasted_iota(I32, (1, QB), 1)
    idx_scale = (IDX_DIM ** -0.5) * (IDX_HEADS ** -0.5)
    wt = (wi_ref[...] * idx_scale).T
    w_rows = [wt[DT_LANES + h:DT_LANES + h + 1, :] for h in range(IDX_HEADS)]

    heads_per_group = ATTN_HEADS // ATTN_KV_HEADS
    for p in range(ATTN_HEADS // 2):
        slab = q_ref[:, p * LANES:(p + 1) * LANES].astype(F32) * ((hd ** -0.5) * LOG2E)
        swapped = pltpu.roll(slab, hd, 1)
        if (2 * p) // heads_per_group == 0:
            even, odd = jnp.where(left, slab, 0.0), jnp.where(left, swapped, 0.0)
        else:
            even, odd = jnp.where(left, 0.0, swapped), jnp.where(left, 0.0, slab)
        qs_ref[2 * p] = even.astype(BF16)
        qs_ref[2 * p + 1] = odd.astype(BF16)
    for p in range(IDX_HEADS // 2):
        slab = qi_ref[:, p * LANES:(p + 1) * LANES]
        qis_ref[2 * p] = slab.astype(BF16)
        qis_ref[2 * p + 1] = pltpu.roll(slab, hd, 1).astype(BF16)

    n_tiles = n_chunks * TPC
    nkt = seq // KT
    rows_per_word = KT // 32

    def score_tile(t):
        kic = kib_ref[pl.ds(pl.multiple_of(t * KT, KT), KT), :]
        acc = jnp.zeros((KT, QB), F32)
        for p in range(IDX_HEADS // 2):
            rel = _dot_nt(kic, qis_ref[2 * p:2 * p + 2].reshape(2 * QB, LANES))
            acc = acc + w_rows[2 * p] * jnp.maximum(rel[:, :QB], 0.0)
            acc = acc + w_rows[2 * p + 1] * jnp.maximum(rel[:, QB:], 0.0)
        kpos = t * KT + lax.broadcasted_iota(I32, (KT, QB), 0)
        keys = jnp.where(kpos <= qpos, _sortable_key(acc), INT_MIN)
        keys_ref[t] = keys
        unsigned = keys ^ INT_MIN
        planes = _bit_planes([unsigned[r * rows_per_word:(r + 1) * rows_per_word, :] for r in range(32)])
        for bit in range(32):
            planes_ref[t, bit] = planes[bit]

    def score_chunk(c, _):
        for sub in range(TPC):
            score_tile(c * TPC + sub)
        return 0
    lax.fori_loop(0, n_chunks, score_chunk, 0)

    select_all = qpos < topk
    for t in range(nkt):
        cand_ref[t] = jnp.where(t < n_tiles, jnp.full((rows_per_word, QB), -1, I32), 0)
        above_ref[t] = jnp.zeros((rows_per_word, QB), I32)

    def radix_step(it, need):
        cnt = jnp.zeros((rows_per_word, QB), I32)
        for t in range(nkt):
            cnt = cnt + lax.population_count(cand_ref[t] & planes_ref[t, it])
        cnt = jnp.sum(cnt, axis=0, keepdims=True)
        take = cnt >= need
        for t in range(nkt):
            cand = cand_ref[t]
            ones = cand & planes_ref[t, it]
            cand_ref[t] = jnp.where(take, ones, cand ^ ones)
            above_ref[t] = jnp.where(take, above_ref[t], above_ref[t] | ones)
        return jnp.where(take, need, need - cnt)
    need = lax.fori_loop(0, 32, radix_step, jnp.full((1, QB), topk, I32))

    ties = jnp.zeros((rows_per_word, QB), I32)
    for t in range(nkt):
        ties = ties + lax.population_count(cand_ref[t])
        sel_ref[t] = jnp.where(select_all, -1, above_ref[t] | cand_ref[t])
    surplus = jnp.where(select_all, 0, jnp.sum(ties, axis=0, keepdims=True) - need)

    @pl.when(jnp.max(surplus) > 0)
    def _():
        def count(pred):
            def body(c, acc):
                kc = keys_ref[pl.ds(c * TPC, TPC)].reshape(KC, QB)
                hit = jnp.where(pred(kc, c), 1, 0)
                return acc + jnp.sum(hit.reshape(KC // 8, 8, QB), axis=0)
            acc = lax.fori_loop(0, n_chunks, body, jnp.zeros((8, QB), I32))
            return jnp.sum(acc, axis=0, keepdims=True)

        def bit_step(it, thr):
            cand = thr ^ lax.shift_left(jnp.int32(1), 31 - it)
            cnt = count(lambda kc, c: kc >= cand)
            return jnp.where(cnt >= topk, cand, thr)
        thr = lax.fori_loop(0, 32, bit_step, jnp.full((1, QB), INT_MIN, I32))
        want = topk - count(lambda kc, c: kc > thr)
        n_bits = seq.bit_length()

        def tie_step(it, jb):
            cand = jb | lax.shift_left(jnp.int32(1), n_bits - 1 - it)
            def pred(kc, c):
                kpos = c * KC + lax.broadcasted_iota(I32, (KC, QB), 0)
                return jnp.logical_and(kc == thr, kpos < cand)
            return jnp.where(count(pred) <= want, cand, jb)
        jb = lax.fori_loop(0, n_bits, tie_step, jnp.zeros((1, QB), I32))

        def pack_tile(t, _):
            kc = keys_ref[t]
            kpos = t * KT + lax.broadcasted_iota(I32, (KT, QB), 0)
            chosen = jnp.logical_or(kc > thr, jnp.logical_and(kc == thr, kpos < jb))
            word = jnp.zeros((rows_per_word, QB), I32)
            for r in range(32):
                bit = np.int32(np.uint32(1 << r))
                word = word | jnp.where(chosen[r * rows_per_word:(r + 1) * rows_per_word, :], bit, 0)
            sel_ref[t] = jnp.where(select_all, -1, word)
            return 0
        lax.fori_loop(0, n_tiles, pack_tile, 0)

    m_ref[...] = jnp.full(m_ref.shape, NEG_BIG, F32)
    l_ref[...] = jnp.zeros(l_ref.shape, F32)
    acc_ref[...] = jnp.zeros(acc_ref.shape, F32)

    def masked_scores(t, dst_ref, dmax_ref):
        word = sel_ref[t]
        picked = jnp.concatenate([lax.shift_left(word, jnp.int32(31 - r)) for r in range(32)], axis=0)
        kpos = t * KT + lax.broadcasted_iota(I32, (KT, QB), 0)
        sel = jnp.logical_and(picked < 0, kpos <= qpos)
        bias = jnp.where(sel, 0.0, NEG_BIG)
        bias2 = jnp.concatenate([bias, bias], axis=1)
        kc = kb_ref[pl.ds(pl.multiple_of(t * KT, KT), KT), :]
        for p in range(ATTN_HEADS // 2):
            s = _dot_nt(kc, qs_ref[2 * p:2 * p + 2].reshape(2 * QB, LANES)) + bias2
            dst_ref[p] = s
            dmax_ref[p] = jnp.max(s, axis=0, keepdims=True)

    def softmax_pv(t, src_ref, smax_ref):
        for p in range(ATTN_HEADS // 2):
            g = (2 * p) // heads_per_group
            m_old = m_ref[p]
            m_new = jnp.maximum(m_old, smax_ref[p])
            alpha = jnp.exp2(m_old - m_new)
            pt = jnp.exp2(src_ref[p] - m_new).astype(BF16)
            m_ref[p] = m_new
            pv = _dot(vt_ref[t, g], pt)
            l_ref[p] = alpha * l_ref[p] + pv[hd:hd + 1, :]
            acc_ref[p] = alpha * acc_ref[p] + pv[0:hd, :]

    masked_scores(0, s0_ref, x0_ref)

    def attend(u, _):
        t = 2 * u
        masked_scores(t + 1, s1_ref, x1_ref)
        softmax_pv(t, s0_ref, x0_ref)
        masked_scores(jnp.minimum(t + 2, n_tiles - 1), s0_ref, x0_ref)
        softmax_pv(t + 1, s1_ref, x1_ref)
        return 0
    lax.fori_loop(0, n_tiles // 2, attend, 0)

    for p in range(ATTN_HEADS // 2):
        o = acc_ref[p] / l_ref[p]
        both = jnp.concatenate([o[:, :QB], o[:, QB:]], axis=0)
        o_ref[:, p * LANES:(p + 1) * LANES] = both.T.astype(o_ref.dtype)


def _dsa_branch(main3d, misc3d):
    bsz, seq, _ = main3d.shape
    topk = min(TOPK_MAX, seq // 4)
    width = ATTN_HEADS * ATTN_HEAD_DIM
    nkt = seq // KEY_TILE
    n_pairs = ATTN_HEADS // 2
    kern = functools.partial(_dsa_kernel, topk=topk)
    return pl.pallas_call(
        kern,
        grid=(bsz, seq // Q_BLOCK),
        in_specs=[
            pl.BlockSpec((None, Q_BLOCK, width), lambda b, i: (b, i, Q_TILE)),
            pl.BlockSpec((None, Q_BLOCK, IDX_HEADS * IDX_DIM), lambda b, i: (b, i, COL_QI)),
            pl.BlockSpec((None, seq, LANES), lambda b, i: (b, 0, COL_K)),
            pl.BlockSpec((None, seq, LANES), lambda b, i: (b, 0, COL_V)),
            pl.BlockSpec((None, seq, LANES), lambda b, i: (b, 0, COL_KI)),
            pl.BlockSpec((None, Q_BLOCK, LANES), lambda b, i: (b, i, COL_DTWI)),
        ],
        out_specs=pl.BlockSpec((None, Q_BLOCK, width), lambda b, i: (b, i, 0)),
        out_shape=jax.ShapeDtypeStruct((bsz, seq, width), BF16),
        scratch_shapes=[
            pltpu.VMEM((seq, LANES), BF16),
            pltpu.VMEM((seq, LANES), BF16),
            pltpu.VMEM((nkt, ATTN_KV_HEADS, ATTN_HEAD_DIM + ONES_ROWS, KEY_TILE), BF16),
            pltpu.VMEM((ATTN_HEADS, Q_BLOCK, LANES), BF16),
            pltpu.VMEM((IDX_HEADS, Q_BLOCK, LANES), BF16),
            pltpu.VMEM((nkt, KEY_TILE, Q_BLOCK), I32),
            pltpu.VMEM((nkt, 32, KEY_TILE // 32, Q_BLOCK), I32),
            pltpu.VMEM((nkt, KEY_TILE // 32, Q_BLOCK), I32),
            pltpu.VMEM((nkt, KEY_TILE // 32, Q_BLOCK), I32),
            pltpu.VMEM((nkt, KEY_TILE // 32, Q_BLOCK), I32),
            pltpu.VMEM((n_pairs, KEY_TILE, 2 * Q_BLOCK), F32),
            pltpu.VMEM((n_pairs, KEY_TILE, 2 * Q_BLOCK), F32),
            pltpu.VMEM((n_pairs, 1, 2 * Q_BLOCK), F32),
            pltpu.VMEM((n_pairs, 1, 2 * Q_BLOCK), F32),
            pltpu.VMEM((n_pairs, 1, 2 * Q_BLOCK), F32),
            pltpu.VMEM((n_pairs, 1, 2 * Q_BLOCK), F32),
            pltpu.VMEM((n_pairs, ATTN_HEAD_DIM, 2 * Q_BLOCK), F32),
        ],
        compiler_params=_cparams(("parallel", "arbitrary")),
        name="dsa_attention",
    )(main3d, misc3d, misc3d, misc3d, misc3d, misc3d)


def _merge_kernel(h_ref, ys_ref, ya_ref, gs_ref, ga_ref, wso_ref, wao_ref, wo_ref, o_ref):
    a = _dot(ys_ref[...], wso_ref[...])
    b = _dot(ya_ref[...], wao_ref[...])
    merged = jax.nn.sigmoid(gs_ref[...].astype(F32)) * a + jax.nn.sigmoid(ga_ref[...].astype(F32)) * b
    o_ref[...] = h_ref[...] + _dot(merged.astype(BF16), wo_ref[...])


def _merge(h2d, y_ssd, y_attn, proj2d, wso, wao, wo):
    n, d = h2d.shape
    tm = 512
    full = lambda a: pl.BlockSpec(a.shape, lambda i: (0, 0))
    return pl.pallas_call(
        _merge_kernel,
        grid=(n // tm,),
        in_specs=[
            pl.BlockSpec((tm, d), lambda i: (i, 0)),
            pl.BlockSpec((tm, y_ssd.shape[1]), lambda i: (i, 0)),
            pl.BlockSpec((tm, y_attn.shape[1]), lambda i: (i, 0)),
            pl.BlockSpec((tm, d), lambda i: (i, 6)),
            pl.BlockSpec((tm, d), lambda i: (i, 7)),
            full(wso), full(wao), full(wo),
        ],
        out_specs=pl.BlockSpec((tm, d), lambda i: (i, 0)),
        out_shape=jax.ShapeDtypeStruct((n, d), F32),
        compiler_params=_cparams(("parallel",)),
        name="merge_out",
    )(h2d, y_ssd, y_attn, proj2d, proj2d, wso, wao, wo)


def _normproj_kernel(x_ref, g_ref, w_ref, o_ref):
    o_ref[...] = _dot(_rms(x_ref[...], g_ref[...]).astype(BF16), w_ref[...]).astype(o_ref.dtype)


def _norm_project(x2d, gain, w_bf16, out_dtype):
    n, d = x2d.shape
    tm = min(512, n)
    return pl.pallas_call(
        _normproj_kernel,
        grid=(n // tm,),
        in_specs=[pl.BlockSpec((tm, d), lambda i: (i, 0)), pl.BlockSpec((1, d), lambda i: (0, 0)),
                  pl.BlockSpec(w_bf16.shape, lambda i: (0, 0))],
        out_specs=pl.BlockSpec((tm, w_bf16.shape[1]), lambda i: (i, 0)),
        out_shape=jax.ShapeDtypeStruct((n, w_bf16.shape[1]), out_dtype),
        compiler_params=_cparams(("parallel",)),
        name="mem_kv_projection",
    )(x2d, gain.reshape(1, d), w_bf16)


def _xattn_kernel(h_ref, g_ref, kv_ref, wq_ref, wo_ref, o_ref):
    d = h_ref.shape[1]
    hd = d // XATTN_HEADS
    h = h_ref[...]
    q = _dot(_rms(h, g_ref[...]).astype(BF16), wq_ref[...]) * (hd ** -0.5)
    qb = q.astype(BF16)
    outs = []
    for a in range(XATTN_HEADS):
        k = kv_ref[:, a * hd:(a + 1) * hd]
        v = kv_ref[:, d + a * hd:d + (a + 1) * hd]
        s = _dot_nt(qb[:, a * hd:(a + 1) * hd], k)
        p = jnp.exp(s - jnp.max(s, axis=1, keepdims=True))
        o = _dot(p.astype(BF16), v) / jnp.sum(p, axis=1, keepdims=True)
        outs.append(o.astype(BF16))
    o_ref[...] = h + _dot(jnp.concatenate(outs, axis=1), wo_ref[...])


def _cross_attention(h3d, gain, kv3d, wq, wo):
    bsz, seq, d = h3d.shape
    tm = min(512, seq)
    mlen = kv3d.shape[1]
    return pl.pallas_call(
        _xattn_kernel,
        grid=(bsz, seq // tm),
        in_specs=[
            pl.BlockSpec((None, tm, d), lambda b, i: (b, i, 0)),
            pl.BlockSpec((1, d), lambda b, i: (0, 0)),
            pl.BlockSpec((None, mlen, 2 * d), lambda b, i: (b, 0, 0)),
            pl.BlockSpec(wq.shape, lambda b, i: (0, 0)),
            pl.BlockSpec(wo.shape, lambda b, i: (0, 0)),
        ],
        out_specs=pl.BlockSpec((None, tm, d), lambda b, i: (b, i, 0)),
        out_shape=jax.ShapeDtypeStruct((bsz, seq, d), F32),
        compiler_params=_cparams(("parallel", "parallel")),
        name="cross_attention",
    )(h3d, gain.reshape(1, d), kv3d, wq, wo)


def _router_kernel(h_ref, g_ref, wr_ref, br_ref, trit_ref, aug_ref, rank_ref, gsel_ref, cnt_ref, base_ref):
    i = pl.program_id(0)
    d = h_ref.shape[1]

    @pl.when(i == 0)
    def _():
        base_ref[...] = jnp.zeros(base_ref.shape, F32)

    hn = _rms(h_ref[...], g_ref[...])
    aug_ref[:, 0:d] = hn
    h_hi, h_mid, h_lo = _split3(hn)
    w_hi, w_mid, w_lo = wr_ref[0], wr_ref[1], wr_ref[2]
    logits = (_dot(h_hi, w_hi) + (_dot(h_hi, w_mid) + _dot(h_mid, w_hi))
              + (_dot(h_hi, w_lo) + _dot(h_mid, w_mid) + _dot(h_lo, w_hi))) + br_ref[...]
    tm = logits.shape[0]
    lane = lax.broadcasted_iota(I32, (tm, LANES), 1)
    neg_inf = -jnp.inf
    n_e = MOE_GROUPS * MOE_EXPERTS_PER_GROUP
    is_g = jnp.logical_and(lane >= n_e, lane < n_e + MOE_GROUPS)
    gl = jnp.where(is_g, logits, neg_inf)
    g_max = jnp.max(gl, axis=1, keepdims=True)
    g_sel = jnp.min(jnp.where(gl == g_max, lane, LANES), axis=1, keepdims=True) - n_e
    g_w = 1.0 / jnp.sum(jnp.where(is_g, jnp.exp(gl - g_max), 0.0), axis=1, keepdims=True)
    in_grp = jnp.logical_and(lane >= g_sel * MOE_EXPERTS_PER_GROUP,
                             lane < (g_sel + 1) * MOE_EXPERTS_PER_GROUP)
    el = jnp.where(in_grp, logits, neg_inf)
    v1 = jnp.max(el, axis=1, keepdims=True)
    i1 = jnp.min(jnp.where(el == v1, lane, LANES), axis=1, keepdims=True)
    el2 = jnp.where(lane == i1, neg_inf, el)
    v2 = jnp.max(el2, axis=1, keepdims=True)
    i2 = jnp.min(jnp.where(el2 == v2, lane, LANES), axis=1, keepdims=True)
    e2 = jnp.exp(v2 - v1)
    w1 = g_w / (1.0 + e2)
    w2 = g_w * e2 / (1.0 + e2)
    aug_ref[:, d:] = jnp.where(lane == i1, w1, 0.0) + jnp.where(lane == i2, w2, 0.0)

    onehot_t = jnp.where(lane == g_sel, 1.0, 0.0).T
    upto = _dot(onehot_t.astype(BF16), trit_ref[...]) + base_ref[...]
    grp = lax.broadcasted_iota(I32, onehot_t.shape, 0).astype(F32)
    rank_ref[...] = (jnp.sum(onehot_t * upto, axis=0, keepdims=True) - 1.0).astype(I32)
    gsel_ref[...] = jnp.sum(onehot_t * grp, axis=0, keepdims=True).astype(I32)
    base_ref[...] = upto[:, tm - 1:tm]
    cnt_ref[...] = jnp.broadcast_to(upto[:, tm - 1:tm], cnt_ref.shape)


def _moe_route(h2d, gain, wr3, br):
    n, d = h2d.shape
    tm = 512
    nt = n // tm
    trit = (jnp.arange(tm)[:, None] <= jnp.arange(tm)[None, :]).astype(BF16)
    return pl.pallas_call(
        _router_kernel,
        grid=(nt,),
        in_specs=[pl.BlockSpec((tm, d), lambda i: (i, 0)), pl.BlockSpec((1, d), lambda i: (0, 0)),
                  pl.BlockSpec(wr3.shape, lambda i: (0, 0, 0)), pl.BlockSpec((1, LANES), lambda i: (0, 0)),
                  pl.BlockSpec((tm, tm), lambda i: (0, 0))],
        out_specs=[pl.BlockSpec((tm, d + LANES), lambda i: (i, 0)),
                   pl.BlockSpec((None, 1, tm), lambda i: (i, 0, 0)),
                   pl.BlockSpec((None, 1, tm), lambda i: (i, 0, 0)),
                   pl.BlockSpec((LANES, LANES), lambda i: (0, 0))],
        out_shape=[jax.ShapeDtypeStruct((n, d + LANES), F32),
                   jax.ShapeDtypeStruct((nt, 1, tm), I32),
                   jax.ShapeDtypeStruct((nt, 1, tm), I32),
                   jax.ShapeDtypeStruct((LANES, LANES), F32)],
        scratch_shapes=[pltpu.VMEM((LANES, 1), F32)],
        compiler_params=_cparams(("arbitrary",)),
        name="moe_router",
    )(h2d, gain.reshape(1, d), wr3, br, trit)


MOE_TILE = 512
ROW_BLOCK = 256
ISSUE_UNROLL = 8


def _scatter_rows_kernel(dest_ref, src_ref, init_ref, out_ref, sem):
    del init_ref
    base = pl.program_id(0) * ROW_BLOCK

    def row_copy(r):
        return pltpu.make_async_copy(src_ref.at[pl.ds(r, 1)], out_ref.at[pl.ds(dest_ref[base + r], 1)], sem)

    def issue(r, _):
        row_copy(r).start()
        return 0
    lax.fori_loop(0, ROW_BLOCK, issue, 0, unroll=ISSUE_UNROLL)
    pltpu.make_async_copy(src_ref, out_ref.at[pl.ds(0, ROW_BLOCK)], sem).wait()


def _scatter_rows(src, dest, n_out):
    n, w = src.shape
    init = jnp.zeros((n_out, w), src.dtype)
    return pl.pallas_call(
        _scatter_rows_kernel,
        grid_spec=pltpu.PrefetchScalarGridSpec(
            num_scalar_prefetch=1,
            grid=(n // ROW_BLOCK,),
            in_specs=[pl.BlockSpec((ROW_BLOCK, w), lambda i, dest: (i, 0)),
                      pl.BlockSpec(memory_space=pl.ANY)],
            out_specs=pl.BlockSpec(memory_space=pl.ANY),
            scratch_shapes=[pltpu.SemaphoreType.DMA(())],
        ),
        out_shape=jax.ShapeDtypeStruct((n_out, w), src.dtype),
        input_output_aliases={2: 0},
        compiler_params=_cparams(("arbitrary",)),
        name="moe_sort_rows",
    )(dest, src, init)


def _sorted_experts_kernel(tile_group_ref, n_used_ref, hn_ref, cw_ref, wg_ref, wu_ref, wd_ref, o_ref):
    j = pl.program_id(0)

    @pl.when(j < n_used_ref[0])
    def _():
        hn = hn_ref[...].astype(BF16)
        first = tile_group_ref[j] * MOE_EXPERTS_PER_GROUP
        lane = lax.broadcasted_iota(I32, cw_ref.shape, 1)
        cw = cw_ref[...]
        acc = jnp.zeros(o_ref.shape, F32)
        for e in range(MOE_EXPERTS_PER_GROUP):
            w_e = jnp.sum(jnp.where(lane == first + e, cw, 0.0), axis=1, keepdims=True)
            hid = _silu(_dot(hn, wg_ref[e])) * _dot(hn, wu_ref[e]) * w_e
            acc = acc + _dot(hid.astype(BF16), wd_ref[e])
        o_ref[...] = acc

    @pl.when(j >= n_used_ref[0])
    def _():
        o_ref[...] = jnp.zeros(o_ref.shape, F32)


def _sorted_experts(aug_sorted, tile_group, n_used, wg, wu, wd):
    n_pad, wa = aug_sorted.shape
    d = wa - LANES
    _, n_e, _, dff = wg.shape
    tm = MOE_TILE
    return pl.pallas_call(
        _sorted_experts_kernel,
        grid_spec=pltpu.PrefetchScalarGridSpec(
            num_scalar_prefetch=2,
            grid=(n_pad // tm,),
            in_specs=[
                pl.BlockSpec((tm, d), lambda j, tg, nu: (j, 0)),
                pl.BlockSpec((tm, LANES), lambda j, tg, nu: (j, d // LANES)),
                pl.BlockSpec((None, n_e, d, dff), lambda j, tg, nu: (tg[j], 0, 0, 0)),
                pl.BlockSpec((None, n_e, d, dff), lambda j, tg, nu: (tg[j], 0, 0, 0)),
                pl.BlockSpec((None, n_e, dff, d), lambda j, tg, nu: (tg[j], 0, 0, 0)),
            ],
            out_specs=pl.BlockSpec((tm, d), lambda j, tg, nu: (j, 0)),
        ),
        out_shape=jax.ShapeDtypeStruct((n_pad, d), F32),
        compiler_params=_cparams(("arbitrary",)),
        name="moe_experts",
    )(tile_group, n_used, aug_sorted, aug_sorted, wg, wu, wd)


def _gather_add_kernel(dest_ref, h_ref, delta_ref, fg_ref, o_ref, buf_ref, sem, *, final_norm):
    i = pl.program_id(0)
    n_steps = pl.num_programs(0)

    def row_copy(step, r):
        slot = lax.rem(step, 2)
        return pltpu.make_async_copy(delta_ref.at[pl.ds(dest_ref[step * ROW_BLOCK + r], 1)],
                                     buf_ref.at[slot, pl.ds(r, 1)], sem.at[slot])

    def issue_all(step):
        def issue(r, _):
            row_copy(step, r).start()
            return 0
        lax.fori_loop(0, ROW_BLOCK, issue, 0, unroll=ISSUE_UNROLL)

    @pl.when(i == 0)
    def _():
        issue_all(0)

    @pl.when(i + 1 < n_steps)
    def _():
        issue_all(i + 1)

    slot = lax.rem(i, 2)
    pltpu.make_async_copy(delta_ref.at[pl.ds(0, ROW_BLOCK)], buf_ref.at[slot], sem.at[slot]).wait()

    out = h_ref[...] + buf_ref[slot]
    if final_norm:
        out = _rms(out, fg_ref[...])
    o_ref[...] = out


def _gather_add(h2d, delta_sorted, dest, final_gain, final_norm):
    n, d = h2d.shape
    kern = functools.partial(_gather_add_kernel, final_norm=final_norm)
    return pl.pallas_call(
        kern,
        grid_spec=pltpu.PrefetchScalarGridSpec(
            num_scalar_prefetch=1,
            grid=(n // ROW_BLOCK,),
            in_specs=[pl.BlockSpec((ROW_BLOCK, d), lambda i, dest: (i, 0)),
                      pl.BlockSpec(memory_space=pl.ANY),
                      pl.BlockSpec((1, d), lambda i, dest: (0, 0))],
            out_specs=pl.BlockSpec((ROW_BLOCK, d), lambda i, dest: (i, 0)),
            scratch_shapes=[pltpu.VMEM((2, ROW_BLOCK, d), F32), pltpu.SemaphoreType.DMA((2,))],
        ),
        out_shape=jax.ShapeDtypeStruct((n, d), F32),
        compiler_params=_cparams(("arbitrary",)),
        name="moe_unsort_add",
    )(dest, h2d, delta_sorted, final_gain.reshape(1, d))


def _moe_layout(rank, gsel, counts, n):
    tm = MOE_TILE
    cnt = counts[:MOE_GROUPS, 0].astype(I32)
    seg = ((cnt + tm - 1) // tm) * tm
    ends = jnp.cumsum(seg)
    dest = (ends - seg)[gsel.reshape(n)] + rank.reshape(n)
    n_tiles = n // tm + MOE_GROUPS
    starts = jnp.arange(n_tiles, dtype=I32) * tm
    tile_group = jnp.minimum(jnp.sum(starts[:, None] >= ends[None, :], axis=1), MOE_GROUPS - 1).astype(I32)
    n_used = (ends[-1:] // tm).astype(I32)
    return dest.astype(I32), tile_group, n_used, n_tiles * tm


def _prep_router(w_group, b_group, w_router, b_router):
    w = jnp.concatenate([w_router, w_group], axis=1).astype(F32)
    w = jnp.pad(w, ((0, 0), (0, LANES - w.shape[1])))
    hi = w.astype(BF16)
    r1 = w - hi.astype(F32)
    mid = r1.astype(BF16)
    lo = (r1 - mid.astype(F32)).astype(BF16)
    b = jnp.concatenate([b_router, b_group]).astype(F32)
    b = jnp.pad(b, (0, LANES - b.shape[0])).reshape(1, LANES)
    return jnp.stack([hi, mid, lo]), b


def kernel(x, mem, norm_mix, w_in, conv_w, conv_b, dt_bias, a_log, d_skip, ssd_norm, w_ssd_o, w_attn_o, w_out,
           norm_xattn, norm_mem, w_cq, w_ckv, w_co, norm_ffn, w_group, b_group, w_router, b_router,
           w_gate_e, w_up_e, w_down_e, norm_final):
    bsz, seq, d = x.shape
    depth = w_in.shape[0]
    mlen = mem.shape[1]
    heads = dt_bias.shape[1]
    d_inner = heads * SSD_HEAD_DIM
    conv_ch = conv_w.shape[2]
    n = bsz * seq
    rope_c, rope_s1, rope_s2 = _rope_tables(seq)
    h = x.reshape(n, d)
    mem2d = mem.reshape(bsz * mlen, d)
    for l in range(depth):
        w_in_l = _prep_w_in(w_in[l], d_inner, conv_ch, heads)
        main, misc = _in_projection(h, norm_mix[l], w_in_l, rope_c, rope_s1, rope_s2, seq)
        main3d = main.reshape(bsz, seq, main.shape[1])
        misc3d = misc.reshape(bsz, seq, misc.shape[1])
        y_ssd = _ssd_branch(main3d, misc3d, conv_w[l], conv_b[l], dt_bias[l], a_log[l], d_skip[l], ssd_norm[l])
        y_attn = _dsa_branch(main3d, misc3d)
        h = _merge(h, y_ssd.reshape(n, d_inner), y_attn.reshape(n, -1), main,
                   w_ssd_o[l].astype(BF16), w_attn_o[l].astype(BF16), w_out[l].astype(BF16))
        kv = _norm_project(mem2d, norm_mem[l], w_ckv[l].astype(BF16), BF16)
        h = _cross_attention(h.reshape(bsz, seq, d), norm_xattn[l], kv.reshape(bsz, mlen, 2 * d),
                             w_cq[l].astype(BF16), w_co[l].astype(BF16)).reshape(n, d)
        wr3, br = _prep_router(w_group[l], b_group[l], w_router[l], b_router[l])
        aug, rank, gsel, counts = _moe_route(h, norm_ffn[l], wr3, br)
        dest, tile_group, n_used, n_pad = _moe_layout(rank, gsel, counts, n)
        aug_sorted = _scatter_rows(aug, dest, n_pad)
        delta = _sorted_experts(aug_sorted, tile_group, n_used, w_gate_e[l].astype(BF16),
                                w_up_e[l].astype(BF16), w_down_e[l].astype(BF16))
        h = _gather_add(h, delta, dest, norm_final, final_norm=(l == depth - 1))
    return h.reshape(bsz, seq, d)
```

```python
import functools
import math

import jax
import jax.numpy as jnp
import numpy as np
from jax import lax
from jax.experimental import pallas as pl
from jax.experimental.pallas import tpu as pltpu

F32 = jnp.float32
BF16 = jnp.bfloat16
I32 = jnp.int32

SSD_HEAD_DIM = 64
SSD_GROUPS = 4
SSD_STATE = 128
SSD_CONV = 4
SSD_CHUNK = 128
SSD_SEQS_PER_STEP = 2
ATTN_HEADS = 16
ATTN_KV_HEADS = 2
ATTN_HEAD_DIM = 64
Q_BLOCK = 128
TOPK_MAX = 256
IDX_HEADS = 8
IDX_DIM = 64
ROPE_THETA = 10000.0
XATTN_HEADS = 4
MOE_GROUPS = 4
MOE_EXPERTS_PER_GROUP = 8
RMS_EPS = 1e-6

LANES = 128
KEY_CHUNK = 512
KEY_TILE = 256
INT_MIN = -2 ** 31
NEG_BIG = -1e30
LOG2E = math.log2(math.e)
ONES_ROWS = 16
VMEM_LIMIT = 56 * 1024 * 1024


def _cparams(sem):
    return pltpu.CompilerParams(dimension_semantics=sem, vmem_limit_bytes=VMEM_LIMIT)


def _split3(v):
    hi = v.astype(BF16)
    r1 = v - hi.astype(F32)
    mid = r1.astype(BF16)
    lo = (r1 - mid.astype(F32)).astype(BF16)
    return hi, mid, lo


def _dot(a, b):
    return jnp.dot(a, b, preferred_element_type=F32)


def _dot_nt(a, b):
    return lax.dot_general(a, b, (((1,), (1,)), ((), ())), preferred_element_type=F32)


def _spread(v, m01, terms):
    parts = _split3(v)[:terms]
    out = _dot(parts[0], m01)
    for part in parts[1:]:
        out = out + _dot(part, m01)
    return out


def _dot3_lhs01(m01, v):
    hi, mid, lo = _split3(v)
    return _dot(m01, hi) + _dot(m01, mid) + _dot(m01, lo)


def _rms(x, g):
    ms = jnp.mean(x * x, axis=-1, keepdims=True)
    return x * lax.rsqrt(ms + RMS_EPS) * g


def _silu(x):
    return x * jax.nn.sigmoid(x)


Q_TILE = 5
MISC_TILE = 8
MISC_ROPE_CHUNKS = (0, 1, 2, 3, 4, 6)
COL_QI = 0
COL_K, COL_V, COL_KI, COL_DTWI = 4, 5, 6, 7
DT_LANES = 32


def _rope_chunk(a, c, s):
    return a * c + pltpu.roll(a, LANES // 2, 1) * s


def _inproj_kernel(x_ref, g_ref, w_ref, c_ref, s_ref, o_ref, misc_ref, xn_ref):
    j = pl.program_id(1)

    @pl.when(j == 0)
    def _():
        xn_ref[...] = _rms(x_ref[...], g_ref[...]).astype(BF16)

    acc = _dot(xn_ref[...], w_ref[...])
    n_chunks = acc.shape[1] // LANES

    def roped(dst_ref, chunks):
        c, sn = c_ref[...], s_ref[...]
        for k in range(n_chunks):
            a = acc[:, k * LANES:(k + 1) * LANES]
            if k in chunks:
                a = _rope_chunk(a, c, sn)
            dst_ref[:, k * LANES:(k + 1) * LANES] = a.astype(dst_ref.dtype)

    @pl.when(j == Q_TILE)
    def _():
        roped(o_ref, tuple(range(n_chunks)))

    @pl.when(j == MISC_TILE)
    def _():
        roped(misc_ref, MISC_ROPE_CHUNKS)

    @pl.when(jnp.logical_and(j != Q_TILE, j != MISC_TILE))
    def _():
        o_ref[...] = acc.astype(o_ref.dtype)


def _in_projection(x2d, gain, w_bf16, rope_c, rope_s, seq):
    n, d = x2d.shape
    tm = min(1024, seq)
    tn = 1024
    pos_tiles = seq // tm
    n_tiles = w_bf16.shape[1] // tn
    assert n_tiles == MISC_TILE + 1
    return pl.pallas_call(
        _inproj_kernel,
        grid=(n // tm, n_tiles),
        in_specs=[
            pl.BlockSpec((tm, d), lambda i, j: (i, 0)),
            pl.BlockSpec((1, d), lambda i, j: (0, 0)),
            pl.BlockSpec((d, tn), lambda i, j: (0, j)),
            pl.BlockSpec((tm, LANES), lambda i, j: (i % pos_tiles, 0)),
            pl.BlockSpec((tm, LANES), lambda i, j: (i % pos_tiles, 0)),
        ],
        out_specs=[pl.BlockSpec((tm, tn), lambda i, j: (i, jnp.minimum(j, MISC_TILE - 1))),
                   pl.BlockSpec((tm, tn), lambda i, j: (i, 0))],
        out_shape=[jax.ShapeDtypeStruct((n, MISC_TILE * tn), BF16), jax.ShapeDtypeStruct((n, tn), F32)],
        scratch_shapes=[pltpu.VMEM((tm, d), BF16)],
        compiler_params=_cparams(("parallel", "arbitrary")),
        name="in_projection",
    )(x2d, gain.reshape(1, d), w_bf16, rope_c, rope_s)


def _pair_halves(w):
    d = w.shape[0]
    half = ATTN_HEAD_DIM // 2
    return w.reshape(d, -1, 2, 2, half).transpose(0, 1, 3, 2, 4).reshape(d, -1)


def _prep_w_in(w, d_inner, conv_ch, ssd_heads):
    widths = (d_inner, conv_ch, ssd_heads, ATTN_HEADS * ATTN_HEAD_DIM, ATTN_KV_HEADS * ATTN_HEAD_DIM,
              ATTN_KV_HEADS * ATTN_HEAD_DIM, IDX_HEADS * IDX_DIM, IDX_DIM, IDX_HEADS,
              w.shape[0], w.shape[0])
    splits = [int(p) for p in np.cumsum(widths)[:-1]]
    z, xbc, dt, q, k, v, qi, ki, wi, gs, ga = jnp.split(w, splits, axis=1)

    def pad(a, n):
        return jnp.pad(a, ((0, 0), (0, n - a.shape[1])))

    q, k, qi, ki = _pair_halves(q), _pair_halves(k), _pair_halves(qi), _pair_halves(pad(ki, LANES))
    misc = jnp.concatenate([qi, k, v, ki, pad(jnp.concatenate([dt, wi], axis=1), LANES)], axis=1)
    return jnp.concatenate([z, xbc, q, gs, ga, misc], axis=1).astype(BF16)


def _rope_tables(seq):
    inv = 1.0 / (ROPE_THETA ** (jnp.arange(0, ATTN_HEAD_DIM, 2, dtype=F32) / ATTN_HEAD_DIM))
    ang = jnp.arange(seq, dtype=F32)[:, None] * inv[None, :]
    cos, sin = jnp.cos(ang), jnp.sin(ang)
    c = jnp.concatenate([cos, cos, cos, cos], axis=1)
    s = jnp.concatenate([-sin, -sin, sin, sin], axis=1)
    return c, s


def _ssd_chunk(z_ref, xs_ref, bc_ref, dt_ref, cw_ref, cb_ref, dtb_ref, alog_ref, dsk_ref, ng_ref,
               r64_ref, tri_ref, shift_ref, y_ref, prev_ref, st_ref, act_ref, heads, groups):
    L = SSD_CHUNK
    P = SSD_HEAD_DIM
    hpg = heads // groups
    d_inner = heads * P
    gw = hpg * P
    nst = SSD_STATE

    def conv(cur_ref, lo, hi):
        cur = cur_ref[...]
        both = jnp.concatenate([prev_ref[:, lo:hi], cur], axis=0)
        acc = cb_ref[:, lo:hi] + cw_ref[SSD_CONV - 1:SSD_CONV, lo:hi] * cur.astype(F32)
        for k in range(1, SSD_CONV):
            acc = acc + cw_ref[SSD_CONV - 1 - k:SSD_CONV - k, lo:hi] * _dot(shift_ref[k - 1], both)
        prev_ref[:, lo:hi] = cur
        return _silu(acc)

    xs = conv(xs_ref, 0, d_inner)
    bc = conv(bc_ref, d_inner, prev_ref.shape[1])
    bmat = bc[:, :groups * nst]
    cmat = bc[:, groups * nst:]

    dt = jax.nn.softplus(dt_ref[...] + dtb_ref[...])
    da = dt * (-jnp.exp(alog_ref[...]))
    tri = tri_ref[...]
    acum = _dot3_lhs01(tri, da)
    act_ref[...] = acum.T
    e_acum = jnp.exp(acum)
    e_tail = jnp.exp(acum[L - 1:L, :] - acum)
    r64 = r64_ref[...]
    e64 = _spread(e_acum, r64, 2)
    t64 = _spread(e_tail, r64, 1)
    d64 = _spread(dt, r64, 1)
    xdt = xs * d64
    xdt_b = xdt.astype(BF16)
    xtl_b = (xdt * t64).astype(BF16)

    row = lax.broadcasted_iota(I32, (L, L), 0)
    col = lax.broadcasted_iota(I32, (L, L), 1)
    causal = row >= col
    first_head = lax.broadcasted_iota(I32, (1, 2 * P), 1) < P

    for g in range(groups):
        bg = bmat[:, g * nst:(g + 1) * nst]
        cg_b = cmat[:, g * nst:(g + 1) * nst].astype(BF16)
        cb = _dot_nt(cg_b, bg.astype(BF16))
        sg = st_ref[g]
        y_off = _dot(cg_b, sg.astype(BF16))
        y_heads = []
        for hh in range(0, hpg, 2):
            pair = []
            for h in (g * hpg + hh, g * hpg + hh + 1):
                seg = jnp.broadcast_to(acum[:, h:h + 1], (L, LANES)) - act_ref[h:h + 1, :]
                pair.append((cb * jnp.where(causal, jnp.exp(seg), 0.0)).astype(BF16))
            slab = xdt_b[:, (g * hpg + hh) * P:(g * hpg + hh + 2) * P]
            stacked = jnp.concatenate([jnp.where(first_head, slab, 0), jnp.where(first_head, 0, slab)], axis=0)
            y_heads.append(_dot(jnp.concatenate(pair, axis=1), stacked))
        lo, hi = g * gw, (g + 1) * gw
        y_g = jnp.concatenate(y_heads, axis=1) + y_off * e64[:, lo:hi]
        st_ref[g] = sg * e64[L - 1:L, lo:hi] + _dot(bg.T.astype(BF16), xtl_b[:, lo:hi])
        y_g = (y_g + dsk_ref[:, lo:hi] * xs[:, lo:hi]) * _silu(z_ref[:, lo:hi].astype(F32))
        ms = jnp.mean(y_g * y_g, axis=-1, keepdims=True)
        y_ref[:, lo:hi] = (y_g * lax.rsqrt(ms + RMS_EPS) * ng_ref[:, lo:hi]).astype(y_ref.dtype)


def _ssd_kernel(z_ref, xs_ref, bc_ref, dt_ref, cw_ref, cb_ref, dtb_ref, alog_ref, dsk_ref, ng_ref,
                r64_ref, tri_ref, shift_ref, y_ref, prev_ref, st_ref, act_ref, *, heads, groups):
    @pl.when(pl.program_id(1) == 0)
    def _():
        prev_ref[...] = jnp.zeros(prev_ref.shape, BF16)
        st_ref[...] = jnp.zeros(st_ref.shape, F32)

    for nb in range(z_ref.shape[0]):
        _ssd_chunk(z_ref.at[nb], xs_ref.at[nb], bc_ref.at[nb], dt_ref.at[nb], cw_ref, cb_ref, dtb_ref,
                   alog_ref, dsk_ref, ng_ref, r64_ref, tri_ref, shift_ref, y_ref.at[nb],
                   prev_ref.at[nb], st_ref.at[nb], act_ref.at[nb], heads, groups)


def _ssd_branch(main3d, misc3d, conv_w, conv_b, dt_bias, a_log, d_skip, ssd_norm):
    bsz, seq, _ = main3d.shape
    heads = dt_bias.shape[0]
    d_inner = heads * SSD_HEAD_DIM
    groups = SSD_GROUPS
    conv_ch = conv_w.shape[1]
    bc_w = conv_ch - d_inner
    L = SSD_CHUNK
    nb = SSD_SEQS_PER_STEP if bsz % SSD_SEQS_PER_STEP == 0 else 1

    def pad_lanes(v):
        return jnp.pad(v.astype(F32), (0, LANES - v.shape[0])).reshape(1, LANES)

    hidx = jnp.arange(LANES)[:, None]
    r64 = (hidx == (jnp.arange(d_inner) // SSD_HEAD_DIM)[None, :]).astype(BF16)
    tri = (jnp.arange(L)[:, None] >= jnp.arange(L)[None, :]).astype(BF16)
    taps = jnp.arange(1, SSD_CONV)[:, None, None]
    shift = (jnp.arange(2 * L)[None, None, :] == (L + jnp.arange(L)[None, :, None] - taps)).astype(BF16)
    dsk = jnp.repeat(d_skip.astype(F32), SSD_HEAD_DIM).reshape(1, d_inner)
    const = lambda shape: pl.BlockSpec(shape, lambda b, c: (0,) * len(shape))
    kern = functools.partial(_ssd_kernel, heads=heads, groups=groups)
    return pl.pallas_call(
        kern,
        grid=(bsz // nb, seq // L),
        in_specs=[
            pl.BlockSpec((nb, L, d_inner), lambda b, c: (b, c, 0)),
            pl.BlockSpec((nb, L, d_inner), lambda b, c: (b, c, 1)),
            pl.BlockSpec((nb, L, bc_w), lambda b, c: (b, c, 2 * d_inner // bc_w)),
            pl.BlockSpec((nb, L, LANES), lambda b, c: (b, c, COL_DTWI)),
            const((SSD_CONV, conv_ch)), const((1, conv_ch)), const((1, LANES)), const((1, LANES)),
            const((1, d_inner)), const((1, d_inner)),
            const((LANES, d_inner)), const((L, L)),
            const((SSD_CONV - 1, L, 2 * L)),
        ],
        out_specs=pl.BlockSpec((nb, L, d_inner), lambda b, c: (b, c, 0)),
        out_shape=jax.ShapeDtypeStruct((bsz, seq, d_inner), BF16),
        scratch_shapes=[
            pltpu.VMEM((nb, L, conv_ch), BF16),
            pltpu.VMEM((nb, groups, SSD_STATE, d_inner // groups), F32),
            pltpu.VMEM((nb, LANES, L), F32),
        ],
        compiler_params=_cparams(("parallel", "arbitrary")),
        name="ssd_scan",
    )(main3d, main3d, main3d, misc3d, conv_w.astype(F32), conv_b.reshape(1, conv_ch).astype(F32),
      pad_lanes(dt_bias), pad_lanes(a_log), dsk, ssd_norm.reshape(1, d_inner).astype(F32), r64, tri, shift)


def _sortable_key(x):
    x = jnp.where(x == 0.0, 0.0, x)
    b = pltpu.bitcast(x, I32)
    return b ^ (lax.shift_right_arithmetic(b, 31) & 0x7FFFFFFF)


_BIT_MASKS = (0x0000FFFF, 0x00FF00FF, 0x0F0F0F0F, 0x33333333, 0x55555555)


def _bit_planes(rows):
    x = list(rows[::-1])
    j = 16
    for m in _BIT_MASKS:
        mask = np.int32(np.uint32(m))
        k = 0
        while k < 32:
            t = (x[k] ^ lax.shift_right_logical(x[k + j], jnp.int32(j))) & mask
            x[k] = x[k] ^ t
            x[k + j] = x[k + j] ^ lax.shift_left(t, jnp.int32(j))
            k = (k + j + 1) & ~j
        j >>= 1
    return x


def _dsa_kernel(q_ref, qi_ref, k_ref, v_ref, ki_ref, wi_ref, o_ref,
                kb_ref, kib_ref, vt_ref, qs_ref, qis_ref, keys_ref, planes_ref, sel_ref, cand_ref, above_ref,
                s0_ref, s1_ref, x0_ref, x1_ref, m_ref, l_ref, acc_ref, *, topk):
    i = pl.program_id(1)
    QB, KC, KT = Q_BLOCK, KEY_CHUNK, KEY_TILE
    TPC = KC // KT
    hd = ATTN_HEAD_DIM
    seq = k_ref.shape[0]
    n_chunks = (i * QB + QB + KC - 1) // KC
    lane = lax.broadcasted_iota(I32, (1, LANES), 1)
    quarter = hd // 2
    first = (lane // quarter) % 2 == 0

    @pl.when(i == 0)
    def _():
        def fill(t, _):
            rows = pl.ds(pl.multiple_of(t * KT, KT), KT)
            kb_ref[rows, :] = k_ref[rows, :].astype(BF16)
            kib_ref[rows, :] = ki_ref[rows, :].astype(BF16)
            vt = v_ref[rows, :].T.astype(BF16)
            for g in range(ATTN_KV_HEADS):
                vt_ref[t, g, 0:hd, :] = vt[g * hd:(g + 1) * hd, :]
                vt_ref[t, g, hd:hd + ONES_ROWS, :] = jnp.ones((ONES_ROWS, KT), BF16)
            return 0
        lax.fori_loop(0, seq // KT, fill, 0)
        planes_ref[...] = jnp.zeros(planes_ref.shape, I32)

    qpos = i * QB + lax.broadcasted_iota(I32, (1, QB), 1)
    idx_scale = (IDX_DIM ** -0.5) * (IDX_HEADS ** -0.5)
    wt = (wi_ref[...] * idx_scale).T
    w_rows = [wt[DT_LANES + h:DT_LANES + h + 1, :] for h in range(IDX_HEADS)]

    heads_per_group = ATTN_HEADS // ATTN_KV_HEADS
    for p in range(ATTN_HEADS // 2):
        slab = q_ref[:, p * LANES:(p + 1) * LANES].astype(F32) * ((hd ** -0.5) * LOG2E)
        if (2 * p) // heads_per_group == 0:
            even = jnp.where(first, slab, 0.0)
            odd = jnp.where(first, pltpu.roll(slab, LANES - quarter, 1), 0.0)
        else:
            even = jnp.where(first, 0.0, pltpu.roll(slab, quarter, 1))
            odd = jnp.where(first, 0.0, slab)
        qs_ref[2 * p] = even.astype(BF16)
        qs_ref[2 * p + 1] = odd.astype(BF16)
    for p in range(IDX_HEADS // 2):
        slab = qi_ref[:, p * LANES:(p + 1) * LANES]
        qis_ref[2 * p] = slab.astype(BF16)
        qis_ref[2 * p + 1] = pltpu.roll(slab, LANES - quarter, 1).astype(BF16)

    n_tiles = n_chunks * TPC
    nkt = seq // KT
    rows_per_word = KT // 32

    def score_tile(t):
        kic = kib_ref[pl.ds(pl.multiple_of(t * KT, KT), KT), :]
        acc = jnp.zeros((KT, QB), F32)
        for p in range(IDX_HEADS // 2):
            rel = _dot_nt(kic, qis_ref[2 * p:2 * p + 2].reshape(2 * QB, LANES))
            acc = acc + w_rows[2 * p] * jnp.maximum(rel[:, :QB], 0.0)
            acc = acc + w_rows[2 * p + 1] * jnp.maximum(rel[:, QB:], 0.0)
        kpos = t * KT + lax.broadcasted_iota(I32, (KT, QB), 0)
        keys = jnp.where(kpos <= qpos, _sortable_key(acc), INT_MIN)
        keys_ref[t] = keys
        unsigned = keys ^ INT_MIN
        planes = _bit_planes([unsigned[r * rows_per_word:(r + 1) * rows_per_word, :] for r in range(32)])
        for bit in range(32):
            planes_ref[t, bit] = planes[bit]

    def score_chunk(c, _):
        for sub in range(TPC):
            score_tile(c * TPC + sub)
        return 0
    lax.fori_loop(0, n_chunks, score_chunk, 0)

    select_all = qpos < topk
    for t in range(nkt):
        cand_ref[t] = jnp.where(t < n_tiles, jnp.full((rows_per_word, QB), -1, I32), 0)
        above_ref[t] = jnp.zeros((rows_per_word, QB), I32)

    def radix_step(it, need):
        cnt = jnp.zeros((rows_per_word, QB), I32)
        for t in range(nkt):
            cnt = cnt + lax.population_count(cand_ref[t] & planes_ref[t, it])
        cnt = jnp.sum(cnt, axis=0, keepdims=True)
        take = cnt >= need
        for t in range(nkt):
            cand = cand_ref[t]
            ones = cand & planes_ref[t, it]
            cand_ref[t] = jnp.where(take, ones, cand ^ ones)
            above_ref[t] = jnp.where(take, above_ref[t], above_ref[t] | ones)
        return jnp.where(take, need, need - cnt)
    need = lax.fori_loop(0, 32, radix_step, jnp.full((1, QB), topk, I32))

    ties = jnp.zeros((rows_per_word, QB), I32)
    for t in range(nkt):
        ties = ties + lax.population_count(cand_ref[t])
        sel_ref[t] = jnp.where(select_all, -1, above_ref[t] | cand_ref[t])
    surplus = jnp.where(select_all, 0, jnp.sum(ties, axis=0, keepdims=True) - need)

    @pl.when(jnp.max(surplus) > 0)
    def _():
        def count(pred):
            def body(c, acc):
                kc = keys_ref[pl.ds(c * TPC, TPC)].reshape(KC, QB)
                hit = jnp.where(pred(kc, c), 1, 0)
                return acc + jnp.sum(hit.reshape(KC // 8, 8, QB), axis=0)
            acc = lax.fori_loop(0, n_chunks, body, jnp.zeros((8, QB), I32))
            return jnp.sum(acc, axis=0, keepdims=True)

        def bit_step(it, thr):
            cand = thr ^ lax.shift_left(jnp.int32(1), 31 - it)
            cnt = count(lambda kc, c: kc >= cand)
            return jnp.where(cnt >= topk, cand, thr)
        thr = lax.fori_loop(0, 32, bit_step, jnp.full((1, QB), INT_MIN, I32))
        want = topk - count(lambda kc, c: kc > thr)
        n_bits = seq.bit_length()

        def tie_step(it, jb):
            cand = jb | lax.shift_left(jnp.int32(1), n_bits - 1 - it)
            def pred(kc, c):
                kpos = c * KC + lax.broadcasted_iota(I32, (KC, QB), 0)
                return jnp.logical_and(kc == thr, kpos < cand)
            return jnp.where(count(pred) <= want, cand, jb)
        jb = lax.fori_loop(0, n_bits, tie_step, jnp.zeros((1, QB), I32))

        def pack_tile(t, _):
            kc = keys_ref[t]
            kpos = t * KT + lax.broadcasted_iota(I32, (KT, QB), 0)
            chosen = jnp.logical_or(kc > thr, jnp.logical_and(kc == thr, kpos < jb))
            word = jnp.zeros((rows_per_word, QB), I32)
            for r in range(32):
                bit = np.int32(np.uint32(1 << r))
                word = word | jnp.where(chosen[r * rows_per_word:(r + 1) * rows_per_word, :], bit, 0)
            sel_ref[t] = jnp.where(select_all, -1, word)
            return 0
        lax.fori_loop(0, n_tiles, pack_tile, 0)

    m_ref[...] = jnp.full(m_ref.shape, NEG_BIG, F32)
    l_ref[...] = jnp.zeros(l_ref.shape, F32)
    acc_ref[...] = jnp.zeros(acc_ref.shape, F32)

    def masked_scores(t, dst_ref, dmax_ref):
        word = sel_ref[t]
        picked = jnp.concatenate([lax.shift_left(word, jnp.int32(31 - r)) for r in range(32)], axis=0)
        kpos = t * KT + lax.broadcasted_iota(I32, (KT, QB), 0)
        sel = jnp.logical_and(picked < 0, kpos <= qpos)
        bias = jnp.where(sel, 0.0, NEG_BIG)
        bias2 = jnp.concatenate([bias, bias], axis=1)
        kc = kb_ref[pl.ds(pl.multiple_of(t * KT, KT), KT), :]
        for p in range(ATTN_HEADS // 2):
            s = _dot_nt(kc, qs_ref[2 * p:2 * p + 2].reshape(2 * QB, LANES)) + bias2
            dst_ref[p] = s
            dmax_ref[p] = jnp.max(s, axis=0, keepdims=True)

    def softmax_pv(t, src_ref, smax_ref):
        for p in range(ATTN_HEADS // 2):
            g = (2 * p) // heads_per_group
            m_old = m_ref[p]
            m_new = jnp.maximum(m_old, smax_ref[p])
            alpha = jnp.exp2(m_old - m_new)
            pt = jnp.exp2(src_ref[p] - m_new).astype(BF16)
            m_ref[p] = m_new
            pv = _dot(vt_ref[t, g], pt)
            l_ref[p] = alpha * l_ref[p] + pv[hd:hd + 1, :]
            acc_ref[p] = alpha * acc_ref[p] + pv[0:hd, :]

    masked_scores(0, s0_ref, x0_ref)

    def attend(u, _):
        t = 2 * u
        masked_scores(t + 1, s1_ref, x1_ref)
        softmax_pv(t, s0_ref, x0_ref)
        masked_scores(jnp.minimum(t + 2, n_tiles - 1), s0_ref, x0_ref)
        softmax_pv(t + 1, s1_ref, x1_ref)
        return 0
    lax.fori_loop(0, n_tiles // 2, attend, 0)

    for p in range(ATTN_HEADS // 2):
        o = acc_ref[p] / l_ref[p]
        both = jnp.concatenate([o[:, :QB], o[:, QB:]], axis=0)
        o_ref[:, p * LANES:(p + 1) * LANES] = both.T.astype(o_ref.dtype)


def _dsa_branch(main3d, misc3d):
    bsz, seq, _ = main3d.shape
    topk = min(TOPK_MAX, seq // 4)
    width = ATTN_HEADS * ATTN_HEAD_DIM
    nkt = seq // KEY_TILE
    n_pairs = ATTN_HEADS // 2
    kern = functools.partial(_dsa_kernel, topk=topk)
    return pl.pallas_call(
        kern,
        grid=(bsz, seq // Q_BLOCK),
        in_specs=[
            pl.BlockSpec((None, Q_BLOCK, width), lambda b, i: (b, i, Q_TILE)),
            pl.BlockSpec((None, Q_BLOCK, IDX_HEADS * IDX_DIM), lambda b, i: (b, i, COL_QI)),
            pl.BlockSpec((None, seq, LANES), lambda b, i: (b, 0, COL_K)),
            pl.BlockSpec((None, seq, LANES), lambda b, i: (b, 0, COL_V)),
            pl.BlockSpec((None, seq, LANES), lambda b, i: (b, 0, COL_KI)),
            pl.BlockSpec((None, Q_BLOCK, LANES), lambda b, i: (b, i, COL_DTWI)),
        ],
        out_specs=pl.BlockSpec((None, Q_BLOCK, width), lambda b, i: (b, i, 0)),
        out_shape=jax.ShapeDtypeStruct((bsz, seq, width), BF16),
        scratch_shapes=[
            pltpu.VMEM((seq, LANES), BF16),
            pltpu.VMEM((seq, LANES), BF16),
            pltpu.VMEM((nkt, ATTN_KV_HEADS, ATTN_HEAD_DIM + ONES_ROWS, KEY_TILE), BF16),
            pltpu.VMEM((ATTN_HEADS, Q_BLOCK, LANES), BF16),
            pltpu.VMEM((IDX_HEADS, Q_BLOCK, LANES), BF16),
            pltpu.VMEM((nkt, KEY_TILE, Q_BLOCK), I32),
            pltpu.VMEM((nkt, 32, KEY_TILE // 32, Q_BLOCK), I32),
            pltpu.VMEM((nkt, KEY_TILE // 32, Q_BLOCK), I32),
            pltpu.VMEM((nkt, KEY_TILE // 32, Q_BLOCK), I32),
            pltpu.VMEM((nkt, KEY_TILE // 32, Q_BLOCK), I32),
            pltpu.VMEM((n_pairs, KEY_TILE, 2 * Q_BLOCK), F32),
            pltpu.VMEM((n_pairs, KEY_TILE, 2 * Q_BLOCK), F32),
            pltpu.VMEM((n_pairs, 1, 2 * Q_BLOCK), F32),
            pltpu.VMEM((n_pairs, 1, 2 * Q_BLOCK), F32),
            pltpu.VMEM((n_pairs, 1, 2 * Q_BLOCK), F32),
            pltpu.VMEM((n_pairs, 1, 2 * Q_BLOCK), F32),
            pltpu.VMEM((n_pairs, ATTN_HEAD_DIM, 2 * Q_BLOCK), F32),
        ],
        compiler_params=_cparams(("parallel", "arbitrary")),
        name="dsa_attention",
    )(main3d, misc3d, misc3d, misc3d, misc3d, misc3d)


def _merge_kernel(h_ref, ys_ref, ya_ref, gs_ref, ga_ref, wso_ref, wao_ref, wo_ref, o_ref):
    a = _dot(ys_ref[...], wso_ref[...])
    b = _dot(ya_ref[...], wao_ref[...])
    merged = jax.nn.sigmoid(gs_ref[...].astype(F32)) * a + jax.nn.sigmoid(ga_ref[...].astype(F32)) * b
    o_ref[...] = h_ref[...] + _dot(merged.astype(BF16), wo_ref[...])


def _merge(h2d, y_ssd, y_attn, proj2d, wso, wao, wo):
    n, d = h2d.shape
    tm = 512
    full = lambda a: pl.BlockSpec(a.shape, lambda i: (0, 0))
    return pl.pallas_call(
        _merge_kernel,
        grid=(n // tm,),
        in_specs=[
            pl.BlockSpec((tm, d), lambda i: (i, 0)),
            pl.BlockSpec((tm, y_ssd.shape[1]), lambda i: (i, 0)),
            pl.BlockSpec((tm, y_attn.shape[1]), lambda i: (i, 0)),
            pl.BlockSpec((tm, d), lambda i: (i, 6)),
            pl.BlockSpec((tm, d), lambda i: (i, 7)),
            full(wso), full(wao), full(wo),
        ],
        out_specs=pl.BlockSpec((tm, d), lambda i: (i, 0)),
        out_shape=jax.ShapeDtypeStruct((n, d), F32),
        compiler_params=_cparams(("parallel",)),
        name="merge_out",
    )(h2d, y_ssd, y_attn, proj2d, proj2d, wso, wao, wo)


def _normproj_kernel(x_ref, g_ref, w_ref, o_ref):
    o_ref[...] = _dot(_rms(x_ref[...], g_ref[...]).astype(BF16), w_ref[...]).astype(o_ref.dtype)


def _norm_project(x2d, gain, w_bf16, out_dtype):
    n, d = x2d.shape
    tm = min(512, n)
    return pl.pallas_call(
        _normproj_kernel,
        grid=(n // tm,),
        in_specs=[pl.BlockSpec((tm, d), lambda i: (i, 0)), pl.BlockSpec((1, d), lambda i: (0, 0)),
                  pl.BlockSpec(w_bf16.shape, lambda i: (0, 0))],
        out_specs=pl.BlockSpec((tm, w_bf16.shape[1]), lambda i: (i, 0)),
        out_shape=jax.ShapeDtypeStruct((n, w_bf16.shape[1]), out_dtype),
        compiler_params=_cparams(("parallel",)),
        name="mem_kv_projection",
    )(x2d, gain.reshape(1, d), w_bf16)


def _xattn_kernel(h_ref, g_ref, kv_ref, wq_ref, wo_ref, o_ref):
    d = h_ref.shape[1]
    hd = d // XATTN_HEADS
    h = h_ref[...]
    q = _dot(_rms(h, g_ref[...]).astype(BF16), wq_ref[...]) * (hd ** -0.5)
    qb = q.astype(BF16)
    outs = []
    for a in range(XATTN_HEADS):
        k = kv_ref[:, a * hd:(a + 1) * hd]
        v = kv_ref[:, d + a * hd:d + (a + 1) * hd]
        s = _dot_nt(qb[:, a * hd:(a + 1) * hd], k)
        p = jnp.exp(s - jnp.max(s, axis=1, keepdims=True))
        o = _dot(p.astype(BF16), v) / jnp.sum(p, axis=1, keepdims=True)
        outs.append(o.astype(BF16))
    o_ref[...] = h + _dot(jnp.concatenate(outs, axis=1), wo_ref[...])


def _cross_attention(h3d, gain, kv3d, wq, wo):
    bsz, seq, d = h3d.shape
    tm = min(512, seq)
    mlen = kv3d.shape[1]
    return pl.pallas_call(
        _xattn_kernel,
        grid=(bsz, seq // tm),
        in_specs=[
            pl.BlockSpec((None, tm, d), lambda b, i: (b, i, 0)),
            pl.BlockSpec((1, d), lambda b, i: (0, 0)),
            pl.BlockSpec((None, mlen, 2 * d), lambda b, i: (b, 0, 0)),
            pl.BlockSpec(wq.shape, lambda b, i: (0, 0)),
            pl.BlockSpec(wo.shape, lambda b, i: (0, 0)),
        ],
        out_specs=pl.BlockSpec((None, tm, d), lambda b, i: (b, i, 0)),
        out_shape=jax.ShapeDtypeStruct((bsz, seq, d), F32),
        compiler_params=_cparams(("parallel", "parallel")),
        name="cross_attention",
    )(h3d, gain.reshape(1, d), kv3d, wq, wo)


def _router_kernel(h_ref, g_ref, wr_ref, br_ref, trit_ref, aug_ref, rank_ref, gsel_ref, cnt_ref, base_ref):
    i = pl.program_id(0)
    d = h_ref.shape[1]

    @pl.when(i == 0)
    def _():
        base_ref[...] = jnp.zeros(base_ref.shape, F32)

    hn = _rms(h_ref[...], g_ref[...])
    aug_ref[:, 0:d] = hn
    h_hi, h_mid, h_lo = _split3(hn)
    w_hi, w_mid, w_lo = wr_ref[0], wr_ref[1], wr_ref[2]
    logits = (_dot(h_hi, w_hi) + (_dot(h_hi, w_mid) + _dot(h_mid, w_hi))
              + (_dot(h_hi, w_lo) + _dot(h_mid, w_mid) + _dot(h_lo, w_hi))) + br_ref[...]
    tm = logits.shape[0]
    lane = lax.broadcasted_iota(I32, (tm, LANES), 1)
    neg_inf = -jnp.inf
    n_e = MOE_GROUPS * MOE_EXPERTS_PER_GROUP
    is_g = jnp.logical_and(lane >= n_e, lane < n_e + MOE_GROUPS)
    gl = jnp.where(is_g, logits, neg_inf)
    g_max = jnp.max(gl, axis=1, keepdims=True)
    g_sel = jnp.min(jnp.where(gl == g_max, lane, LANES), axis=1, keepdims=True) - n_e
    g_w = 1.0 / jnp.sum(jnp.where(is_g, jnp.exp(gl - g_max), 0.0), axis=1, keepdims=True)
    in_grp = jnp.logical_and(lane >= g_sel * MOE_EXPERTS_PER_GROUP,
                             lane < (g_sel + 1) * MOE_EXPERTS_PER_GROUP)
    el = jnp.where(in_grp, logits, neg_inf)
    v1 = jnp.max(el, axis=1, keepdims=True)
    i1 = jnp.min(jnp.where(el == v1, lane, LANES), axis=1, keepdims=True)
    el2 = jnp.where(lane == i1, neg_inf, el)
    v2 = jnp.max(el2, axis=1, keepdims=True)
    i2 = jnp.min(jnp.where(el2 == v2, lane, LANES), axis=1, keepdims=True)
    e2 = jnp.exp(v2 - v1)
    w1 = g_w / (1.0 + e2)
    w2 = g_w * e2 / (1.0 + e2)
    aug_ref[:, d:] = jnp.where(lane == i1, w1, 0.0) + jnp.where(lane == i2, w2, 0.0)

    onehot_t = jnp.where(lane == g_sel, 1.0, 0.0).T
    upto = _dot(onehot_t.astype(BF16), trit_ref[...]) + base_ref[...]
    grp = lax.broadcasted_iota(I32, onehot_t.shape, 0).astype(F32)
    rank_ref[...] = (jnp.sum(onehot_t * upto, axis=0, keepdims=True) - 1.0).astype(I32)
    gsel_ref[...] = jnp.sum(onehot_t * grp, axis=0, keepdims=True).astype(I32)
    base_ref[...] = upto[:, tm - 1:tm]
    cnt_ref[...] = jnp.broadcast_to(upto[:, tm - 1:tm], cnt_ref.shape)


def _moe_route(h2d, gain, wr3, br):
    n, d = h2d.shape
    tm = 512
    nt = n // tm
    trit = (jnp.arange(tm)[:, None] <= jnp.arange(tm)[None, :]).astype(BF16)
    return pl.pallas_call(
        _router_kernel,
        grid=(nt,),
        in_specs=[pl.BlockSpec((tm, d), lambda i: (i, 0)), pl.BlockSpec((1, d), lambda i: (0, 0)),
                  pl.BlockSpec(wr3.shape, lambda i: (0, 0, 0)), pl.BlockSpec((1, LANES), lambda i: (0, 0)),
                  pl.BlockSpec((tm, tm), lambda i: (0, 0))],
        out_specs=[pl.BlockSpec((tm, d + LANES), lambda i: (i, 0)),
                   pl.BlockSpec((None, 1, tm), lambda i: (i, 0, 0)),
                   pl.BlockSpec((None, 1, tm), lambda i: (i, 0, 0)),
                   pl.BlockSpec((LANES, LANES), lambda i: (0, 0))],
        out_shape=[jax.ShapeDtypeStruct((n, d + LANES), F32),
                   jax.ShapeDtypeStruct((nt, 1, tm), I32),
                   jax.ShapeDtypeStruct((nt, 1, tm), I32),
                   jax.ShapeDtypeStruct((LANES, LANES), F32)],
        scratch_shapes=[pltpu.VMEM((LANES, 1), F32)],
        compiler_params=_cparams(("arbitrary",)),
        name="moe_router",
    )(h2d, gain.reshape(1, d), wr3, br, trit)


MOE_TILE = 512
ROW_BLOCK = 256
ISSUE_UNROLL = 8


def _scatter_rows_kernel(dest_ref, src_ref, init_ref, out_ref, sem):
    del init_ref
    base = pl.program_id(0) * ROW_BLOCK

    def row_copy(r):
        return pltpu.make_async_copy(src_ref.at[pl.ds(r, 1)], out_ref.at[pl.ds(dest_ref[base + r], 1)], sem)

    def issue(r, _):
        row_copy(r).start()
        return 0
    lax.fori_loop(0, ROW_BLOCK, issue, 0, unroll=ISSUE_UNROLL)
    pltpu.make_async_copy(src_ref, out_ref.at[pl.ds(0, ROW_BLOCK)], sem).wait()


def _scatter_rows(src, dest, n_out):
    n, w = src.shape
    init = jnp.zeros((n_out, w), src.dtype)
    return pl.pallas_call(
        _scatter_rows_kernel,
        grid_spec=pltpu.PrefetchScalarGridSpec(
            num_scalar_prefetch=1,
            grid=(n // ROW_BLOCK,),
            in_specs=[pl.BlockSpec((ROW_BLOCK, w), lambda i, dest: (i, 0)),
                      pl.BlockSpec(memory_space=pl.ANY)],
            out_specs=pl.BlockSpec(memory_space=pl.ANY),
            scratch_shapes=[pltpu.SemaphoreType.DMA(())],
        ),
        out_shape=jax.ShapeDtypeStruct((n_out, w), src.dtype),
        input_output_aliases={2: 0},
        compiler_params=_cparams(("arbitrary",)),
        name="moe_sort_rows",
    )(dest, src, init)


def _sorted_experts_kernel(tile_group_ref, n_used_ref, hn_ref, cw_ref, wg_ref, wu_ref, wd_ref, o_ref):
    j = pl.program_id(0)

    @pl.when(j < n_used_ref[0])
    def _():
        hn = hn_ref[...].astype(BF16)
        first = tile_group_ref[j] * MOE_EXPERTS_PER_GROUP
        lane = lax.broadcasted_iota(I32, cw_ref.shape, 1)
        cw = cw_ref[...]
        acc = jnp.zeros(o_ref.shape, F32)
        for e in range(MOE_EXPERTS_PER_GROUP):
            w_e = jnp.sum(jnp.where(lane == first + e, cw, 0.0), axis=1, keepdims=True)
            hid = _silu(_dot(hn, wg_ref[e])) * _dot(hn, wu_ref[e]) * w_e
            acc = acc + _dot(hid.astype(BF16), wd_ref[e])
        o_ref[...] = acc

    @pl.when(j >= n_used_ref[0])
    def _():
        o_ref[...] = jnp.zeros(o_ref.shape, F32)


def _sorted_experts(aug_sorted, tile_group, n_used, wg, wu, wd):
    n_pad, wa = aug_sorted.shape
    d = wa - LANES
    _, n_e, _, dff = wg.shape
    tm = MOE_TILE
    return pl.pallas_call(
        _sorted_experts_kernel,
        grid_spec=pltpu.PrefetchScalarGridSpec(
            num_scalar_prefetch=2,
            grid=(n_pad // tm,),
            in_specs=[
                pl.BlockSpec((tm, d), lambda j, tg, nu: (j, 0)),
                pl.BlockSpec((tm, LANES), lambda j, tg, nu: (j, d // LANES)),
                pl.BlockSpec((None, n_e, d, dff), lambda j, tg, nu: (tg[j], 0, 0, 0)),
                pl.BlockSpec((None, n_e, d, dff), lambda j, tg, nu: (tg[j], 0, 0, 0)),
                pl.BlockSpec((None, n_e, dff, d), lambda j, tg, nu: (tg[j], 0, 0, 0)),
            ],
            out_specs=pl.BlockSpec((tm, d), lambda j, tg, nu: (j, 0)),
        ),
        out_shape=jax.ShapeDtypeStruct((n_pad, d), F32),
        compiler_params=_cparams(("arbitrary",)),
        name="moe_experts",
    )(tile_group, n_used, aug_sorted, aug_sorted, wg, wu, wd)


def _gather_add_kernel(dest_ref, h_ref, delta_ref, fg_ref, o_ref, buf_ref, sem, *, final_norm):
    i = pl.program_id(0)
    n_steps = pl.num_programs(0)

    def row_copy(step, r):
        slot = lax.rem(step, 2)
        return pltpu.make_async_copy(delta_ref.at[pl.ds(dest_ref[step * ROW_BLOCK + r], 1)],
                                     buf_ref.at[slot, pl.ds(r, 1)], sem.at[slot])

    def issue_all(step):
        def issue(r, _):
            row_copy(step, r).start()
            return 0
        lax.fori_loop(0, ROW_BLOCK, issue, 0, unroll=ISSUE_UNROLL)

    @pl.when(i == 0)
    def _():
        issue_all(0)

    @pl.when(i + 1 < n_steps)
    def _():
        issue_all(i + 1)

    slot = lax.rem(i, 2)
    pltpu.make_async_copy(delta_ref.at[pl.ds(0, ROW_BLOCK)], buf_ref.at[slot], sem.at[slot]).wait()

    out = h_ref[...] + buf_ref[slot]
    if final_norm:
        out = _rms(out, fg_ref[...])
    o_ref[...] = out


def _gather_add(h2d, delta_sorted, dest, final_gain, final_norm):
    n, d = h2d.shape
    kern = functools.partial(_gather_add_kernel, final_norm=final_norm)
    return pl.pallas_call(
        kern,
        grid_spec=pltpu.PrefetchScalarGridSpec(
            num_scalar_prefetch=1,
            grid=(n // ROW_BLOCK,),
            in_specs=[pl.BlockSpec((ROW_BLOCK, d), lambda i, dest: (i, 0)),
                      pl.BlockSpec(memory_space=pl.ANY),
                      pl.BlockSpec((1, d), lambda i, dest: (0, 0))],
            out_specs=pl.BlockSpec((ROW_BLOCK, d), lambda i, dest: (i, 0)),
            scratch_shapes=[pltpu.VMEM((2, ROW_BLOCK, d), F32), pltpu.SemaphoreType.DMA((2,))],
        ),
        out_shape=jax.ShapeDtypeStruct((n, d), F32),
        compiler_params=_cparams(("arbitrary",)),
        name="moe_unsort_add",
    )(dest, h2d, delta_sorted, final_gain.reshape(1, d))


def _moe_layout(rank, gsel, counts, n):
    tm = MOE_TILE
    cnt = counts[:MOE_GROUPS, 0].astype(I32)
    seg = ((cnt + tm - 1) // tm) * tm
    ends = jnp.cumsum(seg)
    dest = (ends - seg)[gsel.reshape(n)] + rank.reshape(n)
    n_tiles = n // tm + MOE_GROUPS
    starts = jnp.arange(n_tiles, dtype=I32) * tm
    tile_group = jnp.minimum(jnp.sum(starts[:, None] >= ends[None, :], axis=1), MOE_GROUPS - 1).astype(I32)
    n_used = (ends[-1:] // tm).astype(I32)
    return dest.astype(I32), tile_group, n_used, n_tiles * tm


def _prep_router(w_group, b_group, w_router, b_router):
    w = jnp.concatenate([w_router, w_group], axis=1).astype(F32)
    w = jnp.pad(w, ((0, 0), (0, LANES - w.shape[1])))
    hi = w.astype(BF16)
    r1 = w - hi.astype(F32)
    mid = r1.astype(BF16)
    lo = (r1 - mid.astype(F32)).astype(BF16)
    b = jnp.concatenate([b_router, b_group]).astype(F32)
    b = jnp.pad(b, (0, LANES - b.shape[0])).reshape(1, LANES)
    return jnp.stack([hi, mid, lo]), b


def kernel(x, mem, norm_mix, w_in, conv_w, conv_b, dt_bias, a_log, d_skip, ssd_norm, w_ssd_o, w_attn_o, w_out,
           norm_xattn, norm_mem, w_cq, w_ckv, w_co, norm_ffn, w_group, b_group, w_router, b_router,
           w_gate_e, w_up_e, w_down_e, norm_final):
    bsz, seq, d = x.shape
    depth = w_in.shape[0]
    mlen = mem.shape[1]
    heads = dt_bias.shape[1]
    d_inner = heads * SSD_HEAD_DIM
    conv_ch = conv_w.shape[2]
    n = bsz * seq
    rope_c, rope_s = _rope_tables(seq)
    h = x.reshape(n, d)
    mem2d = mem.reshape(bsz * mlen, d)
    for l in range(depth):
        w_in_l = _prep_w_in(w_in[l], d_inner, conv_ch, heads)
        main, misc = _in_projection(h, norm_mix[l], w_in_l, rope_c, rope_s, seq)
        main3d = main.reshape(bsz, seq, main.shape[1])
        misc3d = misc.reshape(bsz, seq, misc.shape[1])
        y_ssd = _ssd_branch(main3d, misc3d, conv_w[l], conv_b[l], dt_bias[l], a_log[l], d_skip[l], ssd_norm[l])
        y_attn = _dsa_branch(main3d, misc3d)
        h = _merge(h, y_ssd.reshape(n, d_inner), y_attn.reshape(n, -1), main,
                   w_ssd_o[l].astype(BF16), w_attn_o[l].astype(BF16), w_out[l].astype(BF16))
        kv = _norm_project(mem2d, norm_mem[l], w_ckv[l].astype(BF16), BF16)
        h = _cross_attention(h.reshape(bsz, seq, d), norm_xattn[l], kv.reshape(bsz, mlen, 2 * d),
                             w_cq[l].astype(BF16), w_co[l].astype(BF16)).reshape(n, d)
        wr3, br = _prep_router(w_group[l], b_group[l], w_router[l], b_router[l])
        aug, rank, gsel, counts = _moe_route(h, norm_ffn[l], wr3, br)
        dest, tile_group, n_used, n_pad = _moe_layout(rank, gsel, counts, n)
        aug_sorted = _scatter_rows(aug, dest, n_pad)
        delta = _sorted_experts(aug_sorted, tile_group, n_used, w_gate_e[l].astype(BF16),
                                w_up_e[l].astype(BF16), w_down_e[l].astype(BF16))
        h = _gather_add(h, delta, dest, norm_final, final_norm=(l == depth - 1))
    return h.reshape(bsz, seq, d)
```

```python
import functools
import math

import jax
import jax.numpy as jnp
import numpy as np
from jax import lax
from jax.experimental import pallas as pl
from jax.experimental.pallas import tpu as pltpu

F32 = jnp.float32
BF16 = jnp.bfloat16
I32 = jnp.int32

SSD_HEAD_DIM = 64
SSD_GROUPS = 4
SSD_STATE = 128
SSD_CONV = 4
SSD_CHUNK = 128
SSD_SEQS_PER_STEP = 2
ATTN_HEADS = 16
ATTN_KV_HEADS = 2
ATTN_HEAD_DIM = 64
Q_BLOCK = 128
TOPK_MAX = 256
IDX_HEADS = 8
IDX_DIM = 64
ROPE_THETA = 10000.0
XATTN_HEADS = 4
MOE_GROUPS = 4
MOE_EXPERTS_PER_GROUP = 8
RMS_EPS = 1e-6

LANES = 128
SUBLANES = 8
KEY_TILE = 256
INT_MIN = -2 ** 31
NEG_BIG = -1e30
LOG2E = math.log2(math.e)
ONES_ROWS = 16
VMEM_LIMIT = 56 * 1024 * 1024


def _cparams(sem):
    return pltpu.CompilerParams(dimension_semantics=sem, vmem_limit_bytes=VMEM_LIMIT)


def _split3(v):
    hi = v.astype(BF16)
    r1 = v - hi.astype(F32)
    mid = r1.astype(BF16)
    lo = (r1 - mid.astype(F32)).astype(BF16)
    return hi, mid, lo


def _dot(a, b):
    return jnp.dot(a, b, preferred_element_type=F32)


def _dot_nt(a, b):
    return lax.dot_general(a, b, (((1,), (1,)), ((), ())), preferred_element_type=F32)


def _spread(v, m01, terms):
    parts = _split3(v)[:terms]
    out = _dot(parts[0], m01)
    for part in parts[1:]:
        out = out + _dot(part, m01)
    return out


def _dot3_lhs01(m01, v):
    hi, mid, lo = _split3(v)
    return _dot(m01, hi) + _dot(m01, mid) + _dot(m01, lo)


def _rms(x, g):
    ms = jnp.mean(x * x, axis=-1, keepdims=True)
    return x * lax.rsqrt(ms + RMS_EPS) * g


def _silu(x):
    return x * jax.nn.sigmoid(x)


Q_TILE = 5
MISC_TILE = 8
MISC_ROPE_CHUNKS = (0, 1, 2, 3, 4, 6)
COL_QI = 0
COL_K, COL_V, COL_KI, COL_DTWI = 4, 5, 6, 7
DT_LANES = 32


def _rope_chunk(a, c, s):
    return a * c + pltpu.roll(a, LANES // 2, 1) * s


def _inproj_kernel(x_ref, g_ref, w_ref, c_ref, s_ref, o_ref, misc_ref, xn_ref):
    j = pl.program_id(1)

    @pl.when(j == 0)
    def _():
        xn_ref[...] = _rms(x_ref[...], g_ref[...]).astype(BF16)

    acc = _dot(xn_ref[...], w_ref[...])
    n_chunks = acc.shape[1] // LANES

    def roped(dst_ref, chunks):
        c, sn = c_ref[...], s_ref[...]
        for k in range(n_chunks):
            a = acc[:, k * LANES:(k + 1) * LANES]
            if k in chunks:
                a = _rope_chunk(a, c, sn)
            dst_ref[:, k * LANES:(k + 1) * LANES] = a.astype(dst_ref.dtype)

    @pl.when(j == Q_TILE)
    def _():
        roped(o_ref, tuple(range(n_chunks)))

    @pl.when(j == MISC_TILE)
    def _():
        roped(misc_ref, MISC_ROPE_CHUNKS)

    @pl.when(jnp.logical_and(j != Q_TILE, j != MISC_TILE))
    def _():
        o_ref[...] = acc.astype(o_ref.dtype)


def _in_projection(x2d, gain, w_bf16, rope_c, rope_s, seq):
    n, d = x2d.shape
    tm = min(1024, seq)
    tn = 1024
    pos_tiles = seq // tm
    n_tiles = w_bf16.shape[1] // tn
    assert n_tiles == MISC_TILE + 1
    return pl.pallas_call(
        _inproj_kernel,
        grid=(n // tm, n_tiles),
        in_specs=[
            pl.BlockSpec((tm, d), lambda i, j: (i, 0)),
            pl.BlockSpec((1, d), lambda i, j: (0, 0)),
            pl.BlockSpec((d, tn), lambda i, j: (0, j)),
            pl.BlockSpec((tm, LANES), lambda i, j: (i % pos_tiles, 0)),
            pl.BlockSpec((tm, LANES), lambda i, j: (i % pos_tiles, 0)),
        ],
        out_specs=[pl.BlockSpec((tm, tn), lambda i, j: (i, jnp.minimum(j, MISC_TILE - 1))),
                   pl.BlockSpec((tm, tn), lambda i, j: (i, 0))],
        out_shape=[jax.ShapeDtypeStruct((n, MISC_TILE * tn), BF16), jax.ShapeDtypeStruct((n, tn), F32)],
        scratch_shapes=[pltpu.VMEM((tm, d), BF16)],
        compiler_params=_cparams(("parallel", "arbitrary")),
        name="in_projection",
    )(x2d, gain.reshape(1, d), w_bf16, rope_c, rope_s)


def _pair_halves(w):
    d = w.shape[0]
    half = ATTN_HEAD_DIM // 2
    return w.reshape(d, -1, 2, 2, half).transpose(0, 1, 3, 2, 4).reshape(d, -1)


def _prep_w_in(w, d_inner, conv_ch, ssd_heads):
    widths = (d_inner, conv_ch, ssd_heads, ATTN_HEADS * ATTN_HEAD_DIM, ATTN_KV_HEADS * ATTN_HEAD_DIM,
              ATTN_KV_HEADS * ATTN_HEAD_DIM, IDX_HEADS * IDX_DIM, IDX_DIM, IDX_HEADS,
              w.shape[0], w.shape[0])
    splits = [int(p) for p in np.cumsum(widths)[:-1]]
    z, xbc, dt, q, k, v, qi, ki, wi, gs, ga = jnp.split(w, splits, axis=1)

    def pad(a, n):
        return jnp.pad(a, ((0, 0), (0, n - a.shape[1])))

    q, k, qi, ki = _pair_halves(q), _pair_halves(k), _pair_halves(qi), _pair_halves(pad(ki, LANES))
    misc = jnp.concatenate([qi, k, v, ki, pad(jnp.concatenate([dt, wi], axis=1), LANES)], axis=1)
    return jnp.concatenate([z, xbc, q, gs, ga, misc], axis=1).astype(BF16)


def _rope_tables(seq):
    inv = 1.0 / (ROPE_THETA ** (jnp.arange(0, ATTN_HEAD_DIM, 2, dtype=F32) / ATTN_HEAD_DIM))
    ang = jnp.arange(seq, dtype=F32)[:, None] * inv[None, :]
    cos, sin = jnp.cos(ang), jnp.sin(ang)
    c = jnp.concatenate([cos, cos, cos, cos], axis=1)
    s = jnp.concatenate([-sin, -sin, sin, sin], axis=1)
    return c, s


def _ssd_chunk(z_ref, xs_ref, bc_ref, dt_ref, cw_ref, cb_ref, dtb_ref, alog_ref, dsk_ref, ng_ref,
               r64_ref, tri_ref, shift_ref, y_ref, prev_ref, st_ref, act_ref, heads, groups):
    L = SSD_CHUNK
    P = SSD_HEAD_DIM
    hpg = heads // groups
    d_inner = heads * P
    gw = hpg * P
    nst = SSD_STATE

    def conv(cur_ref, lo, hi):
        cur = cur_ref[...]
        both = jnp.concatenate([prev_ref[:, lo:hi], cur], axis=0)
        acc = cb_ref[:, lo:hi] + cw_ref[SSD_CONV - 1:SSD_CONV, lo:hi] * cur.astype(F32)
        for k in range(1, SSD_CONV):
            acc = acc + cw_ref[SSD_CONV - 1 - k:SSD_CONV - k, lo:hi] * _dot(shift_ref[k - 1], both)
        prev_ref[:, lo:hi] = cur
        return _silu(acc)

    xs = conv(xs_ref, 0, d_inner)
    bc = conv(bc_ref, d_inner, prev_ref.shape[1])
    bmat = bc[:, :groups * nst]
    cmat = bc[:, groups * nst:]

    dt = jax.nn.softplus(dt_ref[...] + dtb_ref[...])
    da = dt * (-jnp.exp(alog_ref[...]))
    tri = tri_ref[...]
    acum = _dot3_lhs01(tri, da)
    act_ref[...] = acum.T
    e_acum = jnp.exp(acum)
    e_tail = jnp.exp(acum[L - 1:L, :] - acum)
    r64 = r64_ref[...]
    e64 = _spread(e_acum, r64, 2)
    t64 = _spread(e_tail, r64, 1)
    d64 = _spread(dt, r64, 1)
    xdt = xs * d64
    xdt_b = xdt.astype(BF16)
    xtl_b = (xdt * t64).astype(BF16)

    row = lax.broadcasted_iota(I32, (L, L), 0)
    col = lax.broadcasted_iota(I32, (L, L), 1)
    causal = row >= col
    first_head = lax.broadcasted_iota(I32, (1, 2 * P), 1) < P

    for g in range(groups):
        bg = bmat[:, g * nst:(g + 1) * nst]
        cg_b = cmat[:, g * nst:(g + 1) * nst].astype(BF16)
        cb = _dot_nt(cg_b, bg.astype(BF16))
        sg = st_ref[g]
        y_off = _dot(cg_b, sg.astype(BF16))
        y_heads = []
        for hh in range(0, hpg, 2):
            pair = []
            for h in (g * hpg + hh, g * hpg + hh + 1):
                seg = jnp.broadcast_to(acum[:, h:h + 1], (L, LANES)) - act_ref[h:h + 1, :]
                pair.append((cb * jnp.where(causal, jnp.exp(seg), 0.0)).astype(BF16))
            slab = xdt_b[:, (g * hpg + hh) * P:(g * hpg + hh + 2) * P]
            stacked = jnp.concatenate([jnp.where(first_head, slab, 0), jnp.where(first_head, 0, slab)], axis=0)
            y_heads.append(_dot(jnp.concatenate(pair, axis=1), stacked))
        lo, hi = g * gw, (g + 1) * gw
        y_g = jnp.concatenate(y_heads, axis=1) + y_off * e64[:, lo:hi]
        st_ref[g] = sg * e64[L - 1:L, lo:hi] + _dot(bg.T.astype(BF16), xtl_b[:, lo:hi])
        y_g = (y_g + dsk_ref[:, lo:hi] * xs[:, lo:hi]) * _silu(z_ref[:, lo:hi].astype(F32))
        ms = jnp.mean(y_g * y_g, axis=-1, keepdims=True)
        y_ref[:, lo:hi] = (y_g * lax.rsqrt(ms + RMS_EPS) * ng_ref[:, lo:hi]).astype(y_ref.dtype)


def _ssd_kernel(z_ref, xs_ref, bc_ref, dt_ref, cw_ref, cb_ref, dtb_ref, alog_ref, dsk_ref, ng_ref,
                r64_ref, tri_ref, shift_ref, y_ref, prev_ref, st_ref, act_ref, *, heads, groups):
    @pl.when(pl.program_id(1) == 0)
    def _():
        prev_ref[...] = jnp.zeros(prev_ref.shape, BF16)
        st_ref[...] = jnp.zeros(st_ref.shape, F32)

    for nb in range(z_ref.shape[0]):
        _ssd_chunk(z_ref.at[nb], xs_ref.at[nb], bc_ref.at[nb], dt_ref.at[nb], cw_ref, cb_ref, dtb_ref,
                   alog_ref, dsk_ref, ng_ref, r64_ref, tri_ref, shift_ref, y_ref.at[nb],
                   prev_ref.at[nb], st_ref.at[nb], act_ref.at[nb], heads, groups)


def _ssd_branch(main3d, misc3d, conv_w, conv_b, dt_bias, a_log, d_skip, ssd_norm):
    bsz, seq, _ = main3d.shape
    heads = dt_bias.shape[0]
    d_inner = heads * SSD_HEAD_DIM
    groups = SSD_GROUPS
    conv_ch = conv_w.shape[1]
    bc_w = conv_ch - d_inner
    L = SSD_CHUNK
    nb = SSD_SEQS_PER_STEP if bsz % SSD_SEQS_PER_STEP == 0 else 1

    def pad_lanes(v):
        return jnp.pad(v.astype(F32), (0, LANES - v.shape[0])).reshape(1, LANES)

    hidx = jnp.arange(LANES)[:, None]
    r64 = (hidx == (jnp.arange(d_inner) // SSD_HEAD_DIM)[None, :]).astype(BF16)
    tri = (jnp.arange(L)[:, None] >= jnp.arange(L)[None, :]).astype(BF16)
    taps = jnp.arange(1, SSD_CONV)[:, None, None]
    shift = (jnp.arange(2 * L)[None, None, :] == (L + jnp.arange(L)[None, :, None] - taps)).astype(BF16)
    dsk = jnp.repeat(d_skip.astype(F32), SSD_HEAD_DIM).reshape(1, d_inner)
    const = lambda shape: pl.BlockSpec(shape, lambda b, c: (0,) * len(shape))
    kern = functools.partial(_ssd_kernel, heads=heads, groups=groups)
    return pl.pallas_call(
        kern,
        grid=(bsz // nb, seq // L),
        in_specs=[
            pl.BlockSpec((nb, L, d_inner), lambda b, c: (b, c, 0)),
            pl.BlockSpec((nb, L, d_inner), lambda b, c: (b, c, 1)),
            pl.BlockSpec((nb, L, bc_w), lambda b, c: (b, c, 2 * d_inner // bc_w)),
            pl.BlockSpec((nb, L, LANES), lambda b, c: (b, c, COL_DTWI)),
            const((SSD_CONV, conv_ch)), const((1, conv_ch)), const((1, LANES)), const((1, LANES)),
            const((1, d_inner)), const((1, d_inner)),
            const((LANES, d_inner)), const((L, L)),
            const((SSD_CONV - 1, L, 2 * L)),
        ],
        out_specs=pl.BlockSpec((nb, L, d_inner), lambda b, c: (b, c, 0)),
        out_shape=jax.ShapeDtypeStruct((bsz, seq, d_inner), BF16),
        scratch_shapes=[
            pltpu.VMEM((nb, L, conv_ch), BF16),
            pltpu.VMEM((nb, groups, SSD_STATE, d_inner // groups), F32),
            pltpu.VMEM((nb, LANES, L), F32),
        ],
        compiler_params=_cparams(("parallel", "arbitrary")),
        name="ssd_scan",
    )(main3d, main3d, main3d, misc3d, conv_w.astype(F32), conv_b.reshape(1, conv_ch).astype(F32),
      pad_lanes(dt_bias), pad_lanes(a_log), dsk, ssd_norm.reshape(1, d_inner).astype(F32), r64, tri, shift)


def _sortable_key(x):
    x = jnp.where(x == 0.0, 0.0, x)
    b = pltpu.bitcast(x, I32)
    return b ^ (lax.shift_right_arithmetic(b, 31) & 0x7FFFFFFF)


_BIT_MASKS = (0x0000FFFF, 0x00FF00FF, 0x0F0F0F0F, 0x33333333, 0x55555555)


def _bit_planes(rows):
    x = list(rows[::-1])
    j = 16
    for m in _BIT_MASKS:
        mask = np.int32(np.uint32(m))
        k = 0
        while k < 32:
            t = (x[k] ^ lax.shift_right_logical(x[k + j], jnp.int32(j))) & mask
            x[k] = x[k] ^ t
            x[k + j] = x[k + j] ^ lax.shift_left(t, jnp.int32(j))
            k = (k + j + 1) & ~j
        j >>= 1
    return x


def _dsa_kernel(q_ref, qi_ref, k_ref, v_ref, ki_ref, wi_ref, o_ref,
                kb_ref, kib_ref, vt_ref, qs_ref, qis_ref, keys_ref, planes_ref, sel_ref, cand_ref, above_ref,
                s0_ref, s1_ref, x0_ref, x1_ref, m_ref, l_ref, acc_ref, *, topk):
    i = pl.program_id(1)
    QB, KT = Q_BLOCK, KEY_TILE
    hd = ATTN_HEAD_DIM
    seq = k_ref.shape[0]
    n_tiles = (i * QB + QB + KT - 1) // KT
    n_pairs = n_tiles // 2
    odd_tile = n_tiles % 2 == 1
    lane = lax.broadcasted_iota(I32, (1, LANES), 1)
    quarter = hd // 2
    first = (lane // quarter) % 2 == 0

    @pl.when(i == 0)
    def _():
        def fill(t, _):
            rows = pl.ds(pl.multiple_of(t * KT, KT), KT)
            kb_ref[rows, :] = k_ref[rows, :].astype(BF16)
            kib_ref[rows, :] = ki_ref[rows, :].astype(BF16)
            vt = v_ref[rows, :].T.astype(BF16)
            for g in range(ATTN_KV_HEADS):
                vt_ref[t, g, 0:hd, :] = vt[g * hd:(g + 1) * hd, :]
                vt_ref[t, g, hd:hd + ONES_ROWS, :] = jnp.ones((ONES_ROWS, KT), BF16)
            return 0
        lax.fori_loop(0, seq // KT, fill, 0)
        planes_ref[...] = jnp.zeros(planes_ref.shape, I32)

    qpos = i * QB + lax.broadcasted_iota(I32, (1, QB), 1)
    idx_scale = (IDX_DIM ** -0.5) * (IDX_HEADS ** -0.5)
    wt = (wi_ref[...] * idx_scale).T
    w_rows = [wt[DT_LANES + h:DT_LANES + h + 1, :] for h in range(IDX_HEADS)]

    heads_per_group = ATTN_HEADS // ATTN_KV_HEADS
    for p in range(ATTN_HEADS // 2):
        slab = q_ref[:, p * LANES:(p + 1) * LANES].astype(F32) * ((hd ** -0.5) * LOG2E)
        if (2 * p) // heads_per_group == 0:
            even = jnp.where(first, slab, 0.0)
            odd = jnp.where(first, pltpu.roll(slab, LANES - quarter, 1), 0.0)
        else:
            even = jnp.where(first, 0.0, pltpu.roll(slab, quarter, 1))
            odd = jnp.where(first, 0.0, slab)
        qs_ref[2 * p] = even.astype(BF16)
        qs_ref[2 * p + 1] = odd.astype(BF16)
    for p in range(IDX_HEADS // 2):
        slab = qi_ref[:, p * LANES:(p + 1) * LANES]
        qis_ref[2 * p] = slab.astype(BF16)
        qis_ref[2 * p + 1] = pltpu.roll(slab, LANES - quarter, 1).astype(BF16)

    nkt = seq // KT
    rows_per_word = KT // 32
    assert rows_per_word == SUBLANES

    def score_tile(t):
        kic = kib_ref[pl.ds(pl.multiple_of(t * KT, KT), KT), :]
        acc = jnp.zeros((KT, QB), F32)
        for p in range(IDX_HEADS // 2):
            rel = _dot_nt(kic, qis_ref[2 * p:2 * p + 2].reshape(2 * QB, LANES))
            acc = acc + w_rows[2 * p] * jnp.maximum(rel[:, :QB], 0.0)
            acc = acc + w_rows[2 * p + 1] * jnp.maximum(rel[:, QB:], 0.0)
        kpos = t * KT + lax.broadcasted_iota(I32, (KT, QB), 0)
        keys = jnp.where(kpos <= qpos, _sortable_key(acc), INT_MIN)
        keys_ref[t] = keys
        unsigned = keys ^ INT_MIN
        planes = _bit_planes([unsigned[r * rows_per_word:(r + 1) * rows_per_word, :] for r in range(32)])
        for bit in range(32):
            planes_ref[t, bit] = planes[bit]

    def score_pair(u, _):
        score_tile(2 * u)
        score_tile(2 * u + 1)
        return 0
    lax.fori_loop(0, n_pairs, score_pair, 0)

    @pl.when(odd_tile)
    def _():
        score_tile(n_tiles - 1)

    select_all = qpos < topk
    for t in range(nkt):
        cand_ref[t] = jnp.where(t < n_tiles, jnp.full((rows_per_word, QB), -1, I32), 0)
        above_ref[t] = jnp.zeros((rows_per_word, QB), I32)

    def radix_step(it, need):
        cnt = jnp.zeros((rows_per_word, QB), I32)
        for t in range(nkt):
            cnt = cnt + lax.population_count(cand_ref[t] & planes_ref[t, it])
        cnt = jnp.sum(cnt, axis=0, keepdims=True)
        take = cnt >= need
        for t in range(nkt):
            cand = cand_ref[t]
            ones = cand & planes_ref[t, it]
            cand_ref[t] = jnp.where(take, ones, cand ^ ones)
            above_ref[t] = jnp.where(take, above_ref[t], above_ref[t] | ones)
        return jnp.where(take, need, need - cnt)
    need = lax.fori_loop(0, 32, radix_step, jnp.full((1, QB), topk, I32))

    ties = jnp.zeros((rows_per_word, QB), I32)
    for t in range(nkt):
        ties = ties + lax.population_count(cand_ref[t])
        sel_ref[t] = jnp.where(select_all, -1, above_ref[t] | cand_ref[t])
    surplus = jnp.where(select_all, 0, jnp.sum(ties, axis=0, keepdims=True) - need)

    @pl.when(jnp.max(surplus) > 0)
    def _():
        def count(pred):
            def body(t, acc):
                hit = jnp.where(pred(keys_ref[t], t), 1, 0)
                return acc + jnp.sum(hit.reshape(KT // SUBLANES, SUBLANES, QB), axis=0)
            acc = lax.fori_loop(0, n_tiles, body, jnp.zeros((SUBLANES, QB), I32))
            return jnp.sum(acc, axis=0, keepdims=True)

        def bit_step(it, thr):
            cand = thr ^ lax.shift_left(jnp.int32(1), 31 - it)
            cnt = count(lambda kc, c: kc >= cand)
            return jnp.where(cnt >= topk, cand, thr)
        thr = lax.fori_loop(0, 32, bit_step, jnp.full((1, QB), INT_MIN, I32))
        want = topk - count(lambda kc, c: kc > thr)
        n_bits = seq.bit_length()

        def tie_step(it, jb):
            cand = jb | lax.shift_left(jnp.int32(1), n_bits - 1 - it)
            def pred(kc, c):
                kpos = c * KT + lax.broadcasted_iota(I32, (KT, QB), 0)
                return jnp.logical_and(kc == thr, kpos < cand)
            return jnp.where(count(pred) <= want, cand, jb)
        jb = lax.fori_loop(0, n_bits, tie_step, jnp.zeros((1, QB), I32))

        def pack_tile(t, _):
            kc = keys_ref[t]
            kpos = t * KT + lax.broadcasted_iota(I32, (KT, QB), 0)
            chosen = jnp.logical_or(kc > thr, jnp.logical_and(kc == thr, kpos < jb))
            word = jnp.zeros((rows_per_word, QB), I32)
            for r in range(32):
                bit = np.int32(np.uint32(1 << r))
                word = word | jnp.where(chosen[r * rows_per_word:(r + 1) * rows_per_word, :], bit, 0)
            sel_ref[t] = jnp.where(select_all, -1, word)
            return 0
        lax.fori_loop(0, n_tiles, pack_tile, 0)

    m_ref[...] = jnp.full(m_ref.shape, NEG_BIG, F32)
    l_ref[...] = jnp.zeros(l_ref.shape, F32)
    acc_ref[...] = jnp.zeros(acc_ref.shape, F32)

    def masked_scores(t, dst_ref, dmax_ref):
        word = sel_ref[t]
        picked = jnp.concatenate([lax.shift_left(word, jnp.int32(31 - r)) for r in range(32)], axis=0)
        kpos = t * KT + lax.broadcasted_iota(I32, (KT, QB), 0)
        sel = jnp.logical_and(picked < 0, kpos <= qpos)
        bias = jnp.where(sel, 0.0, NEG_BIG)
        bias2 = jnp.concatenate([bias, bias], axis=1)
        kc = kb_ref[pl.ds(pl.multiple_of(t * KT, KT), KT), :]
        for p in range(ATTN_HEADS // 2):
            s = _dot_nt(kc, qs_ref[2 * p:2 * p + 2].reshape(2 * QB, LANES)) + bias2
            dst_ref[p] = s
            dmax_ref[p] = jnp.max(s, axis=0, keepdims=True)

    def softmax_pv(t, src_ref, smax_ref):
        for p in range(ATTN_HEADS // 2):
            g = (2 * p) // heads_per_group
            m_old = m_ref[p]
            m_new = jnp.maximum(m_old, smax_ref[p])
            alpha = jnp.exp2(m_old - m_new)
            pt = jnp.exp2(src_ref[p] - m_new).astype(BF16)
            m_ref[p] = m_new
            pv = _dot(vt_ref[t, g], pt)
            l_ref[p] = alpha * l_ref[p] + pv[hd:hd + 1, :]
            acc_ref[p] = alpha * acc_ref[p] + pv[0:hd, :]

    masked_scores(0, s0_ref, x0_ref)

    def attend(u, _):
        t = 2 * u
        masked_scores(t + 1, s1_ref, x1_ref)
        softmax_pv(t, s0_ref, x0_ref)
        masked_scores(jnp.minimum(t + 2, n_tiles - 1), s0_ref, x0_ref)
        softmax_pv(t + 1, s1_ref, x1_ref)
        return 0
    lax.fori_loop(0, n_pairs, attend, 0)

    @pl.when(odd_tile)
    def _():
        softmax_pv(n_tiles - 1, s0_ref, x0_ref)

    for p in range(ATTN_HEADS // 2):
        o = acc_ref[p] / l_ref[p]
        both = jnp.concatenate([o[:, :QB], o[:, QB:]], axis=0)
        o_ref[:, p * LANES:(p + 1) * LANES] = both.T.astype(o_ref.dtype)


def _dsa_branch(main3d, misc3d):
    bsz, seq, _ = main3d.shape
    topk = min(TOPK_MAX, seq // 4)
    width = ATTN_HEADS * ATTN_HEAD_DIM
    nkt = seq // KEY_TILE
    n_pairs = ATTN_HEADS // 2
    kern = functools.partial(_dsa_kernel, topk=topk)
    return pl.pallas_call(
        kern,
        grid=(bsz, seq // Q_BLOCK),
        in_specs=[
            pl.BlockSpec((None, Q_BLOCK, width), lambda b, i: (b, i, Q_TILE)),
            pl.BlockSpec((None, Q_BLOCK, IDX_HEADS * IDX_DIM), lambda b, i: (b, i, COL_QI)),
            pl.BlockSpec((None, seq, LANES), lambda b, i: (b, 0, COL_K)),
            pl.BlockSpec((None, seq, LANES), lambda b, i: (b, 0, COL_V)),
            pl.BlockSpec((None, seq, LANES), lambda b, i: (b, 0, COL_KI)),
            pl.BlockSpec((None, Q_BLOCK, LANES), lambda b, i: (b, i, COL_DTWI)),
        ],
        out_specs=pl.BlockSpec((None, Q_BLOCK, width), lambda b, i: (b, i, 0)),
        out_shape=jax.ShapeDtypeStruct((bsz, seq, width), BF16),
        scratch_shapes=[
            pltpu.VMEM((seq, LANES), BF16),
            pltpu.VMEM((seq, LANES), BF16),
            pltpu.VMEM((nkt, ATTN_KV_HEADS, ATTN_HEAD_DIM + ONES_ROWS, KEY_TILE), BF16),
            pltpu.VMEM((ATTN_HEADS, Q_BLOCK, LANES), BF16),
            pltpu.VMEM((IDX_HEADS, Q_BLOCK, LANES), BF16),
            pltpu.VMEM((nkt, KEY_TILE, Q_BLOCK), I32),
            pltpu.VMEM((nkt, 32, KEY_TILE // 32, Q_BLOCK), I32),
            pltpu.VMEM((nkt, KEY_TILE // 32, Q_BLOCK), I32),
            pltpu.VMEM((nkt, KEY_TILE // 32, Q_BLOCK), I32),
            pltpu.VMEM((nkt, KEY_TILE // 32, Q_BLOCK), I32),
            pltpu.VMEM((n_pairs, KEY_TILE, 2 * Q_BLOCK), F32),
            pltpu.VMEM((n_pairs, KEY_TILE, 2 * Q_BLOCK), F32),
            pltpu.VMEM((n_pairs, 1, 2 * Q_BLOCK), F32),
            pltpu.VMEM((n_pairs, 1, 2 * Q_BLOCK), F32),
            pltpu.VMEM((n_pairs, 1, 2 * Q_BLOCK), F32),
            pltpu.VMEM((n_pairs, 1, 2 * Q_BLOCK), F32),
            pltpu.VMEM((n_pairs, ATTN_HEAD_DIM, 2 * Q_BLOCK), F32),
        ],
        compiler_params=_cparams(("parallel", "arbitrary")),
        name="dsa_attention",
    )(main3d, misc3d, misc3d, misc3d, misc3d, misc3d)


def _merge_kernel(h_ref, ys_ref, ya_ref, gs_ref, ga_ref, wso_ref, wao_ref, wo_ref, o_ref):
    a = _dot(ys_ref[...], wso_ref[...])
    b = _dot(ya_ref[...], wao_ref[...])
    merged = jax.nn.sigmoid(gs_ref[...].astype(F32)) * a + jax.nn.sigmoid(ga_ref[...].astype(F32)) * b
    o_ref[...] = h_ref[...] + _dot(merged.astype(BF16), wo_ref[...])


def _merge(h2d, y_ssd, y_attn, proj2d, wso, wao, wo):
    n, d = h2d.shape
    tm = 512
    full = lambda a: pl.BlockSpec(a.shape, lambda i: (0, 0))
    return pl.pallas_call(
        _merge_kernel,
        grid=(n // tm,),
        in_specs=[
            pl.BlockSpec((tm, d), lambda i: (i, 0)),
            pl.BlockSpec((tm, y_ssd.shape[1]), lambda i: (i, 0)),
            pl.BlockSpec((tm, y_attn.shape[1]), lambda i: (i, 0)),
            pl.BlockSpec((tm, d), lambda i: (i, 6)),
            pl.BlockSpec((tm, d), lambda i: (i, 7)),
            full(wso), full(wao), full(wo),
        ],
        out_specs=pl.BlockSpec((tm, d), lambda i: (i, 0)),
        out_shape=jax.ShapeDtypeStruct((n, d), F32),
        compiler_params=_cparams(("parallel",)),
        name="merge_out",
    )(h2d, y_ssd, y_attn, proj2d, proj2d, wso, wao, wo)


def _normproj_kernel(x_ref, g_ref, w_ref, o_ref):
    o_ref[...] = _dot(_rms(x_ref[...], g_ref[...]).astype(BF16), w_ref[...]).astype(o_ref.dtype)


def _norm_project(x2d, gain, w_bf16, out_dtype):
    n, d = x2d.shape
    tm = min(512, n)
    return pl.pallas_call(
        _normproj_kernel,
        grid=(n // tm,),
        in_specs=[pl.BlockSpec((tm, d), lambda i: (i, 0)), pl.BlockSpec((1, d), lambda i: (0, 0)),
                  pl.BlockSpec(w_bf16.shape, lambda i: (0, 0))],
        out_specs=pl.BlockSpec((tm, w_bf16.shape[1]), lambda i: (i, 0)),
        out_shape=jax.ShapeDtypeStruct((n, w_bf16.shape[1]), out_dtype),
        compiler_params=_cparams(("parallel",)),
        name="mem_kv_projection",
    )(x2d, gain.reshape(1, d), w_bf16)


def _xattn_kernel(h_ref, g_ref, kv_ref, wq_ref, wo_ref, o_ref):
    d = h_ref.shape[1]
    hd = d // XATTN_HEADS
    h = h_ref[...]
    q = _dot(_rms(h, g_ref[...]).astype(BF16), wq_ref[...]) * (hd ** -0.5)
    qb = q.astype(BF16)
    outs = []
    for a in range(XATTN_HEADS):
        k = kv_ref[:, a * hd:(a + 1) * hd]
        v = kv_ref[:, d + a * hd:d + (a + 1) * hd]
        s = _dot_nt(qb[:, a * hd:(a + 1) * hd], k)
        p = jnp.exp(s - jnp.max(s, axis=1, keepdims=True))
        o = _dot(p.astype(BF16), v) / jnp.sum(p, axis=1, keepdims=True)
        outs.append(o.astype(BF16))
    o_ref[...] = h + _dot(jnp.concatenate(outs, axis=1), wo_ref[...])


def _cross_attention(h3d, gain, kv3d, wq, wo):
    bsz, seq, d = h3d.shape
    tm = min(512, seq)
    mlen = kv3d.shape[1]
    return pl.pallas_call(
        _xattn_kernel,
        grid=(bsz, seq // tm),
        in_specs=[
            pl.BlockSpec((None, tm, d), lambda b, i: (b, i, 0)),
            pl.BlockSpec((1, d), lambda b, i: (0, 0)),
            pl.BlockSpec((None, mlen, 2 * d), lambda b, i: (b, 0, 0)),
            pl.BlockSpec(wq.shape, lambda b, i: (0, 0)),
            pl.BlockSpec(wo.shape, lambda b, i: (0, 0)),
        ],
        out_specs=pl.BlockSpec((None, tm, d), lambda b, i: (b, i, 0)),
        out_shape=jax.ShapeDtypeStruct((bsz, seq, d), F32),
        compiler_params=_cparams(("parallel", "parallel")),
        name="cross_attention",
    )(h3d, gain.reshape(1, d), kv3d, wq, wo)


def _router_kernel(h_ref, g_ref, wr_ref, br_ref, trit_ref, aug_ref, rank_ref, gsel_ref, cnt_ref, base_ref):
    i = pl.program_id(0)
    d = h_ref.shape[1]

    @pl.when(i == 0)
    def _():
        base_ref[...] = jnp.zeros(base_ref.shape, F32)

    hn = _rms(h_ref[...], g_ref[...])
    aug_ref[:, 0:d] = hn
    h_hi, h_mid, h_lo = _split3(hn)
    w_hi, w_mid, w_lo = wr_ref[0], wr_ref[1], wr_ref[2]
    logits = (_dot(h_hi, w_hi) + (_dot(h_hi, w_mid) + _dot(h_mid, w_hi))
              + (_dot(h_hi, w_lo) + _dot(h_mid, w_mid) + _dot(h_lo, w_hi))) + br_ref[...]
    tm = logits.shape[0]
    lane = lax.broadcasted_iota(I32, (tm, LANES), 1)
    neg_inf = -jnp.inf
    n_e = MOE_GROUPS * MOE_EXPERTS_PER_GROUP
    is_g = jnp.logical_and(lane >= n_e, lane < n_e + MOE_GROUPS)
    gl = jnp.where(is_g, logits, neg_inf)
    g_max = jnp.max(gl, axis=1, keepdims=True)
    g_sel = jnp.min(jnp.where(gl == g_max, lane, LANES), axis=1, keepdims=True) - n_e
    g_w = 1.0 / jnp.sum(jnp.where(is_g, jnp.exp(gl - g_max), 0.0), axis=1, keepdims=True)
    in_grp = jnp.logical_and(lane >= g_sel * MOE_EXPERTS_PER_GROUP,
                             lane < (g_sel + 1) * MOE_EXPERTS_PER_GROUP)
    el = jnp.where(in_grp, logits, neg_inf)
    v1 = jnp.max(el, axis=1, keepdims=True)
    i1 = jnp.min(jnp.where(el == v1, lane, LANES), axis=1, keepdims=True)
    el2 = jnp.where(lane == i1, neg_inf, el)
    v2 = jnp.max(el2, axis=1, keepdims=True)
    i2 = jnp.min(jnp.where(el2 == v2, lane, LANES), axis=1, keepdims=True)
    e2 = jnp.exp(v2 - v1)
    w1 = g_w / (1.0 + e2)
    w2 = g_w * e2 / (1.0 + e2)
    aug_ref[:, d:] = jnp.where(lane == i1, w1, 0.0) + jnp.where(lane == i2, w2, 0.0)

    onehot_t = jnp.where(lane == g_sel, 1.0, 0.0).T
    upto = _dot(onehot_t.astype(BF16), trit_ref[...]) + base_ref[...]
    grp = lax.broadcasted_iota(I32, onehot_t.shape, 0).astype(F32)
    rank_ref[...] = (jnp.sum(onehot_t * upto, axis=0, keepdims=True) - 1.0).astype(I32)
    gsel_ref[...] = jnp.sum(onehot_t * grp, axis=0, keepdims=True).astype(I32)
    base_ref[...] = upto[:, tm - 1:tm]
    cnt_ref[...] = jnp.broadcast_to(upto[:, tm - 1:tm], cnt_ref.shape)


def _moe_route(h2d, gain, wr3, br):
    n, d = h2d.shape
    tm = 512
    nt = n // tm
    trit = (jnp.arange(tm)[:, None] <= jnp.arange(tm)[None, :]).astype(BF16)
    return pl.pallas_call(
        _router_kernel,
        grid=(nt,),
        in_specs=[pl.BlockSpec((tm, d), lambda i: (i, 0)), pl.BlockSpec((1, d), lambda i: (0, 0)),
                  pl.BlockSpec(wr3.shape, lambda i: (0, 0, 0)), pl.BlockSpec((1, LANES), lambda i: (0, 0)),
                  pl.BlockSpec((tm, tm), lambda i: (0, 0))],
        out_specs=[pl.BlockSpec((tm, d + LANES), lambda i: (i, 0)),
                   pl.BlockSpec((None, 1, tm), lambda i: (i, 0, 0)),
                   pl.BlockSpec((None, 1, tm), lambda i: (i, 0, 0)),
                   pl.BlockSpec((LANES, LANES), lambda i: (0, 0))],
        out_shape=[jax.ShapeDtypeStruct((n, d + LANES), F32),
                   jax.ShapeDtypeStruct((nt, 1, tm), I32),
                   jax.ShapeDtypeStruct((nt, 1, tm), I32),
                   jax.ShapeDtypeStruct((LANES, LANES), F32)],
        scratch_shapes=[pltpu.VMEM((LANES, 1), F32)],
        compiler_params=_cparams(("arbitrary",)),
        name="moe_router",
    )(h2d, gain.reshape(1, d), wr3, br, trit)


MOE_TILE = 512
ROW_BLOCK = 256
ISSUE_UNROLL = 8


def _scatter_rows_kernel(dest_ref, src_ref, init_ref, out_ref, sem):
    del init_ref
    base = pl.program_id(0) * ROW_BLOCK

    def row_copy(r):
        return pltpu.make_async_copy(src_ref.at[pl.ds(r, 1)], out_ref.at[pl.ds(dest_ref[base + r], 1)], sem)

    def issue(r, _):
        row_copy(r).start()
        return 0
    lax.fori_loop(0, ROW_BLOCK, issue, 0, unroll=ISSUE_UNROLL)
    pltpu.make_async_copy(src_ref, out_ref.at[pl.ds(0, ROW_BLOCK)], sem).wait()


def _scatter_rows(src, dest, n_out):
    n, w = src.shape
    init = jnp.zeros((n_out, w), src.dtype)
    return pl.pallas_call(
        _scatter_rows_kernel,
        grid_spec=pltpu.PrefetchScalarGridSpec(
            num_scalar_prefetch=1,
            grid=(n // ROW_BLOCK,),
            in_specs=[pl.BlockSpec((ROW_BLOCK, w), lambda i, dest: (i, 0)),
                      pl.BlockSpec(memory_space=pl.ANY)],
            out_specs=pl.BlockSpec(memory_space=pl.ANY),
            scratch_shapes=[pltpu.SemaphoreType.DMA(())],
        ),
        out_shape=jax.ShapeDtypeStruct((n_out, w), src.dtype),
        input_output_aliases={2: 0},
        compiler_params=_cparams(("arbitrary",)),
        name="moe_sort_rows",
    )(dest, src, init)


def _sorted_experts_kernel(tile_group_ref, n_used_ref, hn_ref, cw_ref, wg_ref, wu_ref, wd_ref, o_ref):
    j = pl.program_id(0)

    @pl.when(j < n_used_ref[0])
    def _():
        hn = hn_ref[...].astype(BF16)
        first = tile_group_ref[j] * MOE_EXPERTS_PER_GROUP
        lane = lax.broadcasted_iota(I32, cw_ref.shape, 1)
        cw = cw_ref[...]
        acc = jnp.zeros(o_ref.shape, F32)
        for e in range(MOE_EXPERTS_PER_GROUP):
            w_e = jnp.sum(jnp.where(lane == first + e, cw, 0.0), axis=1, keepdims=True)
            hid = _silu(_dot(hn, wg_ref[e])) * _dot(hn, wu_ref[e]) * w_e
            acc = acc + _dot(hid.astype(BF16), wd_ref[e])
        o_ref[...] = acc

    @pl.when(j >= n_used_ref[0])
    def _():
        o_ref[...] = jnp.zeros(o_ref.shape, F32)


def _sorted_experts(aug_sorted, tile_group, n_used, wg, wu, wd):
    n_pad, wa = aug_sorted.shape
    d = wa - LANES
    _, n_e, _, dff = wg.shape
    tm = MOE_TILE
    return pl.pallas_call(
        _sorted_experts_kernel,
        grid_spec=pltpu.PrefetchScalarGridSpec(
            num_scalar_prefetch=2,
            grid=(n_pad // tm,),
            in_specs=[
                pl.BlockSpec((tm, d), lambda j, tg, nu: (j, 0)),
                pl.BlockSpec((tm, LANES), lambda j, tg, nu: (j, d // LANES)),
                pl.BlockSpec((None, n_e, d, dff), lambda j, tg, nu: (tg[j], 0, 0, 0)),
                pl.BlockSpec((None, n_e, d, dff), lambda j, tg, nu: (tg[j], 0, 0, 0)),
                pl.BlockSpec((None, n_e, dff, d), lambda j, tg, nu: (tg[j], 0, 0, 0)),
            ],
            out_specs=pl.BlockSpec((tm, d), lambda j, tg, nu: (j, 0)),
        ),
        out_shape=jax.ShapeDtypeStruct((n_pad, d), F32),
        compiler_params=_cparams(("arbitrary",)),
        name="moe_experts",
    )(tile_group, n_used, aug_sorted, aug_sorted, wg, wu, wd)


def _gather_add_kernel(dest_ref, h_ref, delta_ref, fg_ref, o_ref, buf_ref, sem, *, final_norm):
    i = pl.program_id(0)
    n_steps = pl.num_programs(0)

    def row_copy(step, r):
        slot = lax.rem(step, 2)
        return pltpu.make_async_copy(delta_ref.at[pl.ds(dest_ref[step * ROW_BLOCK + r], 1)],
                                     buf_ref.at[slot, pl.ds(r, 1)], sem.at[slot])

    def issue_all(step):
        def issue(r, _):
            row_copy(step, r).start()
            return 0
        lax.fori_loop(0, ROW_BLOCK, issue, 0, unroll=ISSUE_UNROLL)

    @pl.when(i == 0)
    def _():
        issue_all(0)

    @pl.when(i + 1 < n_steps)
    def _():
        issue_all(i + 1)

    slot = lax.rem(i, 2)
    pltpu.make_async_copy(delta_ref.at[pl.ds(0, ROW_BLOCK)], buf_ref.at[slot], sem.at[slot]).wait()

    out = h_ref[...] + buf_ref[slot]
    if final_norm:
        out = _rms(out, fg_ref[...])
    o_ref[...] = out


def _gather_add(h2d, delta_sorted, dest, final_gain, final_norm):
    n, d = h2d.shape
    kern = functools.partial(_gather_add_kernel, final_norm=final_norm)
    return pl.pallas_call(
        kern,
        grid_spec=pltpu.PrefetchScalarGridSpec(
            num_scalar_prefetch=1,
            grid=(n // ROW_BLOCK,),
            in_specs=[pl.BlockSpec((ROW_BLOCK, d), lambda i, dest: (i, 0)),
                      pl.BlockSpec(memory_space=pl.ANY),
                      pl.BlockSpec((1, d), lambda i, dest: (0, 0))],
            out_specs=pl.BlockSpec((ROW_BLOCK, d), lambda i, dest: (i, 0)),
            scratch_shapes=[pltpu.VMEM((2, ROW_BLOCK, d), F32), pltpu.SemaphoreType.DMA((2,))],
        ),
        out_shape=jax.ShapeDtypeStruct((n, d), F32),
        compiler_params=_cparams(("arbitrary",)),
        name="moe_unsort_add",
    )(dest, h2d, delta_sorted, final_gain.reshape(1, d))


def _moe_layout(rank, gsel, counts, n):
    tm = MOE_TILE
    cnt = counts[:MOE_GROUPS, 0].astype(I32)
    seg = ((cnt + tm - 1) // tm) * tm
    ends = jnp.cumsum(seg)
    dest = (ends - seg)[gsel.reshape(n)] + rank.reshape(n)
    n_tiles = n // tm + MOE_GROUPS
    starts = jnp.arange(n_tiles, dtype=I32) * tm
    tile_group = jnp.minimum(jnp.sum(starts[:, None] >= ends[None, :], axis=1), MOE_GROUPS - 1).astype(I32)
    n_used = (ends[-1:] // tm).astype(I32)
    return dest.astype(I32), tile_group, n_used, n_tiles * tm


def _prep_router(w_group, b_group, w_router, b_router):
    w = jnp.concatenate([w_router, w_group], axis=1).astype(F32)
    w = jnp.pad(w, ((0, 0), (0, LANES - w.shape[1])))
    hi = w.astype(BF16)
    r1 = w - hi.astype(F32)
    mid = r1.astype(BF16)
    lo = (r1 - mid.astype(F32)).astype(BF16)
    b = jnp.concatenate([b_router, b_group]).astype(F32)
    b = jnp.pad(b, (0, LANES - b.shape[0])).reshape(1, LANES)
    return jnp.stack([hi, mid, lo]), b


def kernel(x, mem, norm_mix, w_in, conv_w, conv_b, dt_bias, a_log, d_skip, ssd_norm, w_ssd_o, w_attn_o, w_out,
           norm_xattn, norm_mem, w_cq, w_ckv, w_co, norm_ffn, w_group, b_group, w_router, b_router,
           w_gate_e, w_up_e, w_down_e, norm_final):
    bsz, seq, d = x.shape
    depth = w_in.shape[0]
    mlen = mem.shape[1]
    heads = dt_bias.shape[1]
    d_inner = heads * SSD_HEAD_DIM
    conv_ch = conv_w.shape[2]
    n = bsz * seq
    rope_c, rope_s = _rope_tables(seq)
    h = x.reshape(n, d)
    mem2d = mem.reshape(bsz * mlen, d)
    for l in range(depth):
        w_in_l = _prep_w_in(w_in[l], d_inner, conv_ch, heads)
        main, misc = _in_projection(h, norm_mix[l], w_in_l, rope_c, rope_s, seq)
        main3d = main.reshape(bsz, seq, main.shape[1])
        misc3d = misc.reshape(bsz, seq, misc.shape[1])
        y_ssd = _ssd_branch(main3d, misc3d, conv_w[l], conv_b[l], dt_bias[l], a_log[l], d_skip[l], ssd_norm[l])
        y_attn = _dsa_branch(main3d, misc3d)
        h = _merge(h, y_ssd.reshape(n, d_inner), y_attn.reshape(n, -1), main,
                   w_ssd_o[l].astype(BF16), w_attn_o[l].astype(BF16), w_out[l].astype(BF16))
        kv = _norm_project(mem2d, norm_mem[l], w_ckv[l].astype(BF16), BF16)
        h = _cross_attention(h.reshape(bsz, seq, d), norm_xattn[l], kv.reshape(bsz, mlen, 2 * d),
                             w_cq[l].astype(BF16), w_co[l].astype(BF16)).reshape(n, d)
        wr3, br = _prep_router(w_group[l], b_group[l], w_router[l], b_router[l])
        aug, rank, gsel, counts = _moe_route(h, norm_ffn[l], wr3, br)
        dest, tile_group, n_used, n_pad = _moe_layout(rank, gsel, counts, n)
        aug_sorted = _scatter_rows(aug, dest, n_pad)
        delta = _sorted_experts(aug_sorted, tile_group, n_used, w_gate_e[l].astype(BF16),
                                w_up_e[l].astype(BF16), w_down_e[l].astype(BF16))
        h = _gather_add(h, delta, dest, norm_final, final_norm=(l == depth - 1))
    return h.reshape(bsz, seq, d)
```

```python
import functools
import math

import jax
import jax.numpy as jnp
import numpy as np
from jax import lax
from jax.experimental import pallas as pl
from jax.experimental.pallas import tpu as pltpu

F32 = jnp.float32
BF16 = jnp.bfloat16
I32 = jnp.int32

SSD_HEAD_DIM = 64
SSD_GROUPS = 4
SSD_STATE = 128
SSD_CONV = 4
SSD_CHUNK = 128
SSD_SEQS_PER_STEP = 2
ATTN_HEADS = 16
ATTN_KV_HEADS = 2
ATTN_HEAD_DIM = 64
Q_BLOCK = 128
TOPK_MAX = 256
IDX_HEADS = 8
IDX_DIM = 64
ROPE_THETA = 10000.0
XATTN_HEADS = 4
MOE_GROUPS = 4
MOE_EXPERTS_PER_GROUP = 8
RMS_EPS = 1e-6

LANES = 128
SUBLANES = 8
KEY_TILE = 256
INT_MIN = -2 ** 31
NEG_BIG = -1e30
LOG2E = math.log2(math.e)
ONES_ROWS = 16
VMEM_LIMIT = 56 * 1024 * 1024


def _cparams(sem):
    return pltpu.CompilerParams(dimension_semantics=sem, vmem_limit_bytes=VMEM_LIMIT)


def _split3(v):
    hi = v.astype(BF16)
    r1 = v - hi.astype(F32)
    mid = r1.astype(BF16)
    lo = (r1 - mid.astype(F32)).astype(BF16)
    return hi, mid, lo


def _dot(a, b):
    return jnp.dot(a, b, preferred_element_type=F32)


def _dot_nt(a, b):
    return lax.dot_general(a, b, (((1,), (1,)), ((), ())), preferred_element_type=F32)


def _spread(v, m01, terms):
    parts = _split3(v)[:terms]
    out = _dot(parts[0], m01)
    for part in parts[1:]:
        out = out + _dot(part, m01)
    return out


def _dot3_lhs01(m01, v):
    hi, mid, lo = _split3(v)
    return _dot(m01, hi) + _dot(m01, mid) + _dot(m01, lo)


def _rms(x, g):
    ms = jnp.mean(x * x, axis=-1, keepdims=True)
    return x * lax.rsqrt(ms + RMS_EPS) * g


def _silu(x):
    return x * jax.nn.sigmoid(x)


Q_TILE = 5
MISC_TILE = 8
MISC_ROPE_CHUNKS = (0, 1, 2, 3, 4, 6)
COL_QI = 0
COL_K, COL_V, COL_KI, COL_DTWI = 4, 5, 6, 7
DT_LANES = 32


def _rope_chunk(a, c, s):
    return a * c + pltpu.roll(a, LANES // 2, 1) * s


def _inproj_kernel(x_ref, g_ref, w_ref, c_ref, s_ref, o_ref, misc_ref, xn_ref):
    j = pl.program_id(1)

    @pl.when(j == 0)
    def _():
        xn_ref[...] = _rms(x_ref[...], g_ref[...]).astype(BF16)

    acc = _dot(xn_ref[...], w_ref[...])
    n_chunks = acc.shape[1] // LANES

    def roped(dst_ref, chunks):
        c, sn = c_ref[...], s_ref[...]
        for k in range(n_chunks):
            a = acc[:, k * LANES:(k + 1) * LANES]
            if k in chunks:
                a = _rope_chunk(a, c, sn)
            dst_ref[:, k * LANES:(k + 1) * LANES] = a.astype(dst_ref.dtype)

    @pl.when(j == Q_TILE)
    def _():
        roped(o_ref, tuple(range(n_chunks)))

    @pl.when(j == MISC_TILE)
    def _():
        roped(misc_ref, MISC_ROPE_CHUNKS)

    @pl.when(jnp.logical_and(j != Q_TILE, j != MISC_TILE))
    def _():
        o_ref[...] = acc.astype(o_ref.dtype)


def _in_projection(x2d, gain, w_bf16, rope_c, rope_s, seq):
    n, d = x2d.shape
    tm = min(1024, seq)
    tn = 1024
    pos_tiles = seq // tm
    n_tiles = w_bf16.shape[1] // tn
    assert n_tiles == MISC_TILE + 1
    return pl.pallas_call(
        _inproj_kernel,
        grid=(n // tm, n_tiles),
        in_specs=[
            pl.BlockSpec((tm, d), lambda i, j: (i, 0)),
            pl.BlockSpec((1, d), lambda i, j: (0, 0)),
            pl.BlockSpec((d, tn), lambda i, j: (0, j)),
            pl.BlockSpec((tm, LANES), lambda i, j: (i % pos_tiles, 0)),
            pl.BlockSpec((tm, LANES), lambda i, j: (i % pos_tiles, 0)),
        ],
        out_specs=[pl.BlockSpec((tm, tn), lambda i, j: (i, jnp.minimum(j, MISC_TILE - 1))),
                   pl.BlockSpec((tm, tn), lambda i, j: (i, 0))],
        out_shape=[jax.ShapeDtypeStruct((n, MISC_TILE * tn), BF16), jax.ShapeDtypeStruct((n, tn), F32)],
        scratch_shapes=[pltpu.VMEM((tm, d), BF16)],
        compiler_params=_cparams(("parallel", "arbitrary")),
        name="in_projection",
    )(x2d, gain.reshape(1, d), w_bf16, rope_c, rope_s)


def _pair_halves(w):
    d = w.shape[0]
    half = ATTN_HEAD_DIM // 2
    return w.reshape(d, -1, 2, 2, half).transpose(0, 1, 3, 2, 4).reshape(d, -1)


def _prep_w_in(w, d_inner, conv_ch, ssd_heads):
    widths = (d_inner, conv_ch, ssd_heads, ATTN_HEADS * ATTN_HEAD_DIM, ATTN_KV_HEADS * ATTN_HEAD_DIM,
              ATTN_KV_HEADS * ATTN_HEAD_DIM, IDX_HEADS * IDX_DIM, IDX_DIM, IDX_HEADS,
              w.shape[0], w.shape[0])
    splits = [int(p) for p in np.cumsum(widths)[:-1]]
    z, xbc, dt, q, k, v, qi, ki, wi, gs, ga = jnp.split(w, splits, axis=1)

    def pad(a, n):
        return jnp.pad(a, ((0, 0), (0, n - a.shape[1])))

    q, k, qi, ki = _pair_halves(q), _pair_halves(k), _pair_halves(qi), _pair_halves(pad(ki, LANES))
    misc = jnp.concatenate([qi, k, v, ki, pad(jnp.concatenate([dt, wi], axis=1), LANES)], axis=1)
    return jnp.concatenate([z, xbc, q, gs, ga, misc], axis=1).astype(BF16)


def _rope_tables(seq):
    inv = 1.0 / (ROPE_THETA ** (jnp.arange(0, ATTN_HEAD_DIM, 2, dtype=F32) / ATTN_HEAD_DIM))
    ang = jnp.arange(seq, dtype=F32)[:, None] * inv[None, :]
    cos, sin = jnp.cos(ang), jnp.sin(ang)
    c = jnp.concatenate([cos, cos, cos, cos], axis=1)
    s = jnp.concatenate([-sin, -sin, sin, sin], axis=1)
    return c, s


def _ssd_chunk(z_ref, xs_ref, bc_ref, dt_ref, cw_ref, cb_ref, dtb_ref, alog_ref, dsk_ref, ng_ref,
               r64_ref, tri_ref, shift_ref, y_ref, prev_ref, st_ref, act_ref, heads, groups):
    L = SSD_CHUNK
    P = SSD_HEAD_DIM
    hpg = heads // groups
    d_inner = heads * P
    gw = hpg * P
    nst = SSD_STATE

    def conv(cur_ref, lo, hi):
        cur = cur_ref[...]
        both = jnp.concatenate([prev_ref[:, lo:hi], cur], axis=0)
        acc = cb_ref[:, lo:hi] + cw_ref[SSD_CONV - 1:SSD_CONV, lo:hi] * cur.astype(F32)
        for k in range(1, SSD_CONV):
            acc = acc + cw_ref[SSD_CONV - 1 - k:SSD_CONV - k, lo:hi] * _dot(shift_ref[k - 1], both)
        prev_ref[:, lo:hi] = cur
        return _silu(acc)

    xs = conv(xs_ref, 0, d_inner)
    bc = conv(bc_ref, d_inner, prev_ref.shape[1])
    bmat = bc[:, :groups * nst]
    cmat = bc[:, groups * nst:]

    dt = jax.nn.softplus(dt_ref[...] + dtb_ref[...])
    da = dt * (-jnp.exp(alog_ref[...]))
    tri = tri_ref[...]
    acum = _dot3_lhs01(tri, da)
    act_ref[...] = acum.T
    e_acum = jnp.exp(acum)
    e_tail = jnp.exp(acum[L - 1:L, :] - acum)
    r64 = r64_ref[...]
    e64 = _spread(e_acum, r64, 2)
    t64 = _spread(e_tail, r64, 1)
    d64 = _spread(dt, r64, 1)
    xdt = xs * d64
    xdt_b = xdt.astype(BF16)
    xtl_b = (xdt * t64).astype(BF16)

    row = lax.broadcasted_iota(I32, (L, L), 0)
    col = lax.broadcasted_iota(I32, (L, L), 1)
    causal = row >= col
    first_head = lax.broadcasted_iota(I32, (1, 2 * P), 1) < P

    for g in range(groups):
        bg = bmat[:, g * nst:(g + 1) * nst]
        cg_b = cmat[:, g * nst:(g + 1) * nst].astype(BF16)
        cb = _dot_nt(cg_b, bg.astype(BF16))
        sg = st_ref[g]
        y_off = _dot(cg_b, sg.astype(BF16))
        y_heads = []
        for hh in range(0, hpg, 2):
            pair = []
            for h in (g * hpg + hh, g * hpg + hh + 1):
                seg = jnp.broadcast_to(acum[:, h:h + 1], (L, LANES)) - act_ref[h:h + 1, :]
                pair.append((cb * jnp.where(causal, jnp.exp(seg), 0.0)).astype(BF16))
            slab = xdt_b[:, (g * hpg + hh) * P:(g * hpg + hh + 2) * P]
            stacked = jnp.concatenate([jnp.where(first_head, slab, 0), jnp.where(first_head, 0, slab)], axis=0)
            y_heads.append(_dot(jnp.concatenate(pair, axis=1), stacked))
        lo, hi = g * gw, (g + 1) * gw
        y_g = jnp.concatenate(y_heads, axis=1) + y_off * e64[:, lo:hi]
        st_ref[g] = sg * e64[L - 1:L, lo:hi] + _dot(bg.T.astype(BF16), xtl_b[:, lo:hi])
        y_g = (y_g + dsk_ref[:, lo:hi] * xs[:, lo:hi]) * _silu(z_ref[:, lo:hi].astype(F32))
        ms = jnp.mean(y_g * y_g, axis=-1, keepdims=True)
        y_ref[:, lo:hi] = (y_g * lax.rsqrt(ms + RMS_EPS) * ng_ref[:, lo:hi]).astype(y_ref.dtype)


def _ssd_kernel(z_ref, xs_ref, bc_ref, dt_ref, cw_ref, cb_ref, dtb_ref, alog_ref, dsk_ref, ng_ref,
                r64_ref, tri_ref, shift_ref, y_ref, prev_ref, st_ref, act_ref, *, heads, groups):
    @pl.when(pl.program_id(1) == 0)
    def _():
        prev_ref[...] = jnp.zeros(prev_ref.shape, BF16)
        st_ref[...] = jnp.zeros(st_ref.shape, F32)

    for nb in range(z_ref.shape[0]):
        _ssd_chunk(z_ref.at[nb], xs_ref.at[nb], bc_ref.at[nb], dt_ref.at[nb], cw_ref, cb_ref, dtb_ref,
                   alog_ref, dsk_ref, ng_ref, r64_ref, tri_ref, shift_ref, y_ref.at[nb],
                   prev_ref.at[nb], st_ref.at[nb], act_ref.at[nb], heads, groups)


def _ssd_branch(main3d, misc3d, conv_w, conv_b, dt_bias, a_log, d_skip, ssd_norm):
    bsz, seq, _ = main3d.shape
    heads = dt_bias.shape[0]
    d_inner = heads * SSD_HEAD_DIM
    groups = SSD_GROUPS
    conv_ch = conv_w.shape[1]
    bc_w = conv_ch - d_inner
    L = SSD_CHUNK
    nb = SSD_SEQS_PER_STEP if bsz % SSD_SEQS_PER_STEP == 0 else 1

    def pad_lanes(v):
        return jnp.pad(v.astype(F32), (0, LANES - v.shape[0])).reshape(1, LANES)

    hidx = jnp.arange(LANES)[:, None]
    r64 = (hidx == (jnp.arange(d_inner) // SSD_HEAD_DIM)[None, :]).astype(BF16)
    tri = (jnp.arange(L)[:, None] >= jnp.arange(L)[None, :]).astype(BF16)
    taps = jnp.arange(1, SSD_CONV)[:, None, None]
    shift = (jnp.arange(2 * L)[None, None, :] == (L + jnp.arange(L)[None, :, None] - taps)).astype(BF16)
    dsk = jnp.repeat(d_skip.astype(F32), SSD_HEAD_DIM).reshape(1, d_inner)
    const = lambda shape: pl.BlockSpec(shape, lambda b, c: (0,) * len(shape))
    kern = functools.partial(_ssd_kernel, heads=heads, groups=groups)
    return pl.pallas_call(
        kern,
        grid=(bsz // nb, seq // L),
        in_specs=[
            pl.BlockSpec((nb, L, d_inner), lambda b, c: (b, c, 0)),
            pl.BlockSpec((nb, L, d_inner), lambda b, c: (b, c, 1)),
            pl.BlockSpec((nb, L, bc_w), lambda b, c: (b, c, 2 * d_inner // bc_w)),
            pl.BlockSpec((nb, L, LANES), lambda b, c: (b, c, COL_DTWI)),
            const((SSD_CONV, conv_ch)), const((1, conv_ch)), const((1, LANES)), const((1, LANES)),
            const((1, d_inner)), const((1, d_inner)),
            const((LANES, d_inner)), const((L, L)),
            const((SSD_CONV - 1, L, 2 * L)),
        ],
        out_specs=pl.BlockSpec((nb, L, d_inner), lambda b, c: (b, c, 0)),
        out_shape=jax.ShapeDtypeStruct((bsz, seq, d_inner), BF16),
        scratch_shapes=[
            pltpu.VMEM((nb, L, conv_ch), BF16),
            pltpu.VMEM((nb, groups, SSD_STATE, d_inner // groups), F32),
            pltpu.VMEM((nb, LANES, L), F32),
        ],
        compiler_params=_cparams(("parallel", "arbitrary")),
        name="ssd_scan",
    )(main3d, main3d, main3d, misc3d, conv_w.astype(F32), conv_b.reshape(1, conv_ch).astype(F32),
      pad_lanes(dt_bias), pad_lanes(a_log), dsk, ssd_norm.reshape(1, d_inner).astype(F32), r64, tri, shift)


def _sortable_key(x):
    x = jnp.where(x == 0.0, 0.0, x)
    b = pltpu.bitcast(x, I32)
    return b ^ (lax.shift_right_arithmetic(b, 31) & 0x7FFFFFFF)


_BIT_MASKS = (0x0000FFFF, 0x00FF00FF, 0x0F0F0F0F, 0x33333333, 0x55555555)


def _bit_planes(rows):
    x = list(rows[::-1])
    j = 16
    for m in _BIT_MASKS:
        mask = np.int32(np.uint32(m))
        k = 0
        while k < 32:
            t = (x[k] ^ lax.shift_right_logical(x[k + j], jnp.int32(j))) & mask
            x[k] = x[k] ^ t
            x[k + j] = x[k + j] ^ lax.shift_left(t, jnp.int32(j))
            k = (k + j + 1) & ~j
        j >>= 1
    return x


def _dsa_kernel(q_ref, qi_ref, k_ref, v_ref, ki_ref, wi_ref, o_ref,
                kb_ref, kib_ref, vt_ref, qs_ref, qis_ref, keys_ref, planes_ref, sel_ref, cand_ref, above_ref,
                s0_ref, s1_ref, x0_ref, x1_ref, m_ref, l_ref, acc_ref, *, topk):
    i = pl.program_id(1)
    QB, KT = Q_BLOCK, KEY_TILE
    hd = ATTN_HEAD_DIM
    seq = k_ref.shape[0]
    n_tiles = (i * QB + QB + KT - 1) // KT
    n_pairs = n_tiles // 2
    odd_tile = n_tiles % 2 == 1
    lane = lax.broadcasted_iota(I32, (1, LANES), 1)
    quarter = hd // 2
    first = (lane // quarter) % 2 == 0

    @pl.when(i == 0)
    def _():
        def fill(t, _):
            rows = pl.ds(pl.multiple_of(t * KT, KT), KT)
            kb_ref[rows, :] = k_ref[rows, :].astype(BF16)
            kib_ref[rows, :] = ki_ref[rows, :].astype(BF16)
            vt = v_ref[rows, :].T.astype(BF16)
            for g in range(ATTN_KV_HEADS):
                vt_ref[t, g, 0:hd, :] = vt[g * hd:(g + 1) * hd, :]
                vt_ref[t, g, hd:hd + ONES_ROWS, :] = jnp.ones((ONES_ROWS, KT), BF16)
            return 0
        lax.fori_loop(0, seq // KT, fill, 0)
        planes_ref[...] = jnp.zeros(planes_ref.shape, I32)

    qpos = i * QB + lax.broadcasted_iota(I32, (1, QB), 1)
    idx_scale = (IDX_DIM ** -0.5) * (IDX_HEADS ** -0.5)
    wt = (wi_ref[...] * idx_scale).T
    w_rows = [wt[DT_LANES + h:DT_LANES + h + 1, :] for h in range(IDX_HEADS)]

    heads_per_group = ATTN_HEADS // ATTN_KV_HEADS
    for p in range(ATTN_HEADS // 2):
        slab = q_ref[:, p * LANES:(p + 1) * LANES].astype(F32) * ((hd ** -0.5) * LOG2E)
        if (2 * p) // heads_per_group == 0:
            even = jnp.where(first, slab, 0.0)
            odd = jnp.where(first, pltpu.roll(slab, LANES - quarter, 1), 0.0)
        else:
            even = jnp.where(first, 0.0, pltpu.roll(slab, quarter, 1))
            odd = jnp.where(first, 0.0, slab)
        qs_ref[2 * p] = even.astype(BF16)
        qs_ref[2 * p + 1] = odd.astype(BF16)
    for p in range(IDX_HEADS // 2):
        slab = qi_ref[:, p * LANES:(p + 1) * LANES]
        qis_ref[2 * p] = slab.astype(BF16)
        qis_ref[2 * p + 1] = pltpu.roll(slab, LANES - quarter, 1).astype(BF16)

    nkt = seq // KT
    rows_per_word = KT // 32
    assert rows_per_word == SUBLANES

    def score_tile(t):
        kic = kib_ref[pl.ds(pl.multiple_of(t * KT, KT), KT), :]
        acc = jnp.zeros((KT, QB), F32)
        for p in range(IDX_HEADS // 2):
            rel = _dot_nt(kic, qis_ref[2 * p:2 * p + 2].reshape(2 * QB, LANES))
            acc = acc + w_rows[2 * p] * jnp.maximum(rel[:, :QB], 0.0)
            acc = acc + w_rows[2 * p + 1] * jnp.maximum(rel[:, QB:], 0.0)
        kpos = t * KT + lax.broadcasted_iota(I32, (KT, QB), 0)
        keys = jnp.where(kpos <= qpos, _sortable_key(acc), INT_MIN)
        keys_ref[t] = keys
        unsigned = keys ^ INT_MIN
        planes = _bit_planes([unsigned[r * rows_per_word:(r + 1) * rows_per_word, :] for r in range(32)])
        for bit in range(32):
            planes_ref[t, bit] = planes[bit]

    def score_pair(u, _):
        score_tile(2 * u)
        score_tile(2 * u + 1)
        return 0
    lax.fori_loop(0, n_pairs, score_pair, 0)

    @pl.when(odd_tile)
    def _():
        score_tile(n_tiles - 1)

    select_all = qpos < topk
    for t in range(nkt):
        cand_ref[t] = jnp.where(t < n_tiles, jnp.full((rows_per_word, QB), -1, I32), 0)
        above_ref[t] = jnp.zeros((rows_per_word, QB), I32)

    def radix_step(it, need):
        cnt = jnp.zeros((rows_per_word, QB), I32)
        for t in range(nkt):
            cnt = cnt + lax.population_count(cand_ref[t] & planes_ref[t, it])
        cnt = jnp.sum(cnt, axis=0, keepdims=True)
        take = cnt >= need
        for t in range(nkt):
            cand = cand_ref[t]
            ones = cand & planes_ref[t, it]
            cand_ref[t] = jnp.where(take, ones, cand ^ ones)
            above_ref[t] = jnp.where(take, above_ref[t], above_ref[t] | ones)
        return jnp.where(take, need, need - cnt)
    need = lax.fori_loop(0, 32, radix_step, jnp.full((1, QB), topk, I32))

    ties = jnp.zeros((rows_per_word, QB), I32)
    for t in range(nkt):
        ties = ties + lax.population_count(cand_ref[t])
        sel_ref[t] = jnp.where(select_all, -1, above_ref[t] | cand_ref[t])
    surplus = jnp.where(select_all, 0, jnp.sum(ties, axis=0, keepdims=True) - need)

    @pl.when(jnp.max(surplus) > 0)
    def _():
        def count(pred):
            def body(t, acc):
                hit = jnp.where(pred(keys_ref[t], t), 1, 0)
                return acc + jnp.sum(hit.reshape(KT // SUBLANES, SUBLANES, QB), axis=0)
            acc = lax.fori_loop(0, n_tiles, body, jnp.zeros((SUBLANES, QB), I32))
            return jnp.sum(acc, axis=0, keepdims=True)

        def bit_step(it, thr):
            cand = thr ^ lax.shift_left(jnp.int32(1), 31 - it)
            cnt = count(lambda kc, c: kc >= cand)
            return jnp.where(cnt >= topk, cand, thr)
        thr = lax.fori_loop(0, 32, bit_step, jnp.full((1, QB), INT_MIN, I32))
        want = topk - count(lambda kc, c: kc > thr)
        n_bits = seq.bit_length()

        def tie_step(it, jb):
            cand = jb | lax.shift_left(jnp.int32(1), n_bits - 1 - it)
            def pred(kc, c):
                kpos = c * KT + lax.broadcasted_iota(I32, (KT, QB), 0)
                return jnp.logical_and(kc == thr, kpos < cand)
            return jnp.where(count(pred) <= want, cand, jb)
        jb = lax.fori_loop(0, n_bits, tie_step, jnp.zeros((1, QB), I32))

        def pack_tile(t, _):
            kc = keys_ref[t]
            kpos = t * KT + lax.broadcasted_iota(I32, (KT, QB), 0)
            chosen = jnp.logical_or(kc > thr, jnp.logical_and(kc == thr, kpos < jb))
            word = jnp.zeros((rows_per_word, QB), I32)
            for r in range(32):
                bit = np.int32(np.uint32(1 << r))
                word = word | jnp.where(chosen[r * rows_per_word:(r + 1) * rows_per_word, :], bit, 0)
            sel_ref[t] = jnp.where(select_all, -1, word)
            return 0
        lax.fori_loop(0, n_tiles, pack_tile, 0)

    m_ref[...] = jnp.full(m_ref.shape, NEG_BIG, F32)
    l_ref[...] = jnp.zeros(l_ref.shape, F32)
    acc_ref[...] = jnp.zeros(acc_ref.shape, F32)

    def masked_scores(t, dst_ref, dmax_ref):
        word = sel_ref[t]
        picked = jnp.concatenate([lax.shift_left(word, jnp.int32(31 - r)) for r in range(32)], axis=0)
        kpos = t * KT + lax.broadcasted_iota(I32, (KT, QB), 0)
        sel = jnp.logical_and(picked < 0, kpos <= qpos)
        bias = jnp.where(sel, 0.0, NEG_BIG)
        bias2 = jnp.concatenate([bias, bias], axis=1)
        kc = kb_ref[pl.ds(pl.multiple_of(t * KT, KT), KT), :]
        for p in range(ATTN_HEADS // 2):
            s = _dot_nt(kc, qs_ref[2 * p:2 * p + 2].reshape(2 * QB, LANES)) + bias2
            dst_ref[p] = s
            dmax_ref[p] = jnp.max(s, axis=0, keepdims=True)

    def softmax_pv(t, src_ref, smax_ref):
        for p in range(ATTN_HEADS // 2):
            g = (2 * p) // heads_per_group
            m_old = m_ref[p]
            m_new = jnp.maximum(m_old, smax_ref[p])
            alpha = jnp.exp2(m_old - m_new)
            pt = jnp.exp2(src_ref[p] - m_new).astype(BF16)
            m_ref[p] = m_new
            pv = _dot(vt_ref[t, g], pt)
            l_ref[p] = alpha * l_ref[p] + pv[hd:hd + 1, :]
            acc_ref[p] = alpha * acc_ref[p] + pv[0:hd, :]

    masked_scores(0, s0_ref, x0_ref)

    def attend(u, _):
        t = 2 * u
        masked_scores(t + 1, s1_ref, x1_ref)
        softmax_pv(t, s0_ref, x0_ref)
        masked_scores(t + 2, s0_ref, x0_ref)
        softmax_pv(t + 1, s1_ref, x1_ref)
        return 0
    lax.fori_loop(0, jnp.where(odd_tile, n_pairs, n_pairs - 1), attend, 0)

    @pl.when(odd_tile)
    def _():
        softmax_pv(n_tiles - 1, s0_ref, x0_ref)

    @pl.when(jnp.logical_not(odd_tile))
    def _():
        masked_scores(n_tiles - 1, s1_ref, x1_ref)
        softmax_pv(n_tiles - 2, s0_ref, x0_ref)
        softmax_pv(n_tiles - 1, s1_ref, x1_ref)

    for p in range(ATTN_HEADS // 2):
        o = acc_ref[p] / l_ref[p]
        both = jnp.concatenate([o[:, :QB], o[:, QB:]], axis=0)
        o_ref[:, p * LANES:(p + 1) * LANES] = both.T.astype(o_ref.dtype)


def _dsa_branch(main3d, misc3d):
    bsz, seq, _ = main3d.shape
    topk = min(TOPK_MAX, seq // 4)
    width = ATTN_HEADS * ATTN_HEAD_DIM
    nkt = seq // KEY_TILE
    n_pairs = ATTN_HEADS // 2
    kern = functools.partial(_dsa_kernel, topk=topk)
    return pl.pallas_call(
        kern,
        grid=(bsz, seq // Q_BLOCK),
        in_specs=[
            pl.BlockSpec((None, Q_BLOCK, width), lambda b, i: (b, i, Q_TILE)),
            pl.BlockSpec((None, Q_BLOCK, IDX_HEADS * IDX_DIM), lambda b, i: (b, i, COL_QI)),
            pl.BlockSpec((None, seq, LANES), lambda b, i: (b, 0, COL_K)),
            pl.BlockSpec((None, seq, LANES), lambda b, i: (b, 0, COL_V)),
            pl.BlockSpec((None, seq, LANES), lambda b, i: (b, 0, COL_KI)),
            pl.BlockSpec((None, Q_BLOCK, LANES), lambda b, i: (b, i, COL_DTWI)),
        ],
        out_specs=pl.BlockSpec((None, Q_BLOCK, width), lambda b, i: (b, i, 0)),
        out_shape=jax.ShapeDtypeStruct((bsz, seq, width), BF16),
        scratch_shapes=[
            pltpu.VMEM((seq, LANES), BF16),
            pltpu.VMEM((seq, LANES), BF16),
            pltpu.VMEM((nkt, ATTN_KV_HEADS, ATTN_HEAD_DIM + ONES_ROWS, KEY_TILE), BF16),
            pltpu.VMEM((ATTN_HEADS, Q_BLOCK, LANES), BF16),
            pltpu.VMEM((IDX_HEADS, Q_BLOCK, LANES), BF16),
            pltpu.VMEM((nkt, KEY_TILE, Q_BLOCK), I32),
            pltpu.VMEM((nkt, 32, KEY_TILE // 32, Q_BLOCK), I32),
            pltpu.VMEM((nkt, KEY_TILE // 32, Q_BLOCK), I32),
            pltpu.VMEM((nkt, KEY_TILE // 32, Q_BLOCK), I32),
            pltpu.VMEM((nkt, KEY_TILE // 32, Q_BLOCK), I32),
            pltpu.VMEM((n_pairs, KEY_TILE, 2 * Q_BLOCK), F32),
            pltpu.VMEM((n_pairs, KEY_TILE, 2 * Q_BLOCK), F32),
            pltpu.VMEM((n_pairs, 1, 2 * Q_BLOCK), F32),
            pltpu.VMEM((n_pairs, 1, 2 * Q_BLOCK), F32),
            pltpu.VMEM((n_pairs, 1, 2 * Q_BLOCK), F32),
            pltpu.VMEM((n_pairs, 1, 2 * Q_BLOCK), F32),
            pltpu.VMEM((n_pairs, ATTN_HEAD_DIM, 2 * Q_BLOCK), F32),
        ],
        compiler_params=_cparams(("parallel", "arbitrary")),
        name="dsa_attention",
    )(main3d, misc3d, misc3d, misc3d, misc3d, misc3d)


def _merge_kernel(h_ref, ys_ref, ya_ref, gs_ref, ga_ref, wso_ref, wao_ref, wo_ref, o_ref):
    a = _dot(ys_ref[...], wso_ref[...])
    b = _dot(ya_ref[...], wao_ref[...])
    merged = jax.nn.sigmoid(gs_ref[...].astype(F32)) * a + jax.nn.sigmoid(ga_ref[...].astype(F32)) * b
    o_ref[...] = h_ref[...] + _dot(merged.astype(BF16), wo_ref[...])


def _merge(h2d, y_ssd, y_attn, proj2d, wso, wao, wo):
    n, d = h2d.shape
    tm = 512
    full = lambda a: pl.BlockSpec(a.shape, lambda i: (0, 0))
    return pl.pallas_call(
        _merge_kernel,
        grid=(n // tm,),
        in_specs=[
            pl.BlockSpec((tm, d), lambda i: (i, 0)),
            pl.BlockSpec((tm, y_ssd.shape[1]), lambda i: (i, 0)),
            pl.BlockSpec((tm, y_attn.shape[1]), lambda i: (i, 0)),
            pl.BlockSpec((tm, d), lambda i: (i, 6)),
            pl.BlockSpec((tm, d), lambda i: (i, 7)),
            full(wso), full(wao), full(wo),
        ],
        out_specs=pl.BlockSpec((tm, d), lambda i: (i, 0)),
        out_shape=jax.ShapeDtypeStruct((n, d), F32),
        compiler_params=_cparams(("parallel",)),
        name="merge_out",
    )(h2d, y_ssd, y_attn, proj2d, proj2d, wso, wao, wo)


def _normproj_kernel(x_ref, g_ref, w_ref, o_ref):
    o_ref[...] = _dot(_rms(x_ref[...], g_ref[...]).astype(BF16), w_ref[...]).astype(o_ref.dtype)


def _norm_project(x2d, gain, w_bf16, out_dtype):
    n, d = x2d.shape
    tm = min(512, n)
    return pl.pallas_call(
        _normproj_kernel,
        grid=(n // tm,),
        in_specs=[pl.BlockSpec((tm, d), lambda i: (i, 0)), pl.BlockSpec((1, d), lambda i: (0, 0)),
                  pl.BlockSpec(w_bf16.shape, lambda i: (0, 0))],
        out_specs=pl.BlockSpec((tm, w_bf16.shape[1]), lambda i: (i, 0)),
        out_shape=jax.ShapeDtypeStruct((n, w_bf16.shape[1]), out_dtype),
        compiler_params=_cparams(("parallel",)),
        name="mem_kv_projection",
    )(x2d, gain.reshape(1, d), w_bf16)


def _xattn_kernel(h_ref, g_ref, kv_ref, wq_ref, wo_ref, o_ref):
    d = h_ref.shape[1]
    hd = d // XATTN_HEADS
    h = h_ref[...]
    q = _dot(_rms(h, g_ref[...]).astype(BF16), wq_ref[...]) * (hd ** -0.5)
    qb = q.astype(BF16)
    outs = []
    for a in range(XATTN_HEADS):
        k = kv_ref[:, a * hd:(a + 1) * hd]
        v = kv_ref[:, d + a * hd:d + (a + 1) * hd]
        s = _dot_nt(qb[:, a * hd:(a + 1) * hd], k)
        p = jnp.exp(s - jnp.max(s, axis=1, keepdims=True))
        o = _dot(p.astype(BF16), v) / jnp.sum(p, axis=1, keepdims=True)
        outs.append(o.astype(BF16))
    o_ref[...] = h + _dot(jnp.concatenate(outs, axis=1), wo_ref[...])


def _cross_attention(h3d, gain, kv3d, wq, wo):
    bsz, seq, d = h3d.shape
    tm = min(512, seq)
    mlen = kv3d.shape[1]
    return pl.pallas_call(
        _xattn_kernel,
        grid=(bsz, seq // tm),
        in_specs=[
            pl.BlockSpec((None, tm, d), lambda b, i: (b, i, 0)),
            pl.BlockSpec((1, d), lambda b, i: (0, 0)),
            pl.BlockSpec((None, mlen, 2 * d), lambda b, i: (b, 0, 0)),
            pl.BlockSpec(wq.shape, lambda b, i: (0, 0)),
            pl.BlockSpec(wo.shape, lambda b, i: (0, 0)),
        ],
        out_specs=pl.BlockSpec((None, tm, d), lambda b, i: (b, i, 0)),
        out_shape=jax.ShapeDtypeStruct((bsz, seq, d), F32),
        compiler_params=_cparams(("parallel", "parallel")),
        name="cross_attention",
    )(h3d, gain.reshape(1, d), kv3d, wq, wo)


def _router_kernel(h_ref, g_ref, wr_ref, br_ref, trit_ref, aug_ref, rank_ref, gsel_ref, cnt_ref, base_ref):
    i = pl.program_id(0)
    d = h_ref.shape[1]

    @pl.when(i == 0)
    def _():
        base_ref[...] = jnp.zeros(base_ref.shape, F32)

    hn = _rms(h_ref[...], g_ref[...])
    aug_ref[:, 0:d] = hn
    h_hi, h_mid, h_lo = _split3(hn)
    w_hi, w_mid, w_lo = wr_ref[0], wr_ref[1], wr_ref[2]
    logits = (_dot(h_hi, w_hi) + (_dot(h_hi, w_mid) + _dot(h_mid, w_hi))
              + (_dot(h_hi, w_lo) + _dot(h_mid, w_mid) + _dot(h_lo, w_hi))) + br_ref[...]
    tm = logits.shape[0]
    lane = lax.broadcasted_iota(I32, (tm, LANES), 1)
    neg_inf = -jnp.inf
    n_e = MOE_GROUPS * MOE_EXPERTS_PER_GROUP
    is_g = jnp.logical_and(lane >= n_e, lane < n_e + MOE_GROUPS)
    gl = jnp.where(is_g, logits, neg_inf)
    g_max = jnp.max(gl, axis=1, keepdims=True)
    g_sel = jnp.min(jnp.where(gl == g_max, lane, LANES), axis=1, keepdims=True) - n_e
    g_w = 1.0 / jnp.sum(jnp.where(is_g, jnp.exp(gl - g_max), 0.0), axis=1, keepdims=True)
    in_grp = jnp.logical_and(lane >= g_sel * MOE_EXPERTS_PER_GROUP,
                             lane < (g_sel + 1) * MOE_EXPERTS_PER_GROUP)
    el = jnp.where(in_grp, logits, neg_inf)
    v1 = jnp.max(el, axis=1, keepdims=True)
    i1 = jnp.min(jnp.where(el == v1, lane, LANES), axis=1, keepdims=True)
    el2 = jnp.where(lane == i1, neg_inf, el)
    v2 = jnp.max(el2, axis=1, keepdims=True)
    i2 = jnp.min(jnp.where(el2 == v2, lane, LANES), axis=1, keepdims=True)
    e2 = jnp.exp(v2 - v1)
    w1 = g_w / (1.0 + e2)
    w2 = g_w * e2 / (1.0 + e2)
    aug_ref[:, d:] = jnp.where(lane == i1, w1, 0.0) + jnp.where(lane == i2, w2, 0.0)

    onehot_t = jnp.where(lane == g_sel, 1.0, 0.0).T
    upto = _dot(onehot_t.astype(BF16), trit_ref[...]) + base_ref[...]
    grp = lax.broadcasted_iota(I32, onehot_t.shape, 0).astype(F32)
    rank_ref[...] = (jnp.sum(onehot_t * upto, axis=0, keepdims=True) - 1.0).astype(I32)
    gsel_ref[...] = jnp.sum(onehot_t * grp, axis=0, keepdims=True).astype(I32)
    base_ref[...] = upto[:, tm - 1:tm]
    cnt_ref[...] = jnp.broadcast_to(upto[:, tm - 1:tm], cnt_ref.shape)


def _moe_route(h2d, gain, wr3, br):
    n, d = h2d.shape
    tm = 512
    nt = n // tm
    trit = (jnp.arange(tm)[:, None] <= jnp.arange(tm)[None, :]).astype(BF16)
    return pl.pallas_call(
        _router_kernel,
        grid=(nt,),
        in_specs=[pl.BlockSpec((tm, d), lambda i: (i, 0)), pl.BlockSpec((1, d), lambda i: (0, 0)),
                  pl.BlockSpec(wr3.shape, lambda i: (0, 0, 0)), pl.BlockSpec((1, LANES), lambda i: (0, 0)),
                  pl.BlockSpec((tm, tm), lambda i: (0, 0))],
        out_specs=[pl.BlockSpec((tm, d + LANES), lambda i: (i, 0)),
                   pl.BlockSpec((None, 1, tm), lambda i: (i, 0, 0)),
                   pl.BlockSpec((None, 1, tm), lambda i: (i, 0, 0)),
                   pl.BlockSpec((LANES, LANES), lambda i: (0, 0))],
        out_shape=[jax.ShapeDtypeStruct((n, d + LANES), F32),
                   jax.ShapeDtypeStruct((nt, 1, tm), I32),
                   jax.ShapeDtypeStruct((nt, 1, tm), I32),
                   jax.ShapeDtypeStruct((LANES, LANES), F32)],
        scratch_shapes=[pltpu.VMEM((LANES, 1), F32)],
        compiler_params=_cparams(("arbitrary",)),
        name="moe_router",
    )(h2d, gain.reshape(1, d), wr3, br, trit)


MOE_TILE = 512
ROW_BLOCK = 256
ISSUE_UNROLL = 8


def _scatter_rows_kernel(dest_ref, src_ref, init_ref, out_ref, sem):
    del init_ref
    base = pl.program_id(0) * ROW_BLOCK

    def row_copy(r):
        return pltpu.make_async_copy(src_ref.at[pl.ds(r, 1)], out_ref.at[pl.ds(dest_ref[base + r], 1)], sem)

    def issue(r, _):
        row_copy(r).start()
        return 0
    lax.fori_loop(0, ROW_BLOCK, issue, 0, unroll=ISSUE_UNROLL)
    pltpu.make_async_copy(src_ref, out_ref.at[pl.ds(0, ROW_BLOCK)], sem).wait()


def _scatter_rows(src, dest, n_out):
    n, w = src.shape
    init = jnp.zeros((n_out, w), src.dtype)
    return pl.pallas_call(
        _scatter_rows_kernel,
        grid_spec=pltpu.PrefetchScalarGridSpec(
            num_scalar_prefetch=1,
            grid=(n // ROW_BLOCK,),
            in_specs=[pl.BlockSpec((ROW_BLOCK, w), lambda i, dest: (i, 0)),
                      pl.BlockSpec(memory_space=pl.ANY)],
            out_specs=pl.BlockSpec(memory_space=pl.ANY),
            scratch_shapes=[pltpu.SemaphoreType.DMA(())],
        ),
        out_shape=jax.ShapeDtypeStruct((n_out, w), src.dtype),
        input_output_aliases={2: 0},
        compiler_params=_cparams(("arbitrary",)),
        name="moe_sort_rows",
    )(dest, src, init)


def _sorted_experts_kernel(tile_group_ref, n_used_ref, hn_ref, cw_ref, wg_ref, wu_ref, wd_ref, o_ref):
    j = pl.program_id(0)

    @pl.when(j < n_used_ref[0])
    def _():
        hn = hn_ref[...].astype(BF16)
        first = tile_group_ref[j] * MOE_EXPERTS_PER_GROUP
        lane = lax.broadcasted_iota(I32, cw_ref.shape, 1)
        cw = cw_ref[...]
        acc = jnp.zeros(o_ref.shape, F32)
        for e in range(MOE_EXPERTS_PER_GROUP):
            w_e = jnp.sum(jnp.where(lane == first + e, cw, 0.0), axis=1, keepdims=True)
            hid = _silu(_dot(hn, wg_ref[e])) * _dot(hn, wu_ref[e]) * w_e
            acc = acc + _dot(hid.astype(BF16), wd_ref[e])
        o_ref[...] = acc

    @pl.when(j >= n_used_ref[0])
    def _():
        o_ref[...] = jnp.zeros(o_ref.shape, F32)


def _sorted_experts(aug_sorted, tile_group, n_used, wg, wu, wd):
    n_pad, wa = aug_sorted.shape
    d = wa - LANES
    _, n_e, _, dff = wg.shape
    tm = MOE_TILE
    return pl.pallas_call(
        _sorted_experts_kernel,
        grid_spec=pltpu.PrefetchScalarGridSpec(
            num_scalar_prefetch=2,
            grid=(n_pad // tm,),
            in_specs=[
                pl.BlockSpec((tm, d), lambda j, tg, nu: (j, 0)),
                pl.BlockSpec((tm, LANES), lambda j, tg, nu: (j, d // LANES)),
                pl.BlockSpec((None, n_e, d, dff), lambda j, tg, nu: (tg[j], 0, 0, 0)),
                pl.BlockSpec((None, n_e, d, dff), lambda j, tg, nu: (tg[j], 0, 0, 0)),
                pl.BlockSpec((None, n_e, dff, d), lambda j, tg, nu: (tg[j], 0, 0, 0)),
            ],
            out_specs=pl.BlockSpec((tm, d), lambda j, tg, nu: (j, 0)),
        ),
        out_shape=jax.ShapeDtypeStruct((n_pad, d), F32),
        compiler_params=_cparams(("arbitrary",)),
        name="moe_experts",
    )(tile_group, n_used, aug_sorted, aug_sorted, wg, wu, wd)


def _gather_add_kernel(dest_ref, h_ref, delta_ref, fg_ref, o_ref, buf_ref, sem, *, final_norm):
    i = pl.program_id(0)
    n_steps = pl.num_programs(0)

    def row_copy(step, r):
        slot = lax.rem(step, 2)
        return pltpu.make_async_copy(delta_ref.at[pl.ds(dest_ref[step * ROW_BLOCK + r], 1)],
                                     buf_ref.at[slot, pl.ds(r, 1)], sem.at[slot])

    def issue_all(step):
        def issue(r, _):
            row_copy(step, r).start()
            return 0
        lax.fori_loop(0, ROW_BLOCK, issue, 0, unroll=ISSUE_UNROLL)

    @pl.when(i == 0)
    def _():
        issue_all(0)

    @pl.when(i + 1 < n_steps)
    def _():
        issue_all(i + 1)

    slot = lax.rem(i, 2)
    pltpu.make_async_copy(delta_ref.at[pl.ds(0, ROW_BLOCK)], buf_ref.at[slot], sem.at[slot]).wait()

    out = h_ref[...] + buf_ref[slot]
    if final_norm:
        out = _rms(out, fg_ref[...])
    o_ref[...] = out


def _gather_add(h2d, delta_sorted, dest, final_gain, final_norm):
    n, d = h2d.shape
    kern = functools.partial(_gather_add_kernel, final_norm=final_norm)
    return pl.pallas_call(
        kern,
        grid_spec=pltpu.PrefetchScalarGridSpec(
            num_scalar_prefetch=1,
            grid=(n // ROW_BLOCK,),
            in_specs=[pl.BlockSpec((ROW_BLOCK, d), lambda i, dest: (i, 0)),
                      pl.BlockSpec(memory_space=pl.ANY),
                      pl.BlockSpec((1, d), lambda i, dest: (0, 0))],
            out_specs=pl.BlockSpec((ROW_BLOCK, d), lambda i, dest: (i, 0)),
            scratch_shapes=[pltpu.VMEM((2, ROW_BLOCK, d), F32), pltpu.SemaphoreType.DMA((2,))],
        ),
        out_shape=jax.ShapeDtypeStruct((n, d), F32),
        compiler_params=_cparams(("arbitrary",)),
        name="moe_unsort_add",
    )(dest, h2d, delta_sorted, final_gain.reshape(1, d))


def _moe_layout(rank, gsel, counts, n):
    tm = MOE_TILE
    cnt = counts[:MOE_GROUPS, 0].astype(I32)
    seg = ((cnt + tm - 1) // tm) * tm
    ends = jnp.cumsum(seg)
    dest = (ends - seg)[gsel.reshape(n)] + rank.reshape(n)
    n_tiles = n // tm + MOE_GROUPS
    starts = jnp.arange(n_tiles, dtype=I32) * tm
    tile_group = jnp.minimum(jnp.sum(starts[:, None] >= ends[None, :], axis=1), MOE_GROUPS - 1).astype(I32)
    n_used = (ends[-1:] // tm).astype(I32)
    return dest.astype(I32), tile_group, n_used, n_tiles * tm


def _prep_router(w_group, b_group, w_router, b_router):
    w = jnp.concatenate([w_router, w_group], axis=1).astype(F32)
    w = jnp.pad(w, ((0, 0), (0, LANES - w.shape[1])))
    hi = w.astype(BF16)
    r1 = w - hi.astype(F32)
    mid = r1.astype(BF16)
    lo = (r1 - mid.astype(F32)).astype(BF16)
    b = jnp.concatenate([b_router, b_group]).astype(F32)
    b = jnp.pad(b, (0, LANES - b.shape[0])).reshape(1, LANES)
    return jnp.stack([hi, mid, lo]), b


def kernel(x, mem, norm_mix, w_in, conv_w, conv_b, dt_bias, a_log, d_skip, ssd_norm, w_ssd_o, w_attn_o, w_out,
           norm_xattn, norm_mem, w_cq, w_ckv, w_co, norm_ffn, w_group, b_group, w_router, b_router,
           w_gate_e, w_up_e, w_down_e, norm_final):
    bsz, seq, d = x.shape
    depth = w_in.shape[0]
    mlen = mem.shape[1]
    heads = dt_bias.shape[1]
    d_inner = heads * SSD_HEAD_DIM
    conv_ch = conv_w.shape[2]
    n = bsz * seq
    rope_c, rope_s = _rope_tables(seq)
    h = x.reshape(n, d)
    mem2d = mem.reshape(bsz * mlen, d)
    for l in range(depth):
        w_in_l = _prep_w_in(w_in[l], d_inner, conv_ch, heads)
        main, misc = _in_projection(h, norm_mix[l], w_in_l, rope_c, rope_s, seq)
        main3d = main.reshape(bsz, seq, main.shape[1])
        misc3d = misc.reshape(bsz, seq, misc.shape[1])
        y_ssd = _ssd_branch(main3d, misc3d, conv_w[l], conv_b[l], dt_bias[l], a_log[l], d_skip[l], ssd_norm[l])
        y_attn = _dsa_branch(main3d, misc3d)
        h = _merge(h, y_ssd.reshape(n, d_inner), y_attn.reshape(n, -1), main,
                   w_ssd_o[l].astype(BF16), w_attn_o[l].astype(BF16), w_out[l].astype(BF16))
        kv = _norm_project(mem2d, norm_mem[l], w_ckv[l].astype(BF16), BF16)
        h = _cross_attention(h.reshape(bsz, seq, d), norm_xattn[l], kv.reshape(bsz, mlen, 2 * d),
                             w_cq[l].astype(BF16), w_co[l].astype(BF16)).reshape(n, d)
        wr3, br = _prep_router(w_group[l], b_group[l], w_router[l], b_router[l])
        aug, rank, gsel, counts = _moe_route(h, norm_ffn[l], wr3, br)
        dest, tile_group, n_used, n_pad = _moe_layout(rank, gsel, counts, n)
        aug_sorted = _scatter_rows(aug, dest, n_pad)
        delta = _sorted_experts(aug_sorted, tile_group, n_used, w_gate_e[l].astype(BF16),
                                w_up_e[l].astype(BF16), w_down_e[l].astype(BF16))
        h = _gather_add(h, delta, dest, norm_final, final_norm=(l == depth - 1))
    return h.reshape(bsz, seq, d)
```

```python
import functools
import math

import jax
import jax.numpy as jnp
import numpy as np
from jax import lax
from jax.experimental import pallas as pl
from jax.experimental.pallas import tpu as pltpu

F32 = jnp.float32
BF16 = jnp.bfloat16
I32 = jnp.int32

SSD_HEAD_DIM = 64
SSD_GROUPS = 4
SSD_STATE = 128
SSD_CONV = 4
SSD_CHUNK = 128
SSD_SEQS_PER_STEP = 2
ATTN_HEADS = 16
ATTN_KV_HEADS = 2
ATTN_HEAD_DIM = 64
Q_BLOCK = 128
TOPK_MAX = 256
IDX_HEADS = 8
IDX_DIM = 64
ROPE_THETA = 10000.0
XATTN_HEADS = 4
MOE_GROUPS = 4
MOE_EXPERTS_PER_GROUP = 8
RMS_EPS = 1e-6

LANES = 128
SUBLANES = 8
KEY_TILE = 256
INT_MIN = -2 ** 31
NEG_BIG = -1e30
LOG2E = math.log2(math.e)
ONES_ROWS = 16
VMEM_LIMIT = 56 * 1024 * 1024


def _cparams(sem):
    return pltpu.CompilerParams(dimension_semantics=sem, vmem_limit_bytes=VMEM_LIMIT)


def _split3(v):
    hi = v.astype(BF16)
    r1 = v - hi.astype(F32)
    mid = r1.astype(BF16)
    lo = (r1 - mid.astype(F32)).astype(BF16)
    return hi, mid, lo


def _dot(a, b):
    return jnp.dot(a, b, preferred_element_type=F32)


def _dot_nt(a, b):
    return lax.dot_general(a, b, (((1,), (1,)), ((), ())), preferred_element_type=F32)


def _spread(v, m01, terms):
    parts = _split3(v)[:terms]
    out = _dot(parts[0], m01)
    for part in parts[1:]:
        out = out + _dot(part, m01)
    return out


def _dot3_lhs01(m01, v):
    hi, mid, lo = _split3(v)
    return _dot(m01, hi) + _dot(m01, mid) + _dot(m01, lo)


def _rms(x, g):
    ms = jnp.mean(x * x, axis=-1, keepdims=True)
    return x * lax.rsqrt(ms + RMS_EPS) * g


def _silu(x):
    return x * jax.nn.sigmoid(x)


Q_TILE = 5
MISC_TILE = 8
MISC_ROPE_CHUNKS = (0, 1, 2, 3, 4, 6)
COL_QI = 0
COL_K, COL_V, COL_KI, COL_DTWI = 4, 5, 6, 7
DT_LANES = 32


def _rope_chunk(a, c, s):
    return a * c + pltpu.roll(a, LANES // 2, 1) * s


def _inproj_kernel(x_ref, g_ref, w_ref, c_ref, s_ref, o_ref, misc_ref, xn_ref):
    j = pl.program_id(1)

    @pl.when(j == 0)
    def _():
        xn_ref[...] = _rms(x_ref[...], g_ref[...]).astype(BF16)

    acc = _dot(xn_ref[...], w_ref[...])
    n_chunks = acc.shape[1] // LANES

    def roped(dst_ref, chunks):
        c, sn = c_ref[...], s_ref[...]
        for k in range(n_chunks):
            a = acc[:, k * LANES:(k + 1) * LANES]
            if k in chunks:
                a = _rope_chunk(a, c, sn)
            dst_ref[:, k * LANES:(k + 1) * LANES] = a.astype(dst_ref.dtype)

    @pl.when(j == Q_TILE)
    def _():
        roped(o_ref, tuple(range(n_chunks)))

    @pl.when(j == MISC_TILE)
    def _():
        roped(misc_ref, MISC_ROPE_CHUNKS)

    @pl.when(jnp.logical_and(j != Q_TILE, j != MISC_TILE))
    def _():
        o_ref[...] = acc.astype(o_ref.dtype)


def _in_projection(x2d, gain, w_bf16, rope_c, rope_s, seq):
    n, d = x2d.shape
    tm = min(1024, seq)
    tn = 1024
    pos_tiles = seq // tm
    n_tiles = w_bf16.shape[1] // tn
    assert n_tiles == MISC_TILE + 1
    return pl.pallas_call(
        _inproj_kernel,
        grid=(n // tm, n_tiles),
        in_specs=[
            pl.BlockSpec((tm, d), lambda i, j: (i, 0)),
            pl.BlockSpec((1, d), lambda i, j: (0, 0)),
            pl.BlockSpec((d, tn), lambda i, j: (0, j)),
            pl.BlockSpec((tm, LANES), lambda i, j: (i % pos_tiles, 0)),
            pl.BlockSpec((tm, LANES), lambda i, j: (i % pos_tiles, 0)),
        ],
        out_specs=[pl.BlockSpec((tm, tn), lambda i, j: (i, jnp.minimum(j, MISC_TILE - 1))),
                   pl.BlockSpec((tm, tn), lambda i, j: (i, 0))],
        out_shape=[jax.ShapeDtypeStruct((n, MISC_TILE * tn), BF16), jax.ShapeDtypeStruct((n, tn), F32)],
        scratch_shapes=[pltpu.VMEM((tm, d), BF16)],
        compiler_params=_cparams(("parallel", "arbitrary")),
        name="in_projection",
    )(x2d, gain.reshape(1, d), w_bf16, rope_c, rope_s)


def _pair_halves(w):
    d = w.shape[0]
    half = ATTN_HEAD_DIM // 2
    return w.reshape(d, -1, 2, 2, half).transpose(0, 1, 3, 2, 4).reshape(d, -1)


def _prep_w_in(w, d_inner, conv_ch, ssd_heads):
    widths = (d_inner, conv_ch, ssd_heads, ATTN_HEADS * ATTN_HEAD_DIM, ATTN_KV_HEADS * ATTN_HEAD_DIM,
              ATTN_KV_HEADS * ATTN_HEAD_DIM, IDX_HEADS * IDX_DIM, IDX_DIM, IDX_HEADS,
              w.shape[0], w.shape[0])
    splits = [int(p) for p in np.cumsum(widths)[:-1]]
    z, xbc, dt, q, k, v, qi, ki, wi, gs, ga = jnp.split(w, splits, axis=1)

    def pad(a, n):
        return jnp.pad(a, ((0, 0), (0, n - a.shape[1])))

    q, k, qi, ki = _pair_halves(q), _pair_halves(k), _pair_halves(qi), _pair_halves(pad(ki, LANES))
    misc = jnp.concatenate([qi, k, v, ki, pad(jnp.concatenate([dt, wi], axis=1), LANES)], axis=1)
    return jnp.concatenate([z, xbc, q, gs, ga, misc], axis=1).astype(BF16)


def _rope_tables(seq):
    inv = 1.0 / (ROPE_THETA ** (jnp.arange(0, ATTN_HEAD_DIM, 2, dtype=F32) / ATTN_HEAD_DIM))
    ang = jnp.arange(seq, dtype=F32)[:, None] * inv[None, :]
    cos, sin = jnp.cos(ang), jnp.sin(ang)
    c = jnp.concatenate([cos, cos, cos, cos], axis=1)
    s = jnp.concatenate([-sin, -sin, sin, sin], axis=1)
    return c, s


def _ssd_chunk(z_ref, xs_ref, bc_ref, dt_ref, cw_ref, cb_ref, dtb_ref, alog_ref, dsk_ref, ng_ref,
               r64_ref, tri_ref, shift_ref, y_ref, prev_ref, st_ref, act_ref, heads, groups):
    L = SSD_CHUNK
    P = SSD_HEAD_DIM
    hpg = heads // groups
    d_inner = heads * P
    gw = hpg * P
    nst = SSD_STATE

    def conv(cur_ref, lo, hi):
        cur = cur_ref[...]
        both = jnp.concatenate([prev_ref[:, lo:hi], cur], axis=0)
        acc = cb_ref[:, lo:hi] + cw_ref[SSD_CONV - 1:SSD_CONV, lo:hi] * cur.astype(F32)
        for k in range(1, SSD_CONV):
            acc = acc + cw_ref[SSD_CONV - 1 - k:SSD_CONV - k, lo:hi] * _dot(shift_ref[k - 1], both)
        prev_ref[:, lo:hi] = cur
        return _silu(acc)

    xs = conv(xs_ref, 0, d_inner)
    bc = conv(bc_ref, d_inner, prev_ref.shape[1])
    bmat = bc[:, :groups * nst]
    cmat = bc[:, groups * nst:]

    dt = jax.nn.softplus(dt_ref[...] + dtb_ref[...])
    da = dt * (-jnp.exp(alog_ref[...]))
    tri = tri_ref[...]
    acum = _dot3_lhs01(tri, da)
    act_ref[...] = acum.T
    e_acum = jnp.exp(acum)
    e_tail = jnp.exp(acum[L - 1:L, :] - acum)
    r64 = r64_ref[...]
    e64 = _spread(e_acum, r64, 2)
    t64 = _spread(e_tail, r64, 1)
    d64 = _spread(dt, r64, 1)
    xdt = xs * d64
    xdt_b = xdt.astype(BF16)
    xtl_b = (xdt * t64).astype(BF16)

    row = lax.broadcasted_iota(I32, (L, L), 0)
    col = lax.broadcasted_iota(I32, (L, L), 1)
    causal = row >= col
    first_head = lax.broadcasted_iota(I32, (1, 2 * P), 1) < P

    for g in range(groups):
        bg = bmat[:, g * nst:(g + 1) * nst]
        cg_b = cmat[:, g * nst:(g + 1) * nst].astype(BF16)
        cb = _dot_nt(cg_b, bg.astype(BF16))
        sg = st_ref[g]
        y_off = _dot(cg_b, sg.astype(BF16))
        y_heads = []
        for hh in range(0, hpg, 2):
            pair = []
            for h in (g * hpg + hh, g * hpg + hh + 1):
                seg = jnp.broadcast_to(acum[:, h:h + 1], (L, LANES)) - act_ref[h:h + 1, :]
                pair.append((cb * jnp.where(causal, jnp.exp(seg), 0.0)).astype(BF16))
            slab = xdt_b[:, (g * hpg + hh) * P:(g * hpg + hh + 2) * P]
            stacked = jnp.concatenate([jnp.where(first_head, slab, 0), jnp.where(first_head, 0, slab)], axis=0)
            y_heads.append(_dot(jnp.concatenate(pair, axis=1), stacked))
        lo, hi = g * gw, (g + 1) * gw
        y_g = jnp.concatenate(y_heads, axis=1) + y_off * e64[:, lo:hi]
        st_ref[g] = sg * e64[L - 1:L, lo:hi] + _dot(bg.T.astype(BF16), xtl_b[:, lo:hi])
        y_g = (y_g + dsk_ref[:, lo:hi] * xs[:, lo:hi]) * _silu(z_ref[:, lo:hi].astype(F32))
        ms = jnp.mean(y_g * y_g, axis=-1, keepdims=True)
        y_ref[:, lo:hi] = (y_g * lax.rsqrt(ms + RMS_EPS) * ng_ref[:, lo:hi]).astype(y_ref.dtype)


def _ssd_kernel(z_ref, xs_ref, bc_ref, dt_ref, cw_ref, cb_ref, dtb_ref, alog_ref, dsk_ref, ng_ref,
                r64_ref, tri_ref, shift_ref, y_ref, prev_ref, st_ref, act_ref, *, heads, groups):
    @pl.when(pl.program_id(1) == 0)
    def _():
        prev_ref[...] = jnp.zeros(prev_ref.shape, BF16)
        st_ref[...] = jnp.zeros(st_ref.shape, F32)

    for nb in range(z_ref.shape[0]):
        _ssd_chunk(z_ref.at[nb], xs_ref.at[nb], bc_ref.at[nb], dt_ref.at[nb], cw_ref, cb_ref, dtb_ref,
                   alog_ref, dsk_ref, ng_ref, r64_ref, tri_ref, shift_ref, y_ref.at[nb],
                   prev_ref.at[nb], st_ref.at[nb], act_ref.at[nb], heads, groups)


def _ssd_branch(main3d, misc3d, conv_w, conv_b, dt_bias, a_log, d_skip, ssd_norm):
    bsz, seq, _ = main3d.shape
    heads = dt_bias.shape[0]
    d_inner = heads * SSD_HEAD_DIM
    groups = SSD_GROUPS
    conv_ch = conv_w.shape[1]
    bc_w = conv_ch - d_inner
    L = SSD_CHUNK
    nb = SSD_SEQS_PER_STEP if bsz % SSD_SEQS_PER_STEP == 0 else 1

    def pad_lanes(v):
        return jnp.pad(v.astype(F32), (0, LANES - v.shape[0])).reshape(1, LANES)

    hidx = jnp.arange(LANES)[:, None]
    r64 = (hidx == (jnp.arange(d_inner) // SSD_HEAD_DIM)[None, :]).astype(BF16)
    tri = (jnp.arange(L)[:, None] >= jnp.arange(L)[None, :]).astype(BF16)
    taps = jnp.arange(1, SSD_CONV)[:, None, None]
    shift = (jnp.arange(2 * L)[None, None, :] == (L + jnp.arange(L)[None, :, None] - taps)).astype(BF16)
    dsk = jnp.repeat(d_skip.astype(F32), SSD_HEAD_DIM).reshape(1, d_inner)
    const = lambda shape: pl.BlockSpec(shape, lambda b, c: (0,) * len(shape))
    kern = functools.partial(_ssd_kernel, heads=heads, groups=groups)
    return pl.pallas_call(
        kern,
        grid=(bsz // nb, seq // L),
        in_specs=[
            pl.BlockSpec((nb, L, d_inner), lambda b, c: (b, c, 0)),
            pl.BlockSpec((nb, L, d_inner), lambda b, c: (b, c, 1)),
            pl.BlockSpec((nb, L, bc_w), lambda b, c: (b, c, 2 * d_inner // bc_w)),
            pl.BlockSpec((nb, L, LANES), lambda b, c: (b, c, COL_DTWI)),
            const((SSD_CONV, conv_ch)), const((1, conv_ch)), const((1, LANES)), const((1, LANES)),
            const((1, d_inner)), const((1, d_inner)),
            const((LANES, d_inner)), const((L, L)),
            const((SSD_CONV - 1, L, 2 * L)),
        ],
        out_specs=pl.BlockSpec((nb, L, d_inner), lambda b, c: (b, c, 0)),
        out_shape=jax.ShapeDtypeStruct((bsz, seq, d_inner), BF16),
        scratch_shapes=[
            pltpu.VMEM((nb, L, conv_ch), BF16),
            pltpu.VMEM((nb, groups, SSD_STATE, d_inner // groups), F32),
            pltpu.VMEM((nb, LANES, L), F32),
        ],
        compiler_params=_cparams(("parallel", "arbitrary")),
        name="ssd_scan",
    )(main3d, main3d, main3d, misc3d, conv_w.astype(F32), conv_b.reshape(1, conv_ch).astype(F32),
      pad_lanes(dt_bias), pad_lanes(a_log), dsk, ssd_norm.reshape(1, d_inner).astype(F32), r64, tri, shift)


def _sortable_key(x):
    x = jnp.where(x == 0.0, 0.0, x)
    b = pltpu.bitcast(x, I32)
    return b ^ (lax.shift_right_arithmetic(b, 31) & 0x7FFFFFFF)


_BIT_MASKS = (0x0000FFFF, 0x00FF00FF, 0x0F0F0F0F, 0x33333333, 0x55555555)


def _bit_planes(rows):
    x = list(rows[::-1])
    j = 16
    for m in _BIT_MASKS:
        mask = np.int32(np.uint32(m))
        k = 0
        while k < 32:
            t = (x[k] ^ lax.shift_right_logical(x[k + j], jnp.int32(j))) & mask
            x[k] = x[k] ^ t
            x[k + j] = x[k + j] ^ lax.shift_left(t, jnp.int32(j))
            k = (k + j + 1) & ~j
        j >>= 1
    return x


def _dsa_kernel(q_ref, qi_ref, k_ref, v_ref, ki_ref, wi_ref, o_ref,
                kb_ref, kib_ref, vt_ref, qs_ref, qis_ref, keys_ref, planes_ref, sel_ref, cand_ref, above_ref,
                s0_ref, s1_ref, x0_ref, x1_ref, m_ref, l_ref, acc_ref, *, topk):
    i = pl.program_id(1)
    QB, KT = Q_BLOCK, KEY_TILE
    hd = ATTN_HEAD_DIM
    seq = k_ref.shape[0]
    n_tiles = (i * QB + QB + KT - 1) // KT
    n_pairs = n_tiles // 2
    odd_tile = n_tiles % 2 == 1
    lane = lax.broadcasted_iota(I32, (1, LANES), 1)
    quarter = hd // 2
    first = (lane // quarter) % 2 == 0

    @pl.when(i == 0)
    def _():
        def fill(t, _):
            rows = pl.ds(pl.multiple_of(t * KT, KT), KT)
            kb_ref[rows, :] = k_ref[rows, :].astype(BF16)
            kib_ref[rows, :] = ki_ref[rows, :].astype(BF16)
            vt = v_ref[rows, :].T.astype(BF16)
            for g in range(ATTN_KV_HEADS):
                vt_ref[t, g, 0:hd, :] = vt[g * hd:(g + 1) * hd, :]
                vt_ref[t, g, hd:hd + ONES_ROWS, :] = jnp.ones((ONES_ROWS, KT), BF16)
            return 0
        lax.fori_loop(0, seq // KT, fill, 0)
        planes_ref[...] = jnp.zeros(planes_ref.shape, I32)

    qpos = i * QB + lax.broadcasted_iota(I32, (1, QB), 1)
    idx_scale = (IDX_DIM ** -0.5) * (IDX_HEADS ** -0.5)
    wt = (wi_ref[...] * idx_scale).T
    w_rows = [wt[DT_LANES + h:DT_LANES + h + 1, :] for h in range(IDX_HEADS)]

    heads_per_group = ATTN_HEADS // ATTN_KV_HEADS
    for p in range(ATTN_HEADS // 2):
        slab = q_ref[:, p * LANES:(p + 1) * LANES].astype(F32) * ((hd ** -0.5) * LOG2E)
        if (2 * p) // heads_per_group == 0:
            even = jnp.where(first, slab, 0.0)
            odd = jnp.where(first, pltpu.roll(slab, LANES - quarter, 1), 0.0)
        else:
            even = jnp.where(first, 0.0, pltpu.roll(slab, quarter, 1))
            odd = jnp.where(first, 0.0, slab)
        qs_ref[2 * p] = even.astype(BF16)
        qs_ref[2 * p + 1] = odd.astype(BF16)
    for p in range(IDX_HEADS // 2):
        slab = qi_ref[:, p * LANES:(p + 1) * LANES]
        qis_ref[2 * p] = slab.astype(BF16)
        qis_ref[2 * p + 1] = pltpu.roll(slab, LANES - quarter, 1).astype(BF16)

    nkt = seq // KT
    rows_per_word = KT // 32
    assert rows_per_word == SUBLANES

    def score_tile(t):
        kic = kib_ref[pl.ds(pl.multiple_of(t * KT, KT), KT), :]
        acc = jnp.zeros((KT, QB), F32)
        for p in range(IDX_HEADS // 2):
            rel = _dot_nt(kic, qis_ref[2 * p:2 * p + 2].reshape(2 * QB, LANES))
            acc = acc + w_rows[2 * p] * jnp.maximum(rel[:, :QB], 0.0)
            acc = acc + w_rows[2 * p + 1] * jnp.maximum(rel[:, QB:], 0.0)
        kpos = t * KT + lax.broadcasted_iota(I32, (KT, QB), 0)
        keys = jnp.where(kpos <= qpos, _sortable_key(acc), INT_MIN)
        keys_ref[t] = keys
        unsigned = keys ^ INT_MIN
        planes = _bit_planes([unsigned[r * rows_per_word:(r + 1) * rows_per_word, :] for r in range(32)])
        for bit in range(32):
            planes_ref[t, bit] = planes[bit]

    def score_pair(u, _):
        score_tile(2 * u)
        score_tile(2 * u + 1)
        return 0
    lax.fori_loop(0, n_pairs, score_pair, 0)

    @pl.when(odd_tile)
    def _():
        score_tile(n_tiles - 1)

    select_all = qpos < topk
    for t in range(nkt):
        cand_ref[t] = jnp.where(t < n_tiles, jnp.full((rows_per_word, QB), -1, I32), 0)
        above_ref[t] = jnp.zeros((rows_per_word, QB), I32)

    def radix_step(it, need):
        cnt = jnp.zeros((rows_per_word, QB), I32)
        for t in range(nkt):
            cnt = cnt + lax.population_count(cand_ref[t] & planes_ref[t, it])
        cnt = jnp.sum(cnt, axis=0, keepdims=True)
        take = cnt >= need
        for t in range(nkt):
            cand = cand_ref[t]
            ones = cand & planes_ref[t, it]
            cand_ref[t] = jnp.where(take, ones, cand ^ ones)
            above_ref[t] = jnp.where(take, above_ref[t], above_ref[t] | ones)
        return jnp.where(take, need, need - cnt)
    need = lax.fori_loop(0, 32, radix_step, jnp.full((1, QB), topk, I32), unroll=4)

    ties = jnp.zeros((rows_per_word, QB), I32)
    for t in range(nkt):
        ties = ties + lax.population_count(cand_ref[t])
        sel_ref[t] = jnp.where(select_all, -1, above_ref[t] | cand_ref[t])
    surplus = jnp.where(select_all, 0, jnp.sum(ties, axis=0, keepdims=True) - need)

    @pl.when(jnp.max(surplus) > 0)
    def _():
        def count(pred):
            def body(t, acc):
                hit = jnp.where(pred(keys_ref[t], t), 1, 0)
                return acc + jnp.sum(hit.reshape(KT // SUBLANES, SUBLANES, QB), axis=0)
            acc = lax.fori_loop(0, n_tiles, body, jnp.zeros((SUBLANES, QB), I32))
            return jnp.sum(acc, axis=0, keepdims=True)

        def bit_step(it, thr):
            cand = thr ^ lax.shift_left(jnp.int32(1), 31 - it)
            cnt = count(lambda kc, c: kc >= cand)
            return jnp.where(cnt >= topk, cand, thr)
        thr = lax.fori_loop(0, 32, bit_step, jnp.full((1, QB), INT_MIN, I32))
        want = topk - count(lambda kc, c: kc > thr)
        n_bits = seq.bit_length()

        def tie_step(it, jb):
            cand = jb | lax.shift_left(jnp.int32(1), n_bits - 1 - it)
            def pred(kc, c):
                kpos = c * KT + lax.broadcasted_iota(I32, (KT, QB), 0)
                return jnp.logical_and(kc == thr, kpos < cand)
            return jnp.where(count(pred) <= want, cand, jb)
        jb = lax.fori_loop(0, n_bits, tie_step, jnp.zeros((1, QB), I32))

        def pack_tile(t, _):
            kc = keys_ref[t]
            kpos = t * KT + lax.broadcasted_iota(I32, (KT, QB), 0)
            chosen = jnp.logical_or(kc > thr, jnp.logical_and(kc == thr, kpos < jb))
            word = jnp.zeros((rows_per_word, QB), I32)
            for r in range(32):
                bit = np.int32(np.uint32(1 << r))
                word = word | jnp.where(chosen[r * rows_per_word:(r + 1) * rows_per_word, :], bit, 0)
            sel_ref[t] = jnp.where(select_all, -1, word)
            return 0
        lax.fori_loop(0, n_tiles, pack_tile, 0)

    m_ref[...] = jnp.full(m_ref.shape, NEG_BIG, F32)
    l_ref[...] = jnp.zeros(l_ref.shape, F32)
    acc_ref[...] = jnp.zeros(acc_ref.shape, F32)

    def masked_scores(t, dst_ref, dmax_ref):
        word = sel_ref[t]
        picked = jnp.concatenate([lax.shift_left(word, jnp.int32(31 - r)) for r in range(32)], axis=0)
        kpos = t * KT + lax.broadcasted_iota(I32, (KT, QB), 0)
        sel = jnp.logical_and(picked < 0, kpos <= qpos)
        bias = jnp.where(sel, 0.0, NEG_BIG)
        bias2 = jnp.concatenate([bias, bias], axis=1)
        kc = kb_ref[pl.ds(pl.multiple_of(t * KT, KT), KT), :]
        for p in range(ATTN_HEADS // 2):
            s = _dot_nt(kc, qs_ref[2 * p:2 * p + 2].reshape(2 * QB, LANES)) + bias2
            dst_ref[p] = s
            dmax_ref[p] = jnp.max(s, axis=0, keepdims=True)

    def softmax_pv(t, src_ref, smax_ref):
        for p in range(ATTN_HEADS // 2):
            g = (2 * p) // heads_per_group
            m_old = m_ref[p]
            m_new = jnp.maximum(m_old, smax_ref[p])
            alpha = jnp.exp2(m_old - m_new)
            pt = jnp.exp2(src_ref[p] - m_new).astype(BF16)
            m_ref[p] = m_new
            pv = _dot(vt_ref[t, g], pt)
            l_ref[p] = alpha * l_ref[p] + pv[hd:hd + 1, :]
            acc_ref[p] = alpha * acc_ref[p] + pv[0:hd, :]

    masked_scores(0, s0_ref, x0_ref)

    def attend(u, _):
        t = 2 * u
        masked_scores(t + 1, s1_ref, x1_ref)
        softmax_pv(t, s0_ref, x0_ref)
        masked_scores(t + 2, s0_ref, x0_ref)
        softmax_pv(t + 1, s1_ref, x1_ref)
        return 0
    lax.fori_loop(0, jnp.where(odd_tile, n_pairs, n_pairs - 1), attend, 0)

    @pl.when(odd_tile)
    def _():
        softmax_pv(n_tiles - 1, s0_ref, x0_ref)

    @pl.when(jnp.logical_not(odd_tile))
    def _():
        masked_scores(n_tiles - 1, s1_ref, x1_ref)
        softmax_pv(n_tiles - 2, s0_ref, x0_ref)
        softmax_pv(n_tiles - 1, s1_ref, x1_ref)

    for p in range(ATTN_HEADS // 2):
        o = acc_ref[p] / l_ref[p]
        both = jnp.concatenate([o[:, :QB], o[:, QB:]], axis=0)
        o_ref[:, p * LANES:(p + 1) * LANES] = both.T.astype(o_ref.dtype)


def _dsa_branch(main3d, misc3d):
    bsz, seq, _ = main3d.shape
    topk = min(TOPK_MAX, seq // 4)
    width = ATTN_HEADS * ATTN_HEAD_DIM
    nkt = seq // KEY_TILE
    n_pairs = ATTN_HEADS // 2
    kern = functools.partial(_dsa_kernel, topk=topk)
    return pl.pallas_call(
        kern,
        grid=(bsz, seq // Q_BLOCK),
        in_specs=[
            pl.BlockSpec((None, Q_BLOCK, width), lambda b, i: (b, i, Q_TILE)),
            pl.BlockSpec((None, Q_BLOCK, IDX_HEADS * IDX_DIM), lambda b, i: (b, i, COL_QI)),
            pl.BlockSpec((None, seq, LANES), lambda b, i: (b, 0, COL_K)),
            pl.BlockSpec((None, seq, LANES), lambda b, i: (b, 0, COL_V)),
            pl.BlockSpec((None, seq, LANES), lambda b, i: (b, 0, COL_KI)),
            pl.BlockSpec((None, Q_BLOCK, LANES), lambda b, i: (b, i, COL_DTWI)),
        ],
        out_specs=pl.BlockSpec((None, Q_BLOCK, width), lambda b, i: (b, i, 0)),
        out_shape=jax.ShapeDtypeStruct((bsz, seq, width), BF16),
        scratch_shapes=[
            pltpu.VMEM((seq, LANES), BF16),
            pltpu.VMEM((seq, LANES), BF16),
            pltpu.VMEM((nkt, ATTN_KV_HEADS, ATTN_HEAD_DIM + ONES_ROWS, KEY_TILE), BF16),
            pltpu.VMEM((ATTN_HEADS, Q_BLOCK, LANES), BF16),
            pltpu.VMEM((IDX_HEADS, Q_BLOCK, LANES), BF16),
            pltpu.VMEM((nkt, KEY_TILE, Q_BLOCK), I32),
            pltpu.VMEM((nkt, 32, KEY_TILE // 32, Q_BLOCK), I32),
            pltpu.VMEM((nkt, KEY_TILE // 32, Q_BLOCK), I32),
            pltpu.VMEM((nkt, KEY_TILE // 32, Q_BLOCK), I32),
            pltpu.VMEM((nkt, KEY_TILE // 32, Q_BLOCK), I32),
            pltpu.VMEM((n_pairs, KEY_TILE, 2 * Q_BLOCK), F32),
            pltpu.VMEM((n_pairs, KEY_TILE, 2 * Q_BLOCK), F32),
            pltpu.VMEM((n_pairs, 1, 2 * Q_BLOCK), F32),
            pltpu.VMEM((n_pairs, 1, 2 * Q_BLOCK), F32),
            pltpu.VMEM((n_pairs, 1, 2 * Q_BLOCK), F32),
            pltpu.VMEM((n_pairs, 1, 2 * Q_BLOCK), F32),
            pltpu.VMEM((n_pairs, ATTN_HEAD_DIM, 2 * Q_BLOCK), F32),
        ],
        compiler_params=_cparams(("parallel", "arbitrary")),
        name="dsa_attention",
    )(main3d, misc3d, misc3d, misc3d, misc3d, misc3d)


def _merge_kernel(h_ref, ys_ref, ya_ref, gs_ref, ga_ref, wso_ref, wao_ref, wo_ref, o_ref):
    a = _dot(ys_ref[...], wso_ref[...])
    b = _dot(ya_ref[...], wao_ref[...])
    merged = jax.nn.sigmoid(gs_ref[...].astype(F32)) * a + jax.nn.sigmoid(ga_ref[...].astype(F32)) * b
    o_ref[...] = h_ref[...] + _dot(merged.astype(BF16), wo_ref[...])


def _merge(h2d, y_ssd, y_attn, proj2d, wso, wao, wo):
    n, d = h2d.shape
    tm = 512
    full = lambda a: pl.BlockSpec(a.shape, lambda i: (0, 0))
    return pl.pallas_call(
        _merge_kernel,
        grid=(n // tm,),
        in_specs=[
            pl.BlockSpec((tm, d), lambda i: (i, 0)),
            pl.BlockSpec((tm, y_ssd.shape[1]), lambda i: (i, 0)),
            pl.BlockSpec((tm, y_attn.shape[1]), lambda i: (i, 0)),
            pl.BlockSpec((tm, d), lambda i: (i, 6)),
            pl.BlockSpec((tm, d), lambda i: (i, 7)),
            full(wso), full(wao), full(wo),
        ],
        out_specs=pl.BlockSpec((tm, d), lambda i: (i, 0)),
        out_shape=jax.ShapeDtypeStruct((n, d), F32),
        compiler_params=_cparams(("parallel",)),
        name="merge_out",
    )(h2d, y_ssd, y_attn, proj2d, proj2d, wso, wao, wo)


def _normproj_kernel(x_ref, g_ref, w_ref, o_ref):
    o_ref[...] = _dot(_rms(x_ref[...], g_ref[...]).astype(BF16), w_ref[...]).astype(o_ref.dtype)


def _norm_project(x2d, gain, w_bf16, out_dtype):
    n, d = x2d.shape
    tm = min(512, n)
    return pl.pallas_call(
        _normproj_kernel,
        grid=(n // tm,),
        in_specs=[pl.BlockSpec((tm, d), lambda i: (i, 0)), pl.BlockSpec((1, d), lambda i: (0, 0)),
                  pl.BlockSpec(w_bf16.shape, lambda i: (0, 0))],
        out_specs=pl.BlockSpec((tm, w_bf16.shape[1]), lambda i: (i, 0)),
        out_shape=jax.ShapeDtypeStruct((n, w_bf16.shape[1]), out_dtype),
        compiler_params=_cparams(("parallel",)),
        name="mem_kv_projection",
    )(x2d, gain.reshape(1, d), w_bf16)


def _xattn_kernel(h_ref, g_ref, kv_ref, wq_ref, wo_ref, o_ref):
    d = h_ref.shape[1]
    hd = d // XATTN_HEADS
    h = h_ref[...]
    q = _dot(_rms(h, g_ref[...]).astype(BF16), wq_ref[...]) * (hd ** -0.5)
    qb = q.astype(BF16)
    outs = []
    for a in range(XATTN_HEADS):
        k = kv_ref[:, a * hd:(a + 1) * hd]
        v = kv_ref[:, d + a * hd:d + (a + 1) * hd]
        s = _dot_nt(qb[:, a * hd:(a + 1) * hd], k)
        p = jnp.exp(s - jnp.max(s, axis=1, keepdims=True))
        o = _dot(p.astype(BF16), v) / jnp.sum(p, axis=1, keepdims=True)
        outs.append(o.astype(BF16))
    o_ref[...] = h + _dot(jnp.concatenate(outs, axis=1), wo_ref[...])


def _cross_attention(h3d, gain, kv3d, wq, wo):
    bsz, seq, d = h3d.shape
    tm = min(512, seq)
    mlen = kv3d.shape[1]
    return pl.pallas_call(
        _xattn_kernel,
        grid=(bsz, seq // tm),
        in_specs=[
            pl.BlockSpec((None, tm, d), lambda b, i: (b, i, 0)),
            pl.BlockSpec((1, d), lambda b, i: (0, 0)),
            pl.BlockSpec((None, mlen, 2 * d), lambda b, i: (b, 0, 0)),
            pl.BlockSpec(wq.shape, lambda b, i: (0, 0)),
            pl.BlockSpec(wo.shape, lambda b, i: (0, 0)),
        ],
        out_specs=pl.BlockSpec((None, tm, d), lambda b, i: (b, i, 0)),
        out_shape=jax.ShapeDtypeStruct((bsz, seq, d), F32),
        compiler_params=_cparams(("parallel", "parallel")),
        name="cross_attention",
    )(h3d, gain.reshape(1, d), kv3d, wq, wo)


def _router_kernel(h_ref, g_ref, wr_ref, br_ref, trit_ref, aug_ref, rank_ref, gsel_ref, cnt_ref, base_ref):
    i = pl.program_id(0)
    d = h_ref.shape[1]

    @pl.when(i == 0)
    def _():
        base_ref[...] = jnp.zeros(base_ref.shape, F32)

    hn = _rms(h_ref[...], g_ref[...])
    aug_ref[:, 0:d] = hn
    h_hi, h_mid, h_lo = _split3(hn)
    w_hi, w_mid, w_lo = wr_ref[0], wr_ref[1], wr_ref[2]
    logits = (_dot(h_hi, w_hi) + (_dot(h_hi, w_mid) + _dot(h_mid, w_hi))
              + (_dot(h_hi, w_lo) + _dot(h_mid, w_mid) + _dot(h_lo, w_hi))) + br_ref[...]
    tm = logits.shape[0]
    lane = lax.broadcasted_iota(I32, (tm, LANES), 1)
    neg_inf = -jnp.inf
    n_e = MOE_GROUPS * MOE_EXPERTS_PER_GROUP
    is_g = jnp.logical_and(lane >= n_e, lane < n_e + MOE_GROUPS)
    gl = jnp.where(is_g, logits, neg_inf)
    g_max = jnp.max(gl, axis=1, keepdims=True)
    g_sel = jnp.min(jnp.where(gl == g_max, lane, LANES), axis=1, keepdims=True) - n_e
    g_w = 1.0 / jnp.sum(jnp.where(is_g, jnp.exp(gl - g_max), 0.0), axis=1, keepdims=True)
    in_grp = jnp.logical_and(lane >= g_sel * MOE_EXPERTS_PER_GROUP,
                             lane < (g_sel + 1) * MOE_EXPERTS_PER_GROUP)
    el = jnp.where(in_grp, logits, neg_inf)
    v1 = jnp.max(el, axis=1, keepdims=True)
    i1 = jnp.min(jnp.where(el == v1, lane, LANES), axis=1, keepdims=True)
    el2 = jnp.where(lane == i1, neg_inf, el)
    v2 = jnp.max(el2, axis=1, keepdims=True)
    i2 = jnp.min(jnp.where(el2 == v2, lane, LANES), axis=1, keepdims=True)
    e2 = jnp.exp(v2 - v1)
    w1 = g_w / (1.0 + e2)
    w2 = g_w * e2 / (1.0 + e2)
    aug_ref[:, d:] = jnp.where(lane == i1, w1, 0.0) + jnp.where(lane == i2, w2, 0.0)

    onehot_t = jnp.where(lane == g_sel, 1.0, 0.0).T
    upto = _dot(onehot_t.astype(BF16), trit_ref[...]) + base_ref[...]
    grp = lax.broadcasted_iota(I32, onehot_t.shape, 0).astype(F32)
    rank_ref[...] = (jnp.sum(onehot_t * upto, axis=0, keepdims=True) - 1.0).astype(I32)
    gsel_ref[...] = jnp.sum(onehot_t * grp, axis=0, keepdims=True).astype(I32)
    base_ref[...] = upto[:, tm - 1:tm]
    cnt_ref[...] = jnp.broadcast_to(upto[:, tm - 1:tm], cnt_ref.shape)


def _moe_route(h2d, gain, wr3, br):
    n, d = h2d.shape
    tm = 512
    nt = n // tm
    trit = (jnp.arange(tm)[:, None] <= jnp.arange(tm)[None, :]).astype(BF16)
    return pl.pallas_call(
        _router_kernel,
        grid=(nt,),
        in_specs=[pl.BlockSpec((tm, d), lambda i: (i, 0)), pl.BlockSpec((1, d), lambda i: (0, 0)),
                  pl.BlockSpec(wr3.shape, lambda i: (0, 0, 0)), pl.BlockSpec((1, LANES), lambda i: (0, 0)),
                  pl.BlockSpec((tm, tm), lambda i: (0, 0))],
        out_specs=[pl.BlockSpec((tm, d + LANES), lambda i: (i, 0)),
                   pl.BlockSpec((None, 1, tm), lambda i: (i, 0, 0)),
                   pl.BlockSpec((None, 1, tm), lambda i: (i, 0, 0)),
                   pl.BlockSpec((LANES, LANES), lambda i: (0, 0))],
        out_shape=[jax.ShapeDtypeStruct((n, d + LANES), F32),
                   jax.ShapeDtypeStruct((nt, 1, tm), I32),
                   jax.ShapeDtypeStruct((nt, 1, tm), I32),
                   jax.ShapeDtypeStruct((LANES, LANES), F32)],
        scratch_shapes=[pltpu.VMEM((LANES, 1), F32)],
        compiler_params=_cparams(("arbitrary",)),
        name="moe_router",
    )(h2d, gain.reshape(1, d), wr3, br, trit)


MOE_TILE = 512
ROW_BLOCK = 256
ISSUE_UNROLL = 8


def _scatter_rows_kernel(dest_ref, src_ref, init_ref, out_ref, sem):
    del init_ref
    base = pl.program_id(0) * ROW_BLOCK

    def row_copy(r):
        return pltpu.make_async_copy(src_ref.at[pl.ds(r, 1)], out_ref.at[pl.ds(dest_ref[base + r], 1)], sem)

    def issue(r, _):
        row_copy(r).start()
        return 0
    lax.fori_loop(0, ROW_BLOCK, issue, 0, unroll=ISSUE_UNROLL)
    pltpu.make_async_copy(src_ref, out_ref.at[pl.ds(0, ROW_BLOCK)], sem).wait()


def _scatter_rows(src, dest, n_out):
    n, w = src.shape
    init = jnp.zeros((n_out, w), src.dtype)
    return pl.pallas_call(
        _scatter_rows_kernel,
        grid_spec=pltpu.PrefetchScalarGridSpec(
            num_scalar_prefetch=1,
            grid=(n // ROW_BLOCK,),
            in_specs=[pl.BlockSpec((ROW_BLOCK, w), lambda i, dest: (i, 0)),
                      pl.BlockSpec(memory_space=pl.ANY)],
            out_specs=pl.BlockSpec(memory_space=pl.ANY),
            scratch_shapes=[pltpu.SemaphoreType.DMA(())],
        ),
        out_shape=jax.ShapeDtypeStruct((n_out, w), src.dtype),
        input_output_aliases={2: 0},
        compiler_params=_cparams(("arbitrary",)),
        name="moe_sort_rows",
    )(dest, src, init)


def _sorted_experts_kernel(tile_group_ref, n_used_ref, hn_ref, cw_ref, wg_ref, wu_ref, wd_ref, o_ref):
    j = pl.program_id(0)

    @pl.when(j < n_used_ref[0])
    def _():
        hn = hn_ref[...].astype(BF16)
        first = tile_group_ref[j] * MOE_EXPERTS_PER_GROUP
        lane = lax.broadcasted_iota(I32, cw_ref.shape, 1)
        cw = cw_ref[...]
        acc = jnp.zeros(o_ref.shape, F32)
        for e in range(MOE_EXPERTS_PER_GROUP):
            w_e = jnp.sum(jnp.where(lane == first + e, cw, 0.0), axis=1, keepdims=True)
            hid = _silu(_dot(hn, wg_ref[e])) * _dot(hn, wu_ref[e]) * w_e
            acc = acc + _dot(hid.astype(BF16), wd_ref[e])
        o_ref[...] = acc

    @pl.when(j >= n_used_ref[0])
    def _():
        o_ref[...] = jnp.zeros(o_ref.shape, F32)


def _sorted_experts(aug_sorted, tile_group, n_used, wg, wu, wd):
    n_pad, wa = aug_sorted.shape
    d = wa - LANES
    _, n_e, _, dff = wg.shape
    tm = MOE_TILE
    return pl.pallas_call(
        _sorted_experts_kernel,
        grid_spec=pltpu.PrefetchScalarGridSpec(
            num_scalar_prefetch=2,
            grid=(n_pad // tm,),
            in_specs=[
                pl.BlockSpec((tm, d), lambda j, tg, nu: (j, 0)),
                pl.BlockSpec((tm, LANES), lambda j, tg, nu: (j, d // LANES)),
                pl.BlockSpec((None, n_e, d, dff), lambda j, tg, nu: (tg[j], 0, 0, 0)),
                pl.BlockSpec((None, n_e, d, dff), lambda j, tg, nu: (tg[j], 0, 0, 0)),
                pl.BlockSpec((None, n_e, dff, d), lambda j, tg, nu: (tg[j], 0, 0, 0)),
            ],
            out_specs=pl.BlockSpec((tm, d), lambda j, tg, nu: (j, 0)),
        ),
        out_shape=jax.ShapeDtypeStruct((n_pad, d), F32),
        compiler_params=_cparams(("arbitrary",)),
        name="moe_experts",
    )(tile_group, n_used, aug_sorted, aug_sorted, wg, wu, wd)


def _gather_add_kernel(dest_ref, h_ref, delta_ref, fg_ref, o_ref, buf_ref, sem, *, final_norm):
    i = pl.program_id(0)
    n_steps = pl.num_programs(0)

    def row_copy(step, r):
        slot = lax.rem(step, 2)
        return pltpu.make_async_copy(delta_ref.at[pl.ds(dest_ref[step * ROW_BLOCK + r], 1)],
                                     buf_ref.at[slot, pl.ds(r, 1)], sem.at[slot])

    def issue_all(step):
        def issue(r, _):
            row_copy(step, r).start()
            return 0
        lax.fori_loop(0, ROW_BLOCK, issue, 0, unroll=ISSUE_UNROLL)

    @pl.when(i == 0)
    def _():
        issue_all(0)

    @pl.when(i + 1 < n_steps)
    def _():
        issue_all(i + 1)

    slot = lax.rem(i, 2)
    pltpu.make_async_copy(delta_ref.at[pl.ds(0, ROW_BLOCK)], buf_ref.at[slot], sem.at[slot]).wait()

    out = h_ref[...] + buf_ref[slot]
    if final_norm:
        out = _rms(out, fg_ref[...])
    o_ref[...] = out


def _gather_add(h2d, delta_sorted, dest, final_gain, final_norm):
    n, d = h2d.shape
    kern = functools.partial(_gather_add_kernel, final_norm=final_norm)
    return pl.pallas_call(
        kern,
        grid_spec=pltpu.PrefetchScalarGridSpec(
            num_scalar_prefetch=1,
            grid=(n // ROW_BLOCK,),
            in_specs=[pl.BlockSpec((ROW_BLOCK, d), lambda i, dest: (i, 0)),
                      pl.BlockSpec(memory_space=pl.ANY),
                      pl.BlockSpec((1, d), lambda i, dest: (0, 0))],
            out_specs=pl.BlockSpec((ROW_BLOCK, d), lambda i, dest: (i, 0)),
            scratch_shapes=[pltpu.VMEM((2, ROW_BLOCK, d), F32), pltpu.SemaphoreType.DMA((2,))],
        ),
        out_shape=jax.ShapeDtypeStruct((n, d), F32),
        compiler_params=_cparams(("arbitrary",)),
        name="moe_unsort_add",
    )(dest, h2d, delta_sorted, final_gain.reshape(1, d))


def _moe_layout(rank, gsel, counts, n):
    tm = MOE_TILE
    cnt = counts[:MOE_GROUPS, 0].astype(I32)
    seg = ((cnt + tm - 1) // tm) * tm
    ends = jnp.cumsum(seg)
    dest = (ends - seg)[gsel.reshape(n)] + rank.reshape(n)
    n_tiles = n // tm + MOE_GROUPS
    starts = jnp.arange(n_tiles, dtype=I32) * tm
    tile_group = jnp.minimum(jnp.sum(starts[:, None] >= ends[None, :], axis=1), MOE_GROUPS - 1).astype(I32)
    n_used = (ends[-1:] // tm).astype(I32)
    return dest.astype(I32), tile_group, n_used, n_tiles * tm


def _prep_router(w_group, b_group, w_router, b_router):
    w = jnp.concatenate([w_router, w_group], axis=1).astype(F32)
    w = jnp.pad(w, ((0, 0), (0, LANES - w.shape[1])))
    hi = w.astype(BF16)
    r1 = w - hi.astype(F32)
    mid = r1.astype(BF16)
    lo = (r1 - mid.astype(F32)).astype(BF16)
    b = jnp.concatenate([b_router, b_group]).astype(F32)
    b = jnp.pad(b, (0, LANES - b.shape[0])).reshape(1, LANES)
    return jnp.stack([hi, mid, lo]), b


def kernel(x, mem, norm_mix, w_in, conv_w, conv_b, dt_bias, a_log, d_skip, ssd_norm, w_ssd_o, w_attn_o, w_out,
           norm_xattn, norm_mem, w_cq, w_ckv, w_co, norm_ffn, w_group, b_group, w_router, b_router,
           w_gate_e, w_up_e, w_down_e, norm_final):
    bsz, seq, d = x.shape
    depth = w_in.shape[0]
    mlen = mem.shape[1]
    heads = dt_bias.shape[1]
    d_inner = heads * SSD_HEAD_DIM
    conv_ch = conv_w.shape[2]
    n = bsz * seq
    rope_c, rope_s = _rope_tables(seq)
    h = x.reshape(n, d)
    mem2d = mem.reshape(bsz * mlen, d)
    for l in range(depth):
        w_in_l = _prep_w_in(w_in[l], d_inner, conv_ch, heads)
        main, misc = _in_projection(h, norm_mix[l], w_in_l, rope_c, rope_s, seq)
        main3d = main.reshape(bsz, seq, main.shape[1])
        misc3d = misc.reshape(bsz, seq, misc.shape[1])
        y_ssd = _ssd_branch(main3d, misc3d, conv_w[l], conv_b[l], dt_bias[l], a_log[l], d_skip[l], ssd_norm[l])
        y_attn = _dsa_branch(main3d, misc3d)
        h = _merge(h, y_ssd.reshape(n, d_inner), y_attn.reshape(n, -1), main,
                   w_ssd_o[l].astype(BF16), w_attn_o[l].astype(BF16), w_out[l].astype(BF16))
        kv = _norm_project(mem2d, norm_mem[l], w_ckv[l].astype(BF16), BF16)
        h = _cross_attention(h.reshape(bsz, seq, d), norm_xattn[l], kv.reshape(bsz, mlen, 2 * d),
                             w_cq[l].astype(BF16), w_co[l].astype(BF16)).reshape(n, d)
        wr3, br = _prep_router(w_group[l], b_group[l], w_router[l], b_router[l])
        aug, rank, gsel, counts = _moe_route(h, norm_ffn[l], wr3, br)
        dest, tile_group, n_used, n_pad = _moe_layout(rank, gsel, counts, n)
        aug_sorted = _scatter_rows(aug, dest, n_pad)
        delta = _sorted_experts(aug_sorted, tile_group, n_used, w_gate_e[l].astype(BF16),
                                w_up_e[l].astype(BF16), w_down_e[l].astype(BF16))
        h = _gather_add(h, delta, dest, norm_final, final_norm=(l == depth - 1))
    return h.reshape(bsz, seq, d)
```

```python
import functools
import math

import jax
import jax.numpy as jnp
import numpy as np
from jax import lax
from jax.experimental import pallas as pl
from jax.experimental.pallas import tpu as pltpu

F32 = jnp.float32
BF16 = jnp.bfloat16
I32 = jnp.int32

SSD_HEAD_DIM = 64
SSD_GROUPS = 4
SSD_STATE = 128
SSD_CONV = 4
SSD_CHUNK = 128
SSD_SEQS_PER_STEP = 2
ATTN_HEADS = 16
ATTN_KV_HEADS = 2
ATTN_HEAD_DIM = 64
Q_BLOCK = 128
TOPK_MAX = 256
IDX_HEADS = 8
IDX_DIM = 64
ROPE_THETA = 10000.0
XATTN_HEADS = 4
MOE_GROUPS = 4
MOE_EXPERTS_PER_GROUP = 8
RMS_EPS = 1e-6

LANES = 128
SUBLANES = 8
KEY_TILE = 256
INT_MIN = -2 ** 31
NEG_BIG = -1e30
LOG2E = math.log2(math.e)
ONES_ROWS = 16
VMEM_LIMIT = 56 * 1024 * 1024


def _cparams(sem):
    return pltpu.CompilerParams(dimension_semantics=sem, vmem_limit_bytes=VMEM_LIMIT)


def _split3(v):
    hi = v.astype(BF16)
    r1 = v - hi.astype(F32)
    mid = r1.astype(BF16)
    lo = (r1 - mid.astype(F32)).astype(BF16)
    return hi, mid, lo


def _dot(a, b):
    return jnp.dot(a, b, preferred_element_type=F32)


def _dot_nt(a, b):
    return lax.dot_general(a, b, (((1,), (1,)), ((), ())), preferred_element_type=F32)


def _spread(v, m01, terms):
    parts = _split3(v)[:terms]
    out = _dot(parts[0], m01)
    for part in parts[1:]:
        out = out + _dot(part, m01)
    return out


def _dot3_lhs01(m01, v):
    hi, mid, lo = _split3(v)
    return _dot(m01, hi) + _dot(m01, mid) + _dot(m01, lo)


def _rms(x, g):
    ms = jnp.mean(x * x, axis=-1, keepdims=True)
    return x * lax.rsqrt(ms + RMS_EPS) * g


def _silu(x):
    return x * jax.nn.sigmoid(x)


Q_TILE = 5
MISC_TILE = 8
MISC_ROPE_CHUNKS = (0, 1, 2, 3, 4, 6)
COL_QI = 0
COL_K, COL_V, COL_KI, COL_DTWI = 4, 5, 6, 7
DT_LANES = 32


def _rope_chunk(a, c, s):
    return a * c + pltpu.roll(a, LANES // 2, 1) * s


def _inproj_kernel(x_ref, g_ref, w_ref, c_ref, s_ref, o_ref, misc_ref, xn_ref):
    j = pl.program_id(1)

    @pl.when(j == 0)
    def _():
        xn_ref[...] = _rms(x_ref[...], g_ref[...]).astype(BF16)

    acc = _dot(xn_ref[...], w_ref[...])
    n_chunks = acc.shape[1] // LANES

    def roped(dst_ref, chunks):
        c, sn = c_ref[...], s_ref[...]
        for k in range(n_chunks):
            a = acc[:, k * LANES:(k + 1) * LANES]
            if k in chunks:
                a = _rope_chunk(a, c, sn)
            dst_ref[:, k * LANES:(k + 1) * LANES] = a.astype(dst_ref.dtype)

    @pl.when(j == Q_TILE)
    def _():
        roped(o_ref, tuple(range(n_chunks)))

    @pl.when(j == MISC_TILE)
    def _():
        roped(misc_ref, MISC_ROPE_CHUNKS)

    @pl.when(jnp.logical_and(j != Q_TILE, j != MISC_TILE))
    def _():
        o_ref[...] = acc.astype(o_ref.dtype)


def _in_projection(x2d, gain, w_bf16, rope_c, rope_s, seq):
    n, d = x2d.shape
    tm = min(1024, seq)
    tn = 1024
    pos_tiles = seq // tm
    n_tiles = w_bf16.shape[1] // tn
    assert n_tiles == MISC_TILE + 1
    return pl.pallas_call(
        _inproj_kernel,
        grid=(n // tm, n_tiles),
        in_specs=[
            pl.BlockSpec((tm, d), lambda i, j: (i, 0)),
            pl.BlockSpec((1, d), lambda i, j: (0, 0)),
            pl.BlockSpec((d, tn), lambda i, j: (0, j)),
            pl.BlockSpec((tm, LANES), lambda i, j: (i % pos_tiles, 0)),
            pl.BlockSpec((tm, LANES), lambda i, j: (i % pos_tiles, 0)),
        ],
        out_specs=[pl.BlockSpec((tm, tn), lambda i, j: (i, jnp.minimum(j, MISC_TILE - 1))),
                   pl.BlockSpec((tm, tn), lambda i, j: (i, 0))],
        out_shape=[jax.ShapeDtypeStruct((n, MISC_TILE * tn), BF16), jax.ShapeDtypeStruct((n, tn), F32)],
        scratch_shapes=[pltpu.VMEM((tm, d), BF16)],
        compiler_params=_cparams(("parallel", "arbitrary")),
        name="in_projection",
    )(x2d, gain.reshape(1, d), w_bf16, rope_c, rope_s)


def _pair_halves(w):
    d = w.shape[0]
    half = ATTN_HEAD_DIM // 2
    return w.reshape(d, -1, 2, 2, half).transpose(0, 1, 3, 2, 4).reshape(d, -1)


def _prep_w_in(w, d_inner, conv_ch, ssd_heads):
    widths = (d_inner, conv_ch, ssd_heads, ATTN_HEADS * ATTN_HEAD_DIM, ATTN_KV_HEADS * ATTN_HEAD_DIM,
              ATTN_KV_HEADS * ATTN_HEAD_DIM, IDX_HEADS * IDX_DIM, IDX_DIM, IDX_HEADS,
              w.shape[0], w.shape[0])
    splits = [int(p) for p in np.cumsum(widths)[:-1]]
    z, xbc, dt, q, k, v, qi, ki, wi, gs, ga = jnp.split(w, splits, axis=1)

    def pad(a, n):
        return jnp.pad(a, ((0, 0), (0, n - a.shape[1])))

    q, k, qi, ki = _pair_halves(q), _pair_halves(k), _pair_halves(qi), _pair_halves(pad(ki, LANES))
    misc = jnp.concatenate([qi, k, v, ki, pad(jnp.concatenate([dt, wi], axis=1), LANES)], axis=1)
    return jnp.concatenate([z, xbc, q, gs, ga, misc], axis=1).astype(BF16)


def _rope_tables(seq):
    inv = 1.0 / (ROPE_THETA ** (jnp.arange(0, ATTN_HEAD_DIM, 2, dtype=F32) / ATTN_HEAD_DIM))
    ang = jnp.arange(seq, dtype=F32)[:, None] * inv[None, :]
    cos, sin = jnp.cos(ang), jnp.sin(ang)
    c = jnp.concatenate([cos, cos, cos, cos], axis=1)
    s = jnp.concatenate([-sin, -sin, sin, sin], axis=1)
    return c, s


def _ssd_chunk(z_ref, xs_ref, bc_ref, dt_ref, cw_ref, cb_ref, dtb_ref, alog_ref, dsk_ref, ng_ref,
               r64_ref, tri_ref, shift_ref, y_ref, prev_ref, st_ref, act_ref, heads, groups):
    L = SSD_CHUNK
    P = SSD_HEAD_DIM
    hpg = heads // groups
    d_inner = heads * P
    gw = hpg * P
    nst = SSD_STATE

    def conv(cur_ref, lo, hi):
        cur = cur_ref[...]
        both = jnp.concatenate([prev_ref[:, lo:hi], cur], axis=0)
        acc = cb_ref[:, lo:hi] + cw_ref[SSD_CONV - 1:SSD_CONV, lo:hi] * cur.astype(F32)
        for k in range(1, SSD_CONV):
            acc = acc + cw_ref[SSD_CONV - 1 - k:SSD_CONV - k, lo:hi] * _dot(shift_ref[k - 1], both)
        prev_ref[:, lo:hi] = cur
        return _silu(acc)

    xs = conv(xs_ref, 0, d_inner)
    bc = conv(bc_ref, d_inner, prev_ref.shape[1])
    bmat = bc[:, :groups * nst]
    cmat = bc[:, groups * nst:]

    dt = jax.nn.softplus(dt_ref[...] + dtb_ref[...])
    da = dt * (-jnp.exp(alog_ref[...]))
    tri = tri_ref[...]
    acum = _dot3_lhs01(tri, da)
    act_ref[...] = acum.T
    e_acum = jnp.exp(acum)
    e_tail = jnp.exp(acum[L - 1:L, :] - acum)
    r64 = r64_ref[...]
    e64 = _spread(e_acum, r64, 2)
    t64 = _spread(e_tail, r64, 1)
    d64 = _spread(dt, r64, 1)
    xdt = xs * d64
    xdt_b = xdt.astype(BF16)
    xtl_b = (xdt * t64).astype(BF16)

    row = lax.broadcasted_iota(I32, (L, L), 0)
    col = lax.broadcasted_iota(I32, (L, L), 1)
    causal = row >= col
    first_head = lax.broadcasted_iota(I32, (1, 2 * P), 1) < P

    for g in range(groups):
        bg = bmat[:, g * nst:(g + 1) * nst]
        cg_b = cmat[:, g * nst:(g + 1) * nst].astype(BF16)
        cb = _dot_nt(cg_b, bg.astype(BF16))
        sg = st_ref[g]
        y_off = _dot(cg_b, sg.astype(BF16))
        y_heads = []
        for hh in range(0, hpg, 2):
            pair = []
            for h in (g * hpg + hh, g * hpg + hh + 1):
                seg = jnp.broadcast_to(acum[:, h:h + 1], (L, LANES)) - act_ref[h:h + 1, :]
                pair.append((cb * jnp.where(causal, jnp.exp(seg), 0.0)).astype(BF16))
            slab = xdt_b[:, (g * hpg + hh) * P:(g * hpg + hh + 2) * P]
            stacked = jnp.concatenate([jnp.where(first_head, slab, 0), jnp.where(first_head, 0, slab)], axis=0)
            y_heads.append(_dot(jnp.concatenate(pair, axis=1), stacked))
        lo, hi = g * gw, (g + 1) * gw
        y_g = jnp.concatenate(y_heads, axis=1) + y_off * e64[:, lo:hi]
        st_ref[g] = sg * e64[L - 1:L, lo:hi] + _dot(bg.T.astype(BF16), xtl_b[:, lo:hi])
        y_g = (y_g + dsk_ref[:, lo:hi] * xs[:, lo:hi]) * _silu(z_ref[:, lo:hi].astype(F32))
        ms = jnp.mean(y_g * y_g, axis=-1, keepdims=True)
        y_ref[:, lo:hi] = (y_g * lax.rsqrt(ms + RMS_EPS) * ng_ref[:, lo:hi]).astype(y_ref.dtype)


def _ssd_kernel(z_ref, xs_ref, bc_ref, dt_ref, cw_ref, cb_ref, dtb_ref, alog_ref, dsk_ref, ng_ref,
                r64_ref, tri_ref, shift_ref, y_ref, prev_ref, st_ref, act_ref, *, heads, groups):
    @pl.when(pl.program_id(1) == 0)
    def _():
        prev_ref[...] = jnp.zeros(prev_ref.shape, BF16)
        st_ref[...] = jnp.zeros(st_ref.shape, F32)

    for nb in range(z_ref.shape[0]):
        _ssd_chunk(z_ref.at[nb], xs_ref.at[nb], bc_ref.at[nb], dt_ref.at[nb], cw_ref, cb_ref, dtb_ref,
                   alog_ref, dsk_ref, ng_ref, r64_ref, tri_ref, shift_ref, y_ref.at[nb],
                   prev_ref.at[nb], st_ref.at[nb], act_ref.at[nb], heads, groups)


def _ssd_branch(main3d, misc3d, conv_w, conv_b, dt_bias, a_log, d_skip, ssd_norm):
    bsz, seq, _ = main3d.shape
    heads = dt_bias.shape[0]
    d_inner = heads * SSD_HEAD_DIM
    groups = SSD_GROUPS
    conv_ch = conv_w.shape[1]
    bc_w = conv_ch - d_inner
    L = SSD_CHUNK
    nb = SSD_SEQS_PER_STEP if bsz % SSD_SEQS_PER_STEP == 0 else 1

    def pad_lanes(v):
        return jnp.pad(v.astype(F32), (0, LANES - v.shape[0])).reshape(1, LANES)

    hidx = jnp.arange(LANES)[:, None]
    r64 = (hidx == (jnp.arange(d_inner) // SSD_HEAD_DIM)[None, :]).astype(BF16)
    tri = (jnp.arange(L)[:, None] >= jnp.arange(L)[None, :]).astype(BF16)
    taps = jnp.arange(1, SSD_CONV)[:, None, None]
    shift = (jnp.arange(2 * L)[None, None, :] == (L + jnp.arange(L)[None, :, None] - taps)).astype(BF16)
    dsk = jnp.repeat(d_skip.astype(F32), SSD_HEAD_DIM).reshape(1, d_inner)
    const = lambda shape: pl.BlockSpec(shape, lambda b, c: (0,) * len(shape))
    kern = functools.partial(_ssd_kernel, heads=heads, groups=groups)
    return pl.pallas_call(
        kern,
        grid=(bsz // nb, seq // L),
        in_specs=[
            pl.BlockSpec((nb, L, d_inner), lambda b, c: (b, c, 0)),
            pl.BlockSpec((nb, L, d_inner), lambda b, c: (b, c, 1)),
            pl.BlockSpec((nb, L, bc_w), lambda b, c: (b, c, 2 * d_inner // bc_w)),
            pl.BlockSpec((nb, L, LANES), lambda b, c: (b, c, COL_DTWI)),
            const((SSD_CONV, conv_ch)), const((1, conv_ch)), const((1, LANES)), const((1, LANES)),
            const((1, d_inner)), const((1, d_inner)),
            const((LANES, d_inner)), const((L, L)),
            const((SSD_CONV - 1, L, 2 * L)),
        ],
        out_specs=pl.BlockSpec((nb, L, d_inner), lambda b, c: (b, c, 0)),
        out_shape=jax.ShapeDtypeStruct((bsz, seq, d_inner), BF16),
        scratch_shapes=[
            pltpu.VMEM((nb, L, conv_ch), BF16),
            pltpu.VMEM((nb, groups, SSD_STATE, d_inner // groups), F32),
            pltpu.VMEM((nb, LANES, L), F32),
        ],
        compiler_params=_cparams(("parallel", "arbitrary")),
        name="ssd_scan",
    )(main3d, main3d, main3d, misc3d, conv_w.astype(F32), conv_b.reshape(1, conv_ch).astype(F32),
      pad_lanes(dt_bias), pad_lanes(a_log), dsk, ssd_norm.reshape(1, d_inner).astype(F32), r64, tri, shift)


def _sortable_key(x):
    x = jnp.where(x == 0.0, 0.0, x)
    b = pltpu.bitcast(x, I32)
    return b ^ (lax.shift_right_arithmetic(b, 31) & 0x7FFFFFFF)


_BIT_MASKS = (0x0000FFFF, 0x00FF00FF, 0x0F0F0F0F, 0x33333333, 0x55555555)


def _bit_planes(rows):
    x = list(rows[::-1])
    j = 16
    for m in _BIT_MASKS:
        mask = np.int32(np.uint32(m))
        k = 0
        while k < 32:
            t = (x[k] ^ lax.shift_right_logical(x[k + j], jnp.int32(j))) & mask
            x[k] = x[k] ^ t
            x[k + j] = x[k + j] ^ lax.shift_left(t, jnp.int32(j))
            k = (k + j + 1) & ~j
        j >>= 1
    return x


def _dsa_kernel(q_ref, qi_ref, k_ref, v_ref, ki_ref, wi_ref, o_ref,
                kb_ref, kib_ref, vt_ref, qs_ref, qis_ref, keys_ref, planes_ref, sel_ref, cand_ref, above_ref,
                s0_ref, s1_ref, x0_ref, x1_ref, m_ref, l_ref, acc_ref, *, topk):
    i = pl.program_id(1)
    QB, KT = Q_BLOCK, KEY_TILE
    hd = ATTN_HEAD_DIM
    seq = k_ref.shape[0]
    n_tiles = (i * QB + QB + KT - 1) // KT
    n_pairs = n_tiles // 2
    odd_tile = n_tiles % 2 == 1
    lane = lax.broadcasted_iota(I32, (1, LANES), 1)
    quarter = hd // 2
    first = (lane // quarter) % 2 == 0

    @pl.when(i == 0)
    def _():
        def fill(t, _):
            rows = pl.ds(pl.multiple_of(t * KT, KT), KT)
            kb_ref[rows, :] = k_ref[rows, :].astype(BF16)
            kib_ref[rows, :] = ki_ref[rows, :].astype(BF16)
            vt = v_ref[rows, :].T.astype(BF16)
            for g in range(ATTN_KV_HEADS):
                vt_ref[t, g, 0:hd, :] = vt[g * hd:(g + 1) * hd, :]
                vt_ref[t, g, hd:hd + ONES_ROWS, :] = jnp.ones((ONES_ROWS, KT), BF16)
            return 0
        lax.fori_loop(0, seq // KT, fill, 0)
        planes_ref[...] = jnp.zeros(planes_ref.shape, I32)

    qpos = i * QB + lax.broadcasted_iota(I32, (1, QB), 1)
    idx_scale = (IDX_DIM ** -0.5) * (IDX_HEADS ** -0.5)
    wt = (wi_ref[...] * idx_scale).T
    w_rows = [wt[DT_LANES + h:DT_LANES + h + 1, :] for h in range(IDX_HEADS)]

    heads_per_group = ATTN_HEADS // ATTN_KV_HEADS
    for p in range(ATTN_HEADS // 2):
        slab = q_ref[:, p * LANES:(p + 1) * LANES].astype(F32) * ((hd ** -0.5) * LOG2E)
        if (2 * p) // heads_per_group == 0:
            even = jnp.where(first, slab, 0.0)
            odd = jnp.where(first, pltpu.roll(slab, LANES - quarter, 1), 0.0)
        else:
            even = jnp.where(first, 0.0, pltpu.roll(slab, quarter, 1))
            odd = jnp.where(first, 0.0, slab)
        qs_ref[2 * p] = even.astype(BF16)
        qs_ref[2 * p + 1] = odd.astype(BF16)
    for p in range(IDX_HEADS // 2):
        slab = qi_ref[:, p * LANES:(p + 1) * LANES]
        qis_ref[2 * p] = slab.astype(BF16)
        qis_ref[2 * p + 1] = pltpu.roll(slab, LANES - quarter, 1).astype(BF16)

    nkt = seq // KT
    rows_per_word = KT // 32
    assert rows_per_word == SUBLANES

    def score_tile(t):
        kic = kib_ref[pl.ds(pl.multiple_of(t * KT, KT), KT), :]
        acc = jnp.zeros((KT, QB), F32)
        for p in range(IDX_HEADS // 2):
            rel = _dot_nt(kic, qis_ref[2 * p:2 * p + 2].reshape(2 * QB, LANES))
            acc = acc + w_rows[2 * p] * jnp.maximum(rel[:, :QB], 0.0)
            acc = acc + w_rows[2 * p + 1] * jnp.maximum(rel[:, QB:], 0.0)
        kpos = t * KT + lax.broadcasted_iota(I32, (KT, QB), 0)
        keys = jnp.where(kpos <= qpos, _sortable_key(acc), INT_MIN)
        keys_ref[t] = keys
        unsigned = keys ^ INT_MIN
        planes = _bit_planes([unsigned[r * rows_per_word:(r + 1) * rows_per_word, :] for r in range(32)])
        for bit in range(32):
            planes_ref[t, bit] = planes[bit]

    def score_pair(u, _):
        score_tile(2 * u)
        score_tile(2 * u + 1)
        return 0
    lax.fori_loop(0, n_pairs, score_pair, 0)

    @pl.when(odd_tile)
    def _():
        score_tile(n_tiles - 1)

    select_all = qpos < topk
    for t in range(nkt):
        cand_ref[t] = jnp.where(t < n_tiles, jnp.full((rows_per_word, QB), -1, I32), 0)
        above_ref[t] = jnp.zeros((rows_per_word, QB), I32)

    def radix_step(it, need):
        cnt = jnp.zeros((rows_per_word, QB), I32)
        for t in range(nkt):
            cnt = cnt + lax.population_count(cand_ref[t] & planes_ref[t, it])
        cnt = jnp.sum(cnt, axis=0, keepdims=True)
        take = cnt >= need
        for t in range(nkt):
            cand = cand_ref[t]
            ones = cand & planes_ref[t, it]
            cand_ref[t] = jnp.where(take, ones, cand ^ ones)
            above_ref[t] = jnp.where(take, above_ref[t], above_ref[t] | ones)
        return jnp.where(take, need, need - cnt)
    need = lax.fori_loop(0, 32, radix_step, jnp.full((1, QB), topk, I32), unroll=4)

    ties = jnp.zeros((rows_per_word, QB), I32)
    for t in range(nkt):
        ties = ties + lax.population_count(cand_ref[t])
        sel_ref[t] = jnp.where(select_all, -1, above_ref[t] | cand_ref[t])
    surplus = jnp.where(select_all, 0, jnp.sum(ties, axis=0, keepdims=True) - need)

    @pl.when(jnp.max(surplus) > 0)
    def _():
        def count(pred):
            def body(t, acc):
                hit = jnp.where(pred(keys_ref[t], t), 1, 0)
                return acc + jnp.sum(hit.reshape(KT // SUBLANES, SUBLANES, QB), axis=0)
            acc = lax.fori_loop(0, n_tiles, body, jnp.zeros((SUBLANES, QB), I32))
            return jnp.sum(acc, axis=0, keepdims=True)

        def bit_step(it, thr):
            cand = thr ^ lax.shift_left(jnp.int32(1), 31 - it)
            cnt = count(lambda kc, c: kc >= cand)
            return jnp.where(cnt >= topk, cand, thr)
        thr = lax.fori_loop(0, 32, bit_step, jnp.full((1, QB), INT_MIN, I32))
        want = topk - count(lambda kc, c: kc > thr)
        n_bits = seq.bit_length()

        def tie_step(it, jb):
            cand = jb | lax.shift_left(jnp.int32(1), n_bits - 1 - it)
            def pred(kc, c):
                kpos = c * KT + lax.broadcasted_iota(I32, (KT, QB), 0)
                return jnp.logical_and(kc == thr, kpos < cand)
            return jnp.where(count(pred) <= want, cand, jb)
        jb = lax.fori_loop(0, n_bits, tie_step, jnp.zeros((1, QB), I32))

        def pack_tile(t, _):
            kc = keys_ref[t]
            kpos = t * KT + lax.broadcasted_iota(I32, (KT, QB), 0)
            chosen = jnp.logical_or(kc > thr, jnp.logical_and(kc == thr, kpos < jb))
            word = jnp.zeros((rows_per_word, QB), I32)
            for r in range(32):
                bit = np.int32(np.uint32(1 << r))
                word = word | jnp.where(chosen[r * rows_per_word:(r + 1) * rows_per_word, :], bit, 0)
            sel_ref[t] = jnp.where(select_all, -1, word)
            return 0
        lax.fori_loop(0, n_tiles, pack_tile, 0)

    m_ref[...] = jnp.full(m_ref.shape, NEG_BIG, F32)
    l_ref[...] = jnp.zeros(l_ref.shape, F32)
    acc_ref[...] = jnp.zeros(acc_ref.shape, F32)

    def masked_scores(t, dst_ref, dmax_ref):
        word = sel_ref[t]
        picked = jnp.concatenate([lax.shift_left(word, jnp.int32(31 - r)) for r in range(32)], axis=0)
        kpos = t * KT + lax.broadcasted_iota(I32, (KT, QB), 0)
        sel = jnp.logical_and(picked < 0, kpos <= qpos)
        bias = jnp.where(sel, 0.0, NEG_BIG)
        bias2 = jnp.concatenate([bias, bias], axis=1)
        kc = kb_ref[pl.ds(pl.multiple_of(t * KT, KT), KT), :]
        for p in range(ATTN_HEADS // 2):
            s = _dot_nt(kc, qs_ref[2 * p:2 * p + 2].reshape(2 * QB, LANES)) + bias2
            dst_ref[p] = s
            dmax_ref[p] = jnp.max(s, axis=0, keepdims=True)

    def softmax_pv(t, src_ref, smax_ref):
        for p in range(ATTN_HEADS // 2):
            g = (2 * p) // heads_per_group
            m_old = m_ref[p]
            m_new = jnp.maximum(m_old, smax_ref[p])
            alpha = jnp.exp2(m_old - m_new)
            pt = jnp.exp2(src_ref[p] - m_new).astype(BF16)
            m_ref[p] = m_new
            pv = _dot(vt_ref[t, g], pt)
            l_ref[p] = alpha * l_ref[p] + pv[hd:hd + 1, :]
            acc_ref[p] = alpha * acc_ref[p] + pv[0:hd, :]

    masked_scores(0, s0_ref, x0_ref)

    def attend(u, _):
        t = 2 * u
        masked_scores(t + 1, s1_ref, x1_ref)
        softmax_pv(t, s0_ref, x0_ref)
        masked_scores(t + 2, s0_ref, x0_ref)
        softmax_pv(t + 1, s1_ref, x1_ref)
        return 0
    lax.fori_loop(0, jnp.where(odd_tile, n_pairs, n_pairs - 1), attend, 0)

    @pl.when(odd_tile)
    def _():
        softmax_pv(n_tiles - 1, s0_ref, x0_ref)

    @pl.when(jnp.logical_not(odd_tile))
    def _():
        masked_scores(n_tiles - 1, s1_ref, x1_ref)
        softmax_pv(n_tiles - 2, s0_ref, x0_ref)
        softmax_pv(n_tiles - 1, s1_ref, x1_ref)

    for p in range(ATTN_HEADS // 2):
        o = acc_ref[p] / l_ref[p]
        both = jnp.concatenate([o[:, :QB], o[:, QB:]], axis=0)
        o_ref[:, p * LANES:(p + 1) * LANES] = both.T.astype(o_ref.dtype)


def _dsa_branch(main3d, misc3d):
    bsz, seq, _ = main3d.shape
    topk = min(TOPK_MAX, seq // 4)
    width = ATTN_HEADS * ATTN_HEAD_DIM
    nkt = seq // KEY_TILE
    n_pairs = ATTN_HEADS // 2
    kern = functools.partial(_dsa_kernel, topk=topk)
    return pl.pallas_call(
        kern,
        grid=(bsz, seq // Q_BLOCK),
        in_specs=[
            pl.BlockSpec((None, Q_BLOCK, width), lambda b, i: (b, i, Q_TILE)),
            pl.BlockSpec((None, Q_BLOCK, IDX_HEADS * IDX_DIM), lambda b, i: (b, i, COL_QI)),
            pl.BlockSpec((None, seq, LANES), lambda b, i: (b, 0, COL_K)),
            pl.BlockSpec((None, seq, LANES), lambda b, i: (b, 0, COL_V)),
            pl.BlockSpec((None, seq, LANES), lambda b, i: (b, 0, COL_KI)),
            pl.BlockSpec((None, Q_BLOCK, LANES), lambda b, i: (b, i, COL_DTWI)),
        ],
        out_specs=pl.BlockSpec((None, Q_BLOCK, width), lambda b, i: (b, i, 0)),
        out_shape=jax.ShapeDtypeStruct((bsz, seq, width), BF16),
        scratch_shapes=[
            pltpu.VMEM((seq, LANES), BF16),
            pltpu.VMEM((seq, LANES), BF16),
            pltpu.VMEM((nkt, ATTN_KV_HEADS, ATTN_HEAD_DIM + ONES_ROWS, KEY_TILE), BF16),
            pltpu.VMEM((ATTN_HEADS, Q_BLOCK, LANES), BF16),
            pltpu.VMEM((IDX_HEADS, Q_BLOCK, LANES), BF16),
            pltpu.VMEM((nkt, KEY_TILE, Q_BLOCK), I32),
            pltpu.VMEM((nkt, 32, KEY_TILE // 32, Q_BLOCK), I32),
            pltpu.VMEM((nkt, KEY_TILE // 32, Q_BLOCK), I32),
            pltpu.VMEM((nkt, KEY_TILE // 32, Q_BLOCK), I32),
            pltpu.VMEM((nkt, KEY_TILE // 32, Q_BLOCK), I32),
            pltpu.VMEM((n_pairs, KEY_TILE, 2 * Q_BLOCK), F32),
            pltpu.VMEM((n_pairs, KEY_TILE, 2 * Q_BLOCK), F32),
            pltpu.VMEM((n_pairs, 1, 2 * Q_BLOCK), F32),
            pltpu.VMEM((n_pairs, 1, 2 * Q_BLOCK), F32),
            pltpu.VMEM((n_pairs, 1, 2 * Q_BLOCK), F32),
            pltpu.VMEM((n_pairs, 1, 2 * Q_BLOCK), F32),
            pltpu.VMEM((n_pairs, ATTN_HEAD_DIM, 2 * Q_BLOCK), F32),
        ],
        compiler_params=_cparams(("parallel", "arbitrary")),
        name="dsa_attention",
    )(main3d, misc3d, misc3d, misc3d, misc3d, misc3d)


def _merge_kernel(h_ref, ys_ref, ya_ref, gs_ref, ga_ref, wso_ref, wao_ref, wo_ref, o_ref):
    a = _dot(ys_ref[...], wso_ref[...])
    b = _dot(ya_ref[...], wao_ref[...])
    merged = jax.nn.sigmoid(gs_ref[...].astype(F32)) * a + jax.nn.sigmoid(ga_ref[...].astype(F32)) * b
    o_ref[...] = h_ref[...] + _dot(merged.astype(BF16), wo_ref[...])


def _merge(h2d, y_ssd, y_attn, proj2d, wso, wao, wo):
    n, d = h2d.shape
    tm = 512
    full = lambda a: pl.BlockSpec(a.shape, lambda i: (0, 0))
    return pl.pallas_call(
        _merge_kernel,
        grid=(n // tm,),
        in_specs=[
            pl.BlockSpec((tm, d), lambda i: (i, 0)),
            pl.BlockSpec((tm, y_ssd.shape[1]), lambda i: (i, 0)),
            pl.BlockSpec((tm, y_attn.shape[1]), lambda i: (i, 0)),
            pl.BlockSpec((tm, d), lambda i: (i, 6)),
            pl.BlockSpec((tm, d), lambda i: (i, 7)),
            full(wso), full(wao), full(wo),
        ],
        out_specs=pl.BlockSpec((tm, d), lambda i: (i, 0)),
        out_shape=jax.ShapeDtypeStruct((n, d), F32),
        compiler_params=_cparams(("parallel",)),
        name="merge_out",
    )(h2d, y_ssd, y_attn, proj2d, proj2d, wso, wao, wo)


def _normproj_kernel(x_ref, g_ref, w_ref, o_ref):
    o_ref[...] = _dot(_rms(x_ref[...], g_ref[...]).astype(BF16), w_ref[...]).astype(o_ref.dtype)


def _norm_project(x2d, gain, w_bf16, out_dtype):
    n, d = x2d.shape
    tm = min(512, n)
    return pl.pallas_call(
        _normproj_kernel,
        grid=(n // tm,),
        in_specs=[pl.BlockSpec((tm, d), lambda i: (i, 0)), pl.BlockSpec((1, d), lambda i: (0, 0)),
                  pl.BlockSpec(w_bf16.shape, lambda i: (0, 0))],
        out_specs=pl.BlockSpec((tm, w_bf16.shape[1]), lambda i: (i, 0)),
        out_shape=jax.ShapeDtypeStruct((n, w_bf16.shape[1]), out_dtype),
        compiler_params=_cparams(("parallel",)),
        name="mem_kv_projection",
    )(x2d, gain.reshape(1, d), w_bf16)


def _xattn_kernel(h_ref, g_ref, kv_ref, wq_ref, wo_ref, o_ref):
    d = h_ref.shape[1]
    hd = d // XATTN_HEADS
    h = h_ref[...]
    q = _dot(_rms(h, g_ref[...]).astype(BF16), wq_ref[...]) * (hd ** -0.5)
    qb = q.astype(BF16)
    outs = []
    for a in range(XATTN_HEADS):
        k = kv_ref[:, a * hd:(a + 1) * hd]
        v = kv_ref[:, d + a * hd:d + (a + 1) * hd]
        s = _dot_nt(qb[:, a * hd:(a + 1) * hd], k)
        p = jnp.exp(s - jnp.max(s, axis=1, keepdims=True))
        o = _dot(p.astype(BF16), v) / jnp.sum(p, axis=1, keepdims=True)
        outs.append(o.astype(BF16))
    o_ref[...] = h + _dot(jnp.concatenate(outs, axis=1), wo_ref[...])


def _cross_attention(h3d, gain, kv3d, wq, wo):
    bsz, seq, d = h3d.shape
    tm = min(512, seq)
    mlen = kv3d.shape[1]
    return pl.pallas_call(
        _xattn_kernel,
        grid=(bsz, seq // tm),
        in_specs=[
            pl.BlockSpec((None, tm, d), lambda b, i: (b, i, 0)),
            pl.BlockSpec((1, d), lambda b, i: (0, 0)),
            pl.BlockSpec((None, mlen, 2 * d), lambda b, i: (b, 0, 0)),
            pl.BlockSpec(wq.shape, lambda b, i: (0, 0)),
            pl.BlockSpec(wo.shape, lambda b, i: (0, 0)),
        ],
        out_specs=pl.BlockSpec((None, tm, d), lambda b, i: (b, i, 0)),
        out_shape=jax.ShapeDtypeStruct((bsz, seq, d), F32),
        compiler_params=_cparams(("parallel", "parallel")),
        name="cross_attention",
    )(h3d, gain.reshape(1, d), kv3d, wq, wo)


def _router_kernel(h_ref, g_ref, wr_ref, br_ref, trit_ref, aug_ref, rank_ref, gsel_ref, cnt_ref, base_ref):
    i = pl.program_id(0)
    d = h_ref.shape[1]

    @pl.when(i == 0)
    def _():
        base_ref[...] = jnp.zeros(base_ref.shape, F32)

    hn = _rms(h_ref[...], g_ref[...])
    aug_ref[:, 0:d] = hn
    h_hi, h_mid, h_lo = _split3(hn)
    w_hi, w_mid, w_lo = wr_ref[0], wr_ref[1], wr_ref[2]
    logits = (_dot(h_hi, w_hi) + (_dot(h_hi, w_mid) + _dot(h_mid, w_hi))
              + (_dot(h_hi, w_lo) + _dot(h_mid, w_mid) + _dot(h_lo, w_hi))) + br_ref[...]
    tm = logits.shape[0]
    lane = lax.broadcasted_iota(I32, (tm, LANES), 1)
    neg_inf = -jnp.inf
    n_e = MOE_GROUPS * MOE_EXPERTS_PER_GROUP
    is_g = jnp.logical_and(lane >= n_e, lane < n_e + MOE_GROUPS)
    gl = jnp.where(is_g, logits, neg_inf)
    g_max = jnp.max(gl, axis=1, keepdims=True)
    g_sel = jnp.min(jnp.where(gl == g_max, lane, LANES), axis=1, keepdims=True) - n_e
    g_w = 1.0 / jnp.sum(jnp.where(is_g, jnp.exp(gl - g_max), 0.0), axis=1, keepdims=True)
    in_grp = jnp.logical_and(lane >= g_sel * MOE_EXPERTS_PER_GROUP,
                             lane < (g_sel + 1) * MOE_EXPERTS_PER_GROUP)
    el = jnp.where(in_grp, logits, neg_inf)
    v1 = jnp.max(el, axis=1, keepdims=True)
    i1 = jnp.min(jnp.where(el == v1, lane, LANES), axis=1, keepdims=True)
    el2 = jnp.where(lane == i1, neg_inf, el)
    v2 = jnp.max(el2, axis=1, keepdims=True)
    i2 = jnp.min(jnp.where(el2 == v2, lane, LANES), axis=1, keepdims=True)
    e2 = jnp.exp(v2 - v1)
    w1 = g_w / (1.0 + e2)
    w2 = g_w * e2 / (1.0 + e2)
    aug_ref[:, d:] = jnp.where(lane == i1, w1, 0.0) + jnp.where(lane == i2, w2, 0.0)

    onehot_t = jnp.where(lane == g_sel, 1.0, 0.0).T
    upto = _dot(onehot_t.astype(BF16), trit_ref[...]) + base_ref[...]
    grp = lax.broadcasted_iota(I32, onehot_t.shape, 0).astype(F32)
    rank_ref[...] = (jnp.sum(onehot_t * upto, axis=0, keepdims=True) - 1.0).astype(I32)
    gsel_ref[...] = jnp.sum(onehot_t * grp, axis=0, keepdims=True).astype(I32)
    base_ref[...] = upto[:, tm - 1:tm]
    cnt_ref[...] = jnp.broadcast_to(upto[:, tm - 1:tm], cnt_ref.shape)


def _moe_route(h2d, gain, wr3, br):
    n, d = h2d.shape
    tm = 512
    nt = n // tm
    trit = (jnp.arange(tm)[:, None] <= jnp.arange(tm)[None, :]).astype(BF16)
    return pl.pallas_call(
        _router_kernel,
        grid=(nt,),
        in_specs=[pl.BlockSpec((tm, d), lambda i: (i, 0)), pl.BlockSpec((1, d), lambda i: (0, 0)),
                  pl.BlockSpec(wr3.shape, lambda i: (0, 0, 0)), pl.BlockSpec((1, LANES), lambda i: (0, 0)),
                  pl.BlockSpec((tm, tm), lambda i: (0, 0))],
        out_specs=[pl.BlockSpec((tm, d + LANES), lambda i: (i, 0)),
                   pl.BlockSpec((None, 1, tm), lambda i: (i, 0, 0)),
                   pl.BlockSpec((None, 1, tm), lambda i: (i, 0, 0)),
                   pl.BlockSpec((LANES, LANES), lambda i: (0, 0))],
        out_shape=[jax.ShapeDtypeStruct((n, d + LANES), F32),
                   jax.ShapeDtypeStruct((nt, 1, tm), I32),
                   jax.ShapeDtypeStruct((nt, 1, tm), I32),
                   jax.ShapeDtypeStruct((LANES, LANES), F32)],
        scratch_shapes=[pltpu.VMEM((LANES, 1), F32)],
        compiler_params=_cparams(("arbitrary",)),
        name="moe_router",
    )(h2d, gain.reshape(1, d), wr3, br, trit)


MOE_TILE = 512
ROW_BLOCK = 256
ISSUE_UNROLL = 8


def _scatter_rows_kernel(dest_ref, src_ref, init_ref, out_ref, sem):
    del init_ref
    base = pl.program_id(0) * ROW_BLOCK

    def row_copy(r):
        return pltpu.make_async_copy(src_ref.at[pl.ds(r, 1)], out_ref.at[pl.ds(dest_ref[base + r], 1)], sem)

    def issue(g, _):
        for u in range(ISSUE_UNROLL):
            row_copy(g * ISSUE_UNROLL + u).start(priority=u % 2)
        return 0
    lax.fori_loop(0, ROW_BLOCK // ISSUE_UNROLL, issue, 0)
    pltpu.make_async_copy(src_ref, out_ref.at[pl.ds(0, ROW_BLOCK)], sem).wait()


def _scatter_rows(src, dest, n_out):
    n, w = src.shape
    init = jnp.zeros((n_out, w), src.dtype)
    return pl.pallas_call(
        _scatter_rows_kernel,
        grid_spec=pltpu.PrefetchScalarGridSpec(
            num_scalar_prefetch=1,
            grid=(n // ROW_BLOCK,),
            in_specs=[pl.BlockSpec((ROW_BLOCK, w), lambda i, dest: (i, 0)),
                      pl.BlockSpec(memory_space=pl.ANY)],
            out_specs=pl.BlockSpec(memory_space=pl.ANY),
            scratch_shapes=[pltpu.SemaphoreType.DMA(())],
        ),
        out_shape=jax.ShapeDtypeStruct((n_out, w), src.dtype),
        input_output_aliases={2: 0},
        compiler_params=_cparams(("arbitrary",)),
        name="moe_sort_rows",
    )(dest, src, init)


def _sorted_experts_kernel(tile_group_ref, n_used_ref, hn_ref, cw_ref, wg_ref, wu_ref, wd_ref, o_ref):
    j = pl.program_id(0)

    @pl.when(j < n_used_ref[0])
    def _():
        hn = hn_ref[...].astype(BF16)
        first = tile_group_ref[j] * MOE_EXPERTS_PER_GROUP
        lane = lax.broadcasted_iota(I32, cw_ref.shape, 1)
        cw = cw_ref[...]
        acc = jnp.zeros(o_ref.shape, F32)
        for e in range(MOE_EXPERTS_PER_GROUP):
            w_e = jnp.sum(jnp.where(lane == first + e, cw, 0.0), axis=1, keepdims=True)
            hid = _silu(_dot(hn, wg_ref[e])) * _dot(hn, wu_ref[e]) * w_e
            acc = acc + _dot(hid.astype(BF16), wd_ref[e])
        o_ref[...] = acc

    @pl.when(j >= n_used_ref[0])
    def _():
        o_ref[...] = jnp.zeros(o_ref.shape, F32)


def _sorted_experts(aug_sorted, tile_group, n_used, wg, wu, wd):
    n_pad, wa = aug_sorted.shape
    d = wa - LANES
    _, n_e, _, dff = wg.shape
    tm = MOE_TILE
    return pl.pallas_call(
        _sorted_experts_kernel,
        grid_spec=pltpu.PrefetchScalarGridSpec(
            num_scalar_prefetch=2,
            grid=(n_pad // tm,),
            in_specs=[
                pl.BlockSpec((tm, d), lambda j, tg, nu: (j, 0)),
                pl.BlockSpec((tm, LANES), lambda j, tg, nu: (j, d // LANES)),
                pl.BlockSpec((None, n_e, d, dff), lambda j, tg, nu: (tg[j], 0, 0, 0)),
                pl.BlockSpec((None, n_e, d, dff), lambda j, tg, nu: (tg[j], 0, 0, 0)),
                pl.BlockSpec((None, n_e, dff, d), lambda j, tg, nu: (tg[j], 0, 0, 0)),
            ],
            out_specs=pl.BlockSpec((tm, d), lambda j, tg, nu: (j, 0)),
        ),
        out_shape=jax.ShapeDtypeStruct((n_pad, d), F32),
        compiler_params=_cparams(("arbitrary",)),
        name="moe_experts",
    )(tile_group, n_used, aug_sorted, aug_sorted, wg, wu, wd)


def _gather_add_kernel(dest_ref, h_ref, delta_ref, fg_ref, o_ref, buf_ref, sem, *, final_norm):
    i = pl.program_id(0)
    n_steps = pl.num_programs(0)

    def row_copy(step, r):
        slot = lax.rem(step, 2)
        return pltpu.make_async_copy(delta_ref.at[pl.ds(dest_ref[step * ROW_BLOCK + r], 1)],
                                     buf_ref.at[slot, pl.ds(r, 1)], sem.at[slot])

    def issue_all(step):
        def issue(g, _):
            for u in range(ISSUE_UNROLL):
                row_copy(step, g * ISSUE_UNROLL + u).start(priority=u % 2)
            return 0
        lax.fori_loop(0, ROW_BLOCK // ISSUE_UNROLL, issue, 0)

    @pl.when(i == 0)
    def _():
        issue_all(0)

    @pl.when(i + 1 < n_steps)
    def _():
        issue_all(i + 1)

    slot = lax.rem(i, 2)
    pltpu.make_async_copy(delta_ref.at[pl.ds(0, ROW_BLOCK)], buf_ref.at[slot], sem.at[slot]).wait()

    out = h_ref[...] + buf_ref[slot]
    if final_norm:
        out = _rms(out, fg_ref[...])
    o_ref[...] = out


def _gather_add(h2d, delta_sorted, dest, final_gain, final_norm):
    n, d = h2d.shape
    kern = functools.partial(_gather_add_kernel, final_norm=final_norm)
    return pl.pallas_call(
        kern,
        grid_spec=pltpu.PrefetchScalarGridSpec(
            num_scalar_prefetch=1,
            grid=(n // ROW_BLOCK,),
            in_specs=[pl.BlockSpec((ROW_BLOCK, d), lambda i, dest: (i, 0)),
                      pl.BlockSpec(memory_space=pl.ANY),
                      pl.BlockSpec((1, d), lambda i, dest: (0, 0))],
            out_specs=pl.BlockSpec((ROW_BLOCK, d), lambda i, dest: (i, 0)),
            scratch_shapes=[pltpu.VMEM((2, ROW_BLOCK, d), F32), pltpu.SemaphoreType.DMA((2,))],
        ),
        out_shape=jax.ShapeDtypeStruct((n, d), F32),
        compiler_params=_cparams(("arbitrary",)),
        name="moe_unsort_add",
    )(dest, h2d, delta_sorted, final_gain.reshape(1, d))


def _moe_layout(rank, gsel, counts, n):
    tm = MOE_TILE
    cnt = counts[:MOE_GROUPS, 0].astype(I32)
    seg = ((cnt + tm - 1) // tm) * tm
    ends = jnp.cumsum(seg)
    dest = (ends - seg)[gsel.reshape(n)] + rank.reshape(n)
    n_tiles = n // tm + MOE_GROUPS
    starts = jnp.arange(n_tiles, dtype=I32) * tm
    tile_group = jnp.minimum(jnp.sum(starts[:, None] >= ends[None, :], axis=1), MOE_GROUPS - 1).astype(I32)
    n_used = (ends[-1:] // tm).astype(I32)
    return dest.astype(I32), tile_group, n_used, n_tiles * tm


def _prep_router(w_group, b_group, w_router, b_router):
    w = jnp.concatenate([w_router, w_group], axis=1).astype(F32)
    w = jnp.pad(w, ((0, 0), (0, LANES - w.shape[1])))
    hi = w.astype(BF16)
    r1 = w - hi.astype(F32)
    mid = r1.astype(BF16)
    lo = (r1 - mid.astype(F32)).astype(BF16)
    b = jnp.concatenate([b_router, b_group]).astype(F32)
    b = jnp.pad(b, (0, LANES - b.shape[0])).reshape(1, LANES)
    return jnp.stack([hi, mid, lo]), b


def kernel(x, mem, norm_mix, w_in, conv_w, conv_b, dt_bias, a_log, d_skip, ssd_norm, w_ssd_o, w_attn_o, w_out,
           norm_xattn, norm_mem, w_cq, w_ckv, w_co, norm_ffn, w_group, b_group, w_router, b_router,
           w_gate_e, w_up_e, w_down_e, norm_final):
    bsz, seq, d = x.shape
    depth = w_in.shape[0]
    mlen = mem.shape[1]
    heads = dt_bias.shape[1]
    d_inner = heads * SSD_HEAD_DIM
    conv_ch = conv_w.shape[2]
    n = bsz * seq
    rope_c, rope_s = _rope_tables(seq)
    h = x.reshape(n, d)
    mem2d = mem.reshape(bsz * mlen, d)
    for l in range(depth):
        w_in_l = _prep_w_in(w_in[l], d_inner, conv_ch, heads)
        main, misc = _in_projection(h, norm_mix[l], w_in_l, rope_c, rope_s, seq)
        main3d = main.reshape(bsz, seq, main.shape[1])
        misc3d = misc.reshape(bsz, seq, misc.shape[1])
        y_ssd = _ssd_branch(main3d, misc3d, conv_w[l], conv_b[l], dt_bias[l], a_log[l], d_skip[l], ssd_norm[l])
        y_attn = _dsa_branch(main3d, misc3d)
        h = _merge(h, y_ssd.reshape(n, d_inner), y_attn.reshape(n, -1), main,
                   w_ssd_o[l].astype(BF16), w_attn_o[l].astype(BF16), w_out[l].astype(BF16))
        kv = _norm_project(mem2d, norm_mem[l], w_ckv[l].astype(BF16), BF16)
        h = _cross_attention(h.reshape(bsz, seq, d), norm_xattn[l], kv.reshape(bsz, mlen, 2 * d),
                             w_cq[l].astype(BF16), w_co[l].astype(BF16)).reshape(n, d)
        wr3, br = _prep_router(w_group[l], b_group[l], w_router[l], b_router[l])
        aug, rank, gsel, counts = _moe_route(h, norm_ffn[l], wr3, br)
        dest, tile_group, n_used, n_pad = _moe_layout(rank, gsel, counts, n)
        aug_sorted = _scatter_rows(aug, dest, n_pad)
        delta = _sorted_experts(aug_sorted, tile_group, n_used, w_gate_e[l].astype(BF16),
                                w_up_e[l].astype(BF16), w_down_e[l].astype(BF16))
        h = _gather_add(h, delta, dest, norm_final, final_norm=(l == depth - 1))
    return h.reshape(bsz, seq, d)
```
